```python
import jax, jax.numpy as jnp
from jax import lax
import numpy as np

D_MODEL = 1024
BATCH = 2
SEQ = 8192
DEPTH = 2

MIX_WIDTH = D_MODEL
N_MIXERS = 4
GROUP_WIDTH = MIX_WIDTH // N_MIXERS
HEADS_PER_GROUP = 4
HEAD_DIM = GROUP_WIDTH // HEADS_PER_GROUP
SGU_CHUNK = 128
SC_WIDTH = 3
DN_CONV_WIDTH = 4
DN_CHUNK = 64
GLA_CHUNK = 64
GLA_GATE_RANK = 16
GLA_GATE_TEMP = 16.0
D_FF = ((8 * D_MODEL // 3 + 255) // 256) * 256
EPS = 1e-6

_G = GROUP_WIDTH
_H = HEADS_PER_GROUP
IN_SPLITS = (_G, _G,
             _G, _G, _G,
             _G, _G, _G, _H, _H, _G,
             _G, _G, _G, GLA_GATE_RANK, _G)
IN_COLS = sum(IN_SPLITS)

kernel_name = "hybrid_sgu_shortconv_gdn_gla"


def rmsnorm(x, w):
    xf = x.astype(jnp.float32)
    y = xf * lax.rsqrt(jnp.mean(xf * xf, axis=-1, keepdims=True) + EPS)
    return (y * w.astype(jnp.float32)).astype(x.dtype)


def layernorm(x, w, b):
    xf = x.astype(jnp.float32)
    mu = jnp.mean(xf, axis=-1, keepdims=True)
    var = jnp.mean(jnp.square(xf - mu), axis=-1, keepdims=True)
    y = (xf - mu) * lax.rsqrt(var + EPS)
    return (y * w.astype(jnp.float32) + b.astype(jnp.float32)).astype(x.dtype)


def l2norm(t):
    return t * lax.rsqrt(jnp.sum(t * t, axis=-1, keepdims=True) + EPS)


def causal_dwconv(x, w):
    k_width, ch = w.shape
    return lax.conv_general_dilated(
        x, w[:, None, :].astype(x.dtype), window_strides=(1,),
        padding=[(k_width - 1, 0)], dimension_numbers=('NWC', 'WIO', 'NWC'),
        feature_group_count=ch)


def to_heads(t):
    b, s, _ = t.shape
    return t.reshape(b, s, HEADS_PER_GROUP, HEAD_DIM).astype(jnp.float32)


def to_chunks(t, c):
    b, s = t.shape[:2]
    t = t.reshape(b, s // c, c, *t.shape[2:])
    return jnp.moveaxis(t, 3, 1)


def from_chunks(t):
    b, h, n, c, d = t.shape
    return jnp.moveaxis(t, 1, 3).reshape(b, n * c, h, d)


def sgu_mixer(u, v, ln_w, ln_b, w_s, b_s):
    bsz, s, g = u.shape
    n = s // SGU_CHUNK
    u = jax.nn.gelu(u)
    v = layernorm(jax.nn.gelu(v), ln_w, ln_b)
    vc = v.reshape(bsz, n, SGU_CHUNK, HEADS_PER_GROUP, HEAD_DIM)
    mask = jnp.tril(jnp.ones((SGU_CHUNK, SGU_CHUNK), dtype=bool))
    ws = jnp.where(mask, w_s, 0.0).astype(v.dtype)
    mixed = jnp.einsum('hts,bnshd->bnthd', ws, vc) + b_s.T.astype(v.dtype)[None, None, :, :, None]
    return u * mixed.reshape(bsz, s, g)


def short_conv_mixer(gate_b, gate_c, h, w_conv):
    return gate_b * causal_dwconv(gate_c * h, w_conv)


def chunk_gated_delta_rule(q, k, v, g, beta):
    bsz, s, h, dk = q.shape
    dv = v.shape[-1]
    c = DN_CHUNK
    q, k, v = to_chunks(q, c), to_chunks(k, c), to_chunks(v, c)
    g, beta = to_chunks(g, c), to_chunks(beta, c)
    q = q * dk ** -0.5
    gc = jnp.cumsum(g, axis=-1)
    causal = jnp.tril(jnp.ones((c, c), dtype=bool))
    strict = jnp.tril(jnp.ones((c, c), dtype=bool), k=-1)
    decay = jnp.exp(jnp.where(causal, gc[..., :, None] - gc[..., None, :], -jnp.inf))
    kb = k * beta[..., None]
    low = jnp.where(strict, jnp.einsum('bhncd,bhnsd->bhncs', kb, k) * decay, 0.0)
    rhs = jnp.concatenate([v * beta[..., None], kb * jnp.exp(gc)[..., None]], axis=-1)
    sol = lax.linalg.triangular_solve(low, rhs, left_side=True, lower=True, unit_diagonal=True)
    u, w = sol[..., :dv], sol[..., dv:]
    attn = jnp.einsum('bhncd,bhnsd->bhncs', q, k) * decay
    qg = q * jnp.exp(gc)[..., None]
    k_dec = k * jnp.exp(gc[..., -1:] - gc)[..., None]
    chunk_dec = jnp.exp(gc[..., -1])

    def step(state, inp):
        qg_n, w_n, u_n, attn_n, kd_n, dec_n = inp
        v_new = u_n - jnp.einsum('bhcd,bhde->bhce', w_n, state)
        o = jnp.einsum('bhcd,bhde->bhce', qg_n, state) + jnp.einsum('bhcs,bhse->bhce', attn_n, v_new)
        state = state * dec_n[..., None, None] + jnp.einsum('bhcd,bhce->bhde', kd_n, v_new)
        return state, o

    s0 = jnp.zeros((bsz, h, dk, dv), jnp.float32)
    xs = tuple(jnp.moveaxis(t, 2, 0) for t in (qg, w, u, attn, k_dec, chunk_dec))
    _, o = lax.scan(step, s0, xs)
    return from_chunks(jnp.moveaxis(o, 0, 2))


def deltanet_mixer(q, k, v, a, b, z, conv_w, a_log, dt_bias, norm_w):
    bsz, s, _ = q.shape
    qkv = jax.nn.silu(causal_dwconv(jnp.concatenate([q, k, v], axis=-1), conv_w))
    q, k, v = jnp.split(qkv, 3, axis=-1)
    q, k, v = l2norm(to_heads(q)), l2norm(to_heads(k)), to_heads(v)
    g = -jnp.exp(a_log.astype(jnp.float32)) * jax.nn.softplus(a.astype(jnp.float32) + dt_bias.astype(jnp.float32))
    beta = jax.nn.sigmoid(b.astype(jnp.float32))
    o = chunk_gated_delta_rule(q, k, v, g, beta)
    o = rmsnorm(o, norm_w) * jax.nn.silu(to_heads(z))
    return o.reshape(bsz, s, GROUP_WIDTH).astype(z.dtype)


def chunk_gla(q, k, v, log_a):
    bsz, s, h, dk = q.shape
    dv = v.shape[-1]
    c = GLA_CHUNK
    q, k, v, log_a = (to_chunks(t, c) for t in (q, k, v, log_a))
    q = q * dk ** -0.5
    gcum = jnp.cumsum(log_a, axis=3)
    g_mid = gcum[:, :, :, c // 2:c // 2 + 1, :]
    qa = q * jnp.exp(gcum - g_mid)
    ka = k * jnp.exp(g_mid - gcum)
    causal = jnp.tril(jnp.ones((c, c), dtype=bool))
    attn = jnp.where(causal, jnp.einsum('bhncd,bhnsd->bhncs', qa, ka), 0.0)
    o_intra = jnp.einsum('bhncs,bhnse->bhnce', attn, v)
    qg = q * jnp.exp(gcum)
    k_last = k * jnp.exp(gcum[:, :, :, -1:, :] - gcum)
    chunk_dec = jnp.exp(gcum[:, :, :, -1, :])

    def step(state, inp):
        qg_n, kl_n, v_n, dec_n = inp
        o = jnp.einsum('bhcd,bhde->bhce', qg_n, state)
        state = state * dec_n[..., :, None] + jnp.einsum('bhcd,bhce->bhde', kl_n, v_n)
        return state, o

    s0 = jnp.zeros((bsz, h, dk, dv), jnp.float32)
    xs = tuple(jnp.moveaxis(t, 2, 0) for t in (qg, k_last, v, chunk_dec))
    _, o_inter = lax.scan(step, s0, xs)
    return from_chunks(o_intra + jnp.moveaxis(o_inter, 0, 2))


def gla_mixer(q, k, v, g_lr, z, w_gate2, gate_bias, norm_w):
    bsz, s, _ = q.shape
    pre = jnp.einsum('btr,rg->btg', g_lr, w_gate2.astype(g_lr.dtype)).astype(jnp.float32) + gate_bias.astype(jnp.float32)
    log_a = jax.nn.log_sigmoid(pre) / GLA_GATE_TEMP
    o = chunk_gla(to_heads(q), to_heads(k), to_heads(v), to_heads(log_a))
    o = rmsnorm(o, norm_w) * jax.nn.silu(to_heads(z))
    return o.reshape(bsz, s, GROUP_WIDTH).astype(z.dtype)


def swiglu(h, w_gate_up, w_down):
    gu = h @ w_gate_up
    gate, up = jnp.split(gu, 2, axis=-1)
    return (jax.nn.silu(gate) * up) @ w_down


def setup_inputs(seed: int = 0) -> dict:
    key = jax.random.key(seed)
    ks = jax.random.split(key, 24)
    f32 = jnp.float32

    def nrm(k, shape, scale):
        return jax.random.normal(k, shape, f32) * scale

    def gain(k, shape):
        return 1.0 + 0.02 * jax.random.normal(k, shape, f32)

    dt = jnp.exp(jax.random.uniform(ks[11], (DEPTH, HEADS_PER_GROUP), f32, np.log(1e-3), np.log(1e-1)))
    return {
        "x": jax.random.normal(ks[0], (BATCH, SEQ, D_MODEL), f32),
        "norm1_w": gain(ks[1], (DEPTH, D_MODEL)),
        "w_in": nrm(ks[2], (DEPTH, D_MODEL, IN_COLS), D_MODEL ** -0.5),
        "sgu_ln_w": gain(ks[3], (DEPTH, GROUP_WIDTH)),
        "sgu_ln_b": nrm(ks[4], (DEPTH, GROUP_WIDTH), 0.02),
        "sgu_w_spatial": nrm(ks[5], (DEPTH, HEADS_PER_GROUP, SGU_CHUNK, SGU_CHUNK), SGU_CHUNK ** -0.5),
        "sgu_b_spatial": gain(ks[6], (DEPTH, HEADS_PER_GROUP, SGU_CHUNK)),
        "sc_conv_w": nrm(ks[7], (DEPTH, SC_WIDTH, GROUP_WIDTH), SC_WIDTH ** -0.5),
        "dn_conv_w": nrm(ks[8], (DEPTH, DN_CONV_WIDTH, 3 * GROUP_WIDTH), DN_CONV_WIDTH ** -0.5),
        "dn_a_log": jnp.log(jax.random.uniform(ks[9], (DEPTH, HEADS_PER_GROUP), f32, 1.0, 16.0)),
        "dn_dt_bias": dt + jnp.log(-jnp.expm1(-dt)),
        "dn_norm_w": gain(ks[10], (DEPTH, HEAD_DIM)),
        "gla_w_gate2": nrm(ks[12], (DEPTH, GLA_GATE_RANK, GROUP_WIDTH), GLA_GATE_RANK ** -0.5),
        "gla_gate_bias": nrm(ks[13], (DEPTH, GROUP_WIDTH), 0.1),
        "gla_norm_w": gain(ks[14], (DEPTH, HEAD_DIM)),
        "w_out": nrm(ks[15], (DEPTH, MIX_WIDTH, D_MODEL), MIX_WIDTH ** -0.5),
        "norm2_w": gain(ks[16], (DEPTH, D_MODEL)),
        "w_gate_up": nrm(ks[17], (DEPTH, D_MODEL, 2 * D_FF), D_MODEL ** -0.5),
        "w_down": nrm(ks[18], (DEPTH, D_FF, D_MODEL), D_FF ** -0.5),
        "final_norm_w": gain(ks[19], (D_MODEL,)),
    }


def reference(x, norm1_w, w_in, sgu_ln_w, sgu_ln_b, sgu_w_spatial, sgu_b_spatial, sc_conv_w,
              dn_conv_w, dn_a_log, dn_dt_bias, dn_norm_w, gla_w_gate2, gla_gate_bias, gla_norm_w,
              w_out, norm2_w, w_gate_up, w_down, final_norm_w):
    split_idx = [int(i) for i in np.cumsum(IN_SPLITS)[:-1]]
    for l in range(DEPTH):
        h = rmsnorm(x, norm1_w[l])
        p = h @ w_in[l]
        (a_u, a_v, b_b, b_c, b_h, c_q, c_k, c_v, c_a, c_b, c_z,
         d_q, d_k, d_v, d_g, d_z) = jnp.split(p, split_idx, axis=-1)
        y_a = sgu_mixer(a_u, a_v, sgu_ln_w[l], sgu_ln_b[l], sgu_w_spatial[l], sgu_b_spatial[l])
        y_b = short_conv_mixer(b_b, b_c, b_h, sc_conv_w[l])
        y_c = deltanet_mixer(c_q, c_k, c_v, c_a, c_b, c_z, dn_conv_w[l], dn_a_log[l], dn_dt_bias[l], dn_norm_w[l])
        y_d = gla_mixer(d_q, d_k, d_v, d_g, d_z, gla_w_gate2[l], gla_gate_bias[l], gla_norm_w[l])
        mix = jnp.concatenate([y_a, y_b.astype(x.dtype), y_c, y_d], axis=-1)
        x = x + (mix @ w_out[l]).astype(x.dtype)
        x = x + swiglu(rmsnorm(x, norm2_w[l]), w_gate_up[l], w_down[l]).astype(x.dtype)
    return rmsnorm(x, final_norm_w)
```

```python
import functools

import numpy as np
import jax
import jax.numpy as jnp
from jax import lax
from jax.experimental import pallas as pl
from jax.experimental.pallas import tpu as pltpu

F32 = jnp.float32
BF16 = jnp.bfloat16

D_MODEL = 1024
GROUP = 256
HEADS = 4
HEAD_DIM = 64
SGU_CHUNK = 128
SC_WIDTH = 3
DN_CONV_WIDTH = 4
CHUNK = 64
GATE_RANK = 16
GATE_TEMP = 16.0
D_FF = 2816
EPS = 1e-6
GATE_COLS = 128
TAIL = 8

ROW_TILE = 512
FF_CHUNK = 256
VMEM_LIMIT = 56 * 1024 * 1024


def _dot(a, b):
    return jnp.dot(a, b, preferred_element_type=F32)


def _dot_nt(a, b):
    return lax.dot_general(a, b, (((1,), (1,)), ((), ())), preferred_element_type=F32)


def _dot_tn(a, b):
    return lax.dot_general(a, b, (((0,), (0,)), ((), ())), preferred_element_type=F32)


def _split3(x):
    hi = x.astype(BF16)
    r = x - hi.astype(F32)
    mid = r.astype(BF16)
    lo = (r - mid.astype(F32)).astype(BF16)
    return hi, mid, lo


def _dot_sel_rhs(x, sel):
    hi, mid, lo = _split3(x)
    return _dot(hi, sel) + _dot(mid, sel) + _dot(lo, sel)


def _dot_sel_lhs(sel, x):
    hi, mid, lo = _split3(x)
    return _dot(sel, hi) + _dot(sel, mid) + _dot(sel, lo)


def _softplus(x):
    return jnp.maximum(x, 0.0) + jnp.log(1.0 + jnp.exp(-jnp.abs(x)))


def _sigmoid(x):
    return 1.0 / (1.0 + jnp.exp(-x))


def _silu(x):
    return x * _sigmoid(x)


def _gelu_tanh(x):
    return 0.5 * x * (1.0 + jnp.tanh(0.7978845608028654 * (x + 0.044715 * x * x * x)))


def _blockdiag(x, mask):
    xb = x.astype(BF16)
    return jnp.concatenate([xb] * HEADS, axis=0) * mask


def _head_mean_sq(x, bones):
    return _dot((x * x).astype(BF16), bones) * (1.0 / HEAD_DIM)


def _inproj_kernel(x_ref, nw_ref, w_ref, pa_ref, pb_ref, pc_ref, pd_ref, pg_ref):
    x = x_ref[...]
    ms = jnp.mean(x * x, axis=-1, keepdims=True)
    h = (x * lax.rsqrt(ms + EPS) * nw_ref[...]).astype(BF16)
    col = 0
    for ref in (pa_ref, pb_ref, pc_ref, pd_ref, pg_ref):
        n = ref.shape[1]
        ref[...] = _dot(h, w_ref[:, col:col + n]).astype(ref.dtype)
        col += n


def _inproj(x, nw, w):
    m = x.shape[0]
    widths = (2 * GROUP, 3 * GROUP, 4 * GROUP, 4 * GROUP, GATE_COLS)
    dtypes = (BF16, BF16, BF16, BF16, F32)
    row = lambda i: (i, 0)
    fixed = lambda i: (0, 0)
    return pl.pallas_call(
        _inproj_kernel,
        grid=(m // ROW_TILE,),
        in_specs=[pl.BlockSpec((ROW_TILE, D_MODEL), row),
                  pl.BlockSpec((1, D_MODEL), fixed),
                  pl.BlockSpec(w.shape, fixed, pipeline_mode=pl.Buffered(1))],
        out_specs=[pl.BlockSpec((ROW_TILE, n), row) for n in widths],
        out_shape=[jax.ShapeDtypeStruct((m, n), dt) for n, dt in zip(widths, dtypes)],
        compiler_params=pltpu.CompilerParams(dimension_semantics=("arbitrary",),
                                             vmem_limit_bytes=VMEM_LIMIT),
        name="in_proj",
    )(x, nw, w)


def _sgu_sconv_kernel(seq_tiles, pa_ref, pb_ref, lnw_ref, lnb_ref, ws_ref, bs_ref, cw_ref,
                      ya_ref, yb_ref, ext_ref):
    tm = pa_ref.shape[0]
    tri = (lax.broadcasted_iota(jnp.int32, (SGU_CHUNK, SGU_CHUNK), 0)
           >= lax.broadcasted_iota(jnp.int32, (SGU_CHUNK, SGU_CHUNK), 1))
    head_of_lane = lax.broadcasted_iota(jnp.int32, (SGU_CHUNK, GROUP), 1) // HEAD_DIM
    ws = [jnp.where(tri, ws_ref[h], 0.0).astype(BF16) for h in range(HEADS)]
    for c in range(tm // SGU_CHUNK):
        rows = slice(c * SGU_CHUNK, (c + 1) * SGU_CHUNK)
        u = _gelu_tanh(pa_ref[rows, 0:GROUP].astype(F32))
        v = _gelu_tanh(pa_ref[rows, GROUP:2 * GROUP].astype(F32))
        mu = jnp.mean(v, axis=-1, keepdims=True)
        vc = v - mu
        var = jnp.mean(vc * vc, axis=-1, keepdims=True)
        vn = vc * lax.rsqrt(var + EPS) * lnw_ref[...] + lnb_ref[...]
        mixed = bs_ref[...]
        for h in range(HEADS):
            vh = jnp.where(head_of_lane == h, vn, 0.0).astype(BF16)
            mixed = mixed + _dot(ws[h], vh)
        ya_ref[rows, :] = (u * mixed).astype(ya_ref.dtype)

    @pl.when(lax.rem(pl.program_id(0), seq_tiles) == 0)
    def _():
        ext_ref[0:TAIL, :] = jnp.zeros((TAIL, GROUP), F32)

    gate_b = pb_ref[:, 0:GROUP].astype(F32)
    s = pb_ref[:, GROUP:2 * GROUP].astype(F32) * pb_ref[:, 2 * GROUP:3 * GROUP].astype(F32)
    ext_ref[TAIL:TAIL + tm, :] = s
    conv = s * cw_ref[SC_WIDTH - 1:SC_WIDTH, :]
    for j in range(1, SC_WIDTH):
        conv = conv + ext_ref[TAIL - j:TAIL - j + tm, :] * cw_ref[SC_WIDTH - 1 - j:SC_WIDTH - j, :]
    yb_ref[...] = (gate_b * conv).astype(yb_ref.dtype)
    ext_ref[0:TAIL, :] = s[tm - TAIL:tm, :]


def _sgu_sconv(pa, pb, lnw, lnb, ws, bs_wide, cw, seq_len):
    m = pa.shape[0]
    row = lambda i: (i, 0)
    fixed2 = lambda i: (0, 0)
    fixed3 = lambda i: (0, 0, 0)
    return pl.pallas_call(
        functools.partial(_sgu_sconv_kernel, seq_len // ROW_TILE),
        grid=(m // ROW_TILE,),
        in_specs=[pl.BlockSpec((ROW_TILE, 2 * GROUP), row),
                  pl.BlockSpec((ROW_TILE, 3 * GROUP), row),
                  pl.BlockSpec((1, GROUP), fixed2),
                  pl.BlockSpec((1, GROUP), fixed2),
                  pl.BlockSpec(ws.shape, fixed3),
                  pl.BlockSpec(bs_wide.shape, fixed2),
                  pl.BlockSpec(cw.shape, fixed2)],
        out_specs=[pl.BlockSpec((ROW_TILE, GROUP), row), pl.BlockSpec((ROW_TILE, GROUP), row)],
        out_shape=[jax.ShapeDtypeStruct((m, GROUP), BF16), jax.ShapeDtypeStruct((m, GROUP), BF16)],
        scratch_shapes=[pltpu.VMEM((TAIL + ROW_TILE, GROUP), F32)],
        compiler_params=pltpu.CompilerParams(dimension_semantics=("arbitrary",),
                                             vmem_limit_bytes=VMEM_LIMIT),
        name="sgu_sconv",
    )(pa, pb, lnw, lnb, ws, bs_wide, cw)


def _chunk_masks():
    i = lax.broadcasted_iota(jnp.int32, (CHUNK, GROUP), 0)
    j = lax.broadcasted_iota(jnp.int32, (CHUNK, GROUP), 1) % CHUNK
    return i == j, i >= j, i > j


def _const_inputs():
    lane_head = np.arange(GROUP) // HEAD_DIM
    bd = (lane_head[:, None] == lane_head[None, :]).astype(np.float32)
    ltri = np.tril(np.ones((CHUNK, CHUNK), np.float32))
    return jnp.asarray(bd, BF16), jnp.asarray(ltri, BF16)


def _gdn_kernel(seq_tiles, pc_ref, pg_ref, cw_ref, alog_ref, dtb_ref, nw_ref, ea_ref, eb_ref,
                bd_ref, ltri_ref, y_ref,
                ext_ref, q_ref, k_ref, v_ref, g_ref, b_ref, o_ref, s_ref):
    tm = pc_ref.shape[0]
    width = 3 * GROUP

    @pl.when(lax.rem(pl.program_id(0), seq_tiles) == 0)
    def _():
        ext_ref[0:TAIL, :] = jnp.zeros((TAIL, width), F32)
        s_ref[...] = jnp.zeros((GROUP, GROUP), F32)

    bd = bd_ref[...]
    x = pc_ref[:, 0:width].astype(F32)
    ext_ref[TAIL:TAIL + tm, :] = x
    conv = x * cw_ref[DN_CONV_WIDTH - 1:DN_CONV_WIDTH, :]
    for j in range(1, DN_CONV_WIDTH):
        conv = conv + ext_ref[TAIL - j:TAIL - j + tm, :] * cw_ref[DN_CONV_WIDTH - 1 - j:DN_CONV_WIDTH - j, :]
    ext_ref[0:TAIL, :] = x[tm - TAIL:tm, :]
    qkv = _silu(conv)
    q = qkv[:, 0:GROUP]
    k = qkv[:, GROUP:2 * GROUP]
    q_ref[...] = q * lax.rsqrt(_dot((q * q).astype(BF16), bd) + EPS) * (HEAD_DIM ** -0.5)
    k_ref[...] = k * lax.rsqrt(_dot((k * k).astype(BF16), bd) + EPS)
    v_ref[...] = qkv[:, 2 * GROUP:3 * GROUP]
    pg = pg_ref[...]
    a_w = _dot_sel_rhs(pg, ea_ref[...])
    b_w = _dot_sel_rhs(pg, eb_ref[...])
    g_ref[...] = -jnp.exp(alog_ref[...]) * _softplus(a_w + dtb_ref[...])
    b_ref[...] = _sigmoid(b_w)

    def chunk(c, carry):
        eye_w, causal_w, strict_w = _chunk_masks()
        eye_f = eye_w.astype(F32)
        ltri = ltri_ref[...]
        bd = bd_ref[...]
        rows = pl.ds(pl.multiple_of(c * CHUNK, CHUNK), CHUNK)
        qn = q_ref[rows, :]
        kn = k_ref[rows, :]
        v = v_ref[rows, :]
        beta = b_ref[rows, :]
        gc = _dot_sel_lhs(ltri, g_ref[rows, :])
        eg = jnp.exp(gc)
        g_row = jnp.sum(jnp.where(eye_w, gc, 0.0), axis=0, keepdims=True)
        decay = jnp.where(causal_w, jnp.exp(jnp.minimum(gc - g_row, 0.0)), 0.0)
        kb = kn * beta
        r1 = _dot_nt(jnp.concatenate([kb, qn], axis=0).astype(BF16), _blockdiag(kn, bd))
        low = jnp.where(strict_w, r1[0:CHUNK] * decay, 0.0)
        attn = r1[CHUNK:2 * CHUNK] * decay
        t = eye_f - low
        p = _dot(low.astype(BF16), _blockdiag(low, bd))
        n_levels = CHUNK.bit_length() - 1
        for level in range(1, n_levels):
            p_bd = _blockdiag(p, bd)
            if level < n_levels - 1:
                r2 = _dot(jnp.concatenate([p, t], axis=0).astype(BF16), p_bd)
                p = r2[0:CHUNK]
                t = t + r2[CHUNK:2 * CHUNK]
            else:
                t = t + _dot(t.astype(BF16), p_bd)
        tb = t.astype(BF16)
        u = _dot(tb, _blockdiag(v * beta, bd))
        w = _dot(tb, _blockdiag(kb * eg, bd))
        s = s_ref[...]
        r3 = _dot(jnp.concatenate([w, qn * eg], axis=0).astype(BF16), s.astype(BF16))
        v_new = u - r3[0:CHUNK]
        o_ref[rows, :] = r3[CHUNK:2 * CHUNK] + _dot(attn.astype(BF16), _blockdiag(v_new, bd))
        gc_last = gc[CHUNK - 1:CHUNK, :]
        k_dec = kn * jnp.exp(gc_last - gc)
        s_ref[...] = s * jnp.exp(gc_last) + bd.astype(F32) * _dot_tn(k_dec.astype(BF16), v_new.astype(BF16))
        return carry

    lax.fori_loop(0, tm // CHUNK, chunk, 0)

    o = o_ref[...]
    z = pc_ref[:, width:width + GROUP].astype(F32)
    y = o * lax.rsqrt(_head_mean_sq(o, bd) + EPS) * nw_ref[...] * _silu(z)
    y_ref[...] = y.astype(y_ref.dtype)


def _gdn(pc, pg, cw, alog_w, dtb_w, nw_w, ea, eb, bd, ltri, seq_len):
    m = pc.shape[0]
    row = lambda i: (i, 0)
    fixed = lambda i: (0, 0)
    tile = lambda: pltpu.VMEM((ROW_TILE, GROUP), F32)
    return pl.pallas_call(
        functools.partial(_gdn_kernel, seq_len // ROW_TILE),
        grid=(m // ROW_TILE,),
        in_specs=[pl.BlockSpec((ROW_TILE, 4 * GROUP), row),
                  pl.BlockSpec((ROW_TILE, GATE_COLS), row),
                  pl.BlockSpec(cw.shape, fixed),
                  pl.BlockSpec((1, GROUP), fixed),
                  pl.BlockSpec((1, GROUP), fixed),
                  pl.BlockSpec((1, GROUP), fixed),
                  pl.BlockSpec(ea.shape, fixed),
                  pl.BlockSpec(eb.shape, fixed),
                  pl.BlockSpec(bd.shape, fixed),
                  pl.BlockSpec(ltri.shape, fixed)],
        out_specs=pl.BlockSpec((ROW_TILE, GROUP), row),
        out_shape=jax.ShapeDtypeStruct((m, GROUP), BF16),
        scratch_shapes=[pltpu.VMEM((TAIL + ROW_TILE, 3 * GROUP), F32),
                        tile(), tile(), tile(), tile(), tile(), tile(),
                        pltpu.VMEM((GROUP, GROUP), F32)],
        compiler_params=pltpu.CompilerParams(dimension_semantics=("arbitrary",),
                                             vmem_limit_bytes=VMEM_LIMIT),
        name="gdn",
    )(pc, pg, cw, alog_w, dtb_w, nw_w, ea, eb, bd, ltri)


def _gla_kernel(seq_tiles, pd_ref, pg_ref, wg_ref, gb_ref, nw_ref, bd_ref, ltri_ref, y_ref,
                la_ref, o_ref, st_ref):
    tm = pd_ref.shape[0]

    @pl.when(lax.rem(pl.program_id(0), seq_tiles) == 0)
    def _():
        st_ref[...] = jnp.zeros((GROUP, GROUP), F32)

    bd = bd_ref[...]
    pre = _dot(pg_ref[...].astype(BF16), wg_ref[...]) + gb_ref[...]
    la_ref[...] = -_softplus(-pre) * (1.0 / GATE_TEMP)

    mid = CHUNK // 2

    def chunk(c, carry):
        _, causal_w, _ = _chunk_masks()
        ltri = ltri_ref[...]
        bd = bd_ref[...]
        rows = pl.ds(pl.multiple_of(c * CHUNK, CHUNK), CHUNK)
        q = pd_ref[rows, 0:GROUP].astype(F32) * (HEAD_DIM ** -0.5)
        k = pd_ref[rows, GROUP:2 * GROUP].astype(F32)
        v = pd_ref[rows, 2 * GROUP:3 * GROUP]
        gcum = _dot_sel_lhs(ltri, la_ref[rows, :])
        g_mid = gcum[mid:mid + 1, :]
        g_last = gcum[CHUNK - 1:CHUNK, :]
        qa = q * jnp.exp(gcum - g_mid)
        ka = k * jnp.exp(g_mid - gcum)
        attn = jnp.where(causal_w, _dot_nt(qa.astype(BF16), _blockdiag(ka, bd)), 0.0)
        st = st_ref[...]
        o_ref[rows, :] = (_dot(attn.astype(BF16), _blockdiag(v, bd))
                          + _dot_nt((q * jnp.exp(gcum)).astype(BF16), st.astype(BF16)))
        k_last = k * jnp.exp(g_last - gcum)
        st_ref[...] = st * jnp.exp(g_last) + bd.astype(F32) * _dot_tn(v, k_last.astype(BF16))
        return carry

    lax.fori_loop(0, tm // CHUNK, chunk, 0)

    o = o_ref[...]
    z = pd_ref[:, 3 * GROUP:4 * GROUP].astype(F32)
    y = o * lax.rsqrt(_head_mean_sq(o, bd) + EPS) * nw_ref[...] * _silu(z)
    y_ref[...] = y.astype(y_ref.dtype)


def _gla(pd, pg, wg_pad, gb, nw_w, bd, ltri, seq_len):
    m = pd.shape[0]
    row = lambda i: (i, 0)
    fixed = lambda i: (0, 0)
    return pl.pallas_call(
        functools.partial(_gla_kernel, seq_len // ROW_TILE),
        grid=(m // ROW_TILE,),
        in_specs=[pl.BlockSpec((ROW_TILE, 4 * GROUP), row),
                  pl.BlockSpec((ROW_TILE, GATE_COLS), row),
                  pl.BlockSpec(wg_pad.shape, fixed),
                  pl.BlockSpec((1, GROUP), fixed),
                  pl.BlockSpec((1, GROUP), fixed),
                  pl.BlockSpec(bd.shape, fixed),
                  pl.BlockSpec(ltri.shape, fixed)],
        out_specs=pl.BlockSpec((ROW_TILE, GROUP), row),
        out_shape=jax.ShapeDtypeStruct((m, GROUP), BF16),
        scratch_shapes=[pltpu.VMEM((ROW_TILE, GROUP), F32),
                        pltpu.VMEM((ROW_TILE, GROUP), F32),
                        pltpu.VMEM((GROUP, GROUP), F32)],
        compiler_params=pltpu.CompilerParams(dimension_semantics=("arbitrary",),
                                             vmem_limit_bytes=VMEM_LIMIT),
        name="gla",
    )(pd, pg, wg_pad, gb, nw_w, bd, ltri)


def _outproj_ffn_kernel(final, x_ref, ya_ref, yb_ref, yc_ref, yd_ref, wo_ref, n2_ref, wgu_ref, wd_ref,
                        fn_ref, o_ref, act_ref):
    x1 = x_ref[...]
    for idx, y_ref in enumerate((ya_ref, yb_ref, yc_ref, yd_ref)):
        x1 = x1 + _dot(y_ref[...], wo_ref[idx * GROUP:(idx + 1) * GROUP, :])
    ms = jnp.mean(x1 * x1, axis=-1, keepdims=True)
    h = (x1 * lax.rsqrt(ms + EPS) * n2_ref[...]).astype(BF16)
    for j in range(D_FF // FF_CHUNK):
        cols = slice(j * FF_CHUNK, (j + 1) * FF_CHUNK)
        gate = _dot(h, wgu_ref[:, cols])
        up = _dot(h, wgu_ref[:, D_FF + j * FF_CHUNK:D_FF + (j + 1) * FF_CHUNK])
        act_ref[:, cols] = (_silu(gate) * up).astype(BF16)
    x2 = x1 + _dot(act_ref[...], wd_ref[...])
    if final:
        ms2 = jnp.mean(x2 * x2, axis=-1, keepdims=True)
        x2 = x2 * lax.rsqrt(ms2 + EPS) * fn_ref[...]
    o_ref[...] = x2


def _outproj_ffn(x, ys, wo, n2, wgu, wd, fn, final):
    m = x.shape[0]
    row = lambda i: (i, 0)
    fixed = lambda i: (0, 0)
    resident = lambda a: pl.BlockSpec(a.shape, fixed, pipeline_mode=pl.Buffered(1))
    return pl.pallas_call(
        functools.partial(_outproj_ffn_kernel, final),
        grid=(m // ROW_TILE,),
        in_specs=[pl.BlockSpec((ROW_TILE, D_MODEL), row)]
                 + [pl.BlockSpec((ROW_TILE, GROUP), row)] * 4
                 + [resident(wo), pl.BlockSpec((1, D_MODEL), fixed), resident(wgu), resident(wd),
                    pl.BlockSpec((1, D_MODEL), fixed)],
        out_specs=pl.BlockSpec((ROW_TILE, D_MODEL), row),
        out_shape=jax.ShapeDtypeStruct((m, D_MODEL), F32),
        scratch_shapes=[pltpu.VMEM((ROW_TILE, D_FF), BF16)],
        compiler_params=pltpu.CompilerParams(dimension_semantics=("arbitrary",),
                                             vmem_limit_bytes=VMEM_LIMIT),
        name="outproj_ffn",
    )(x, *ys, wo, n2, wgu, wd, fn)


def _relayout_w_in(w):
    g = GROUP
    c0 = 5 * g
    c_a = c0 + 3 * g
    c_z = c_a + 2 * HEADS
    d0 = c_z + g
    d_g = d0 + 3 * g
    d_z = d_g + GATE_RANK
    pad = jnp.zeros((w.shape[0], GATE_COLS - 2 * HEADS - GATE_RANK), w.dtype)
    return jnp.concatenate(
        [w[:, 0:c0], w[:, c0:c_a], w[:, c_z:d0], w[:, d0:d_g], w[:, d_z:d_z + g],
         w[:, c_a:c_z], w[:, d_g:d_z], pad], axis=1).astype(BF16)


def _wide(v):
    return jnp.repeat(v.astype(F32), HEAD_DIM)[None, :]


def _gate_selectors():
    ea = np.zeros((GATE_COLS, GROUP), np.float32)
    eb = np.zeros((GATE_COLS, GROUP), np.float32)
    for h in range(HEADS):
        ea[h, h * HEAD_DIM:(h + 1) * HEAD_DIM] = 1.0
        eb[HEADS + h, h * HEAD_DIM:(h + 1) * HEAD_DIM] = 1.0
    return jnp.asarray(ea, BF16), jnp.asarray(eb, BF16)


def kernel(x, norm1_w, w_in, sgu_ln_w, sgu_ln_b, sgu_w_spatial, sgu_b_spatial, sc_conv_w, dn_conv_w, dn_a_log, dn_dt_bias, dn_norm_w, gla_w_gate2, gla_gate_bias, gla_norm_w, w_out, norm2_w, w_gate_up, w_down, final_norm_w):
    bsz, seq, d = x.shape
    depth = w_in.shape[0]
    assert seq % ROW_TILE == 0 and d == D_MODEL
    xf = x.reshape(bsz * seq, d)
    bd, ltri = _const_inputs()
    ea, eb = _gate_selectors()
    for l in range(depth):
        pa, pb, pc, pd, pg = _inproj(xf, norm1_w[l][None, :], _relayout_w_in(w_in[l]))
        bs_wide = jnp.repeat(sgu_b_spatial[l].T, HEAD_DIM, axis=1)
        ya, yb = _sgu_sconv(pa, pb, sgu_ln_w[l][None, :], sgu_ln_b[l][None, :], sgu_w_spatial[l],
                            bs_wide, sc_conv_w[l], seq)
        yc = _gdn(pc, pg, dn_conv_w[l], _wide(dn_a_log[l]), _wide(dn_dt_bias[l]),
                  jnp.tile(dn_norm_w[l], HEADS)[None, :], ea, eb, bd, ltri, seq)
        wg_pad = jnp.zeros((GATE_COLS, GROUP), BF16).at[2 * HEADS:2 * HEADS + GATE_RANK].set(
            gla_w_gate2[l].astype(BF16))
        yd = _gla(pd, pg, wg_pad, gla_gate_bias[l][None, :], jnp.tile(gla_norm_w[l], HEADS)[None, :],
                  bd, ltri, seq)
        xf = _outproj_ffn(xf, (ya, yb, yc, yd), w_out[l].astype(BF16), norm2_w[l][None, :],
                          w_gate_up[l].astype(BF16), w_down[l].astype(BF16), final_norm_w[None, :],
                          final=(l == depth - 1))
    return xf.reshape(bsz, seq, d)
```

```python
import functools

import numpy as np
import jax
import jax.numpy as jnp
from jax import lax
from jax.experimental import pallas as pl
from jax.experimental.pallas import tpu as pltpu

F32 = jnp.float32
BF16 = jnp.bfloat16

D_MODEL = 1024
GROUP = 256
HEADS = 4
HEAD_DIM = 64
SGU_CHUNK = 128
SC_WIDTH = 3
DN_CONV_WIDTH = 4
CHUNK = 64
GATE_RANK = 16
GATE_TEMP = 16.0
D_FF = 2816
EPS = 1e-6
GATE_COLS = 128
TAIL = 8

ROW_TILE = 512
SEQ_TILE = 256
GDN_PAR = 8
SCAN_PAR = 4
GLA_PAR = 4
FF_CHUNK = 256
VMEM_LIMIT = 56 * 1024 * 1024


def _dot(a, b):
    return jnp.dot(a, b, preferred_element_type=F32)


def _dot_nt(a, b):
    return lax.dot_general(a, b, (((1,), (1,)), ((), ())), preferred_element_type=F32)


def _dot_tn(a, b):
    return lax.dot_general(a, b, (((0,), (0,)), ((), ())), preferred_element_type=F32)


def _split3(x):
    hi = x.astype(BF16)
    r = x - hi.astype(F32)
    mid = r.astype(BF16)
    lo = (r - mid.astype(F32)).astype(BF16)
    return hi, mid, lo


def _dot_sel_rhs(x, sel):
    hi, mid, lo = _split3(x)
    return _dot(hi, sel) + _dot(mid, sel) + _dot(lo, sel)


def _dot_sel_lhs(sel, x):
    hi, mid, lo = _split3(x)
    return _dot(sel, hi) + _dot(sel, mid) + _dot(sel, lo)


def _softplus(x):
    return jnp.maximum(x, 0.0) + jnp.log(1.0 + jnp.exp(-jnp.abs(x)))


def _sigmoid(x):
    return 1.0 / (1.0 + jnp.exp(-x))


def _silu(x):
    return x * _sigmoid(x)


def _gelu_tanh(x):
    return 0.5 * x * (1.0 + jnp.tanh(0.7978845608028654 * (x + 0.044715 * x * x * x)))


def _blockdiag(x, mask):
    xb = x.astype(BF16)
    return jnp.concatenate([xb] * HEADS, axis=0) * mask


def _head_mean_sq(x, bones):
    return _dot((x * x).astype(BF16), bones) * (1.0 / HEAD_DIM)


def _inproj_kernel(x_ref, nw_ref, w_ref, pa_ref, pb_ref, pc_ref, pd_ref, pg_ref):
    x = x_ref[...]
    ms = jnp.mean(x * x, axis=-1, keepdims=True)
    h = (x * lax.rsqrt(ms + EPS) * nw_ref[...]).astype(BF16)
    col = 0
    for ref in (pa_ref, pb_ref, pc_ref, pd_ref, pg_ref):
        n = ref.shape[1]
        ref[...] = _dot(h, w_ref[:, col:col + n]).astype(ref.dtype)
        col += n


def _inproj(x, nw, w):
    m = x.shape[0]
    widths = (2 * GROUP, 3 * GROUP, 4 * GROUP, 4 * GROUP, GATE_COLS)
    dtypes = (BF16, BF16, BF16, BF16, F32)
    row = lambda i: (i, 0)
    fixed = lambda i: (0, 0)
    return pl.pallas_call(
        _inproj_kernel,
        grid=(m // ROW_TILE,),
        in_specs=[pl.BlockSpec((ROW_TILE, D_MODEL), row),
                  pl.BlockSpec((1, D_MODEL), fixed),
                  pl.BlockSpec(w.shape, fixed, pipeline_mode=pl.Buffered(1))],
        out_specs=[pl.BlockSpec((ROW_TILE, n), row) for n in widths],
        out_shape=[jax.ShapeDtypeStruct((m, n), dt) for n, dt in zip(widths, dtypes)],
        compiler_params=pltpu.CompilerParams(dimension_semantics=("arbitrary",),
                                             vmem_limit_bytes=VMEM_LIMIT),
        name="in_proj",
    )(x, nw, w)


def _sgu_sconv_kernel(seq_tiles, pa_ref, pb_ref, lnw_ref, lnb_ref, ws_ref, bs_ref, cw_ref,
                      ya_ref, yb_ref, ext_ref):
    tm = pa_ref.shape[0]
    tri = (lax.broadcasted_iota(jnp.int32, (SGU_CHUNK, SGU_CHUNK), 0)
           >= lax.broadcasted_iota(jnp.int32, (SGU_CHUNK, SGU_CHUNK), 1))
    head_of_lane = lax.broadcasted_iota(jnp.int32, (SGU_CHUNK, GROUP), 1) // HEAD_DIM
    ws = [jnp.where(tri, ws_ref[h], 0.0).astype(BF16) for h in range(HEADS)]
    for c in range(tm // SGU_CHUNK):
        rows = slice(c * SGU_CHUNK, (c + 1) * SGU_CHUNK)
        u = _gelu_tanh(pa_ref[rows, 0:GROUP].astype(F32))
        v = _gelu_tanh(pa_ref[rows, GROUP:2 * GROUP].astype(F32))
        mu = jnp.mean(v, axis=-1, keepdims=True)
        vc = v - mu
        var = jnp.mean(vc * vc, axis=-1, keepdims=True)
        vn = vc * lax.rsqrt(var + EPS) * lnw_ref[...] + lnb_ref[...]
        mixed = bs_ref[...]
        for h in range(HEADS):
            vh = jnp.where(head_of_lane == h, vn, 0.0).astype(BF16)
            mixed = mixed + _dot(ws[h], vh)
        ya_ref[rows, :] = (u * mixed).astype(ya_ref.dtype)

    @pl.when(lax.rem(pl.program_id(0), seq_tiles) == 0)
    def _():
        ext_ref[0:TAIL, :] = jnp.zeros((TAIL, GROUP), F32)

    gate_b = pb_ref[:, 0:GROUP].astype(F32)
    s = pb_ref[:, GROUP:2 * GROUP].astype(F32) * pb_ref[:, 2 * GROUP:3 * GROUP].astype(F32)
    ext_ref[TAIL:TAIL + tm, :] = s
    conv = s * cw_ref[SC_WIDTH - 1:SC_WIDTH, :]
    for j in range(1, SC_WIDTH):
        conv = conv + ext_ref[TAIL - j:TAIL - j + tm, :] * cw_ref[SC_WIDTH - 1 - j:SC_WIDTH - j, :]
    yb_ref[...] = (gate_b * conv).astype(yb_ref.dtype)
    ext_ref[0:TAIL, :] = s[tm - TAIL:tm, :]


def _sgu_sconv(pa, pb, lnw, lnb, ws, bs_wide, cw, seq_len):
    m = pa.shape[0]
    row = lambda i: (i, 0)
    fixed2 = lambda i: (0, 0)
    fixed3 = lambda i: (0, 0, 0)
    return pl.pallas_call(
        functools.partial(_sgu_sconv_kernel, seq_len // ROW_TILE),
        grid=(m // ROW_TILE,),
        in_specs=[pl.BlockSpec((ROW_TILE, 2 * GROUP), row),
                  pl.BlockSpec((ROW_TILE, 3 * GROUP), row),
                  pl.BlockSpec((1, GROUP), fixed2),
                  pl.BlockSpec((1, GROUP), fixed2),
                  pl.BlockSpec(ws.shape, fixed3),
                  pl.BlockSpec(bs_wide.shape, fixed2),
                  pl.BlockSpec(cw.shape, fixed2)],
        out_specs=[pl.BlockSpec((ROW_TILE, GROUP), row), pl.BlockSpec((ROW_TILE, GROUP), row)],
        out_shape=[jax.ShapeDtypeStruct((m, GROUP), BF16), jax.ShapeDtypeStruct((m, GROUP), BF16)],
        scratch_shapes=[pltpu.VMEM((TAIL + ROW_TILE, GROUP), F32)],
        compiler_params=pltpu.CompilerParams(dimension_semantics=("arbitrary",),
                                             vmem_limit_bytes=VMEM_LIMIT),
        name="sgu_sconv",
    )(pa, pb, lnw, lnb, ws, bs_wide, cw)


def _chunk_masks():
    i = lax.broadcasted_iota(jnp.int32, (CHUNK, GROUP), 0)
    j = lax.broadcasted_iota(jnp.int32, (CHUNK, GROUP), 1) % CHUNK
    return i == j, i >= j, i > j


def _const_inputs():
    lane_head = np.arange(GROUP) // HEAD_DIM
    bd = (lane_head[:, None] == lane_head[None, :]).astype(np.float32)
    ltri = np.tril(np.ones((CHUNK, CHUNK), np.float32))
    return jnp.asarray(bd, BF16), jnp.asarray(ltri, BF16)


def _rows(ci):
    start = ci * CHUNK
    if not isinstance(start, int):
        start = pl.multiple_of(start, CHUNK)
    return pl.ds(start, CHUNK)


def _for_groups(n, par, fn):
    if n == par:
        for p in range(par):
            fn(p)
        return

    def body(g, carry):
        for p in range(par):
            fn(g * par + p)
        return carry

    lax.fori_loop(0, n // par, body, 0)


def _gdn_kernel(pc_ref, pg_ref, cw_ref, alog_ref, dtb_ref, nw_ref, ea_ref, eb_ref, bd_ref, ltri_ref, y_ref,
                ext_ref, q_ref, k_ref, v_ref, g_ref, b_ref, ac_ref, bb_ref, d_ref, dec_ref, o_ref, s_ref):
    nb, tm = pc_ref.shape[0], pc_ref.shape[1]
    width = 3 * GROUP
    chunks_per_seq = tm // CHUNK

    @pl.when(pl.program_id(0) == 0)
    def _():
        ext_ref[:, 0:TAIL, :] = jnp.zeros((nb, TAIL, width), F32)
        s_ref[...] = jnp.zeros((nb, GROUP, GROUP), F32)

    for b in range(nb):
        rows = slice(b * tm, (b + 1) * tm)
        bd = bd_ref[...]
        x = pc_ref[b, :, 0:width].astype(F32)
        ext_ref[b, TAIL:TAIL + tm, :] = x
        conv = x * cw_ref[DN_CONV_WIDTH - 1:DN_CONV_WIDTH, :]
        for j in range(1, DN_CONV_WIDTH):
            conv = conv + (ext_ref[b, TAIL - j:TAIL - j + tm, :]
                           * cw_ref[DN_CONV_WIDTH - 1 - j:DN_CONV_WIDTH - j, :])
        ext_ref[b, 0:TAIL, :] = x[tm - TAIL:tm, :]
        qkv = _silu(conv)
        q = qkv[:, 0:GROUP]
        k = qkv[:, GROUP:2 * GROUP]
        q_ref[rows, :] = q * lax.rsqrt(_dot((q * q).astype(BF16), bd) + EPS) * (HEAD_DIM ** -0.5)
        k_ref[rows, :] = k * lax.rsqrt(_dot((k * k).astype(BF16), bd) + EPS)
        v_ref[rows, :] = qkv[:, 2 * GROUP:3 * GROUP]
        pg = pg_ref[b]
        a_w = _dot_sel_rhs(pg, ea_ref[...])
        b_w = _dot_sel_rhs(pg, eb_ref[...])
        g_ref[rows, :] = -jnp.exp(alog_ref[...]) * _softplus(a_w + dtb_ref[...])
        b_ref[rows, :] = _sigmoid(b_w)

    def prepare(cis):
        eye_w, causal_w, strict_w = _chunk_masks()
        eye_f = eye_w.astype(F32)
        ltri = ltri_ref[...]
        bd = bd_ref[...]
        bdf = bd.astype(F32)
        n = len(cis)
        rows = [_rows(ci) for ci in cis]
        qn = [q_ref[r, :] for r in rows]
        kn = [k_ref[r, :] for r in rows]
        gc = [_dot_sel_lhs(ltri, g_ref[r, :]) for r in rows]
        eg = [jnp.exp(x) for x in gc]
        kb = [kn[i] * b_ref[rows[i], :] for i in range(n)]
        r1 = [_dot_nt(jnp.concatenate([kb[i], qn[i]], axis=0).astype(BF16), _blockdiag(kn[i], bd))
              for i in range(n)]
        low, attn = [], []
        for i in range(n):
            g_row = jnp.sum(jnp.where(eye_w, gc[i], 0.0), axis=0, keepdims=True)
            decay = jnp.where(causal_w, jnp.exp(jnp.minimum(gc[i] - g_row, 0.0)), 0.0)
            low.append(jnp.where(strict_w, r1[i][0:CHUNK] * decay, 0.0))
            attn.append((r1[i][CHUNK:2 * CHUNK] * decay).astype(BF16))
        t = [eye_f - x for x in low]
        p = [_dot(x.astype(BF16), _blockdiag(x, bd)) for x in low]
        n_levels = CHUNK.bit_length() - 1
        for level in range(1, n_levels):
            if level < n_levels - 1:
                r2 = [_dot(jnp.concatenate([p[i], t[i]], axis=0).astype(BF16), _blockdiag(p[i], bd))
                      for i in range(n)]
                p = [x[0:CHUNK] for x in r2]
                t = [t[i] + r2[i][CHUNK:2 * CHUNK] for i in range(n)]
            else:
                t = [t[i] + _dot(t[i].astype(BF16), _blockdiag(p[i], bd)) for i in range(n)]
        wu = [_dot(t[i].astype(BF16),
                   jnp.concatenate([_blockdiag(kb[i] * eg[i], bd),
                                    _blockdiag(v_ref[rows[i], :] * b_ref[rows[i], :], bd)], axis=1))
              for i in range(n)]
        gc_last = [x[CHUNK - 1:CHUNK, :] for x in gc]
        kd_wu = [_dot_tn((kn[i] * jnp.exp(gc_last[i] - gc[i])).astype(BF16), wu[i].astype(BF16))
                 for i in range(n)]
        at_wu = [_dot(attn[i], jnp.concatenate([_blockdiag(wu[i][:, 0:GROUP], bd),
                                                _blockdiag(wu[i][:, GROUP:2 * GROUP], bd)], axis=1))
                 for i in range(n)]
        for i, ci in enumerate(cis):
            ac_ref[ci, 0:GROUP, :] = (-kd_wu[i][:, 0:GROUP] * bdf).astype(BF16)
            bb_ref[ci] = kd_wu[i][:, GROUP:2 * GROUP] * bdf
            ac_ref[ci, GROUP:GROUP + CHUNK, :] = (qn[i] * eg[i] - at_wu[i][:, 0:GROUP]).astype(BF16)
            d_ref[rows[i], :] = at_wu[i][:, GROUP:2 * GROUP]
            dec_ref[ci] = jnp.broadcast_to(jnp.exp(gc_last[i]), (TAIL, GROUP))

    _for_groups(nb * chunks_per_seq // GDN_PAR, 1,
                lambda g: prepare([g * GDN_PAR + p for p in range(GDN_PAR)]))

    def scan(c):
        for b in range(nb):
            ci = b * chunks_per_seq + c
            s = s_ref[b]
            r = _dot(ac_ref[ci], s.astype(BF16))
            s_ref[b] = s * dec_ref[ci, 0:1, :] + r[0:GROUP] + bb_ref[ci]
            rows = _rows(ci)
            o_ref[rows, :] = r[GROUP:GROUP + CHUNK] + d_ref[rows, :]

    _for_groups(chunks_per_seq, SCAN_PAR, scan)

    for b in range(nb):
        bd = bd_ref[...]
        o = o_ref[b * tm:(b + 1) * tm, :]
        z = pc_ref[b, :, width:width + GROUP].astype(F32)
        y = o * lax.rsqrt(_head_mean_sq(o, bd) + EPS) * nw_ref[...] * _silu(z)
        y_ref[b] = y.astype(y_ref.dtype)


def _gdn(pc, pg, cw, alog_w, dtb_w, nw_w, ea, eb, bd, ltri):
    nb, seq, _ = pc.shape
    tm = SEQ_TILE
    n_chunks = nb * tm // CHUNK
    row = lambda i: (0, i, 0)
    fixed = lambda i: (0, 0)
    flat = lambda: pltpu.VMEM((nb * tm, GROUP), F32)
    return pl.pallas_call(
        _gdn_kernel,
        grid=(seq // tm,),
        in_specs=[pl.BlockSpec((nb, tm, 4 * GROUP), row),
                  pl.BlockSpec((nb, tm, GATE_COLS), row),
                  pl.BlockSpec(cw.shape, fixed),
                  pl.BlockSpec((1, GROUP), fixed),
                  pl.BlockSpec((1, GROUP), fixed),
                  pl.BlockSpec((1, GROUP), fixed),
                  pl.BlockSpec(ea.shape, fixed),
                  pl.BlockSpec(eb.shape, fixed),
                  pl.BlockSpec(bd.shape, fixed),
                  pl.BlockSpec(ltri.shape, fixed)],
        out_specs=pl.BlockSpec((nb, tm, GROUP), row),
        out_shape=jax.ShapeDtypeStruct((nb, seq, GROUP), BF16),
        scratch_shapes=[pltpu.VMEM((nb, TAIL + tm, 3 * GROUP), F32),
                        flat(), flat(), flat(), flat(), flat(),
                        pltpu.VMEM((n_chunks, GROUP + CHUNK, GROUP), BF16),
                        pltpu.VMEM((n_chunks, GROUP, GROUP), F32),
                        flat(),
                        pltpu.VMEM((n_chunks, TAIL, GROUP), F32),
                        flat(),
                        pltpu.VMEM((nb, GROUP, GROUP), F32)],
        compiler_params=pltpu.CompilerParams(dimension_semantics=("arbitrary",),
                                             vmem_limit_bytes=VMEM_LIMIT),
        name="gdn",
    )(pc, pg, cw, alog_w, dtb_w, nw_w, ea, eb, bd, ltri)


def _gla_kernel(pd_ref, pg_ref, wg_ref, gb_ref, nw_ref, bd_ref, ltri_ref, y_ref,
                la_ref, o_ref, st_ref):
    nb, tm = pd_ref.shape[0], pd_ref.shape[1]
    chunks_per_seq = tm // CHUNK

    @pl.when(pl.program_id(0) == 0)
    def _():
        st_ref[...] = jnp.zeros((nb, GROUP, GROUP), F32)

    for b in range(nb):
        pre = _dot(pg_ref[b].astype(BF16), wg_ref[...]) + gb_ref[...]
        la_ref[b * tm:(b + 1) * tm, :] = -_softplus(-pre) * (1.0 / GATE_TEMP)

    mid = CHUNK // 2

    def chunk_group(g):
        _, causal_w, _ = _chunk_masks()
        ltri = ltri_ref[...]
        bd = bd_ref[...]
        bdf = bd.astype(F32)
        items = [(b, g * GLA_PAR + p) for p in range(GLA_PAR) for b in range(nb)]
        n = len(items)
        rows_in = [_rows(c) for _, c in items]
        rows = [_rows(b * chunks_per_seq + c) for b, c in items]
        q = [pd_ref[items[i][0], rows_in[i], 0:GROUP].astype(F32) * (HEAD_DIM ** -0.5) for i in range(n)]
        k = [pd_ref[items[i][0], rows_in[i], GROUP:2 * GROUP].astype(F32) for i in range(n)]
        v = [pd_ref[items[i][0], rows_in[i], 2 * GROUP:3 * GROUP] for i in range(n)]
        gcum = [_dot_sel_lhs(ltri, la_ref[r, :]) for r in rows]
        g_mid = [x[mid:mid + 1, :] for x in gcum]
        g_last = [x[CHUNK - 1:CHUNK, :] for x in gcum]
        attn = [jnp.where(causal_w,
                          _dot_nt((q[i] * jnp.exp(gcum[i] - g_mid[i])).astype(BF16),
                                  _blockdiag(k[i] * jnp.exp(g_mid[i] - gcum[i]), bd)), 0.0).astype(BF16)
                for i in range(n)]
        upd = [bdf * _dot_tn(v[i], (k[i] * jnp.exp(g_last[i] - gcum[i])).astype(BF16)) for i in range(n)]
        o_intra = [_dot(attn[i], _blockdiag(v[i], bd)) for i in range(n)]
        qg = [(q[i] * jnp.exp(gcum[i])).astype(BF16) for i in range(n)]
        for i, (b, _) in enumerate(items):
            st = st_ref[b]
            o_ref[rows[i], :] = o_intra[i] + _dot_nt(qg[i], st.astype(BF16))
            st_ref[b] = st * jnp.exp(g_last[i]) + upd[i]

    _for_groups(chunks_per_seq // GLA_PAR, 1, chunk_group)

    for b in range(nb):
        bd = bd_ref[...]
        o = o_ref[b * tm:(b + 1) * tm, :]
        z = pd_ref[b, :, 3 * GROUP:4 * GROUP].astype(F32)
        y = o * lax.rsqrt(_head_mean_sq(o, bd) + EPS) * nw_ref[...] * _silu(z)
        y_ref[b] = y.astype(y_ref.dtype)


def _gla(pd, pg, wg_pad, gb, nw_w, bd, ltri):
    nb, seq, _ = pd.shape
    tm = SEQ_TILE
    row = lambda i: (0, i, 0)
    fixed = lambda i: (0, 0)
    return pl.pallas_call(
        _gla_kernel,
        grid=(seq // tm,),
        in_specs=[pl.BlockSpec((nb, tm, 4 * GROUP), row),
                  pl.BlockSpec((nb, tm, GATE_COLS), row),
                  pl.BlockSpec(wg_pad.shape, fixed),
                  pl.BlockSpec((1, GROUP), fixed),
                  pl.BlockSpec((1, GROUP), fixed),
                  pl.BlockSpec(bd.shape, fixed),
                  pl.BlockSpec(ltri.shape, fixed)],
        out_specs=pl.BlockSpec((nb, tm, GROUP), row),
        out_shape=jax.ShapeDtypeStruct((nb, seq, GROUP), BF16),
        scratch_shapes=[pltpu.VMEM((nb * tm, GROUP), F32),
                        pltpu.VMEM((nb * tm, GROUP), F32),
                        pltpu.VMEM((nb, GROUP, GROUP), F32)],
        compiler_params=pltpu.CompilerParams(dimension_semantics=("arbitrary",),
                                             vmem_limit_bytes=VMEM_LIMIT),
        name="gla",
    )(pd, pg, wg_pad, gb, nw_w, bd, ltri)


def _outproj_ffn_kernel(final, x_ref, ya_ref, yb_ref, yc_ref, yd_ref, wo_ref, n2_ref, wgu_ref, wd_ref,
                        fn_ref, o_ref, act_ref):
    x1 = x_ref[...]
    for idx, y_ref in enumerate((ya_ref, yb_ref, yc_ref, yd_ref)):
        x1 = x1 + _dot(y_ref[...], wo_ref[idx * GROUP:(idx + 1) * GROUP, :])
    ms = jnp.mean(x1 * x1, axis=-1, keepdims=True)
    h = (x1 * lax.rsqrt(ms + EPS) * n2_ref[...]).astype(BF16)
    for j in range(D_FF // FF_CHUNK):
        cols = slice(j * FF_CHUNK, (j + 1) * FF_CHUNK)
        gate = _dot(h, wgu_ref[:, cols])
        up = _dot(h, wgu_ref[:, D_FF + j * FF_CHUNK:D_FF + (j + 1) * FF_CHUNK])
        act_ref[:, cols] = (_silu(gate) * up).astype(BF16)
    x2 = x1 + _dot(act_ref[...], wd_ref[...])
    if final:
        ms2 = jnp.mean(x2 * x2, axis=-1, keepdims=True)
        x2 = x2 * lax.rsqrt(ms2 + EPS) * fn_ref[...]
    o_ref[...] = x2


def _outproj_ffn(x, ys, wo, n2, wgu, wd, fn, final):
    m = x.shape[0]
    row = lambda i: (i, 0)
    fixed = lambda i: (0, 0)
    resident = lambda a: pl.BlockSpec(a.shape, fixed, pipeline_mode=pl.Buffered(1))
    return pl.pallas_call(
        functools.partial(_outproj_ffn_kernel, final),
        grid=(m // ROW_TILE,),
        in_specs=[pl.BlockSpec((ROW_TILE, D_MODEL), row)]
                 + [pl.BlockSpec((ROW_TILE, GROUP), row)] * 4
                 + [resident(wo), pl.BlockSpec((1, D_MODEL), fixed), resident(wgu), resident(wd),
                    pl.BlockSpec((1, D_MODEL), fixed)],
        out_specs=pl.BlockSpec((ROW_TILE, D_MODEL), row),
        out_shape=jax.ShapeDtypeStruct((m, D_MODEL), F32),
        scratch_shapes=[pltpu.VMEM((ROW_TILE, D_FF), BF16)],
        compiler_params=pltpu.CompilerParams(dimension_semantics=("arbitrary",),
                                             vmem_limit_bytes=VMEM_LIMIT),
        name="outproj_ffn",
    )(x, *ys, wo, n2, wgu, wd, fn)


def _relayout_w_in(w):
    g = GROUP
    c0 = 5 * g
    c_a = c0 + 3 * g
    c_z = c_a + 2 * HEADS
    d0 = c_z + g
    d_g = d0 + 3 * g
    d_z = d_g + GATE_RANK
    pad = jnp.zeros((w.shape[0], GATE_COLS - 2 * HEADS - GATE_RANK), w.dtype)
    return jnp.concatenate(
        [w[:, 0:c0], w[:, c0:c_a], w[:, c_z:d0], w[:, d0:d_g], w[:, d_z:d_z + g],
         w[:, c_a:c_z], w[:, d_g:d_z], pad], axis=1).astype(BF16)


def _wide(v):
    return jnp.repeat(v.astype(F32), HEAD_DIM)[None, :]


def _gate_selectors():
    ea = np.zeros((GATE_COLS, GROUP), np.float32)
    eb = np.zeros((GATE_COLS, GROUP), np.float32)
    for h in range(HEADS):
        ea[h, h * HEAD_DIM:(h + 1) * HEAD_DIM] = 1.0
        eb[HEADS + h, h * HEAD_DIM:(h + 1) * HEAD_DIM] = 1.0
    return jnp.asarray(ea, BF16), jnp.asarray(eb, BF16)


def kernel(x, norm1_w, w_in, sgu_ln_w, sgu_ln_b, sgu_w_spatial, sgu_b_spatial, sc_conv_w, dn_conv_w, dn_a_log, dn_dt_bias, dn_norm_w, gla_w_gate2, gla_gate_bias, gla_norm_w, w_out, norm2_w, w_gate_up, w_down, final_norm_w):
    bsz, seq, d = x.shape
    depth = w_in.shape[0]
    assert seq % ROW_TILE == 0 and seq % SEQ_TILE == 0 and d == D_MODEL
    xf = x.reshape(bsz * seq, d)
    bd, ltri = _const_inputs()
    ea, eb = _gate_selectors()
    for l in range(depth):
        pa, pb, pc, pd, pg = _inproj(xf, norm1_w[l][None, :], _relayout_w_in(w_in[l]))
        bs_wide = jnp.repeat(sgu_b_spatial[l].T, HEAD_DIM, axis=1)
        ya, yb = _sgu_sconv(pa, pb, sgu_ln_w[l][None, :], sgu_ln_b[l][None, :], sgu_w_spatial[l],
                            bs_wide, sc_conv_w[l], seq)
        pg3 = pg.reshape(bsz, seq, GATE_COLS)
        yc = _gdn(pc.reshape(bsz, seq, 4 * GROUP), pg3, dn_conv_w[l], _wide(dn_a_log[l]),
                  _wide(dn_dt_bias[l]), jnp.tile(dn_norm_w[l], HEADS)[None, :], ea, eb, bd, ltri)
        wg_pad = jnp.zeros((GATE_COLS, GROUP), BF16).at[2 * HEADS:2 * HEADS + GATE_RANK].set(
            gla_w_gate2[l].astype(BF16))
        yd = _gla(pd.reshape(bsz, seq, 4 * GROUP), pg3, wg_pad, gla_gate_bias[l][None, :],
                  jnp.tile(gla_norm_w[l], HEADS)[None, :], bd, ltri)
        xf = _outproj_ffn(xf, (ya, yb, yc.reshape(bsz * seq, GROUP), yd.reshape(bsz * seq, GROUP)),
                          w_out[l].astype(BF16), norm2_w[l][None, :],
                          w_gate_up[l].astype(BF16), w_down[l].astype(BF16), final_norm_w[None, :],
                          final=(l == depth - 1))
    return xf.reshape(bsz, seq, d)
```

```python
import functools

import numpy as np
import jax
import jax.numpy as jnp
from jax import lax
from jax.experimental import pallas as pl
from jax.experimental.pallas import tpu as pltpu

F32 = jnp.float32
BF16 = jnp.bfloat16

D_MODEL = 1024
GROUP = 256
HEADS = 4
HEAD_DIM = 64
SGU_CHUNK = 128
SC_WIDTH = 3
DN_CONV_WIDTH = 4
CHUNK = 64
GATE_RANK = 16
GATE_TEMP = 16.0
D_FF = 2816
EPS = 1e-6
GATE_COLS = 128
TAIL = 8

ROW_TILE = 512
SEQ_TILE = 256
FF_CHUNK = 256
VMEM_LIMIT = 56 * 1024 * 1024

COL_A = 0
COL_B = COL_A + 2 * GROUP
COL_C = COL_B + 3 * GROUP
COL_D = COL_C + 4 * GROUP
COL_G = COL_D + 4 * GROUP
IN_COLS_PADDED = COL_G + GATE_COLS
C_PACK = 5 * GROUP
D_PACK = 4 * GROUP


def _dot(a, b):
    return jnp.dot(a, b, preferred_element_type=F32)


def _dot_nt(a, b):
    return lax.dot_general(a, b, (((1,), (1,)), ((), ())), preferred_element_type=F32)


def _dot_tn(a, b):
    return lax.dot_general(a, b, (((0,), (0,)), ((), ())), preferred_element_type=F32)


def _split3(x):
    hi = x.astype(BF16)
    r = x - hi.astype(F32)
    mid = r.astype(BF16)
    lo = (r - mid.astype(F32)).astype(BF16)
    return hi, mid, lo


def _dot_sel_rhs(x, sel):
    hi, mid, lo = _split3(x)
    return _dot(hi, sel) + _dot(mid, sel) + _dot(lo, sel)


def _dot_sel_lhs(sel, x):
    hi, mid, lo = _split3(x)
    return _dot(sel, hi) + _dot(sel, mid) + _dot(sel, lo)


def _softplus(x):
    return jnp.maximum(x, 0.0) + jnp.log(1.0 + jnp.exp(-jnp.abs(x)))


def _sigmoid(x):
    return 1.0 / (1.0 + jnp.exp(-x))


def _silu(x):
    return x * _sigmoid(x)


def _gelu_tanh(x):
    return 0.5 * x * (1.0 + jnp.tanh(0.7978845608028654 * (x + 0.044715 * x * x * x)))


def _blockdiag(x, mask):
    xb = x.astype(BF16)
    return jnp.concatenate([xb] * HEADS, axis=0) * mask


def _head_mean_sq(x, bones):
    return _dot((x * x).astype(BF16), bones) * (1.0 / HEAD_DIM)


def _causal_conv(x, ext_ref, cw_ref, width):
    tm = x.shape[0]
    ext_ref[TAIL:TAIL + tm, :] = x
    acc = x * cw_ref[width - 1:width, :]
    for j in range(1, width):
        acc = acc + ext_ref[TAIL - j:TAIL - j + tm, :] * cw_ref[width - 1 - j:width - j, :]
    ext_ref[0:TAIL, :] = x[tm - TAIL:tm, :]
    return acc


def _inproj_kernel(seq_tiles, x_ref, nw_ref, w_ref, lnw_ref, lnb_ref, ws_ref, bs_ref, scw_ref,
                   dcw_ref, alog_ref, dtb_ref, ea_ref, eb_ref, bd_ref, wg_ref, gb_ref,
                   ya_ref, yb_ref, cpk_ref, cg_ref, dpk_ref, dla_ref, extb_ref, extc_ref):
    tm = x_ref.shape[0]

    @pl.when(lax.rem(pl.program_id(0), seq_tiles) == 0)
    def _():
        extb_ref[0:TAIL, :] = jnp.zeros((TAIL, GROUP), F32)
        extc_ref[0:TAIL, :] = jnp.zeros((TAIL, 3 * GROUP), F32)

    x = x_ref[...]
    ms = jnp.mean(x * x, axis=-1, keepdims=True)
    h = (x * lax.rsqrt(ms + EPS) * nw_ref[...]).astype(BF16)
    bd = bd_ref[...]

    pa = _dot(h, w_ref[:, COL_A:COL_B])
    pc = _dot(h, w_ref[:, COL_C:COL_D])
    pg = _dot(h, w_ref[:, COL_G:COL_G + GATE_COLS])
    pb = _dot(h, w_ref[:, COL_B:COL_C])

    tri = (lax.broadcasted_iota(jnp.int32, (SGU_CHUNK, SGU_CHUNK), 0)
           >= lax.broadcasted_iota(jnp.int32, (SGU_CHUNK, SGU_CHUNK), 1))
    head_of_lane = lax.broadcasted_iota(jnp.int32, (SGU_CHUNK, GROUP), 1) // HEAD_DIM
    ws = [jnp.where(tri, ws_ref[hd], 0.0).astype(BF16) for hd in range(HEADS)]
    for c in range(tm // SGU_CHUNK):
        rows = slice(c * SGU_CHUNK, (c + 1) * SGU_CHUNK)
        u = _gelu_tanh(pa[rows, 0:GROUP])
        v = _gelu_tanh(pa[rows, GROUP:2 * GROUP])
        mu = jnp.mean(v, axis=-1, keepdims=True)
        vc = v - mu
        var = jnp.mean(vc * vc, axis=-1, keepdims=True)
        vn = vc * lax.rsqrt(var + EPS) * lnw_ref[...] + lnb_ref[...]
        mixed = bs_ref[...]
        for hd in range(HEADS):
            mixed = mixed + _dot(ws[hd], jnp.where(head_of_lane == hd, vn, 0.0).astype(BF16))
        ya_ref[rows, :] = (u * mixed).astype(ya_ref.dtype)

    pd = _dot(h, w_ref[:, COL_D:COL_G])

    conv_b = _causal_conv(pb[:, GROUP:2 * GROUP] * pb[:, 2 * GROUP:3 * GROUP], extb_ref, scw_ref, SC_WIDTH)
    yb_ref[...] = (pb[:, 0:GROUP] * conv_b).astype(yb_ref.dtype)

    qkv = _silu(_causal_conv(pc[:, 0:3 * GROUP], extc_ref, dcw_ref, DN_CONV_WIDTH))
    q = qkv[:, 0:GROUP]
    k = qkv[:, GROUP:2 * GROUP]
    qn = q * lax.rsqrt(_dot((q * q).astype(BF16), bd) + EPS) * (HEAD_DIM ** -0.5)
    kn = k * lax.rsqrt(_dot((k * k).astype(BF16), bd) + EPS)
    g_narrow = -jnp.exp(alog_ref[...]) * _softplus(pg + dtb_ref[...])
    cg_ref[...] = _dot_sel_rhs(g_narrow, ea_ref[...])
    beta = _dot(_sigmoid(pg).astype(BF16), eb_ref[...])
    cpk_ref[:, 0:GROUP] = qn.astype(BF16)
    cpk_ref[:, GROUP:2 * GROUP] = kn.astype(BF16)
    cpk_ref[:, 2 * GROUP:3 * GROUP] = qkv[:, 2 * GROUP:3 * GROUP].astype(BF16)
    cpk_ref[:, 3 * GROUP:4 * GROUP] = beta.astype(BF16)
    cpk_ref[:, 4 * GROUP:5 * GROUP] = _silu(pc[:, 3 * GROUP:4 * GROUP]).astype(BF16)

    pre = _dot(pg.astype(BF16), wg_ref[...]) + gb_ref[...]
    dla_ref[...] = -_softplus(-pre) * (1.0 / GATE_TEMP)
    dpk_ref[:, 0:3 * GROUP] = pd[:, 0:3 * GROUP].astype(BF16)
    dpk_ref[:, 3 * GROUP:4 * GROUP] = _silu(pd[:, 3 * GROUP:4 * GROUP]).astype(BF16)


def _inproj(x, nw, w, lnw, lnb, ws, bs_wide, scw, dcw, alog_n, dtb_n, ea, eb, bd, wg_pad, gb, seq_len):
    m = x.shape[0]
    row = lambda i: (i, 0)
    fixed = lambda i: (0, 0)
    whole = lambda a: pl.BlockSpec(a.shape, (lambda i: (0,) * a.ndim))
    outs = ((GROUP, BF16), (GROUP, BF16), (C_PACK, BF16), (GROUP, F32), (D_PACK, BF16), (GROUP, F32))
    return pl.pallas_call(
        functools.partial(_inproj_kernel, seq_len // ROW_TILE),
        grid=(m // ROW_TILE,),
        in_specs=[pl.BlockSpec((ROW_TILE, D_MODEL), row),
                  whole(nw),
                  pl.BlockSpec(w.shape, fixed, pipeline_mode=pl.Buffered(1)),
                  whole(lnw), whole(lnb), whole(ws), whole(bs_wide), whole(scw),
                  whole(dcw), whole(alog_n), whole(dtb_n), whole(ea), whole(eb), whole(bd),
                  whole(wg_pad), whole(gb)],
        out_specs=[pl.BlockSpec((ROW_TILE, n), row) for n, _ in outs],
        out_shape=[jax.ShapeDtypeStruct((m, n), dt) for n, dt in outs],
        scratch_shapes=[pltpu.VMEM((TAIL + ROW_TILE, GROUP), F32),
                        pltpu.VMEM((TAIL + ROW_TILE, 3 * GROUP), F32)],
        compiler_params=pltpu.CompilerParams(dimension_semantics=("arbitrary",),
                                             vmem_limit_bytes=VMEM_LIMIT),
        name="in_proj",
    )(x, nw, w, lnw, lnb, ws, bs_wide, scw, dcw, alog_n, dtb_n, ea, eb, bd, wg_pad, gb)


def _chunk_masks():
    i = lax.broadcasted_iota(jnp.int32, (CHUNK, GROUP), 0)
    j = lax.broadcasted_iota(jnp.int32, (CHUNK, GROUP), 1) % CHUNK
    return i == j, i >= j, i > j


def _const_inputs():
    lane_head = np.arange(GROUP) // HEAD_DIM
    bd = (lane_head[:, None] == lane_head[None, :]).astype(np.float32)
    ltri = np.tril(np.ones((CHUNK, CHUNK), np.float32))
    return jnp.asarray(bd, BF16), jnp.asarray(ltri, BF16)


def _chunk_rows(c):
    return slice(c * CHUNK, (c + 1) * CHUNK)


def _gdn_kernel(cpk_ref, cg_ref, nw_ref, bd_ref, ltri_ref, y_ref, ac_ref, bb_ref, d_ref, dec_ref, o_ref, s_ref):
    nb, tm = cpk_ref.shape[0], cpk_ref.shape[1]
    cps = tm // CHUNK

    @pl.when(pl.program_id(0) == 0)
    def _():
        s_ref[...] = jnp.zeros((nb, GROUP, GROUP), F32)

    eye_w, causal_w, strict_w = _chunk_masks()
    eye_f = eye_w.astype(F32)
    ltri = ltri_ref[...]
    bd = bd_ref[...]
    bdf = bd.astype(F32)

    items = [(b, c) for b in range(nb) for c in range(cps)]
    n = len(items)

    def load(col, i):
        b, c = items[i]
        return cpk_ref[b, _chunk_rows(c), col * GROUP:(col + 1) * GROUP].astype(F32)

    qn = [load(0, i) for i in range(n)]
    kn = [load(1, i) for i in range(n)]
    beta = [load(3, i) for i in range(n)]
    gc = [_dot_sel_lhs(ltri, cg_ref[b, _chunk_rows(c), :]) for b, c in items]
    eg = [jnp.exp(x) for x in gc]
    kb = [kn[i] * beta[i] for i in range(n)]
    r1 = [_dot_nt(jnp.concatenate([kb[i], qn[i]], axis=0).astype(BF16), _blockdiag(kn[i], bd))
          for i in range(n)]
    low, attn = [], []
    for i in range(n):
        g_row = jnp.sum(jnp.where(eye_w, gc[i], 0.0), axis=0, keepdims=True)
        decay = jnp.where(causal_w, jnp.exp(jnp.minimum(gc[i] - g_row, 0.0)), 0.0)
        low.append(jnp.where(strict_w, r1[i][0:CHUNK] * decay, 0.0))
        attn.append((r1[i][CHUNK:2 * CHUNK] * decay).astype(BF16))
    t = [eye_f - x for x in low]
    p = [_dot(x.astype(BF16), _blockdiag(x, bd)) for x in low]
    n_levels = CHUNK.bit_length() - 1
    for level in range(1, n_levels):
        if level < n_levels - 1:
            r2 = [_dot(jnp.concatenate([p[i], t[i]], axis=0).astype(BF16), _blockdiag(p[i], bd))
                  for i in range(n)]
            p = [x[0:CHUNK] for x in r2]
            t = [t[i] + r2[i][CHUNK:2 * CHUNK] for i in range(n)]
        else:
            t = [t[i] + _dot(t[i].astype(BF16), _blockdiag(p[i], bd)) for i in range(n)]
    wu = [_dot(t[i].astype(BF16),
               jnp.concatenate([_blockdiag(kb[i] * eg[i], bd), _blockdiag(load(2, i) * beta[i], bd)], axis=1))
          for i in range(n)]
    gc_last = [x[CHUNK - 1:CHUNK, :] for x in gc]
    kd_wu = [_dot_tn((kn[i] * jnp.exp(gc_last[i] - gc[i])).astype(BF16), wu[i].astype(BF16))
             for i in range(n)]
    at_wu = [_dot(attn[i], jnp.concatenate([_blockdiag(wu[i][:, 0:GROUP], bd),
                                            _blockdiag(wu[i][:, GROUP:2 * GROUP], bd)], axis=1))
             for i in range(n)]
    for i in range(n):
        ac_ref[i, 0:GROUP, :] = (-kd_wu[i][:, 0:GROUP] * bdf).astype(BF16)
        bb_ref[i] = kd_wu[i][:, GROUP:2 * GROUP] * bdf
        ac_ref[i, GROUP:GROUP + CHUNK, :] = (qn[i] * eg[i] - at_wu[i][:, 0:GROUP]).astype(BF16)
        d_ref[i] = at_wu[i][:, GROUP:2 * GROUP]
        dec_ref[i] = jnp.broadcast_to(jnp.exp(gc_last[i]), (TAIL, GROUP))

    for c in range(cps):
        for b in range(nb):
            i = b * cps + c
            s = s_ref[b]
            r = _dot(ac_ref[i], s.astype(BF16))
            s_ref[b] = s * dec_ref[i, 0:1, :] + r[0:GROUP] + bb_ref[i]
            o_ref[b, _chunk_rows(c), :] = r[GROUP:GROUP + CHUNK] + d_ref[i]

    for b in range(nb):
        o = o_ref[b]
        zg = cpk_ref[b, :, 4 * GROUP:5 * GROUP].astype(F32)
        y_ref[b] = (o * lax.rsqrt(_head_mean_sq(o, bd) + EPS) * nw_ref[...] * zg).astype(y_ref.dtype)


def _gdn(cpk, cg, nw_w, bd, ltri):
    nb, seq, _ = cpk.shape
    tm = SEQ_TILE
    n_chunks = nb * tm // CHUNK
    row = lambda i: (0, i, 0)
    fixed = lambda i: (0, 0)
    return pl.pallas_call(
        _gdn_kernel,
        grid=(seq // tm,),
        in_specs=[pl.BlockSpec((nb, tm, C_PACK), row),
                  pl.BlockSpec((nb, tm, GROUP), row),
                  pl.BlockSpec((1, GROUP), fixed),
                  pl.BlockSpec(bd.shape, fixed),
                  pl.BlockSpec(ltri.shape, fixed)],
        out_specs=pl.BlockSpec((nb, tm, GROUP), row),
        out_shape=jax.ShapeDtypeStruct((nb, seq, GROUP), BF16),
        scratch_shapes=[pltpu.VMEM((n_chunks, GROUP + CHUNK, GROUP), BF16),
                        pltpu.VMEM((n_chunks, GROUP, GROUP), F32),
                        pltpu.VMEM((n_chunks, CHUNK, GROUP), F32),
                        pltpu.VMEM((n_chunks, TAIL, GROUP), F32),
                        pltpu.VMEM((nb, tm, GROUP), F32),
                        pltpu.VMEM((nb, GROUP, GROUP), F32)],
        compiler_params=pltpu.CompilerParams(dimension_semantics=("arbitrary",),
                                             vmem_limit_bytes=VMEM_LIMIT),
        name="gdn",
    )(cpk, cg, nw_w, bd, ltri)


def _gla_kernel(dpk_ref, dla_ref, nw_ref, bd_ref, ltri_ref, y_ref, o_ref, st_ref):
    nb, tm = dpk_ref.shape[0], dpk_ref.shape[1]
    cps = tm // CHUNK

    @pl.when(pl.program_id(0) == 0)
    def _():
        st_ref[...] = jnp.zeros((nb, GROUP, GROUP), F32)

    _, causal_w, _ = _chunk_masks()
    ltri = ltri_ref[...]
    bd = bd_ref[...]
    bdf = bd.astype(F32)
    mid = CHUNK // 2

    items = [(b, c) for c in range(cps) for b in range(nb)]
    n = len(items)
    q = [dpk_ref[b, _chunk_rows(c), 0:GROUP].astype(F32) * (HEAD_DIM ** -0.5) for b, c in items]
    k = [dpk_ref[b, _chunk_rows(c), GROUP:2 * GROUP].astype(F32) for b, c in items]
    v = [dpk_ref[b, _chunk_rows(c), 2 * GROUP:3 * GROUP] for b, c in items]
    gcum = [_dot_sel_lhs(ltri, dla_ref[b, _chunk_rows(c), :]) for b, c in items]
    g_mid = [x[mid:mid + 1, :] for x in gcum]
    g_last = [x[CHUNK - 1:CHUNK, :] for x in gcum]
    attn = [jnp.where(causal_w,
                      _dot_nt((q[i] * jnp.exp(gcum[i] - g_mid[i])).astype(BF16),
                              _blockdiag(k[i] * jnp.exp(g_mid[i] - gcum[i]), bd)), 0.0).astype(BF16)
            for i in range(n)]
    upd = [bdf * _dot_tn(v[i], (k[i] * jnp.exp(g_last[i] - gcum[i])).astype(BF16)) for i in range(n)]
    o_intra = [_dot(attn[i], _blockdiag(v[i], bd)) for i in range(n)]
    qg = [(q[i] * jnp.exp(gcum[i])).astype(BF16) for i in range(n)]
    for i, (b, c) in enumerate(items):
        st = st_ref[b]
        o_ref[b, _chunk_rows(c), :] = o_intra[i] + _dot_nt(qg[i], st.astype(BF16))
        st_ref[b] = st * jnp.exp(g_last[i]) + upd[i]

    for b in range(nb):
        o = o_ref[b]
        zg = dpk_ref[b, :, 3 * GROUP:4 * GROUP].astype(F32)
        y_ref[b] = (o * lax.rsqrt(_head_mean_sq(o, bd) + EPS) * nw_ref[...] * zg).astype(y_ref.dtype)


def _gla(dpk, dla, nw_w, bd, ltri):
    nb, seq, _ = dpk.shape
    tm = SEQ_TILE
    row = lambda i: (0, i, 0)
    fixed = lambda i: (0, 0)
    return pl.pallas_call(
        _gla_kernel,
        grid=(seq // tm,),
        in_specs=[pl.BlockSpec((nb, tm, D_PACK), row),
                  pl.BlockSpec((nb, tm, GROUP), row),
                  pl.BlockSpec((1, GROUP), fixed),
                  pl.BlockSpec(bd.shape, fixed),
                  pl.BlockSpec(ltri.shape, fixed)],
        out_specs=pl.BlockSpec((nb, tm, GROUP), row),
        out_shape=jax.ShapeDtypeStruct((nb, seq, GROUP), BF16),
        scratch_shapes=[pltpu.VMEM((nb, tm, GROUP), F32),
                        pltpu.VMEM((nb, GROUP, GROUP), F32)],
        compiler_params=pltpu.CompilerParams(dimension_semantics=("arbitrary",),
                                             vmem_limit_bytes=VMEM_LIMIT),
        name="gla",
    )(dpk, dla, nw_w, bd, ltri)


def _outproj_ffn_kernel(final, x_ref, ya_ref, yb_ref, yc_ref, yd_ref, wo_ref, n2_ref, wgu_ref, wd_ref,
                        fn_ref, o_ref, act_ref):
    x1 = x_ref[...]
    for idx, y_ref in enumerate((ya_ref, yb_ref, yc_ref, yd_ref)):
        x1 = x1 + _dot(y_ref[...], wo_ref[idx * GROUP:(idx + 1) * GROUP, :])
    ms = jnp.mean(x1 * x1, axis=-1, keepdims=True)
    h = (x1 * lax.rsqrt(ms + EPS) * n2_ref[...]).astype(BF16)
    for j in range(D_FF // FF_CHUNK):
        cols = slice(j * FF_CHUNK, (j + 1) * FF_CHUNK)
        gate = _dot(h, wgu_ref[:, cols])
        up = _dot(h, wgu_ref[:, D_FF + j * FF_CHUNK:D_FF + (j + 1) * FF_CHUNK])
        act_ref[:, cols] = (_silu(gate) * up).astype(BF16)
    x2 = x1 + _dot(act_ref[...], wd_ref[...])
    if final:
        ms2 = jnp.mean(x2 * x2, axis=-1, keepdims=True)
        x2 = x2 * lax.rsqrt(ms2 + EPS) * fn_ref[...]
    o_ref[...] = x2


def _outproj_ffn(x, ys, wo, n2, wgu, wd, fn, final):
    m = x.shape[0]
    row = lambda i: (i, 0)
    fixed = lambda i: (0, 0)
    resident = lambda a: pl.BlockSpec(a.shape, fixed, pipeline_mode=pl.Buffered(1))
    return pl.pallas_call(
        functools.partial(_outproj_ffn_kernel, final),
        grid=(m // ROW_TILE,),
        in_specs=[pl.BlockSpec((ROW_TILE, D_MODEL), row)]
                 + [pl.BlockSpec((ROW_TILE, GROUP), row)] * 4
                 + [resident(wo), pl.BlockSpec((1, D_MODEL), fixed), resident(wgu), resident(wd),
                    pl.BlockSpec((1, D_MODEL), fixed)],
        out_specs=pl.BlockSpec((ROW_TILE, D_MODEL), row),
        out_shape=jax.ShapeDtypeStruct((m, D_MODEL), F32),
        scratch_shapes=[pltpu.VMEM((ROW_TILE, D_FF), BF16)],
        compiler_params=pltpu.CompilerParams(dimension_semantics=("arbitrary",),
                                             vmem_limit_bytes=VMEM_LIMIT),
        name="outproj_ffn",
    )(x, *ys, wo, n2, wgu, wd, fn)


def _relayout_w_in(w):
    g = GROUP
    c0 = 5 * g
    c_a = c0 + 3 * g
    c_z = c_a + 2 * HEADS
    d0 = c_z + g
    d_g = d0 + 3 * g
    d_z = d_g + GATE_RANK
    pad = jnp.zeros((w.shape[0], GATE_COLS - 2 * HEADS - GATE_RANK), w.dtype)
    out = jnp.concatenate(
        [w[:, 0:c0], w[:, c0:c_a], w[:, c_z:d0], w[:, d0:d_g], w[:, d_z:d_z + g],
         w[:, c_a:c_z], w[:, d_g:d_z], pad], axis=1).astype(BF16)
    assert out.shape[1] == IN_COLS_PADDED
    return out


def _gate_vec(v):
    return jnp.zeros((1, GATE_COLS), F32).at[0, 0:HEADS].set(v.astype(F32))


def _gate_selectors():
    ea = np.zeros((GATE_COLS, GROUP), np.float32)
    eb = np.zeros((GATE_COLS, GROUP), np.float32)
    for h in range(HEADS):
        ea[h, h * HEAD_DIM:(h + 1) * HEAD_DIM] = 1.0
        eb[HEADS + h, h * HEAD_DIM:(h + 1) * HEAD_DIM] = 1.0
    return jnp.asarray(ea, BF16), jnp.asarray(eb, BF16)


def kernel(x, norm1_w, w_in, sgu_ln_w, sgu_ln_b, sgu_w_spatial, sgu_b_spatial, sc_conv_w, dn_conv_w, dn_a_log, dn_dt_bias, dn_norm_w, gla_w_gate2, gla_gate_bias, gla_norm_w, w_out, norm2_w, w_gate_up, w_down, final_norm_w):
    bsz, seq, d = x.shape
    depth = w_in.shape[0]
    assert seq % ROW_TILE == 0 and seq % SEQ_TILE == 0 and d == D_MODEL
    m = bsz * seq
    xf = x.reshape(m, d)
    bd, ltri = _const_inputs()
    ea, eb = _gate_selectors()
    for l in range(depth):
        bs_wide = jnp.repeat(sgu_b_spatial[l].T, HEAD_DIM, axis=1)
        wg_pad = jnp.zeros((GATE_COLS, GROUP), BF16).at[2 * HEADS:2 * HEADS + GATE_RANK].set(
            gla_w_gate2[l].astype(BF16))
        ya, yb, cpk, cg, dpk, dla = _inproj(
            xf, norm1_w[l][None, :], _relayout_w_in(w_in[l]),
            sgu_ln_w[l][None, :], sgu_ln_b[l][None, :], sgu_w_spatial[l], bs_wide, sc_conv_w[l],
            dn_conv_w[l], _gate_vec(dn_a_log[l]), _gate_vec(dn_dt_bias[l]), ea, eb, bd,
            wg_pad, gla_gate_bias[l][None, :], seq)
        yc = _gdn(cpk.reshape(bsz, seq, C_PACK), cg.reshape(bsz, seq, GROUP),
                  jnp.tile(dn_norm_w[l], HEADS)[None, :], bd, ltri)
        yd = _gla(dpk.reshape(bsz, seq, D_PACK), dla.reshape(bsz, seq, GROUP),
                  jnp.tile(gla_norm_w[l], HEADS)[None, :], bd, ltri)
        xf = _outproj_ffn(xf, (ya, yb, yc.reshape(m, GROUP), yd.reshape(m, GROUP)),
                          w_out[l].astype(BF16), norm2_w[l][None, :],
                          w_gate_up[l].astype(BF16), w_down[l].astype(BF16), final_norm_w[None, :],
                          final=(l == depth - 1))
    return xf.reshape(bsz, seq, d)
```

```python
import functools

import numpy as np
import jax
import jax.numpy as jnp
from jax import lax
from jax.experimental import pallas as pl
from jax.experimental.pallas import tpu as pltpu

F32 = jnp.float32
BF16 = jnp.bfloat16

D_MODEL = 1024
GROUP = 256
HEADS = 4
HEAD_DIM = 64
SGU_CHUNK = 128
SC_WIDTH = 3
DN_CONV_WIDTH = 4
CHUNK = 64
GATE_RANK = 16
GATE_TEMP = 16.0
D_FF = 2816
EPS = 1e-6
GATE_COLS = 128
TAIL = 8

ROW_TILE = 512
SEQ_TILE = 256
FF_CHUNK = 256
VMEM_LIMIT = 56 * 1024 * 1024

COL_A = 0
COL_B = COL_A + 2 * GROUP
COL_C = COL_B + 3 * GROUP
COL_D = COL_C + 4 * GROUP
COL_G = COL_D + 4 * GROUP
IN_COLS_PADDED = COL_G + GATE_COLS
C_PACK = 5 * GROUP
D_PACK = 4 * GROUP


def _dot(a, b):
    return jnp.dot(a, b, preferred_element_type=F32)


def _dot_nt(a, b):
    return lax.dot_general(a, b, (((1,), (1,)), ((), ())), preferred_element_type=F32)


def _dot_tn(a, b):
    return lax.dot_general(a, b, (((0,), (0,)), ((), ())), preferred_element_type=F32)


def _split2(x):
    hi = x.astype(BF16)
    lo = (x - hi.astype(F32)).astype(BF16)
    return hi, lo


def _dot_sel_rhs(x, sel):
    hi, lo = _split2(x)
    return _dot(hi, sel) + _dot(lo, sel)


def _dot_sel_lhs(sel, x):
    hi, lo = _split2(x)
    return _dot(sel, hi) + _dot(sel, lo)


def _softplus(x):
    return jnp.maximum(x, 0.0) + jnp.log(1.0 + jnp.exp(-jnp.abs(x)))


def _sigmoid(x):
    return 1.0 / (1.0 + jnp.exp(-x))


def _silu(x):
    return x * _sigmoid(x)


def _gelu_tanh(x):
    return 0.5 * x * (1.0 + jnp.tanh(0.7978845608028654 * (x + 0.044715 * x * x * x)))


def _blockdiag(x, mask):
    xb = x.astype(BF16)
    return jnp.concatenate([xb] * HEADS, axis=0) * mask


def _head_mean_sq(x, bones):
    return _dot((x * x).astype(BF16), bones) * (1.0 / HEAD_DIM)


def _causal_conv(x, ext_ref, cw_ref, width):
    tm = x.shape[0]
    ext_ref[TAIL:TAIL + tm, :] = x
    acc = x * cw_ref[width - 1:width, :]
    for j in range(1, width):
        acc = acc + ext_ref[TAIL - j:TAIL - j + tm, :] * cw_ref[width - 1 - j:width - j, :]
    ext_ref[0:TAIL, :] = x[tm - TAIL:tm, :]
    return acc


def _inproj_kernel(seq_tiles, x_ref, nw_ref, w_ref, lnw_ref, lnb_ref, ws_ref, bs_ref, scw_ref,
                   dcw_ref, alog_ref, dtb_ref, ea_ref, eb_ref, bd_ref, wg_ref, gb_ref,
                   ya_ref, yb_ref, cpk_ref, cg_ref, dpk_ref, dla_ref, extb_ref, extc_ref):
    tm = x_ref.shape[0]

    @pl.when(lax.rem(pl.program_id(0), seq_tiles) == 0)
    def _():
        extb_ref[0:TAIL, :] = jnp.zeros((TAIL, GROUP), F32)
        extc_ref[0:TAIL, :] = jnp.zeros((TAIL, 3 * GROUP), F32)

    x = x_ref[...]
    ms = jnp.mean(x * x, axis=-1, keepdims=True)
    h = (x * lax.rsqrt(ms + EPS) * nw_ref[...]).astype(BF16)
    bd = bd_ref[...]

    pa = _dot(h, w_ref[:, COL_A:COL_B])
    pc = _dot(h, w_ref[:, COL_C:COL_D])
    pg = _dot(h, w_ref[:, COL_G:COL_G + GATE_COLS])
    pb = _dot(h, w_ref[:, COL_B:COL_C])

    tri = (lax.broadcasted_iota(jnp.int32, (SGU_CHUNK, SGU_CHUNK), 0)
           >= lax.broadcasted_iota(jnp.int32, (SGU_CHUNK, SGU_CHUNK), 1))
    head_of_lane = lax.broadcasted_iota(jnp.int32, (SGU_CHUNK, GROUP), 1) // HEAD_DIM
    ws = [jnp.where(tri, ws_ref[hd], 0.0).astype(BF16) for hd in range(HEADS)]
    for c in range(tm // SGU_CHUNK):
        rows = slice(c * SGU_CHUNK, (c + 1) * SGU_CHUNK)
        u = _gelu_tanh(pa[rows, 0:GROUP])
        v = _gelu_tanh(pa[rows, GROUP:2 * GROUP])
        mu = jnp.mean(v, axis=-1, keepdims=True)
        vc = v - mu
        var = jnp.mean(vc * vc, axis=-1, keepdims=True)
        vn = vc * lax.rsqrt(var + EPS) * lnw_ref[...] + lnb_ref[...]
        mixed = bs_ref[...]
        for hd in range(HEADS):
            mixed = mixed + _dot(ws[hd], jnp.where(head_of_lane == hd, vn, 0.0).astype(BF16))
        ya_ref[rows, :] = (u * mixed).astype(ya_ref.dtype)

    pd = _dot(h, w_ref[:, COL_D:COL_G])

    conv_b = _causal_conv(pb[:, GROUP:2 * GROUP] * pb[:, 2 * GROUP:3 * GROUP], extb_ref, scw_ref, SC_WIDTH)
    yb_ref[...] = (pb[:, 0:GROUP] * conv_b).astype(yb_ref.dtype)

    qkv = _silu(_causal_conv(pc[:, 0:3 * GROUP], extc_ref, dcw_ref, DN_CONV_WIDTH))
    q = qkv[:, 0:GROUP]
    k = qkv[:, GROUP:2 * GROUP]
    qn = q * lax.rsqrt(_dot((q * q).astype(BF16), bd) + EPS) * (HEAD_DIM ** -0.5)
    kn = k * lax.rsqrt(_dot((k * k).astype(BF16), bd) + EPS)
    g_narrow = -jnp.exp(alog_ref[...]) * _softplus(pg + dtb_ref[...])
    cg_ref[...] = _dot_sel_rhs(g_narrow, ea_ref[...])
    beta = _dot(_sigmoid(pg).astype(BF16), eb_ref[...])
    cpk_ref[:, 0:GROUP] = qn.astype(BF16)
    cpk_ref[:, GROUP:2 * GROUP] = kn.astype(BF16)
    cpk_ref[:, 2 * GROUP:3 * GROUP] = qkv[:, 2 * GROUP:3 * GROUP].astype(BF16)
    cpk_ref[:, 3 * GROUP:4 * GROUP] = beta.astype(BF16)
    cpk_ref[:, 4 * GROUP:5 * GROUP] = _silu(pc[:, 3 * GROUP:4 * GROUP]).astype(BF16)

    pre = _dot(pg.astype(BF16), wg_ref[...]) + gb_ref[...]
    dla_ref[...] = -_softplus(-pre) * (1.0 / GATE_TEMP)
    dpk_ref[:, 0:3 * GROUP] = pd[:, 0:3 * GROUP].astype(BF16)
    dpk_ref[:, 3 * GROUP:4 * GROUP] = _silu(pd[:, 3 * GROUP:4 * GROUP]).astype(BF16)


def _layer_block(stacked, layer):
    zeros = (0,) * (stacked.ndim - 1)
    return pl.BlockSpec((None,) + stacked.shape[1:], lambda i: (layer,) + zeros,
                        pipeline_mode=pl.Buffered(1))


def _inproj(x, nw, w_stack, layer, lnw, lnb, ws, bs_wide, scw, dcw, alog_n, dtb_n, ea, eb, bd, wg_pad, gb,
            seq_len):
    m = x.shape[0]
    w = w_stack
    row = lambda i: (i, 0)
    whole = lambda a: pl.BlockSpec(a.shape, (lambda i: (0,) * a.ndim))
    outs = ((GROUP, BF16), (GROUP, BF16), (C_PACK, BF16), (GROUP, F32), (D_PACK, BF16), (GROUP, F32))
    return pl.pallas_call(
        functools.partial(_inproj_kernel, seq_len // ROW_TILE),
        grid=(m // ROW_TILE,),
        in_specs=[pl.BlockSpec((ROW_TILE, D_MODEL), row),
                  whole(nw),
                  _layer_block(w, layer),
                  whole(lnw), whole(lnb), whole(ws), whole(bs_wide), whole(scw),
                  whole(dcw), whole(alog_n), whole(dtb_n), whole(ea), whole(eb), whole(bd),
                  whole(wg_pad), whole(gb)],
        out_specs=[pl.BlockSpec((ROW_TILE, n), row) for n, _ in outs],
        out_shape=[jax.ShapeDtypeStruct((m, n), dt) for n, dt in outs],
        scratch_shapes=[pltpu.VMEM((TAIL + ROW_TILE, GROUP), F32),
                        pltpu.VMEM((TAIL + ROW_TILE, 3 * GROUP), F32)],
        compiler_params=pltpu.CompilerParams(dimension_semantics=("arbitrary",),
                                             vmem_limit_bytes=VMEM_LIMIT),
        name="in_proj",
    )(x, nw, w, lnw, lnb, ws, bs_wide, scw, dcw, alog_n, dtb_n, ea, eb, bd, wg_pad, gb)


def _chunk_masks():
    i = lax.broadcasted_iota(jnp.int32, (CHUNK, GROUP), 0)
    j = lax.broadcasted_iota(jnp.int32, (CHUNK, GROUP), 1) % CHUNK
    return i == j, i >= j, i > j


def _const_inputs():
    lane_head = np.arange(GROUP) // HEAD_DIM
    bd = (lane_head[:, None] == lane_head[None, :]).astype(np.float32)
    ltri = np.tril(np.ones((CHUNK, CHUNK), np.float32))
    return jnp.asarray(bd, BF16), jnp.asarray(ltri, BF16)


def _chunk_rows(c):
    return slice(c * CHUNK, (c + 1) * CHUNK)


def _gdn_stages(cpk_ref, cg_ref, nw_ref, bd_ref, ltri_ref, y_ref, ac_ref, bb_ref, d_ref, dec_ref, o_ref, s_ref):
    nb, tm = cpk_ref.shape[0], cpk_ref.shape[1]
    cps = tm // CHUNK
    eye_w, causal_w, strict_w = _chunk_masks()
    eye_f = eye_w.astype(F32)
    ltri = ltri_ref[...]
    bd = bd_ref[...]

    items = [(b, c) for b in range(nb) for c in range(cps)]
    n = len(items)

    def load(col, i):
        b, c = items[i]
        return cpk_ref[b, _chunk_rows(c), col * GROUP:(col + 1) * GROUP].astype(F32)

    qn = [load(0, i) for i in range(n)]
    kn = [load(1, i) for i in range(n)]
    beta = [load(3, i) for i in range(n)]
    gc = [_dot_sel_lhs(ltri, cg_ref[b, _chunk_rows(c), :]) for b, c in items]
    yield
    eg = [jnp.exp(x) for x in gc]
    kb = [kn[i] * beta[i] for i in range(n)]
    r1 = [_dot_nt(jnp.concatenate([kb[i], qn[i], eye_f], axis=0).astype(BF16), _blockdiag(kn[i], bd))
          for i in range(n)]
    yield
    low, attn, g_row = [], [], []
    for i in range(n):
        g_row.append(jnp.sum(jnp.where(eye_w, gc[i], 0.0), axis=0, keepdims=True))
        decay = jnp.where(causal_w, jnp.exp(jnp.minimum(gc[i] - g_row[i], 0.0)), 0.0)
        low.append(jnp.where(strict_w, r1[i][0:CHUNK] * decay, 0.0))
        attn.append((r1[i][CHUNK:2 * CHUNK] * decay).astype(BF16))
    t = [eye_f - x for x in low]
    p = [_dot(x.astype(BF16), _blockdiag(x, bd)) for x in low]
    yield
    n_levels = CHUNK.bit_length() - 1
    for level in range(1, n_levels):
        if level < n_levels - 1:
            r2 = [_dot(jnp.concatenate([p[i], t[i]], axis=0).astype(BF16), _blockdiag(p[i], bd))
                  for i in range(n)]
            p = [x[0:CHUNK] for x in r2]
            t = [t[i] + r2[i][CHUNK:2 * CHUNK] for i in range(n)]
        else:
            t = [t[i] + _dot(t[i].astype(BF16), _blockdiag(p[i], bd)) for i in range(n)]
        yield
    wu = [_dot(t[i].astype(BF16),
               jnp.concatenate([_blockdiag(kb[i] * eg[i], bd), _blockdiag(load(2, i) * beta[i], bd)], axis=1))
          for i in range(n)]
    yield
    gc_last = [x[CHUNK - 1:CHUNK, :] for x in gc]
    kd_t = [(r1[i][2 * CHUNK:3 * CHUNK] * jnp.exp(gc_last[i] - g_row[i])).astype(BF16) for i in range(n)]
    prod = [_dot(jnp.concatenate([attn[i], kd_t[i]], axis=0),
                 jnp.concatenate([_blockdiag(wu[i][:, 0:GROUP], bd),
                                  _blockdiag(wu[i][:, GROUP:2 * GROUP], bd)], axis=1))
            for i in range(n)]
    for i in range(n):
        ac_ref[i, 0:CHUNK, :] = (-prod[i][CHUNK:2 * CHUNK, 0:GROUP]).astype(BF16)
        bb_ref[i] = prod[i][CHUNK:2 * CHUNK, GROUP:2 * GROUP]
        ac_ref[i, CHUNK:2 * CHUNK, :] = (qn[i] * eg[i] - prod[i][0:CHUNK, 0:GROUP]).astype(BF16)
        d_ref[i] = prod[i][0:CHUNK, GROUP:2 * GROUP]
        dec_ref[i] = jnp.broadcast_to(jnp.exp(gc_last[i]), (TAIL, GROUP))
    yield

    for c in range(cps):
        for b in range(nb):
            i = b * cps + c
            s = s_ref[b]
            r = _dot(ac_ref[i], _blockdiag(s, bd))
            s_ref[b] = s * dec_ref[i, 0:1, :] + r[0:CHUNK] + bb_ref[i]
            o_ref[b, _chunk_rows(c), :] = r[CHUNK:2 * CHUNK] + d_ref[i]
        yield

    for b in range(nb):
        o = o_ref[b]
        zg = cpk_ref[b, :, 4 * GROUP:5 * GROUP].astype(F32)
        y_ref[b] = (o * lax.rsqrt(_head_mean_sq(o, bd) + EPS) * nw_ref[...] * zg).astype(y_ref.dtype)


def _gla_stages(dpk_ref, dla_ref, nw_ref, bd_ref, ltri_ref, y_ref, o_ref, st_ref):
    nb, tm = dpk_ref.shape[0], dpk_ref.shape[1]
    cps = tm // CHUNK
    _, causal_w, _ = _chunk_masks()
    ltri = ltri_ref[...]
    bd = bd_ref[...]
    bdf = bd.astype(F32)
    mid = CHUNK // 2

    items = [(b, c) for c in range(cps) for b in range(nb)]
    n = len(items)
    q = [dpk_ref[b, _chunk_rows(c), 0:GROUP].astype(F32) * (HEAD_DIM ** -0.5) for b, c in items]
    k = [dpk_ref[b, _chunk_rows(c), GROUP:2 * GROUP].astype(F32) for b, c in items]
    v = [dpk_ref[b, _chunk_rows(c), 2 * GROUP:3 * GROUP] for b, c in items]
    gcum = [_dot_sel_lhs(ltri, dla_ref[b, _chunk_rows(c), :]) for b, c in items]
    yield
    g_mid = [x[mid:mid + 1, :] for x in gcum]
    g_last = [x[CHUNK - 1:CHUNK, :] for x in gcum]
    attn = [jnp.where(causal_w,
                      _dot_nt((q[i] * jnp.exp(gcum[i] - g_mid[i])).astype(BF16),
                              _blockdiag(k[i] * jnp.exp(g_mid[i] - gcum[i]), bd)), 0.0).astype(BF16)
            for i in range(n)]
    yield
    upd = [bdf * _dot_tn(v[i], (k[i] * jnp.exp(g_last[i] - gcum[i])).astype(BF16)) for i in range(n)]
    yield
    o_intra = [_dot(attn[i], _blockdiag(v[i], bd)) for i in range(n)]
    qg = [(q[i] * jnp.exp(gcum[i])).astype(BF16) for i in range(n)]
    yield
    for i, (b, c) in enumerate(items):
        st = st_ref[b]
        o_ref[b, _chunk_rows(c), :] = o_intra[i] + _dot_nt(qg[i], st.astype(BF16))
        st_ref[b] = st * jnp.exp(g_last[i]) + upd[i]
        if b == nb - 1:
            yield

    for b in range(nb):
        o = o_ref[b]
        zg = dpk_ref[b, :, 3 * GROUP:4 * GROUP].astype(F32)
        y_ref[b] = (o * lax.rsqrt(_head_mean_sq(o, bd) + EPS) * nw_ref[...] * zg).astype(y_ref.dtype)


def _recur_kernel(cpk_ref, cg_ref, dpk_ref, dla_ref, nwc_ref, nwd_ref, bd_ref, ltri_ref, yc_ref, yd_ref,
                  ac_ref, bb_ref, d_ref, dec_ref, oc_ref, s_ref, od_ref, st_ref):
    @pl.when(pl.program_id(0) == 0)
    def _():
        s_ref[...] = jnp.zeros(s_ref.shape, F32)
        st_ref[...] = jnp.zeros(st_ref.shape, F32)

    stages = [_gdn_stages(cpk_ref, cg_ref, nwc_ref, bd_ref, ltri_ref, yc_ref,
                          ac_ref, bb_ref, d_ref, dec_ref, oc_ref, s_ref),
              _gla_stages(dpk_ref, dla_ref, nwd_ref, bd_ref, ltri_ref, yd_ref, od_ref, st_ref)]
    while stages:
        for gen in list(stages):
            try:
                next(gen)
            except StopIteration:
                stages.remove(gen)


def _recur(cpk, cg, dpk, dla, nwc, nwd, bd, ltri):
    nb, seq, _ = cpk.shape
    tm = SEQ_TILE
    n_chunks = nb * tm // CHUNK
    row = lambda i: (0, i, 0)
    fixed = lambda i: (0, 0)
    out = jax.ShapeDtypeStruct((nb, seq, GROUP), BF16)
    return pl.pallas_call(
        _recur_kernel,
        grid=(seq // tm,),
        in_specs=[pl.BlockSpec((nb, tm, C_PACK), row),
                  pl.BlockSpec((nb, tm, GROUP), row),
                  pl.BlockSpec((nb, tm, D_PACK), row),
                  pl.BlockSpec((nb, tm, GROUP), row),
                  pl.BlockSpec((1, GROUP), fixed),
                  pl.BlockSpec((1, GROUP), fixed),
                  pl.BlockSpec(bd.shape, fixed),
                  pl.BlockSpec(ltri.shape, fixed)],
        out_specs=[pl.BlockSpec((nb, tm, GROUP), row), pl.BlockSpec((nb, tm, GROUP), row)],
        out_shape=[out, out],
        scratch_shapes=[pltpu.VMEM((n_chunks, 2 * CHUNK, GROUP), BF16),
                        pltpu.VMEM((n_chunks, CHUNK, GROUP), F32),
                        pltpu.VMEM((n_chunks, CHUNK, GROUP), F32),
                        pltpu.VMEM((n_chunks, TAIL, GROUP), F32),
                        pltpu.VMEM((nb, tm, GROUP), F32),
                        pltpu.VMEM((nb, CHUNK, GROUP), F32),
                        pltpu.VMEM((nb, tm, GROUP), F32),
                        pltpu.VMEM((nb, GROUP, GROUP), F32)],
        compiler_params=pltpu.CompilerParams(dimension_semantics=("arbitrary",),
                                             vmem_limit_bytes=VMEM_LIMIT),
        name="recur",
    )(cpk, cg, dpk, dla, nwc, nwd, bd, ltri)


def _outproj_ffn_kernel(final, x_ref, ya_ref, yb_ref, yc_ref, yd_ref, wo_ref, n2_ref, wgu_ref, wd_ref,
                        fn_ref, o_ref, act_ref):
    x1 = x_ref[...]
    for idx, y_ref in enumerate((ya_ref, yb_ref, yc_ref, yd_ref)):
        x1 = x1 + _dot(y_ref[...], wo_ref[idx * GROUP:(idx + 1) * GROUP, :])
    ms = jnp.mean(x1 * x1, axis=-1, keepdims=True)
    h = (x1 * lax.rsqrt(ms + EPS) * n2_ref[...]).astype(BF16)
    for j in range(D_FF // FF_CHUNK):
        cols = slice(j * FF_CHUNK, (j + 1) * FF_CHUNK)
        gate = _dot(h, wgu_ref[:, cols])
        up = _dot(h, wgu_ref[:, D_FF + j * FF_CHUNK:D_FF + (j + 1) * FF_CHUNK])
        act_ref[:, cols] = (_silu(gate) * up).astype(BF16)
    x2 = x1 + _dot(act_ref[...], wd_ref[...])
    if final:
        ms2 = jnp.mean(x2 * x2, axis=-1, keepdims=True)
        x2 = x2 * lax.rsqrt(ms2 + EPS) * fn_ref[...]
    o_ref[...] = x2


def _outproj_ffn(x, ys, wo, n2, wgu, wd, fn, layer, final):
    m = x.shape[0]
    row = lambda i: (i, 0)
    fixed = lambda i: (0, 0)
    resident = lambda a: _layer_block(a, layer)
    return pl.pallas_call(
        functools.partial(_outproj_ffn_kernel, final),
        grid=(m // ROW_TILE,),
        in_specs=[pl.BlockSpec((ROW_TILE, D_MODEL), row)]
                 + [pl.BlockSpec((ROW_TILE, GROUP), row)] * 4
                 + [resident(wo), pl.BlockSpec((1, D_MODEL), fixed), resident(wgu), resident(wd),
                    pl.BlockSpec((1, D_MODEL), fixed)],
        out_specs=pl.BlockSpec((ROW_TILE, D_MODEL), row),
        out_shape=jax.ShapeDtypeStruct((m, D_MODEL), F32),
        scratch_shapes=[pltpu.VMEM((ROW_TILE, D_FF), BF16)],
        compiler_params=pltpu.CompilerParams(dimension_semantics=("arbitrary",),
                                             vmem_limit_bytes=VMEM_LIMIT),
        name="outproj_ffn",
    )(x, *ys, wo, n2, wgu, wd, fn)


def _relayout_w_in(w):
    g = GROUP
    c0 = 5 * g
    c_a = c0 + 3 * g
    c_z = c_a + 2 * HEADS
    d0 = c_z + g
    d_g = d0 + 3 * g
    d_z = d_g + GATE_RANK
    w = w.astype(BF16)
    pad = jnp.zeros(w.shape[:2] + (GATE_COLS - 2 * HEADS - GATE_RANK,), BF16)
    out = jnp.concatenate(
        [w[..., 0:c_a], w[..., c_z:d_g], w[..., d_z:d_z + g], w[..., c_a:c_z], w[..., d_g:d_z], pad], axis=-1)
    assert out.shape[-1] == IN_COLS_PADDED
    return out


def _gate_vec(v):
    return jnp.zeros((1, GATE_COLS), F32).at[0, 0:HEADS].set(v.astype(F32))


def _gate_selectors():
    ea = np.zeros((GATE_COLS, GROUP), np.float32)
    eb = np.zeros((GATE_COLS, GROUP), np.float32)
    for h in range(HEADS):
        ea[h, h * HEAD_DIM:(h + 1) * HEAD_DIM] = 1.0
        eb[HEADS + h, h * HEAD_DIM:(h + 1) * HEAD_DIM] = 1.0
    return jnp.asarray(ea, BF16), jnp.asarray(eb, BF16)


def kernel(x, norm1_w, w_in, sgu_ln_w, sgu_ln_b, sgu_w_spatial, sgu_b_spatial, sc_conv_w, dn_conv_w, dn_a_log, dn_dt_bias, dn_norm_w, gla_w_gate2, gla_gate_bias, gla_norm_w, w_out, norm2_w, w_gate_up, w_down, final_norm_w):
    bsz, seq, d = x.shape
    depth = w_in.shape[0]
    assert seq % ROW_TILE == 0 and seq % SEQ_TILE == 0 and d == D_MODEL
    m = bsz * seq
    xf = x.reshape(m, d)
    bd, ltri = _const_inputs()
    ea, eb = _gate_selectors()
    w_in_b = _relayout_w_in(w_in)
    w_out_b, w_gate_up_b, w_down_b = (w.astype(BF16) for w in (w_out, w_gate_up, w_down))
    for l in range(depth):
        bs_wide = jnp.repeat(sgu_b_spatial[l].T, HEAD_DIM, axis=1)
        wg_pad = jnp.zeros((GATE_COLS, GROUP), BF16).at[2 * HEADS:2 * HEADS + GATE_RANK].set(
            gla_w_gate2[l].astype(BF16))
        ya, yb, cpk, cg, dpk, dla = _inproj(
            xf, norm1_w[l][None, :], w_in_b, l,
            sgu_ln_w[l][None, :], sgu_ln_b[l][None, :], sgu_w_spatial[l], bs_wide, sc_conv_w[l],
            dn_conv_w[l], _gate_vec(dn_a_log[l]), _gate_vec(dn_dt_bias[l]), ea, eb, bd,
            wg_pad, gla_gate_bias[l][None, :], seq)
        yc, yd = _recur(cpk.reshape(bsz, seq, C_PACK), cg.reshape(bsz, seq, GROUP),
                        dpk.reshape(bsz, seq, D_PACK), dla.reshape(bsz, seq, GROUP),
                        jnp.tile(dn_norm_w[l], HEADS)[None, :], jnp.tile(gla_norm_w[l], HEADS)[None, :],
                        bd, ltri)
        xf = _outproj_ffn(xf, (ya, yb, yc.reshape(m, GROUP), yd.reshape(m, GROUP)),
                          w_out_b, norm2_w[l][None, :], w_gate_up_b, w_down_b, final_norm_w[None, :],
                          layer=l, final=(l == depth - 1))
    return xf.reshape(bsz, seq, d)
```

```python
import functools

import numpy as np
import jax
import jax.numpy as jnp
from jax import lax
from jax.experimental import pallas as pl
from jax.experimental.pallas import tpu as pltpu

F32 = jnp.float32
BF16 = jnp.bfloat16

D_MODEL = 1024
GROUP = 256
HEADS = 4
HEAD_DIM = 64
SGU_CHUNK = 128
SC_WIDTH = 3
DN_CONV_WIDTH = 4
CHUNK = 64
GATE_RANK = 16
GATE_TEMP = 16.0
D_FF = 2816
EPS = 1e-6
GATE_COLS = 128
TAIL = 8

ROW_TILE = 512
SEQ_TILE = 256
FF_CHUNK = 256
VMEM_LIMIT = 56 * 1024 * 1024

COL_A = 0
COL_B = COL_A + 2 * GROUP
COL_C = COL_B + 3 * GROUP
COL_D = COL_C + 4 * GROUP
COL_G = COL_D + 4 * GROUP
IN_COLS_PADDED = COL_G + GATE_COLS
C_PACK = 5 * GROUP
D_PACK = 4 * GROUP


def _dot(a, b):
    return jnp.dot(a, b, preferred_element_type=F32)


def _dot_nt(a, b):
    return lax.dot_general(a, b, (((1,), (1,)), ((), ())), preferred_element_type=F32)


def _dot_tn(a, b):
    return lax.dot_general(a, b, (((0,), (0,)), ((), ())), preferred_element_type=F32)


def _split2(x):
    hi = x.astype(BF16)
    lo = (x - hi.astype(F32)).astype(BF16)
    return hi, lo


def _dot_sel_rhs(x, sel):
    hi, lo = _split2(x)
    return _dot(hi, sel) + _dot(lo, sel)


def _dot_sel_lhs(sel, x):
    hi, lo = _split2(x)
    return _dot(sel, hi) + _dot(sel, lo)


def _softplus(x):
    return jnp.maximum(x, 0.0) + jnp.log(1.0 + jnp.exp(-jnp.abs(x)))


def _sigmoid(x):
    return 1.0 / (1.0 + jnp.exp(-x))


def _silu(x):
    return x * _sigmoid(x)


def _gelu_tanh(x):
    return 0.5 * x * (1.0 + jnp.tanh(0.7978845608028654 * (x + 0.044715 * x * x * x)))


def _blockdiag(x, mask):
    xb = x.astype(BF16)
    return jnp.concatenate([xb] * HEADS, axis=0) * mask


def _head_mean_sq(x, bones):
    return _dot((x * x).astype(BF16), bones) * (1.0 / HEAD_DIM)


def _causal_conv(x, ext_ref, cw_ref, width):
    tm = x.shape[0]
    ext_ref[TAIL:TAIL + tm, :] = x
    acc = x * cw_ref[width - 1:width, :]
    for j in range(1, width):
        acc = acc + ext_ref[TAIL - j:TAIL - j + tm, :] * cw_ref[width - 1 - j:width - j, :]
    ext_ref[0:TAIL, :] = x[tm - TAIL:tm, :]
    return acc


def _layer_block(stacked, layer):
    zeros = (0,) * (stacked.ndim - 1)
    return pl.BlockSpec((None,) + stacked.shape[1:], lambda i: (layer,) + zeros,
                        pipeline_mode=pl.Buffered(1))


def _inproj_stages(x_ref, nw_ref, w_ref, lnw_ref, lnb_ref, ws_ref, bs_ref, scw_ref,
                   dcw_ref, alog_ref, dtb_ref, ea_ref, eb_ref, bd_ref, wg_ref, gb_ref,
                   ya_ref, yb_ref, cpk_ref, cg_ref, dpk_ref, dla_ref, extb_ref, extc_ref):
    nb, tm = x_ref.shape[0], x_ref.shape[1]
    m = nb * tm
    batch_rows = [slice(b * tm, (b + 1) * tm) for b in range(nb)]

    x = x_ref[...].reshape(m, D_MODEL)
    ms = jnp.mean(x * x, axis=-1, keepdims=True)
    h = (x * lax.rsqrt(ms + EPS) * nw_ref[...]).astype(BF16)
    bd = bd_ref[...]

    def project(col0, col1):
        parts = []
        for c in range(col0, col1, GROUP):
            parts.append(_dot(h, w_ref[:, c:c + GROUP]))
            yield
        return jnp.concatenate(parts, axis=1)

    pa = yield from project(COL_A, COL_B)
    pc = yield from project(COL_C, COL_D)
    pg = _dot(h, w_ref[:, COL_G:COL_G + GATE_COLS])
    pb = yield from project(COL_B, COL_C)

    tri = (lax.broadcasted_iota(jnp.int32, (SGU_CHUNK, SGU_CHUNK), 0)
           >= lax.broadcasted_iota(jnp.int32, (SGU_CHUNK, SGU_CHUNK), 1))
    head_of_lane = lax.broadcasted_iota(jnp.int32, (SGU_CHUNK, GROUP), 1) // HEAD_DIM
    ws = [jnp.where(tri, ws_ref[hd], 0.0).astype(BF16) for hd in range(HEADS)]
    for b in range(nb):
        for c in range(tm // SGU_CHUNK):
            rows = slice(b * tm + c * SGU_CHUNK, b * tm + (c + 1) * SGU_CHUNK)
            u = _gelu_tanh(pa[rows, 0:GROUP])
            v = _gelu_tanh(pa[rows, GROUP:2 * GROUP])
            mu = jnp.mean(v, axis=-1, keepdims=True)
            vc = v - mu
            var = jnp.mean(vc * vc, axis=-1, keepdims=True)
            vn = vc * lax.rsqrt(var + EPS) * lnw_ref[...] + lnb_ref[...]
            mixed = bs_ref[...]
            for hd in range(HEADS):
                mixed = mixed + _dot(ws[hd], jnp.where(head_of_lane == hd, vn, 0.0).astype(BF16))
            ya_ref[b, c * SGU_CHUNK:(c + 1) * SGU_CHUNK, :] = (u * mixed).astype(ya_ref.dtype)
            yield

    pd = yield from project(COL_D, COL_G)

    for b, rows in enumerate(batch_rows):
        conv_b = _causal_conv(pb[rows, GROUP:2 * GROUP] * pb[rows, 2 * GROUP:3 * GROUP],
                              extb_ref.at[b], scw_ref, SC_WIDTH)
        yb_ref[b] = (pb[rows, 0:GROUP] * conv_b).astype(yb_ref.dtype)

    qkv = jnp.concatenate(
        [_silu(_causal_conv(pc[rows, 0:3 * GROUP], extc_ref.at[b], dcw_ref, DN_CONV_WIDTH))
         for b, rows in enumerate(batch_rows)], axis=0)
    q = qkv[:, 0:GROUP]
    k = qkv[:, GROUP:2 * GROUP]
    qn = q * lax.rsqrt(_dot((q * q).astype(BF16), bd) + EPS) * (HEAD_DIM ** -0.5)
    kn = k * lax.rsqrt(_dot((k * k).astype(BF16), bd) + EPS)
    yield
    g_narrow = -jnp.exp(alog_ref[...]) * _softplus(pg + dtb_ref[...])
    g_wide = _dot_sel_rhs(g_narrow, ea_ref[...])
    beta = _dot(_sigmoid(pg).astype(BF16), eb_ref[...])
    zg_c = _silu(pc[:, 3 * GROUP:4 * GROUP])
    yield
    pre = _dot(pg.astype(BF16), wg_ref[...]) + gb_ref[...]
    log_a = -_softplus(-pre) * (1.0 / GATE_TEMP)
    for b, rows in enumerate(batch_rows):
        cg_ref[b] = g_wide[rows]
        cpk_ref[b, :, 0:GROUP] = qn[rows].astype(BF16)
        cpk_ref[b, :, GROUP:2 * GROUP] = kn[rows].astype(BF16)
        cpk_ref[b, :, 2 * GROUP:3 * GROUP] = qkv[rows, 2 * GROUP:3 * GROUP].astype(BF16)
        cpk_ref[b, :, 3 * GROUP:4 * GROUP] = beta[rows].astype(BF16)
        cpk_ref[b, :, 4 * GROUP:5 * GROUP] = zg_c[rows].astype(BF16)
        dla_ref[b] = log_a[rows]
        dpk_ref[b, :, 0:3 * GROUP] = pd[rows, 0:3 * GROUP].astype(BF16)
        dpk_ref[b, :, 3 * GROUP:4 * GROUP] = _silu(pd[rows, 3 * GROUP:4 * GROUP]).astype(BF16)


def _chunk_masks():
    i = lax.broadcasted_iota(jnp.int32, (CHUNK, GROUP), 0)
    j = lax.broadcasted_iota(jnp.int32, (CHUNK, GROUP), 1) % CHUNK
    return i == j, i >= j, i > j


def _const_inputs():
    lane_head = np.arange(GROUP) // HEAD_DIM
    bd = (lane_head[:, None] == lane_head[None, :]).astype(np.float32)
    ltri = np.tril(np.ones((CHUNK, CHUNK), np.float32))
    return jnp.asarray(bd, BF16), jnp.asarray(ltri, BF16)


def _chunk_rows(c):
    return slice(c * CHUNK, (c + 1) * CHUNK)


def _gdn_stages(cpk_ref, cg_ref, nw_ref, bd_ref, ltri_ref, y_ref, ac_ref, bb_ref, d_ref, dec_ref, o_ref, s_ref):
    nb, tm = cpk_ref.shape[0], cpk_ref.shape[1]
    cps = tm // CHUNK
    eye_w, causal_w, strict_w = _chunk_masks()
    eye_f = eye_w.astype(F32)
    ltri = ltri_ref[...]
    bd = bd_ref[...]

    items = [(b, c) for b in range(nb) for c in range(cps)]
    n = len(items)

    def load(col, i):
        b, c = items[i]
        return cpk_ref[b, _chunk_rows(c), col * GROUP:(col + 1) * GROUP].astype(F32)

    qn = [load(0, i) for i in range(n)]
    kn = [load(1, i) for i in range(n)]
    beta = [load(3, i) for i in range(n)]
    gc = [_dot_sel_lhs(ltri, cg_ref[b, _chunk_rows(c), :]) for b, c in items]
    yield
    eg = [jnp.exp(x) for x in gc]
    kb = [kn[i] * beta[i] for i in range(n)]
    r1 = [_dot_nt(jnp.concatenate([kb[i], qn[i], eye_f], axis=0).astype(BF16), _blockdiag(kn[i], bd))
          for i in range(n)]
    yield
    low, attn, g_row = [], [], []
    for i in range(n):
        g_row.append(jnp.sum(jnp.where(eye_w, gc[i], 0.0), axis=0, keepdims=True))
        decay = jnp.where(causal_w, jnp.exp(jnp.minimum(gc[i] - g_row[i], 0.0)), 0.0)
        low.append(jnp.where(strict_w, r1[i][0:CHUNK] * decay, 0.0))
        attn.append((r1[i][CHUNK:2 * CHUNK] * decay).astype(BF16))
    t = [eye_f - x for x in low]
    p = [_dot(x.astype(BF16), _blockdiag(x, bd)) for x in low]
    yield
    n_levels = CHUNK.bit_length() - 1
    for level in range(1, n_levels):
        if level < n_levels - 1:
            r2 = [_dot(jnp.concatenate([p[i], t[i]], axis=0).astype(BF16), _blockdiag(p[i], bd))
                  for i in range(n)]
            p = [x[0:CHUNK] for x in r2]
            t = [t[i] + r2[i][CHUNK:2 * CHUNK] for i in range(n)]
        else:
            t = [t[i] + _dot(t[i].astype(BF16), _blockdiag(p[i], bd)) for i in range(n)]
        yield
    wu = [_dot(t[i].astype(BF16),
               jnp.concatenate([_blockdiag(kb[i] * eg[i], bd), _blockdiag(load(2, i) * beta[i], bd)], axis=1))
          for i in range(n)]
    yield
    gc_last = [x[CHUNK - 1:CHUNK, :] for x in gc]
    kd_t = [(r1[i][2 * CHUNK:3 * CHUNK] * jnp.exp(gc_last[i] - g_row[i])).astype(BF16) for i in range(n)]
    prod = [_dot(jnp.concatenate([attn[i], kd_t[i]], axis=0),
                 jnp.concatenate([_blockdiag(wu[i][:, 0:GROUP], bd),
                                  _blockdiag(wu[i][:, GROUP:2 * GROUP], bd)], axis=1))
            for i in range(n)]
    for i in range(n):
        ac_ref[i, 0:CHUNK, :] = (-prod[i][CHUNK:2 * CHUNK, 0:GROUP]).astype(BF16)
        bb_ref[i] = prod[i][CHUNK:2 * CHUNK, GROUP:2 * GROUP]
        ac_ref[i, CHUNK:2 * CHUNK, :] = (qn[i] * eg[i] - prod[i][0:CHUNK, 0:GROUP]).astype(BF16)
        d_ref[i] = prod[i][0:CHUNK, GROUP:2 * GROUP]
        dec_ref[i] = jnp.broadcast_to(jnp.exp(gc_last[i]), (TAIL, GROUP))
    yield

    for c in range(cps):
        for b in range(nb):
            i = b * cps + c
            s = s_ref[b]
            r = _dot(ac_ref[i], _blockdiag(s, bd))
            s_ref[b] = s * dec_ref[i, 0:1, :] + r[0:CHUNK] + bb_ref[i]
            o_ref[b, _chunk_rows(c), :] = r[CHUNK:2 * CHUNK] + d_ref[i]
        yield

    for b in range(nb):
        o = o_ref[b]
        zg = cpk_ref[b, :, 4 * GROUP:5 * GROUP].astype(F32)
        y_ref[b] = (o * lax.rsqrt(_head_mean_sq(o, bd) + EPS) * nw_ref[...] * zg).astype(y_ref.dtype)


def _gla_stages(dpk_ref, dla_ref, nw_ref, bd_ref, ltri_ref, y_ref, o_ref, st_ref):
    nb, tm = dpk_ref.shape[0], dpk_ref.shape[1]
    cps = tm // CHUNK
    _, causal_w, _ = _chunk_masks()
    ltri = ltri_ref[...]
    bd = bd_ref[...]
    bdf = bd.astype(F32)
    mid = CHUNK // 2

    items = [(b, c) for c in range(cps) for b in range(nb)]
    n = len(items)
    q = [dpk_ref[b, _chunk_rows(c), 0:GROUP].astype(F32) * (HEAD_DIM ** -0.5) for b, c in items]
    k = [dpk_ref[b, _chunk_rows(c), GROUP:2 * GROUP].astype(F32) for b, c in items]
    v = [dpk_ref[b, _chunk_rows(c), 2 * GROUP:3 * GROUP] for b, c in items]
    gcum = [_dot_sel_lhs(ltri, dla_ref[b, _chunk_rows(c), :]) for b, c in items]
    yield
    g_mid = [x[mid:mid + 1, :] for x in gcum]
    g_last = [x[CHUNK - 1:CHUNK, :] for x in gcum]
    attn = [jnp.where(causal_w,
                      _dot_nt((q[i] * jnp.exp(gcum[i] - g_mid[i])).astype(BF16),
                              _blockdiag(k[i] * jnp.exp(g_mid[i] - gcum[i]), bd)), 0.0).astype(BF16)
            for i in range(n)]
    yield
    upd = [bdf * _dot_tn(v[i], (k[i] * jnp.exp(g_last[i] - gcum[i])).astype(BF16)) for i in range(n)]
    yield
    o_intra = [_dot(attn[i], _blockdiag(v[i], bd)) for i in range(n)]
    qg = [(q[i] * jnp.exp(gcum[i])).astype(BF16) for i in range(n)]
    yield
    for i, (b, c) in enumerate(items):
        st = st_ref[b]
        o_ref[b, _chunk_rows(c), :] = o_intra[i] + _dot_nt(qg[i], st.astype(BF16))
        st_ref[b] = st * jnp.exp(g_last[i]) + upd[i]
        if b == nb - 1:
            yield

    for b in range(nb):
        o = o_ref[b]
        zg = dpk_ref[b, :, 3 * GROUP:4 * GROUP].astype(F32)
        y_ref[b] = (o * lax.rsqrt(_head_mean_sq(o, bd) + EPS) * nw_ref[...] * zg).astype(y_ref.dtype)


def _front_kernel(x_ref, nw_ref, w_ref, lnw_ref, lnb_ref, ws_ref, bs_ref, scw_ref, dcw_ref, alog_ref, dtb_ref,
                  ea_ref, eb_ref, bd_ref, wg_ref, gb_ref, nwc_ref, nwd_ref, ltri_ref,
                  ya_ref, yb_ref, yc_ref, yd_ref,
                  extb_ref, extc_ref, ya_s, yb_s, cpk_s, cg_s, dpk_s, dla_s,
                  ac_ref, bb_ref, d_ref, dec_ref, oc_ref, s_ref, od_ref, st_ref):
    i = pl.program_id(0)
    wslot = lax.rem(i, 2)
    rslot = 1 - wslot

    @pl.when(i == 0)
    def _():
        for ref in (extb_ref, extc_ref, s_ref, st_ref):
            ref[...] = jnp.zeros(ref.shape, ref.dtype)
        for ref in (ya_s, yb_s, cpk_s, cg_s, dpk_s, dla_s):
            ref[1] = jnp.zeros(ref.shape[1:], ref.dtype)

    stages = [_gdn_stages(cpk_s.at[rslot], cg_s.at[rslot], nwc_ref, bd_ref, ltri_ref, yc_ref,
                          ac_ref, bb_ref, d_ref, dec_ref, oc_ref, s_ref),
              _gla_stages(dpk_s.at[rslot], dla_s.at[rslot], nwd_ref, bd_ref, ltri_ref, yd_ref, od_ref, st_ref),
              _inproj_stages(x_ref, nw_ref, w_ref, lnw_ref, lnb_ref, ws_ref, bs_ref, scw_ref, dcw_ref,
                             alog_ref, dtb_ref, ea_ref, eb_ref, bd_ref, wg_ref, gb_ref,
                             ya_s.at[wslot], yb_s.at[wslot], cpk_s.at[wslot], cg_s.at[wslot],
                             dpk_s.at[wslot], dla_s.at[wslot], extb_ref, extc_ref)]
    while stages:
        for gen in list(stages):
            try:
                next(gen)
            except StopIteration:
                stages.remove(gen)
    ya_ref[...] = ya_s[rslot]
    yb_ref[...] = yb_s[rslot]


def _front(x, nw, w_stack, layer, lnw, lnb, ws, bs_wide, scw, dcw, alog_n, dtb_n, ea, eb, bd, wg_pad, gb,
           nwc, nwd, ltri):
    nb, seq, _ = x.shape
    tm = SEQ_TILE
    n_tiles = seq // tm
    n_chunks = nb * tm // CHUNK
    tile_in = lambda i: (0, jnp.minimum(i, n_tiles - 1), 0)
    tile_out = lambda i: (0, jnp.maximum(i - 1, 0), 0)
    whole = lambda a: pl.BlockSpec(a.shape, (lambda i: (0,) * a.ndim))
    out = jax.ShapeDtypeStruct((nb, seq, GROUP), BF16)
    slot = lambda width, dt: pltpu.VMEM((2, nb, tm, width), dt)
    return pl.pallas_call(
        _front_kernel,
        grid=(n_tiles + 1,),
        in_specs=[pl.BlockSpec((nb, tm, D_MODEL), tile_in),
                  whole(nw),
                  _layer_block(w_stack, layer),
                  whole(lnw), whole(lnb), whole(ws), whole(bs_wide), whole(scw),
                  whole(dcw), whole(alog_n), whole(dtb_n), whole(ea), whole(eb), whole(bd),
                  whole(wg_pad), whole(gb), whole(nwc), whole(nwd), whole(ltri)],
        out_specs=[pl.BlockSpec((nb, tm, GROUP), tile_out)] * 4,
        out_shape=[out] * 4,
        scratch_shapes=[pltpu.VMEM((nb, TAIL + tm, GROUP), F32),
                        pltpu.VMEM((nb, TAIL + tm, 3 * GROUP), F32),
                        slot(GROUP, BF16), slot(GROUP, BF16),
                        slot(C_PACK, BF16), slot(GROUP, F32), slot(D_PACK, BF16), slot(GROUP, F32),
                        pltpu.VMEM((n_chunks, 2 * CHUNK, GROUP), BF16),
                        pltpu.VMEM((n_chunks, CHUNK, GROUP), F32),
                        pltpu.VMEM((n_chunks, CHUNK, GROUP), F32),
                        pltpu.VMEM((n_chunks, TAIL, GROUP), F32),
                        pltpu.VMEM((nb, tm, GROUP), F32),
                        pltpu.VMEM((nb, CHUNK, GROUP), F32),
                        pltpu.VMEM((nb, tm, GROUP), F32),
                        pltpu.VMEM((nb, GROUP, GROUP), F32)],
        compiler_params=pltpu.CompilerParams(dimension_semantics=("arbitrary",),
                                             vmem_limit_bytes=VMEM_LIMIT),
        name="front",
    )(x, nw, w_stack, lnw, lnb, ws, bs_wide, scw, dcw, alog_n, dtb_n, ea, eb, bd, wg_pad, gb, nwc, nwd, ltri)


def _outproj_ffn_kernel(final, x_ref, ya_ref, yb_ref, yc_ref, yd_ref, wo_ref, n2_ref, wgu_ref, wd_ref,
                        fn_ref, o_ref, act_ref):
    x1 = x_ref[...]
    for idx, y_ref in enumerate((ya_ref, yb_ref, yc_ref, yd_ref)):
        x1 = x1 + _dot(y_ref[...], wo_ref[idx * GROUP:(idx + 1) * GROUP, :])
    ms = jnp.mean(x1 * x1, axis=-1, keepdims=True)
    h = (x1 * lax.rsqrt(ms + EPS) * n2_ref[...]).astype(BF16)
    for j in range(D_FF // FF_CHUNK):
        cols = slice(j * FF_CHUNK, (j + 1) * FF_CHUNK)
        gate = _dot(h, wgu_ref[:, cols])
        up = _dot(h, wgu_ref[:, D_FF + j * FF_CHUNK:D_FF + (j + 1) * FF_CHUNK])
        act_ref[:, cols] = (_silu(gate) * up).astype(BF16)
    x2 = x1 + _dot(act_ref[...], wd_ref[...])
    if final:
        ms2 = jnp.mean(x2 * x2, axis=-1, keepdims=True)
        x2 = x2 * lax.rsqrt(ms2 + EPS) * fn_ref[...]
    o_ref[...] = x2


def _outproj_ffn(x, ys, wo, n2, wgu, wd, fn, layer, final):
    m = x.shape[0]
    row = lambda i: (i, 0)
    fixed = lambda i: (0, 0)
    resident = lambda a: _layer_block(a, layer)
    return pl.pallas_call(
        functools.partial(_outproj_ffn_kernel, final),
        grid=(m // ROW_TILE,),
        in_specs=[pl.BlockSpec((ROW_TILE, D_MODEL), row)]
                 + [pl.BlockSpec((ROW_TILE, GROUP), row)] * 4
                 + [resident(wo), pl.BlockSpec((1, D_MODEL), fixed), resident(wgu), resident(wd),
                    pl.BlockSpec((1, D_MODEL), fixed)],
        out_specs=pl.BlockSpec((ROW_TILE, D_MODEL), row),
        out_shape=jax.ShapeDtypeStruct((m, D_MODEL), F32),
        scratch_shapes=[pltpu.VMEM((ROW_TILE, D_FF), BF16)],
        compiler_params=pltpu.CompilerParams(dimension_semantics=("arbitrary",),
                                             vmem_limit_bytes=VMEM_LIMIT),
        name="outproj_ffn",
    )(x, *ys, wo, n2, wgu, wd, fn)


def _relayout_w_in(w):
    g = GROUP
    c0 = 5 * g
    c_a = c0 + 3 * g
    c_z = c_a + 2 * HEADS
    d0 = c_z + g
    d_g = d0 + 3 * g
    d_z = d_g + GATE_RANK
    w = w.astype(BF16)
    pad = jnp.zeros(w.shape[:2] + (GATE_COLS - 2 * HEADS - GATE_RANK,), BF16)
    out = jnp.concatenate(
        [w[..., 0:c_a], w[..., c_z:d_g], w[..., d_z:d_z + g], w[..., c_a:c_z], w[..., d_g:d_z], pad], axis=-1)
    assert out.shape[-1] == IN_COLS_PADDED
    return out


def _gate_vec(v):
    return jnp.zeros((1, GATE_COLS), F32).at[0, 0:HEADS].set(v.astype(F32))


def _gate_selectors():
    ea = np.zeros((GATE_COLS, GROUP), np.float32)
    eb = np.zeros((GATE_COLS, GROUP), np.float32)
    for h in range(HEADS):
        ea[h, h * HEAD_DIM:(h + 1) * HEAD_DIM] = 1.0
        eb[HEADS + h, h * HEAD_DIM:(h + 1) * HEAD_DIM] = 1.0
    return jnp.asarray(ea, BF16), jnp.asarray(eb, BF16)


def kernel(x, norm1_w, w_in, sgu_ln_w, sgu_ln_b, sgu_w_spatial, sgu_b_spatial, sc_conv_w, dn_conv_w, dn_a_log, dn_dt_bias, dn_norm_w, gla_w_gate2, gla_gate_bias, gla_norm_w, w_out, norm2_w, w_gate_up, w_down, final_norm_w):
    bsz, seq, d = x.shape
    depth = w_in.shape[0]
    assert seq % ROW_TILE == 0 and seq % SEQ_TILE == 0 and d == D_MODEL
    m = bsz * seq
    xf = x.reshape(m, d)
    bd, ltri = _const_inputs()
    ea, eb = _gate_selectors()
    w_in_b = _relayout_w_in(w_in)
    w_out_b, w_gate_up_b, w_down_b = (w.astype(BF16) for w in (w_out, w_gate_up, w_down))
    for l in range(depth):
        bs_wide = jnp.repeat(sgu_b_spatial[l].T, HEAD_DIM, axis=1)
        wg_pad = jnp.zeros((GATE_COLS, GROUP), BF16).at[2 * HEADS:2 * HEADS + GATE_RANK].set(
            gla_w_gate2[l].astype(BF16))
        ya, yb, yc, yd = _front(
            xf.reshape(bsz, seq, d), norm1_w[l][None, :], w_in_b, l,
            sgu_ln_w[l][None, :], sgu_ln_b[l][None, :], sgu_w_spatial[l], bs_wide, sc_conv_w[l],
            dn_conv_w[l], _gate_vec(dn_a_log[l]), _gate_vec(dn_dt_bias[l]), ea, eb, bd,
            wg_pad, gla_gate_bias[l][None, :],
            jnp.tile(dn_norm_w[l], HEADS)[None, :], jnp.tile(gla_norm_w[l], HEADS)[None, :], ltri)
        xf = _outproj_ffn(xf, tuple(y.reshape(m, GROUP) for y in (ya, yb, yc, yd)),
                          w_out_b, norm2_w[l][None, :], w_gate_up_b, w_down_b, final_norm_w[None, :],
                          layer=l, final=(l == depth - 1))
    return xf.reshape(bsz, seq, d)
```

```python
import functools

import numpy as np
import jax
import jax.numpy as jnp
from jax import lax
from jax.experimental import pallas as pl
from jax.experimental.pallas import tpu as pltpu

F32 = jnp.float32
BF16 = jnp.bfloat16

D_MODEL = 1024
GROUP = 256
HEADS = 4
HEAD_DIM = 64
SGU_CHUNK = 128
SC_WIDTH = 3
DN_CONV_WIDTH = 4
CHUNK = 64
GATE_RANK = 16
GATE_TEMP = 16.0
D_FF = 2816
EPS = 1e-6
GATE_COLS = 128
TAIL = 8

ROW_TILE = 1024
SEQ_TILE = 256
FF_CHUNK = 256
VMEM_LIMIT = 56 * 1024 * 1024

MAIN_A = 0
MAIN_B = MAIN_A + 2 * GROUP
MAIN_C = MAIN_B + 3 * GROUP
MAIN_COLS = MAIN_C + 3 * GROUP
REST_CZ = 0
REST_D = REST_CZ + GROUP
REST_G = REST_D + 4 * GROUP
REST_COLS = REST_G + GATE_COLS
C_PACK = 5 * GROUP
D_PACK = 4 * GROUP


def _dot(a, b):
    return jnp.dot(a, b, preferred_element_type=F32)


def _dot_nt(a, b):
    return lax.dot_general(a, b, (((1,), (1,)), ((), ())), preferred_element_type=F32)


def _dot_tn(a, b):
    return lax.dot_general(a, b, (((0,), (0,)), ((), ())), preferred_element_type=F32)


def _split2(x):
    hi = x.astype(BF16)
    lo = (x - hi.astype(F32)).astype(BF16)
    return hi, lo


def _dot_sel_rhs(x, sel):
    hi, lo = _split2(x)
    return _dot(hi, sel) + _dot(lo, sel)


def _dot_sel_lhs(sel, x):
    hi, lo = _split2(x)
    return _dot(sel, hi) + _dot(sel, lo)


def _softplus(x):
    return jnp.maximum(x, 0.0) + jnp.log(1.0 + jnp.exp(-jnp.abs(x)))


def _sigmoid(x):
    return 1.0 / (1.0 + jnp.exp(-x))


def _silu(x):
    return x * _sigmoid(x)


def _gelu_tanh(x):
    return 0.5 * x * (1.0 + jnp.tanh(0.7978845608028654 * (x + 0.044715 * x * x * x)))


def _blockdiag(x, mask):
    xb = x.astype(BF16)
    return jnp.concatenate([xb] * HEADS, axis=0) * mask


def _head_mean_sq(x, bones):
    return _dot((x * x).astype(BF16), bones) * (1.0 / HEAD_DIM)


def _causal_conv(x, ext_ref, cw_ref, width):
    tm = x.shape[0]
    ext_ref[TAIL:TAIL + tm, :] = x
    acc = x * cw_ref[width - 1:width, :]
    for j in range(1, width):
        acc = acc + ext_ref[TAIL - j:TAIL - j + tm, :] * cw_ref[width - 1 - j:width - j, :]
    ext_ref[0:TAIL, :] = x[tm - TAIL:tm, :]
    return acc


def _layer_param(stacked, layer):
    zeros = (0,) * (stacked.ndim - 1)
    return pl.BlockSpec((None,) + stacked.shape[1:], lambda i: (layer,) + zeros)


def _layer_block(stacked, layer):
    zeros = (0,) * (stacked.ndim - 1)
    return pl.BlockSpec((None,) + stacked.shape[1:], lambda i: (layer,) + zeros,
                        pipeline_mode=pl.Buffered(1))


def _inproj_stages(x_ref, nw_ref, w_ref, wr_ref, lnw_ref, lnb_ref, ws_ref, bs_ref, scw_ref,
                   dcw_ref, alog_ref, dtb_ref, ea_ref, eb_ref, bd_ref, wg_ref, gb_ref,
                   ya_ref, yb_ref, cpk_ref, cg_ref, dpk_ref, dla_ref, extb_ref, extc_ref):
    nb, tm = x_ref.shape[0], x_ref.shape[1]
    m = nb * tm
    batch_rows = [slice(b * tm, (b + 1) * tm) for b in range(nb)]

    x = x_ref[...].reshape(m, D_MODEL)
    ms = jnp.mean(x * x, axis=-1, keepdims=True)
    h = (x * lax.rsqrt(ms + EPS) * nw_ref[...]).astype(BF16)
    bd = bd_ref[...]

    def project(ref, col0, n_slabs):
        parts = []
        for c in range(col0, col0 + n_slabs * GROUP, GROUP):
            parts.append(_dot(h, ref[:, c:c + GROUP]))
            yield
        return jnp.concatenate(parts, axis=1)

    pa = yield from project(w_ref, MAIN_A, 2)
    pc = yield from project(w_ref, MAIN_C, 3)
    pcz = yield from project(wr_ref, REST_CZ, 1)
    pg = _dot(h, wr_ref[:, REST_G:REST_G + GATE_COLS])
    pb = yield from project(w_ref, MAIN_B, 3)

    tri = (lax.broadcasted_iota(jnp.int32, (SGU_CHUNK, SGU_CHUNK), 0)
           >= lax.broadcasted_iota(jnp.int32, (SGU_CHUNK, SGU_CHUNK), 1))
    head_of_lane = lax.broadcasted_iota(jnp.int32, (SGU_CHUNK, GROUP), 1) // HEAD_DIM
    ws = [jnp.where(tri, ws_ref[hd], 0.0).astype(BF16) for hd in range(HEADS)]
    for b in range(nb):
        for c in range(tm // SGU_CHUNK):
            rows = slice(b * tm + c * SGU_CHUNK, b * tm + (c + 1) * SGU_CHUNK)
            u = _gelu_tanh(pa[rows, 0:GROUP])
            v = _gelu_tanh(pa[rows, GROUP:2 * GROUP])
            mu = jnp.mean(v, axis=-1, keepdims=True)
            vc = v - mu
            var = jnp.mean(vc * vc, axis=-1, keepdims=True)
            vn = vc * lax.rsqrt(var + EPS) * lnw_ref[...] + lnb_ref[...]
            mixed = bs_ref[...]
            for hd in range(HEADS):
                mixed = mixed + _dot(ws[hd], jnp.where(head_of_lane == hd, vn, 0.0).astype(BF16))
            ya_ref[b, c * SGU_CHUNK:(c + 1) * SGU_CHUNK, :] = (u * mixed).astype(ya_ref.dtype)
            yield

    pd = yield from project(wr_ref, REST_D, 4)

    for b, rows in enumerate(batch_rows):
        conv_b = _causal_conv(pb[rows, GROUP:2 * GROUP] * pb[rows, 2 * GROUP:3 * GROUP],
                              extb_ref.at[b], scw_ref, SC_WIDTH)
        yb_ref[b] = (pb[rows, 0:GROUP] * conv_b).astype(yb_ref.dtype)

    qkv = jnp.concatenate(
        [_silu(_causal_conv(pc[rows, 0:3 * GROUP], extc_ref.at[b], dcw_ref, DN_CONV_WIDTH))
         for b, rows in enumerate(batch_rows)], axis=0)
    q = qkv[:, 0:GROUP]
    k = qkv[:, GROUP:2 * GROUP]
    qn = q * lax.rsqrt(_dot((q * q).astype(BF16), bd) + EPS) * (HEAD_DIM ** -0.5)
    kn = k * lax.rsqrt(_dot((k * k).astype(BF16), bd) + EPS)
    yield
    g_narrow = -jnp.exp(alog_ref[...]) * _softplus(pg + dtb_ref[...])
    g_wide = _dot_sel_rhs(g_narrow, ea_ref[...])
    beta = _dot(_sigmoid(pg).astype(BF16), eb_ref[...])
    zg_c = _silu(pcz)
    yield
    pre = _dot(pg.astype(BF16), wg_ref[...]) + gb_ref[...]
    log_a = -_softplus(-pre) * (1.0 / GATE_TEMP)
    for b, rows in enumerate(batch_rows):
        cg_ref[b] = g_wide[rows]
        cpk_ref[b, :, 0:GROUP] = qn[rows].astype(BF16)
        cpk_ref[b, :, GROUP:2 * GROUP] = kn[rows].astype(BF16)
        cpk_ref[b, :, 2 * GROUP:3 * GROUP] = qkv[rows, 2 * GROUP:3 * GROUP].astype(BF16)
        cpk_ref[b, :, 3 * GROUP:4 * GROUP] = beta[rows].astype(BF16)
        cpk_ref[b, :, 4 * GROUP:5 * GROUP] = zg_c[rows].astype(BF16)
        dla_ref[b] = log_a[rows]
        dpk_ref[b, :, 0:3 * GROUP] = pd[rows, 0:3 * GROUP].astype(BF16)
        dpk_ref[b, :, 3 * GROUP:4 * GROUP] = _silu(pd[rows, 3 * GROUP:4 * GROUP]).astype(BF16)


def _chunk_masks():
    i = lax.broadcasted_iota(jnp.int32, (CHUNK, GROUP), 0)
    j = lax.broadcasted_iota(jnp.int32, (CHUNK, GROUP), 1) % CHUNK
    return i == j, i >= j, i > j


def _const_inputs():
    lane_head = np.arange(GROUP) // HEAD_DIM
    bd = (lane_head[:, None] == lane_head[None, :]).astype(np.float32)
    ltri = np.tril(np.ones((CHUNK, CHUNK), np.float32))
    return jnp.asarray(bd, BF16), jnp.asarray(ltri, BF16)


def _chunk_rows(c):
    return slice(c * CHUNK, (c + 1) * CHUNK)


def _gdn_stages(cpk_ref, cg_ref, nw_ref, bd_ref, ltri_ref, y_ref, ac_ref, bb_ref, d_ref, dec_ref, o_ref, s_ref):
    nb, tm = cpk_ref.shape[0], cpk_ref.shape[1]
    cps = tm // CHUNK
    eye_w, causal_w, strict_w = _chunk_masks()
    eye_f = eye_w.astype(F32)
    ltri = ltri_ref[...]
    bd = bd_ref[...]

    items = [(b, c) for b in range(nb) for c in range(cps)]
    n = len(items)

    def load(col, i):
        b, c = items[i]
        return cpk_ref[b, _chunk_rows(c), col * GROUP:(col + 1) * GROUP].astype(F32)

    qn = [load(0, i) for i in range(n)]
    kn = [load(1, i) for i in range(n)]
    beta = [load(3, i) for i in range(n)]
    gc = [_dot_sel_lhs(ltri, cg_ref[b, _chunk_rows(c), :]) for b, c in items]
    yield
    eg = [jnp.exp(x) for x in gc]
    kb = [kn[i] * beta[i] for i in range(n)]
    r1 = [_dot_nt(jnp.concatenate([kb[i], qn[i], eye_f], axis=0).astype(BF16), _blockdiag(kn[i], bd))
          for i in range(n)]
    yield
    low, attn, g_row = [], [], []
    for i in range(n):
        g_row.append(jnp.sum(jnp.where(eye_w, gc[i], 0.0), axis=0, keepdims=True))
        decay = jnp.where(causal_w, jnp.exp(jnp.minimum(gc[i] - g_row[i], 0.0)), 0.0)
        low.append(jnp.where(strict_w, r1[i][0:CHUNK] * decay, 0.0))
        attn.append((r1[i][CHUNK:2 * CHUNK] * decay).astype(BF16))
    t = [eye_f - x for x in low]
    p = [_dot(x.astype(BF16), _blockdiag(x, bd)) for x in low]
    yield
    n_levels = CHUNK.bit_length() - 1
    for level in range(1, n_levels):
        if level < n_levels - 1:
            r2 = [_dot(jnp.concatenate([p[i], t[i]], axis=0).astype(BF16), _blockdiag(p[i], bd))
                  for i in range(n)]
            p = [x[0:CHUNK] for x in r2]
            t = [t[i] + r2[i][CHUNK:2 * CHUNK] for i in range(n)]
        else:
            t = [t[i] + _dot(t[i].astype(BF16), _blockdiag(p[i], bd)) for i in range(n)]
        yield
    wu = [_dot(t[i].astype(BF16),
               jnp.concatenate([_blockdiag(kb[i] * eg[i], bd), _blockdiag(load(2, i) * beta[i], bd)], axis=1))
          for i in range(n)]
    yield
    gc_last = [x[CHUNK - 1:CHUNK, :] for x in gc]
    kd_t = [(r1[i][2 * CHUNK:3 * CHUNK] * jnp.exp(gc_last[i] - g_row[i])).astype(BF16) for i in range(n)]
    prod = [_dot(jnp.concatenate([attn[i], kd_t[i]], axis=0),
                 jnp.concatenate([_blockdiag(wu[i][:, 0:GROUP], bd),
                                  _blockdiag(wu[i][:, GROUP:2 * GROUP], bd)], axis=1))
            for i in range(n)]
    for i in range(n):
        ac_ref[i, 0:CHUNK, :] = (-prod[i][CHUNK:2 * CHUNK, 0:GROUP]).astype(BF16)
        bb_ref[i] = prod[i][CHUNK:2 * CHUNK, GROUP:2 * GROUP]
        ac_ref[i, CHUNK:2 * CHUNK, :] = (qn[i] * eg[i] - prod[i][0:CHUNK, 0:GROUP]).astype(BF16)
        d_ref[i] = prod[i][0:CHUNK, GROUP:2 * GROUP]
        dec_ref[i] = jnp.broadcast_to(jnp.exp(gc_last[i]), (TAIL, GROUP))
    yield

    for c in range(cps):
        for b in range(nb):
            i = b * cps + c
            s = s_ref[b]
            r = _dot(ac_ref[i], _blockdiag(s, bd))
            s_ref[b] = s * dec_ref[i, 0:1, :] + r[0:CHUNK] + bb_ref[i]
            o_ref[b, _chunk_rows(c), :] = r[CHUNK:2 * CHUNK] + d_ref[i]
        yield

    for b in range(nb):
        o = o_ref[b]
        zg = cpk_ref[b, :, 4 * GROUP:5 * GROUP].astype(F32)
        y_ref[b] = (o * lax.rsqrt(_head_mean_sq(o, bd) + EPS) * nw_ref[...] * zg).astype(y_ref.dtype)


def _gla_stages(dpk_ref, dla_ref, nw_ref, bd_ref, ltri_ref, y_ref, o_ref, st_ref):
    nb, tm = dpk_ref.shape[0], dpk_ref.shape[1]
    cps = tm // CHUNK
    _, causal_w, _ = _chunk_masks()
    ltri = ltri_ref[...]
    bd = bd_ref[...]
    bdf = bd.astype(F32)
    mid = CHUNK // 2

    items = [(b, c) for c in range(cps) for b in range(nb)]
    n = len(items)
    q = [dpk_ref[b, _chunk_rows(c), 0:GROUP].astype(F32) * (HEAD_DIM ** -0.5) for b, c in items]
    k = [dpk_ref[b, _chunk_rows(c), GROUP:2 * GROUP].astype(F32) for b, c in items]
    v = [dpk_ref[b, _chunk_rows(c), 2 * GROUP:3 * GROUP] for b, c in items]
    gcum = [_dot_sel_lhs(ltri, dla_ref[b, _chunk_rows(c), :]) for b, c in items]
    yield
    g_mid = [x[mid:mid + 1, :] for x in gcum]
    g_last = [x[CHUNK - 1:CHUNK, :] for x in gcum]
    attn = [jnp.where(causal_w,
                      _dot_nt((q[i] * jnp.exp(gcum[i] - g_mid[i])).astype(BF16),
                              _blockdiag(k[i] * jnp.exp(g_mid[i] - gcum[i]), bd)), 0.0).astype(BF16)
            for i in range(n)]
    yield
    upd = [bdf * _dot_tn(v[i], (k[i] * jnp.exp(g_last[i] - gcum[i])).astype(BF16)) for i in range(n)]
    yield
    o_intra = [_dot(attn[i], _blockdiag(v[i], bd)) for i in range(n)]
    qg = [(q[i] * jnp.exp(gcum[i])).astype(BF16) for i in range(n)]
    yield
    for i, (b, c) in enumerate(items):
        st = st_ref[b]
        o_ref[b, _chunk_rows(c), :] = o_intra[i] + _dot_nt(qg[i], st.astype(BF16))
        st_ref[b] = st * jnp.exp(g_last[i]) + upd[i]
        if b == nb - 1:
            yield

    for b in range(nb):
        o = o_ref[b]
        zg = dpk_ref[b, :, 3 * GROUP:4 * GROUP].astype(F32)
        y_ref[b] = (o * lax.rsqrt(_head_mean_sq(o, bd) + EPS) * nw_ref[...] * zg).astype(y_ref.dtype)


def _front_kernel(x_ref, nw_ref, w_ref, wr_ref, lnw_ref, lnb_ref, ws_ref, bs_ref, scw_ref, dcw_ref, alog_ref, dtb_ref,
                  ea_ref, eb_ref, bd_ref, wg_ref, gb_ref, nwc_ref, nwd_ref, ltri_ref,
                  ya_ref, yb_ref, yc_ref, yd_ref,
                  extb_ref, extc_ref, ya_s, yb_s, cpk_s, cg_s, dpk_s, dla_s,
                  ac_ref, bb_ref, d_ref, dec_ref, oc_ref, s_ref, od_ref, st_ref):
    i = pl.program_id(0)
    wslot = lax.rem(i, 2)
    rslot = 1 - wslot

    @pl.when(i == 0)
    def _():
        for ref in (extb_ref, extc_ref, s_ref, st_ref):
            ref[...] = jnp.zeros(ref.shape, ref.dtype)
        for ref in (ya_s, yb_s, cpk_s, cg_s, dpk_s, dla_s):
            ref[1] = jnp.zeros(ref.shape[1:], ref.dtype)

    stages = [_gdn_stages(cpk_s.at[rslot], cg_s.at[rslot], nwc_ref, bd_ref, ltri_ref, yc_ref,
                          ac_ref, bb_ref, d_ref, dec_ref, oc_ref, s_ref),
              _gla_stages(dpk_s.at[rslot], dla_s.at[rslot], nwd_ref, bd_ref, ltri_ref, yd_ref, od_ref, st_ref),
              _inproj_stages(x_ref, nw_ref, w_ref, wr_ref, lnw_ref, lnb_ref, ws_ref, bs_ref, scw_ref, dcw_ref,
                             alog_ref, dtb_ref, ea_ref, eb_ref, bd_ref, wg_ref, gb_ref,
                             ya_s.at[wslot], yb_s.at[wslot], cpk_s.at[wslot], cg_s.at[wslot],
                             dpk_s.at[wslot], dla_s.at[wslot], extb_ref, extc_ref)]
    while stages:
        for gen in list(stages):
            try:
                next(gen)
            except StopIteration:
                stages.remove(gen)
    ya_ref[...] = ya_s[rslot]
    yb_ref[...] = yb_s[rslot]


def _front(x, nw, w_stack, w_rest, layer, lnw, lnb, ws, bs_wide, scw, dcw, alog_n, dtb_n, ea, eb, bd, wg_pad, gb,
           nwc, nwd, ltri):
    nb, seq, _ = x.shape
    tm = SEQ_TILE
    n_tiles = seq // tm
    n_chunks = nb * tm // CHUNK
    tile_in = lambda i: (0, jnp.minimum(i, n_tiles - 1), 0)
    tile_out = lambda i: (0, jnp.maximum(i - 1, 0), 0)
    whole = lambda a: pl.BlockSpec(a.shape, (lambda i: (0,) * a.ndim))
    per_layer = lambda a: _layer_param(a, layer)
    out = jax.ShapeDtypeStruct((nb, seq, GROUP), BF16)
    slot = lambda width, dt: pltpu.VMEM((2, nb, tm, width), dt)
    return pl.pallas_call(
        _front_kernel,
        grid=(n_tiles + 1,),
        in_specs=[pl.BlockSpec((nb, tm, D_MODEL), tile_in),
                  per_layer(nw),
                  pl.BlockSpec((None, D_MODEL, MAIN_COLS), lambda i: (layer, 0, 0),
                               pipeline_mode=pl.Buffered(1)),
                  _layer_block(w_rest, layer),
                  per_layer(lnw), per_layer(lnb), per_layer(ws), per_layer(bs_wide), per_layer(scw),
                  per_layer(dcw), per_layer(alog_n), per_layer(dtb_n), whole(ea), whole(eb), whole(bd),
                  per_layer(wg_pad), per_layer(gb), per_layer(nwc), per_layer(nwd), whole(ltri)],
        out_specs=[pl.BlockSpec((nb, tm, GROUP), tile_out)] * 4,
        out_shape=[out] * 4,
        scratch_shapes=[pltpu.VMEM((nb, TAIL + tm, GROUP), F32),
                        pltpu.VMEM((nb, TAIL + tm, 3 * GROUP), F32),
                        slot(GROUP, BF16), slot(GROUP, BF16),
                        slot(C_PACK, BF16), slot(GROUP, F32), slot(D_PACK, BF16), slot(GROUP, F32),
                        pltpu.VMEM((n_chunks, 2 * CHUNK, GROUP), BF16),
                        pltpu.VMEM((n_chunks, CHUNK, GROUP), F32),
                        pltpu.VMEM((n_chunks, CHUNK, GROUP), F32),
                        pltpu.VMEM((n_chunks, TAIL, GROUP), F32),
                        pltpu.VMEM((nb, tm, GROUP), F32),
                        pltpu.VMEM((nb, CHUNK, GROUP), F32),
                        pltpu.VMEM((nb, tm, GROUP), F32),
                        pltpu.VMEM((nb, GROUP, GROUP), F32)],
        compiler_params=pltpu.CompilerParams(dimension_semantics=("arbitrary",),
                                             vmem_limit_bytes=VMEM_LIMIT),
        name="front",
    )(x, nw, w_stack, w_rest, lnw, lnb, ws, bs_wide, scw, dcw, alog_n, dtb_n, ea, eb, bd, wg_pad, gb,
      nwc, nwd, ltri)


def _outproj_ffn_kernel(final, x_ref, ya_ref, yb_ref, yc_ref, yd_ref, wo_ref, n2_ref, wgu_ref, wd_ref,
                        fn_ref, o_ref, act_ref):
    x1 = x_ref[...]
    for idx, y_ref in enumerate((ya_ref, yb_ref, yc_ref, yd_ref)):
        x1 = x1 + _dot(y_ref[...], wo_ref[idx * GROUP:(idx + 1) * GROUP, :])
    ms = jnp.mean(x1 * x1, axis=-1, keepdims=True)
    h = (x1 * lax.rsqrt(ms + EPS) * n2_ref[...]).astype(BF16)
    for j in range(D_FF // FF_CHUNK):
        cols = slice(j * FF_CHUNK, (j + 1) * FF_CHUNK)
        gate = _dot(h, wgu_ref[:, cols])
        up = _dot(h, wgu_ref[:, D_FF + j * FF_CHUNK:D_FF + (j + 1) * FF_CHUNK])
        act_ref[:, cols] = (_silu(gate) * up).astype(BF16)
    x2 = x1 + _dot(act_ref[...], wd_ref[...])
    if final:
        ms2 = jnp.mean(x2 * x2, axis=-1, keepdims=True)
        x2 = x2 * lax.rsqrt(ms2 + EPS) * fn_ref[...]
    o_ref[...] = x2


def _outproj_ffn(x, ys, wo, n2, wgu, wd, fn, layer, final):
    m = x.shape[0]
    row = lambda i: (i, 0)
    fixed = lambda i: (0, 0)
    resident = lambda a: _layer_block(a, layer)
    return pl.pallas_call(
        functools.partial(_outproj_ffn_kernel, final),
        grid=(m // ROW_TILE,),
        in_specs=[pl.BlockSpec((ROW_TILE, D_MODEL), row)]
                 + [pl.BlockSpec((ROW_TILE, GROUP), row)] * 4
                 + [resident(wo), _layer_param(n2, layer), resident(wgu), resident(wd),
                    pl.BlockSpec((1, D_MODEL), fixed)],
        out_specs=pl.BlockSpec((ROW_TILE, D_MODEL), row),
        out_shape=jax.ShapeDtypeStruct((m, D_MODEL), F32),
        scratch_shapes=[pltpu.VMEM((ROW_TILE, D_FF), BF16)],
        compiler_params=pltpu.CompilerParams(dimension_semantics=("arbitrary",),
                                             vmem_limit_bytes=VMEM_LIMIT),
        name="outproj_ffn",
    )(x, *ys, wo, n2, wgu, wd, fn)


def _repack_w_in_rest(w):
    c_a = MAIN_COLS
    c_z = c_a + 2 * HEADS
    d0 = c_z + GROUP
    d_g = d0 + 3 * GROUP
    d_z = d_g + GATE_RANK
    pad = jnp.zeros(w.shape[:2] + (GATE_COLS - 2 * HEADS - GATE_RANK,), w.dtype)
    out = jnp.concatenate(
        [w[..., c_z:d_g], w[..., d_z:d_z + GROUP], w[..., c_a:c_z], w[..., d_g:d_z], pad], axis=-1)
    assert out.shape[-1] == REST_COLS
    return out


def _gate_vec(v):
    return jnp.pad(v.astype(F32), ((0, 0), (0, GATE_COLS - HEADS)))[:, None, :]


def _row(v):
    return v[:, None, :]


def _gate_selectors():
    ea = np.zeros((GATE_COLS, GROUP), np.float32)
    eb = np.zeros((GATE_COLS, GROUP), np.float32)
    for h in range(HEADS):
        ea[h, h * HEAD_DIM:(h + 1) * HEAD_DIM] = 1.0
        eb[HEADS + h, h * HEAD_DIM:(h + 1) * HEAD_DIM] = 1.0
    return jnp.asarray(ea, BF16), jnp.asarray(eb, BF16)


def kernel(x, norm1_w, w_in, sgu_ln_w, sgu_ln_b, sgu_w_spatial, sgu_b_spatial, sc_conv_w, dn_conv_w, dn_a_log, dn_dt_bias, dn_norm_w, gla_w_gate2, gla_gate_bias, gla_norm_w, w_out, norm2_w, w_gate_up, w_down, final_norm_w):
    bsz, seq, d = x.shape
    depth = w_in.shape[0]
    assert seq % ROW_TILE == 0 and seq % SEQ_TILE == 0 and d == D_MODEL
    m = bsz * seq
    xf = x.reshape(m, d)
    bd, ltri = _const_inputs()
    ea, eb = _gate_selectors()
    w_in_b, w_out_b, w_gate_up_b, w_down_b = (w.astype(BF16) for w in (w_in, w_out, w_gate_up, w_down))
    w_in_rest = _repack_w_in_rest(w_in_b)
    bs_wide = jnp.repeat(jnp.swapaxes(sgu_b_spatial, 1, 2), HEAD_DIM, axis=2)
    wg_pad = jnp.pad(gla_w_gate2.astype(BF16),
                     ((0, 0), (2 * HEADS, GATE_COLS - 2 * HEADS - GATE_RANK), (0, 0)))
    front_params = (_row(sgu_ln_w), _row(sgu_ln_b), sgu_w_spatial, bs_wide, sc_conv_w, dn_conv_w,
                    _gate_vec(dn_a_log), _gate_vec(dn_dt_bias), ea, eb, bd, wg_pad, _row(gla_gate_bias),
                    _row(jnp.tile(dn_norm_w, (1, HEADS))), _row(jnp.tile(gla_norm_w, (1, HEADS))), ltri)
    norm1, norm2 = _row(norm1_w), _row(norm2_w)
    for l in range(depth):
        ya, yb, yc, yd = _front(xf.reshape(bsz, seq, d), norm1, w_in_b, w_in_rest, l, *front_params)
        xf = _outproj_ffn(xf, tuple(y.reshape(m, GROUP) for y in (ya, yb, yc, yd)),
                          w_out_b, norm2, w_gate_up_b, w_down_b, final_norm_w[None, :],
                          layer=l, final=(l == depth - 1))
    return xf.reshape(bsz, seq, d)
```

```python
import functools

import numpy as np
import jax
import jax.numpy as jnp
from jax import lax
from jax.experimental import pallas as pl
from jax.experimental.pallas import tpu as pltpu

F32 = jnp.float32
BF16 = jnp.bfloat16

D_MODEL = 1024
GROUP = 256
HEADS = 4
HEAD_DIM = 64
SGU_CHUNK = 128
SC_WIDTH = 3
DN_CONV_WIDTH = 4
CHUNK = 64
GATE_RANK = 16
GATE_TEMP = 16.0
D_FF = 2816
EPS = 1e-6
GATE_COLS = 128
TAIL = 8

ROW_TILE = 1024
SEQ_TILE = 256
GDN_GROUPS = 1
FF_CHUNK = 256
VMEM_LIMIT = 56 * 1024 * 1024

MAIN_A = 0
MAIN_B = MAIN_A + 2 * GROUP
MAIN_C = MAIN_B + 3 * GROUP
MAIN_COLS = MAIN_C + 3 * GROUP
REST_CZ = 0
REST_D = REST_CZ + GROUP
REST_G = REST_D + 4 * GROUP
REST_COLS = REST_G + GATE_COLS
C_PACK = 5 * GROUP
D_PACK = 4 * GROUP


def _dot(a, b):
    return jnp.dot(a, b, preferred_element_type=F32)


def _dot_nt(a, b):
    return lax.dot_general(a, b, (((1,), (1,)), ((), ())), preferred_element_type=F32)


def _dot_tn(a, b):
    return lax.dot_general(a, b, (((0,), (0,)), ((), ())), preferred_element_type=F32)


def _split2(x):
    hi = x.astype(BF16)
    lo = (x - hi.astype(F32)).astype(BF16)
    return hi, lo


def _dot_sel_rhs(x, sel):
    hi, lo = _split2(x)
    return _dot(hi, sel) + _dot(lo, sel)


def _dot_sel_lhs(sel, x):
    hi, lo = _split2(x)
    return _dot(sel, hi) + _dot(sel, lo)


def _softplus(x):
    return jnp.maximum(x, 0.0) + jnp.log(1.0 + jnp.exp(-jnp.abs(x)))


def _sigmoid(x):
    return 0.5 + 0.5 * jnp.tanh(0.5 * x)


def _silu(x):
    h = 0.5 * x
    return h + h * jnp.tanh(h)


def _gelu_tanh(x):
    h = 0.5 * x
    return h + h * jnp.tanh(x * (0.7978845608028654 + (0.7978845608028654 * 0.044715) * (x * x)))


def _blockdiag(x, mask):
    xb = x.astype(BF16)
    return jnp.concatenate([xb] * HEADS, axis=0) * mask


def _head_mean_sq(x, bones):
    return _dot((x * x).astype(BF16), bones) * (1.0 / HEAD_DIM)


def _causal_conv(x, ext_ref, cw_ref, width):
    tm = x.shape[0]
    ext_ref[TAIL:TAIL + tm, :] = x
    acc = x * cw_ref[width - 1:width, :]
    for j in range(1, width):
        acc = acc + ext_ref[TAIL - j:TAIL - j + tm, :] * cw_ref[width - 1 - j:width - j, :]
    ext_ref[0:TAIL, :] = x[tm - TAIL:tm, :]
    return acc


def _layer_param(stacked, layer):
    zeros = (0,) * (stacked.ndim - 1)
    return pl.BlockSpec((None,) + stacked.shape[1:], lambda i: (layer,) + zeros)


def _layer_block(stacked, layer):
    zeros = (0,) * (stacked.ndim - 1)
    return pl.BlockSpec((None,) + stacked.shape[1:], lambda i: (layer,) + zeros,
                        pipeline_mode=pl.Buffered(1))


def _inproj_stages(x_ref, nw_ref, w_ref, wr_ref, lnw_ref, lnb_ref, ws_ref, bs_ref, scw_ref,
                   dcw_ref, alog_ref, dtb_ref, ea_ref, eb_ref, bd_ref, wg_ref, gb_ref,
                   ya_ref, yb_ref, cpk_ref, cg_ref, dpk_ref, dla_ref, extb_ref, extc_ref):
    nb, tm = x_ref.shape[0], x_ref.shape[1]
    m = nb * tm
    batch_rows = [slice(b * tm, (b + 1) * tm) for b in range(nb)]

    x = x_ref[...].reshape(m, D_MODEL)
    ms = jnp.mean(x * x, axis=-1, keepdims=True)
    h = (x * lax.rsqrt(ms + EPS) * nw_ref[...]).astype(BF16)
    bd = bd_ref[...]

    def project(ref, col0, n_slabs):
        parts = []
        for c in range(col0, col0 + n_slabs * GROUP, GROUP):
            parts.append(_dot(h, ref[:, c:c + GROUP]))
            yield
        return jnp.concatenate(parts, axis=1)

    pa = yield from project(w_ref, MAIN_A, 2)
    pc = yield from project(w_ref, MAIN_C, 3)
    pcz = yield from project(wr_ref, REST_CZ, 1)
    pg = _dot(h, wr_ref[:, REST_G:REST_G + GATE_COLS])
    pb = yield from project(w_ref, MAIN_B, 3)

    tri = (lax.broadcasted_iota(jnp.int32, (SGU_CHUNK, SGU_CHUNK), 0)
           >= lax.broadcasted_iota(jnp.int32, (SGU_CHUNK, SGU_CHUNK), 1))
    head_of_lane = lax.broadcasted_iota(jnp.int32, (SGU_CHUNK, GROUP), 1) // HEAD_DIM
    ws = [jnp.where(tri, ws_ref[hd], 0.0).astype(BF16) for hd in range(HEADS)]
    for b in range(nb):
        for c in range(tm // SGU_CHUNK):
            rows = slice(b * tm + c * SGU_CHUNK, b * tm + (c + 1) * SGU_CHUNK)
            u = _gelu_tanh(pa[rows, 0:GROUP])
            v = _gelu_tanh(pa[rows, GROUP:2 * GROUP])
            mu = jnp.mean(v, axis=-1, keepdims=True)
            vc = v - mu
            var = jnp.mean(vc * vc, axis=-1, keepdims=True)
            vn = vc * lax.rsqrt(var + EPS) * lnw_ref[...] + lnb_ref[...]
            mixed = bs_ref[...]
            for hd in range(HEADS):
                mixed = mixed + _dot(ws[hd], jnp.where(head_of_lane == hd, vn, 0.0).astype(BF16))
            ya_ref[b, c * SGU_CHUNK:(c + 1) * SGU_CHUNK, :] = (u * mixed).astype(ya_ref.dtype)
            yield

    pd = yield from project(wr_ref, REST_D, 4)

    for b, rows in enumerate(batch_rows):
        conv_b = _causal_conv(pb[rows, GROUP:2 * GROUP] * pb[rows, 2 * GROUP:3 * GROUP],
                              extb_ref.at[b], scw_ref, SC_WIDTH)
        yb_ref[b] = (pb[rows, 0:GROUP] * conv_b).astype(yb_ref.dtype)

    qkv = jnp.concatenate(
        [_silu(_causal_conv(pc[rows, 0:3 * GROUP], extc_ref.at[b], dcw_ref, DN_CONV_WIDTH))
         for b, rows in enumerate(batch_rows)], axis=0)
    q = qkv[:, 0:GROUP]
    k = qkv[:, GROUP:2 * GROUP]
    qn = q * lax.rsqrt(_dot((q * q).astype(BF16), bd) + EPS) * (HEAD_DIM ** -0.5)
    kn = k * lax.rsqrt(_dot((k * k).astype(BF16), bd) + EPS)
    yield
    g_narrow = -jnp.exp(alog_ref[...]) * _softplus(pg + dtb_ref[...])
    g_wide = _dot_sel_rhs(g_narrow, ea_ref[...])
    beta = _dot(_sigmoid(pg).astype(BF16), eb_ref[...])
    zg_c = _silu(pcz)
    yield
    pre = _dot(pg.astype(BF16), wg_ref[...]) + gb_ref[...]
    log_a = -_softplus(-pre) * (1.0 / GATE_TEMP)
    for b, rows in enumerate(batch_rows):
        cg_ref[b] = g_wide[rows]
        cpk_ref[b, :, 0:GROUP] = qn[rows].astype(BF16)
        cpk_ref[b, :, GROUP:2 * GROUP] = kn[rows].astype(BF16)
        cpk_ref[b, :, 2 * GROUP:3 * GROUP] = qkv[rows, 2 * GROUP:3 * GROUP].astype(BF16)
        cpk_ref[b, :, 3 * GROUP:4 * GROUP] = beta[rows].astype(BF16)
        cpk_ref[b, :, 4 * GROUP:5 * GROUP] = zg_c[rows].astype(BF16)
        dla_ref[b] = log_a[rows]
        dpk_ref[b, :, 0:3 * GROUP] = pd[rows, 0:3 * GROUP].astype(BF16)
        dpk_ref[b, :, 3 * GROUP:4 * GROUP] = _silu(pd[rows, 3 * GROUP:4 * GROUP]).astype(BF16)


def _chunk_masks():
    i = lax.broadcasted_iota(jnp.int32, (CHUNK, GROUP), 0)
    j = lax.broadcasted_iota(jnp.int32, (CHUNK, GROUP), 1) % CHUNK
    return i == j, i >= j, i > j


def _const_inputs():
    lane_head = np.arange(GROUP) // HEAD_DIM
    bd = (lane_head[:, None] == lane_head[None, :]).astype(np.float32)
    ltri = np.tril(np.ones((CHUNK, CHUNK), np.float32))
    return jnp.asarray(bd, BF16), jnp.asarray(ltri, BF16)


def _chunk_rows(c):
    return slice(c * CHUNK, (c + 1) * CHUNK)


def _gdn_stages(cpk_ref, cg_ref, nw_ref, bd_ref, ltri_ref, y_ref, ac_ref, bb_ref, d_ref, dec_ref, o_ref, s_ref):
    nb, tm = cpk_ref.shape[0], cpk_ref.shape[1]
    cps = tm // CHUNK
    eye_w, causal_w, strict_w = _chunk_masks()
    eye_f = eye_w.astype(F32)
    ltri = ltri_ref[...]
    bd = bd_ref[...]

    all_items = [(b, c) for b in range(nb) for c in range(cps)]
    per_group = len(all_items) // GDN_GROUPS
    for g in range(GDN_GROUPS):
        yield from _gdn_prepare(all_items[g * per_group:(g + 1) * per_group], cps, cpk_ref, cg_ref,
                                ac_ref, bb_ref, d_ref, dec_ref, eye_w, causal_w, strict_w, eye_f, ltri, bd)

    for c in range(cps):
        for b in range(nb):
            i = b * cps + c
            s = s_ref[b]
            r = _dot(ac_ref[i], _blockdiag(s, bd))
            s_ref[b] = s * dec_ref[i, 0:1, :] + r[0:CHUNK] + bb_ref[i]
            o_ref[b, _chunk_rows(c), :] = r[CHUNK:2 * CHUNK] + d_ref[i]
        yield

    for b in range(nb):
        o = o_ref[b]
        zg = cpk_ref[b, :, 4 * GROUP:5 * GROUP].astype(F32)
        y_ref[b] = (o * lax.rsqrt(_head_mean_sq(o, bd) + EPS) * nw_ref[...] * zg).astype(y_ref.dtype)


def _gdn_prepare(items, cps, cpk_ref, cg_ref, ac_ref, bb_ref, d_ref, dec_ref,
                 eye_w, causal_w, strict_w, eye_f, ltri, bd):
    n = len(items)

    def load(col, i):
        b, c = items[i]
        return cpk_ref[b, _chunk_rows(c), col * GROUP:(col + 1) * GROUP]

    qn = [load(0, i) for i in range(n)]
    kn = [load(1, i) for i in range(n)]
    beta = [load(3, i) for i in range(n)]
    gc = [_dot_sel_lhs(ltri, cg_ref[b, _chunk_rows(c), :]) for b, c in items]
    yield
    eg = [jnp.exp(x) for x in gc]
    kb = [kn[i] * beta[i] for i in range(n)]
    eye_b = eye_f.astype(BF16)
    r1 = [_dot_nt(jnp.concatenate([kb[i], qn[i], eye_b], axis=0), _blockdiag(kn[i], bd))
          for i in range(n)]
    yield
    low, attn, g_row = [], [], []
    for i in range(n):
        g_row.append(jnp.sum(jnp.where(eye_w, gc[i], 0.0), axis=0, keepdims=True))
        decay = jnp.where(causal_w, jnp.exp(jnp.minimum(gc[i] - g_row[i], 0.0)), 0.0)
        low.append(jnp.where(strict_w, r1[i][0:CHUNK] * decay, 0.0))
        attn.append((r1[i][CHUNK:2 * CHUNK] * decay).astype(BF16))
    t = [eye_f - x for x in low]
    p = [_dot(x.astype(BF16), _blockdiag(x, bd)) for x in low]
    yield
    n_levels = CHUNK.bit_length() - 1
    for level in range(1, n_levels):
        if level < n_levels - 1:
            r2 = [_dot(jnp.concatenate([p[i], t[i]], axis=0).astype(BF16), _blockdiag(p[i], bd))
                  for i in range(n)]
            p = [x[0:CHUNK] for x in r2]
            t = [t[i] + r2[i][CHUNK:2 * CHUNK] for i in range(n)]
        else:
            t = [t[i] + _dot(t[i].astype(BF16), _blockdiag(p[i], bd)) for i in range(n)]
        yield
    wu = [_dot(t[i].astype(BF16),
               jnp.concatenate([_blockdiag(kb[i] * eg[i], bd), _blockdiag(load(2, i) * beta[i], bd)], axis=1))
          for i in range(n)]
    yield
    gc_last = [x[CHUNK - 1:CHUNK, :] for x in gc]
    kd_t = [(r1[i][2 * CHUNK:3 * CHUNK] * jnp.exp(gc_last[i] - g_row[i])).astype(BF16) for i in range(n)]
    prod = [_dot(jnp.concatenate([attn[i], kd_t[i]], axis=0),
                 jnp.concatenate([_blockdiag(wu[i][:, 0:GROUP], bd),
                                  _blockdiag(wu[i][:, GROUP:2 * GROUP], bd)], axis=1))
            for i in range(n)]
    for i, (b, c) in enumerate(items):
        j = b * cps + c
        ac_ref[j, 0:CHUNK, :] = (-prod[i][CHUNK:2 * CHUNK, 0:GROUP]).astype(BF16)
        bb_ref[j] = prod[i][CHUNK:2 * CHUNK, GROUP:2 * GROUP]
        ac_ref[j, CHUNK:2 * CHUNK, :] = (qn[i] * eg[i] - prod[i][0:CHUNK, 0:GROUP]).astype(BF16)
        d_ref[j] = prod[i][0:CHUNK, GROUP:2 * GROUP]
        dec_ref[j] = jnp.broadcast_to(jnp.exp(gc_last[i]), (TAIL, GROUP))
    yield


def _gla_stages(dpk_ref, dla_ref, nw_ref, bd_ref, ltri_ref, y_ref, o_ref, st_ref):
    nb, tm = dpk_ref.shape[0], dpk_ref.shape[1]
    cps = tm // CHUNK
    _, causal_w, _ = _chunk_masks()
    ltri = ltri_ref[...]
    bd = bd_ref[...]
    bdf = bd.astype(F32)
    mid = CHUNK // 2

    items = [(b, c) for c in range(cps) for b in range(nb)]
    n = len(items)
    q = [dpk_ref[b, _chunk_rows(c), 0:GROUP].astype(F32) * (HEAD_DIM ** -0.5) for b, c in items]
    k = [dpk_ref[b, _chunk_rows(c), GROUP:2 * GROUP].astype(F32) for b, c in items]
    v = [dpk_ref[b, _chunk_rows(c), 2 * GROUP:3 * GROUP] for b, c in items]
    gcum = [_dot_sel_lhs(ltri, dla_ref[b, _chunk_rows(c), :]) for b, c in items]
    yield
    g_mid = [x[mid:mid + 1, :] for x in gcum]
    g_last = [x[CHUNK - 1:CHUNK, :] for x in gcum]
    attn = [jnp.where(causal_w,
                      _dot_nt((q[i] * jnp.exp(gcum[i] - g_mid[i])).astype(BF16),
                              _blockdiag(k[i] * jnp.exp(g_mid[i] - gcum[i]), bd)), 0.0).astype(BF16)
            for i in range(n)]
    yield
    upd = [bdf * _dot_tn(v[i], (k[i] * jnp.exp(g_last[i] - gcum[i])).astype(BF16)) for i in range(n)]
    yield
    o_intra = [_dot(attn[i], _blockdiag(v[i], bd)) for i in range(n)]
    qg = [(q[i] * jnp.exp(gcum[i])).astype(BF16) for i in range(n)]
    yield
    for i, (b, c) in enumerate(items):
        st = st_ref[b]
        o_ref[b, _chunk_rows(c), :] = o_intra[i] + _dot_nt(qg[i], st.astype(BF16))
        st_ref[b] = st * jnp.exp(g_last[i]) + upd[i]
        if b == nb - 1:
            yield

    for b in range(nb):
        o = o_ref[b]
        zg = dpk_ref[b, :, 3 * GROUP:4 * GROUP].astype(F32)
        y_ref[b] = (o * lax.rsqrt(_head_mean_sq(o, bd) + EPS) * nw_ref[...] * zg).astype(y_ref.dtype)


def _front_kernel(x_ref, nw_ref, w_ref, wr_ref, lnw_ref, lnb_ref, ws_ref, bs_ref, scw_ref, dcw_ref, alog_ref, dtb_ref,
                  ea_ref, eb_ref, bd_ref, wg_ref, gb_ref, nwc_ref, nwd_ref, ltri_ref,
                  ya_ref, yb_ref, yc_ref, yd_ref,
                  extb_ref, extc_ref, ya_s, yb_s, cpk_s, cg_s, dpk_s, dla_s,
                  ac_ref, bb_ref, d_ref, dec_ref, oc_ref, s_ref, od_ref, st_ref):
    i = pl.program_id(0)
    wslot = lax.rem(i, 2)
    rslot = 1 - wslot

    @pl.when(i == 0)
    def _():
        for ref in (extb_ref, extc_ref, s_ref, st_ref):
            ref[...] = jnp.zeros(ref.shape, ref.dtype)
        for ref in (ya_s, yb_s, cpk_s, cg_s, dpk_s, dla_s):
            ref[1] = jnp.zeros(ref.shape[1:], ref.dtype)

    stages = [_gdn_stages(cpk_s.at[rslot], cg_s.at[rslot], nwc_ref, bd_ref, ltri_ref, yc_ref,
                          ac_ref, bb_ref, d_ref, dec_ref, oc_ref, s_ref),
              _gla_stages(dpk_s.at[rslot], dla_s.at[rslot], nwd_ref, bd_ref, ltri_ref, yd_ref, od_ref, st_ref),
              _inproj_stages(x_ref, nw_ref, w_ref, wr_ref, lnw_ref, lnb_ref, ws_ref, bs_ref, scw_ref, dcw_ref,
                             alog_ref, dtb_ref, ea_ref, eb_ref, bd_ref, wg_ref, gb_ref,
                             ya_s.at[wslot], yb_s.at[wslot], cpk_s.at[wslot], cg_s.at[wslot],
                             dpk_s.at[wslot], dla_s.at[wslot], extb_ref, extc_ref)]
    while stages:
        for gen in list(stages):
            try:
                next(gen)
            except StopIteration:
                stages.remove(gen)
    ya_ref[...] = ya_s[rslot]
    yb_ref[...] = yb_s[rslot]


def _front(x, nw, w_stack, w_rest, layer, lnw, lnb, ws, bs_wide, scw, dcw, alog_n, dtb_n, ea, eb, bd, wg_pad, gb,
           nwc, nwd, ltri):
    nb, seq, _ = x.shape
    tm = SEQ_TILE
    n_tiles = seq // tm
    n_chunks = nb * tm // CHUNK
    tile_in = lambda i: (0, jnp.minimum(i, n_tiles - 1), 0)
    tile_out = lambda i: (0, jnp.maximum(i - 1, 0), 0)
    whole = lambda a: pl.BlockSpec(a.shape, (lambda i: (0,) * a.ndim))
    per_layer = lambda a: _layer_param(a, layer)
    out = jax.ShapeDtypeStruct((nb, seq, GROUP), BF16)
    slot = lambda width, dt: pltpu.VMEM((2, nb, tm, width), dt)
    return pl.pallas_call(
        _front_kernel,
        grid=(n_tiles + 1,),
        in_specs=[pl.BlockSpec((nb, tm, D_MODEL), tile_in),
                  per_layer(nw),
                  _layer_block(w_stack, layer),
                  _layer_block(w_rest, layer),
                  per_layer(lnw), per_layer(lnb), per_layer(ws), per_layer(bs_wide), per_layer(scw),
                  per_layer(dcw), per_layer(alog_n), per_layer(dtb_n), whole(ea), whole(eb), whole(bd),
                  per_layer(wg_pad), per_layer(gb), per_layer(nwc), per_layer(nwd), whole(ltri)],
        out_specs=[pl.BlockSpec((nb, tm, GROUP), tile_out)] * 4,
        out_shape=[out] * 4,
        scratch_shapes=[pltpu.VMEM((nb, TAIL + tm, GROUP), F32),
                        pltpu.VMEM((nb, TAIL + tm, 3 * GROUP), F32),
                        slot(GROUP, BF16), slot(GROUP, BF16),
                        slot(C_PACK, BF16), slot(GROUP, F32), slot(D_PACK, BF16), slot(GROUP, F32),
                        pltpu.VMEM((n_chunks, 2 * CHUNK, GROUP), BF16),
                        pltpu.VMEM((n_chunks, CHUNK, GROUP), F32),
                        pltpu.VMEM((n_chunks, CHUNK, GROUP), F32),
                        pltpu.VMEM((n_chunks, TAIL, GROUP), F32),
                        pltpu.VMEM((nb, tm, GROUP), F32),
                        pltpu.VMEM((nb, CHUNK, GROUP), F32),
                        pltpu.VMEM((nb, tm, GROUP), F32),
                        pltpu.VMEM((nb, GROUP, GROUP), F32)],
        compiler_params=pltpu.CompilerParams(dimension_semantics=("arbitrary",),
                                             vmem_limit_bytes=VMEM_LIMIT),
        name="front",
    )(x, nw, w_stack, w_rest, lnw, lnb, ws, bs_wide, scw, dcw, alog_n, dtb_n, ea, eb, bd, wg_pad, gb,
      nwc, nwd, ltri)


def _outproj_ffn_kernel(final, x_ref, ya_ref, yb_ref, yc_ref, yd_ref, wo_ref, n2_ref, wgu_ref, wd_ref,
                        fn_ref, o_ref, act_ref):
    x1 = x_ref[...]
    for idx, y_ref in enumerate((ya_ref, yb_ref, yc_ref, yd_ref)):
        x1 = x1 + _dot(y_ref[...], wo_ref[idx * GROUP:(idx + 1) * GROUP, :])
    ms = jnp.mean(x1 * x1, axis=-1, keepdims=True)
    h = (x1 * lax.rsqrt(ms + EPS) * n2_ref[...]).astype(BF16)
    for j in range(D_FF // FF_CHUNK):
        cols = slice(j * FF_CHUNK, (j + 1) * FF_CHUNK)
        gate = _dot(h, wgu_ref[:, cols])
        up = _dot(h, wgu_ref[:, D_FF + j * FF_CHUNK:D_FF + (j + 1) * FF_CHUNK])
        act_ref[:, cols] = (_silu(gate) * up).astype(BF16)
    x2 = x1 + _dot(act_ref[...], wd_ref[...])
    if final:
        ms2 = jnp.mean(x2 * x2, axis=-1, keepdims=True)
        x2 = x2 * lax.rsqrt(ms2 + EPS) * fn_ref[...]
    o_ref[...] = x2


def _outproj_ffn(x, ys, wo, n2, wgu, wd, fn, layer, final):
    m = x.shape[0]
    row = lambda i: (i, 0)
    fixed = lambda i: (0, 0)
    resident = lambda a: _layer_block(a, layer)
    return pl.pallas_call(
        functools.partial(_outproj_ffn_kernel, final),
        grid=(m // ROW_TILE,),
        in_specs=[pl.BlockSpec((ROW_TILE, D_MODEL), row)]
                 + [pl.BlockSpec((ROW_TILE, GROUP), row)] * 4
                 + [resident(wo), _layer_param(n2, layer), resident(wgu), resident(wd),
                    pl.BlockSpec((1, D_MODEL), fixed)],
        out_specs=pl.BlockSpec((ROW_TILE, D_MODEL), row),
        out_shape=jax.ShapeDtypeStruct((m, D_MODEL), F32),
        scratch_shapes=[pltpu.VMEM((ROW_TILE, D_FF), BF16)],
        compiler_params=pltpu.CompilerParams(dimension_semantics=("arbitrary",),
                                             vmem_limit_bytes=VMEM_LIMIT),
        name="outproj_ffn",
    )(x, *ys, wo, n2, wgu, wd, fn)


def _repack_w_in_rest(w):
    c_a = MAIN_COLS
    c_z = c_a + 2 * HEADS
    d0 = c_z + GROUP
    d_g = d0 + 3 * GROUP
    d_z = d_g + GATE_RANK
    pad = jnp.zeros(w.shape[:2] + (GATE_COLS - 2 * HEADS - GATE_RANK,), w.dtype)
    out = jnp.concatenate(
        [w[..., c_z:d_g], w[..., d_z:d_z + GROUP], w[..., c_a:c_z], w[..., d_g:d_z], pad], axis=-1)
    assert out.shape[-1] == REST_COLS
    return out


def _gate_vec(v):
    return jnp.pad(v.astype(F32), ((0, 0), (0, GATE_COLS - HEADS)))[:, None, :]


def _row(v):
    return v[:, None, :]


def _gate_selectors():
    ea = np.zeros((GATE_COLS, GROUP), np.float32)
    eb = np.zeros((GATE_COLS, GROUP), np.float32)
    for h in range(HEADS):
        ea[h, h * HEAD_DIM:(h + 1) * HEAD_DIM] = 1.0
        eb[HEADS + h, h * HEAD_DIM:(h + 1) * HEAD_DIM] = 1.0
    return jnp.asarray(ea, BF16), jnp.asarray(eb, BF16)


def kernel(x, norm1_w, w_in, sgu_ln_w, sgu_ln_b, sgu_w_spatial, sgu_b_spatial, sc_conv_w, dn_conv_w, dn_a_log, dn_dt_bias, dn_norm_w, gla_w_gate2, gla_gate_bias, gla_norm_w, w_out, norm2_w, w_gate_up, w_down, final_norm_w):
    bsz, seq, d = x.shape
    depth = w_in.shape[0]
    assert seq % ROW_TILE == 0 and seq % SEQ_TILE == 0 and d == D_MODEL
    m = bsz * seq
    xf = x.reshape(m, d)
    bd, ltri = _const_inputs()
    ea, eb = _gate_selectors()
    w_out_b, w_gate_up_b, w_down_b = (w.astype(BF16) for w in (w_out, w_gate_up, w_down))
    w_in_b = w_in[..., 0:MAIN_COLS].astype(BF16)
    w_in_rest = _repack_w_in_rest(w_in).astype(BF16)
    bs_wide = jnp.repeat(jnp.swapaxes(sgu_b_spatial, 1, 2), HEAD_DIM, axis=2)
    wg_pad = jnp.pad(gla_w_gate2.astype(BF16),
                     ((0, 0), (2 * HEADS, GATE_COLS - 2 * HEADS - GATE_RANK), (0, 0)))
    front_params = (_row(sgu_ln_w), _row(sgu_ln_b), sgu_w_spatial, bs_wide, sc_conv_w, dn_conv_w,
                    _gate_vec(dn_a_log), _gate_vec(dn_dt_bias), ea, eb, bd, wg_pad, _row(gla_gate_bias),
                    _row(jnp.tile(dn_norm_w, (1, HEADS))), _row(jnp.tile(gla_norm_w, (1, HEADS))), ltri)
    norm1, norm2 = _row(norm1_w), _row(norm2_w)
    for l in range(depth):
        ya, yb, yc, yd = _front(xf.reshape(bsz, seq, d), norm1, w_in_b, w_in_rest, l, *front_params)
        xf = _outproj_ffn(xf, tuple(y.reshape(m, GROUP) for y in (ya, yb, yc, yd)),
                          w_out_b, norm2, w_gate_up_b, w_down_b, final_norm_w[None, :],
                          layer=l, final=(l == depth - 1))
    return xf.reshape(bsz, seq, d)
```

```python
import functools

import numpy as np
import jax
import jax.numpy as jnp
from jax import lax
from jax.experimental import pallas as pl
from jax.experimental.pallas import tpu as pltpu

F32 = jnp.float32
BF16 = jnp.bfloat16

D_MODEL = 1024
GROUP = 256
HEADS = 4
HEAD_DIM = 64
SGU_CHUNK = 128
SC_WIDTH = 3
DN_CONV_WIDTH = 4
CHUNK = 64
GATE_RANK = 16
GATE_TEMP = 16.0
D_FF = 2816
EPS = 1e-6
GATE_COLS = 128
TAIL = 8

ROW_TILE = 1024
SEQ_TILE = 256
GDN_GROUPS = 1
FF_CHUNK = 256
VMEM_LIMIT = 60 * 1024 * 1024

MAIN_A = 0
MAIN_B = MAIN_A + 2 * GROUP
MAIN_C = MAIN_B + 3 * GROUP
MAIN_COLS = MAIN_C + 3 * GROUP
REST_CZ = 0
REST_D = REST_CZ + GROUP
REST_G = REST_D + 4 * GROUP
REST_COLS = REST_G + GATE_COLS
C_PACK = 5 * GROUP
D_PACK = 4 * GROUP


def _dot(a, b):
    return jnp.dot(a, b, preferred_element_type=F32)


def _dot_nt(a, b):
    return lax.dot_general(a, b, (((1,), (1,)), ((), ())), preferred_element_type=F32)


def _dot_tn(a, b):
    return lax.dot_general(a, b, (((0,), (0,)), ((), ())), preferred_element_type=F32)


def _split2(x):
    hi = x.astype(BF16)
    lo = (x - hi.astype(F32)).astype(BF16)
    return hi, lo


def _dot_sel_rhs(x, sel):
    hi, lo = _split2(x)
    return _dot(hi, sel) + _dot(lo, sel)


def _dot_sel_lhs(sel, x):
    hi, lo = _split2(x)
    return _dot(sel, hi) + _dot(sel, lo)


def _softplus(x):
    return jnp.maximum(x, 0.0) + jnp.log(1.0 + jnp.exp(-jnp.abs(x)))


def _sigmoid(x):
    return 0.5 + 0.5 * jnp.tanh(0.5 * x)


def _silu(x):
    h = 0.5 * x
    return h + h * jnp.tanh(h)


def _gelu_tanh(x):
    h = 0.5 * x
    return h + h * jnp.tanh(x * (0.7978845608028654 + (0.7978845608028654 * 0.044715) * (x * x)))


def _blockdiag(x, mask):
    xb = x.astype(BF16)
    return jnp.concatenate([xb] * HEADS, axis=0) * mask


def _head_mean_sq(x, bones):
    return _dot((x * x).astype(BF16), bones) * (1.0 / HEAD_DIM)


def _causal_conv(x, ext_ref, cw_ref, width):
    tm = x.shape[0]
    ext_ref[TAIL:TAIL + tm, :] = x
    acc = x * cw_ref[width - 1:width, :]
    for j in range(1, width):
        acc = acc + ext_ref[TAIL - j:TAIL - j + tm, :] * cw_ref[width - 1 - j:width - j, :]
    ext_ref[0:TAIL, :] = x[tm - TAIL:tm, :]
    return acc


def _layer_param(stacked, layer):
    zeros = (0,) * (stacked.ndim - 1)
    return pl.BlockSpec((None,) + stacked.shape[1:], lambda i: (layer,) + zeros)


def _layer_block(stacked, layer):
    zeros = (0,) * (stacked.ndim - 1)
    return pl.BlockSpec((None,) + stacked.shape[1:], lambda i: (layer,) + zeros,
                        pipeline_mode=pl.Buffered(1))


def _resident(a):
    return pl.BlockSpec(a.shape, lambda i: (0, 0), pipeline_mode=pl.Buffered(1))


CAST_BLOCKS = 16


def _cast_specs(cast_jobs):
    last = CAST_BLOCKS - 1
    in_specs, out_specs, out_shapes = [], [], []
    for stacked, layer, n_cols in cast_jobs:
        rows = stacked.shape[1]
        assert rows % (16 * CAST_BLOCKS) == 0 and n_cols % 128 == 0
        blk = rows // CAST_BLOCKS
        in_specs.append(pl.BlockSpec((None, blk, n_cols),
                                     lambda i, layer=layer: (layer, jnp.minimum(i, last), 0)))
        out_specs.append(pl.BlockSpec((blk, n_cols), lambda i: (jnp.minimum(i, last), 0)))
        out_shapes.append(jax.ShapeDtypeStruct((rows, n_cols), BF16))
    return in_specs, out_specs, out_shapes


def _run_casts(cast_in, cast_out):
    for src, dst in zip(cast_in, cast_out):
        dst[...] = src[...].astype(dst.dtype)


def _split_refs(refs, *counts):
    assert sum(counts) == len(refs)
    groups, pos = [], 0
    for c in counts:
        groups.append(tuple(refs[pos:pos + c]))
        pos += c
    return groups


def _inproj_stages(x_ref, nw_ref, w_ref, wr_ref, lnw_ref, lnb_ref, ws_ref, bs_ref, scw_ref,
                   dcw_ref, alog_ref, dtb_ref, ea_ref, eb_ref, bd_ref, wg_ref, gb_ref,
                   ya_ref, yb_ref, cpk_ref, cg_ref, dpk_ref, dla_ref, extb_ref, extc_ref):
    nb, tm = x_ref.shape[0], x_ref.shape[1]
    m = nb * tm
    batch_rows = [slice(b * tm, (b + 1) * tm) for b in range(nb)]

    x = x_ref[...].reshape(m, D_MODEL)
    ms = jnp.mean(x * x, axis=-1, keepdims=True)
    h = (x * lax.rsqrt(ms + EPS) * nw_ref[...]).astype(BF16)
    bd = bd_ref[...]

    def project(ref, col0, n_slabs):
        parts = []
        for c in range(col0, col0 + n_slabs * GROUP, GROUP):
            parts.append(_dot(h, ref[:, c:c + GROUP]))
            yield
        return jnp.concatenate(parts, axis=1)

    pa = yield from project(w_ref, MAIN_A, 2)
    pc = yield from project(w_ref, MAIN_C, 3)
    pcz = yield from project(wr_ref, REST_CZ, 1)
    pg = _dot(h, wr_ref[:, REST_G:REST_G + GATE_COLS])
    pb = yield from project(w_ref, MAIN_B, 3)

    tri = (lax.broadcasted_iota(jnp.int32, (SGU_CHUNK, SGU_CHUNK), 0)
           >= lax.broadcasted_iota(jnp.int32, (SGU_CHUNK, SGU_CHUNK), 1))
    head_of_lane = lax.broadcasted_iota(jnp.int32, (SGU_CHUNK, GROUP), 1) // HEAD_DIM
    ws = [jnp.where(tri, ws_ref[hd], 0.0).astype(BF16) for hd in range(HEADS)]
    for b in range(nb):
        for c in range(tm // SGU_CHUNK):
            rows = slice(b * tm + c * SGU_CHUNK, b * tm + (c + 1) * SGU_CHUNK)
            u = _gelu_tanh(pa[rows, 0:GROUP])
            v = _gelu_tanh(pa[rows, GROUP:2 * GROUP])
            mu = jnp.mean(v, axis=-1, keepdims=True)
            vc = v - mu
            var = jnp.mean(vc * vc, axis=-1, keepdims=True)
            vn = vc * lax.rsqrt(var + EPS) * lnw_ref[...] + lnb_ref[...]
            mixed = bs_ref[...]
            for hd in range(HEADS):
                mixed = mixed + _dot(ws[hd], jnp.where(head_of_lane == hd, vn, 0.0).astype(BF16))
            ya_ref[b, c * SGU_CHUNK:(c + 1) * SGU_CHUNK, :] = (u * mixed).astype(ya_ref.dtype)
            yield

    pd = yield from project(wr_ref, REST_D, 4)

    for b, rows in enumerate(batch_rows):
        conv_b = _causal_conv(pb[rows, GROUP:2 * GROUP] * pb[rows, 2 * GROUP:3 * GROUP],
                              extb_ref.at[b], scw_ref, SC_WIDTH)
        yb_ref[b] = (pb[rows, 0:GROUP] * conv_b).astype(yb_ref.dtype)

    qkv = jnp.concatenate(
        [_silu(_causal_conv(pc[rows, 0:3 * GROUP], extc_ref.at[b], dcw_ref, DN_CONV_WIDTH))
         for b, rows in enumerate(batch_rows)], axis=0)
    q = qkv[:, 0:GROUP]
    k = qkv[:, GROUP:2 * GROUP]
    qn = q * lax.rsqrt(_dot((q * q).astype(BF16), bd) + EPS) * (HEAD_DIM ** -0.5)
    kn = k * lax.rsqrt(_dot((k * k).astype(BF16), bd) + EPS)
    yield
    g_narrow = -jnp.exp(alog_ref[...]) * _softplus(pg + dtb_ref[...])
    g_wide = _dot_sel_rhs(g_narrow, ea_ref[...])
    beta = _dot(_sigmoid(pg).astype(BF16), eb_ref[...])
    zg_c = _silu(pcz)
    yield
    pre = _dot(pg.astype(BF16), wg_ref[...]) + gb_ref[...]
    log_a = -_softplus(-pre) * (1.0 / GATE_TEMP)
    for b, rows in enumerate(batch_rows):
        cg_ref[b] = g_wide[rows]
        cpk_ref[b, :, 0:GROUP] = qn[rows].astype(BF16)
        cpk_ref[b, :, GROUP:2 * GROUP] = kn[rows].astype(BF16)
        cpk_ref[b, :, 2 * GROUP:3 * GROUP] = qkv[rows, 2 * GROUP:3 * GROUP].astype(BF16)
        cpk_ref[b, :, 3 * GROUP:4 * GROUP] = beta[rows].astype(BF16)
        cpk_ref[b, :, 4 * GROUP:5 * GROUP] = zg_c[rows].astype(BF16)
        dla_ref[b] = log_a[rows]
        dpk_ref[b, :, 0:3 * GROUP] = pd[rows, 0:3 * GROUP].astype(BF16)
        dpk_ref[b, :, 3 * GROUP:4 * GROUP] = _silu(pd[rows, 3 * GROUP:4 * GROUP]).astype(BF16)


def _chunk_masks():
    i = lax.broadcasted_iota(jnp.int32, (CHUNK, GROUP), 0)
    j = lax.broadcasted_iota(jnp.int32, (CHUNK, GROUP), 1) % CHUNK
    return i == j, i >= j, i > j


def _const_inputs():
    lane_head = np.arange(GROUP) // HEAD_DIM
    bd = (lane_head[:, None] == lane_head[None, :]).astype(np.float32)
    ltri = np.tril(np.ones((CHUNK, CHUNK), np.float32))
    return jnp.asarray(bd, BF16), jnp.asarray(ltri, BF16)


def _chunk_rows(c):
    return slice(c * CHUNK, (c + 1) * CHUNK)


def _gdn_stages(cpk_ref, cg_ref, nw_ref, bd_ref, ltri_ref, y_ref, ac_ref, bb_ref, d_ref, dec_ref, o_ref, s_ref):
    nb, tm = cpk_ref.shape[0], cpk_ref.shape[1]
    cps = tm // CHUNK
    eye_w, causal_w, strict_w = _chunk_masks()
    eye_f = eye_w.astype(F32)
    ltri = ltri_ref[...]
    bd = bd_ref[...]

    all_items = [(b, c) for b in range(nb) for c in range(cps)]
    per_group = len(all_items) // GDN_GROUPS
    for g in range(GDN_GROUPS):
        yield from _gdn_prepare(all_items[g * per_group:(g + 1) * per_group], cps, cpk_ref, cg_ref,
                                ac_ref, bb_ref, d_ref, dec_ref, eye_w, causal_w, strict_w, eye_f, ltri, bd)

    for c in range(cps):
        for b in range(nb):
            i = b * cps + c
            s = s_ref[b]
            r = _dot(ac_ref[i], _blockdiag(s, bd))
            s_ref[b] = s * dec_ref[i, 0:1, :] + r[0:CHUNK] + bb_ref[i]
            o_ref[b, _chunk_rows(c), :] = r[CHUNK:2 * CHUNK] + d_ref[i]
        yield

    for b in range(nb):
        o = o_ref[b]
        zg = cpk_ref[b, :, 4 * GROUP:5 * GROUP].astype(F32)
        y_ref[b] = (o * lax.rsqrt(_head_mean_sq(o, bd) + EPS) * nw_ref[...] * zg).astype(y_ref.dtype)


def _gdn_prepare(items, cps, cpk_ref, cg_ref, ac_ref, bb_ref, d_ref, dec_ref,
                 eye_w, causal_w, strict_w, eye_f, ltri, bd):
    n = len(items)

    def load(col, i):
        b, c = items[i]
        return cpk_ref[b, _chunk_rows(c), col * GROUP:(col + 1) * GROUP]

    qn = [load(0, i) for i in range(n)]
    kn = [load(1, i) for i in range(n)]
    beta = [load(3, i) for i in range(n)]
    gc = [_dot_sel_lhs(ltri, cg_ref[b, _chunk_rows(c), :]) for b, c in items]
    yield
    eg = [jnp.exp(x) for x in gc]
    kb = [kn[i] * beta[i] for i in range(n)]
    eye_b = eye_f.astype(BF16)
    r1 = [_dot_nt(jnp.concatenate([kb[i], qn[i], eye_b], axis=0), _blockdiag(kn[i], bd))
          for i in range(n)]
    yield
    low, attn, g_row = [], [], []
    for i in range(n):
        g_row.append(jnp.sum(jnp.where(eye_w, gc[i], 0.0), axis=0, keepdims=True))
        decay = jnp.where(causal_w, jnp.exp(jnp.minimum(gc[i] - g_row[i], 0.0)), 0.0)
        low.append(jnp.where(strict_w, r1[i][0:CHUNK] * decay, 0.0))
        attn.append((r1[i][CHUNK:2 * CHUNK] * decay).astype(BF16))
    t = [eye_f - x for x in low]
    p = [_dot(x.astype(BF16), _blockdiag(x, bd)) for x in low]
    yield
    n_levels = CHUNK.bit_length() - 1
    for level in range(1, n_levels):
        if level < n_levels - 1:
            r2 = [_dot(jnp.concatenate([p[i], t[i]], axis=0).astype(BF16), _blockdiag(p[i], bd))
                  for i in range(n)]
            p = [x[0:CHUNK] for x in r2]
            t = [t[i] + r2[i][CHUNK:2 * CHUNK] for i in range(n)]
        else:
            t = [t[i] + _dot(t[i].astype(BF16), _blockdiag(p[i], bd)) for i in range(n)]
        yield
    wu = [_dot(t[i].astype(BF16),
               jnp.concatenate([_blockdiag(kb[i] * eg[i], bd), _blockdiag(load(2, i) * beta[i], bd)], axis=1))
          for i in range(n)]
    yield
    gc_last = [x[CHUNK - 1:CHUNK, :] for x in gc]
    kd_t = [(r1[i][2 * CHUNK:3 * CHUNK] * jnp.exp(gc_last[i] - g_row[i])).astype(BF16) for i in range(n)]
    prod = [_dot(jnp.concatenate([attn[i], kd_t[i]], axis=0),
                 jnp.concatenate([_blockdiag(wu[i][:, 0:GROUP], bd),
                                  _blockdiag(wu[i][:, GROUP:2 * GROUP], bd)], axis=1))
            for i in range(n)]
    for i, (b, c) in enumerate(items):
        j = b * cps + c
        ac_ref[j, 0:CHUNK, :] = (-prod[i][CHUNK:2 * CHUNK, 0:GROUP]).astype(BF16)
        bb_ref[j] = prod[i][CHUNK:2 * CHUNK, GROUP:2 * GROUP]
        ac_ref[j, CHUNK:2 * CHUNK, :] = (qn[i] * eg[i] - prod[i][0:CHUNK, 0:GROUP]).astype(BF16)
        d_ref[j] = prod[i][0:CHUNK, GROUP:2 * GROUP]
        dec_ref[j] = jnp.broadcast_to(jnp.exp(gc_last[i]), (TAIL, GROUP))
    yield


def _gla_stages(dpk_ref, dla_ref, nw_ref, bd_ref, ltri_ref, y_ref, o_ref, st_ref):
    nb, tm = dpk_ref.shape[0], dpk_ref.shape[1]
    cps = tm // CHUNK
    _, causal_w, _ = _chunk_masks()
    ltri = ltri_ref[...]
    bd = bd_ref[...]
    bdf = bd.astype(F32)
    mid = CHUNK // 2

    items = [(b, c) for c in range(cps) for b in range(nb)]
    n = len(items)
    q = [dpk_ref[b, _chunk_rows(c), 0:GROUP].astype(F32) * (HEAD_DIM ** -0.5) for b, c in items]
    k = [dpk_ref[b, _chunk_rows(c), GROUP:2 * GROUP].astype(F32) for b, c in items]
    v = [dpk_ref[b, _chunk_rows(c), 2 * GROUP:3 * GROUP] for b, c in items]
    gcum = [_dot_sel_lhs(ltri, dla_ref[b, _chunk_rows(c), :]) for b, c in items]
    yield
    g_mid = [x[mid:mid + 1, :] for x in gcum]
    g_last = [x[CHUNK - 1:CHUNK, :] for x in gcum]
    attn = [jnp.where(causal_w,
                      _dot_nt((q[i] * jnp.exp(gcum[i] - g_mid[i])).astype(BF16),
                              _blockdiag(k[i] * jnp.exp(g_mid[i] - gcum[i]), bd)), 0.0).astype(BF16)
            for i in range(n)]
    yield
    upd = [bdf * _dot_tn(v[i], (k[i] * jnp.exp(g_last[i] - gcum[i])).astype(BF16)) for i in range(n)]
    yield
    o_intra = [_dot(attn[i], _blockdiag(v[i], bd)) for i in range(n)]
    qg = [(q[i] * jnp.exp(gcum[i])).astype(BF16) for i in range(n)]
    yield
    for i, (b, c) in enumerate(items):
        st = st_ref[b]
        o_ref[b, _chunk_rows(c), :] = o_intra[i] + _dot_nt(qg[i], st.astype(BF16))
        st_ref[b] = st * jnp.exp(g_last[i]) + upd[i]
        if b == nb - 1:
            yield

    for b in range(nb):
        o = o_ref[b]
        zg = dpk_ref[b, :, 3 * GROUP:4 * GROUP].astype(F32)
        y_ref[b] = (o * lax.rsqrt(_head_mean_sq(o, bd) + EPS) * nw_ref[...] * zg).astype(y_ref.dtype)


def _front_kernel(n_cast, x_ref, nw_ref, w_ref, wr_ref, lnw_ref, lnb_ref, ws_ref, bs_ref, scw_ref, dcw_ref,
                  alog_ref, dtb_ref, ea_ref, eb_ref, bd_ref, wg_ref, gb_ref, nwc_ref, nwd_ref, ltri_ref, *rest):
    cast_in, (ya_ref, yb_ref, yc_ref, yd_ref), cast_out, scratch = _split_refs(
        rest, n_cast, 4, n_cast, len(rest) - 2 * n_cast - 4)
    (extb_ref, extc_ref, ya_s, yb_s, cpk_s, cg_s, dpk_s, dla_s,
     ac_ref, bb_ref, d_ref, dec_ref, oc_ref, s_ref, od_ref, st_ref) = scratch
    _run_casts(cast_in, cast_out)
    i = pl.program_id(0)
    wslot = lax.rem(i, 2)
    rslot = 1 - wslot

    @pl.when(i == 0)
    def _():
        for ref in (extb_ref, extc_ref, s_ref, st_ref):
            ref[...] = jnp.zeros(ref.shape, ref.dtype)
        for ref in (ya_s, yb_s, cpk_s, cg_s, dpk_s, dla_s):
            ref[1] = jnp.zeros(ref.shape[1:], ref.dtype)

    stages = [_gdn_stages(cpk_s.at[rslot], cg_s.at[rslot], nwc_ref, bd_ref, ltri_ref, yc_ref,
                          ac_ref, bb_ref, d_ref, dec_ref, oc_ref, s_ref),
              _gla_stages(dpk_s.at[rslot], dla_s.at[rslot], nwd_ref, bd_ref, ltri_ref, yd_ref, od_ref, st_ref),
              _inproj_stages(x_ref, nw_ref, w_ref, wr_ref, lnw_ref, lnb_ref, ws_ref, bs_ref, scw_ref, dcw_ref,
                             alog_ref, dtb_ref, ea_ref, eb_ref, bd_ref, wg_ref, gb_ref,
                             ya_s.at[wslot], yb_s.at[wslot], cpk_s.at[wslot], cg_s.at[wslot],
                             dpk_s.at[wslot], dla_s.at[wslot], extb_ref, extc_ref)]
    while stages:
        for gen in list(stages):
            try:
                next(gen)
            except StopIteration:
                stages.remove(gen)
    ya_ref[...] = ya_s[rslot]
    yb_ref[...] = yb_s[rslot]


def _front(x, nw, w_main, w_rest, layer, lnw, lnb, ws, bs_wide, scw, dcw, alog_n, dtb_n, ea, eb, bd, wg_pad, gb,
           nwc, nwd, ltri, cast_jobs=()):
    nb, seq, _ = x.shape
    tm = SEQ_TILE
    n_tiles = seq // tm
    assert n_tiles >= CAST_BLOCKS or not cast_jobs
    n_chunks = nb * tm // CHUNK
    cast_in_specs, cast_out_specs, cast_out_shapes = _cast_specs(cast_jobs)
    tile_in = lambda i: (0, jnp.minimum(i, n_tiles - 1), 0)
    tile_out = lambda i: (0, jnp.maximum(i - 1, 0), 0)
    whole = lambda a: pl.BlockSpec(a.shape, (lambda i: (0,) * a.ndim))
    per_layer = lambda a: _layer_param(a, layer)
    out = jax.ShapeDtypeStruct((nb, seq, GROUP), BF16)
    slot = lambda width, dt: pltpu.VMEM((2, nb, tm, width), dt)
    return pl.pallas_call(
        functools.partial(_front_kernel, len(cast_jobs)),
        grid=(n_tiles + 1,),
        in_specs=[pl.BlockSpec((nb, tm, D_MODEL), tile_in),
                  per_layer(nw),
                  _resident(w_main),
                  _layer_block(w_rest, layer),
                  per_layer(lnw), per_layer(lnb), per_layer(ws), per_layer(bs_wide), per_layer(scw),
                  per_layer(dcw), per_layer(alog_n), per_layer(dtb_n), whole(ea), whole(eb), whole(bd),
                  per_layer(wg_pad), per_layer(gb), per_layer(nwc), per_layer(nwd), whole(ltri)]
                 + cast_in_specs,
        out_specs=[pl.BlockSpec((nb, tm, GROUP), tile_out)] * 4 + cast_out_specs,
        out_shape=[out] * 4 + cast_out_shapes,
        scratch_shapes=[pltpu.VMEM((nb, TAIL + tm, GROUP), F32),
                        pltpu.VMEM((nb, TAIL + tm, 3 * GROUP), F32),
                        slot(GROUP, BF16), slot(GROUP, BF16),
                        slot(C_PACK, BF16), slot(GROUP, F32), slot(D_PACK, BF16), slot(GROUP, F32),
                        pltpu.VMEM((n_chunks, 2 * CHUNK, GROUP), BF16),
                        pltpu.VMEM((n_chunks, CHUNK, GROUP), F32),
                        pltpu.VMEM((n_chunks, CHUNK, GROUP), F32),
                        pltpu.VMEM((n_chunks, TAIL, GROUP), F32),
                        pltpu.VMEM((nb, tm, GROUP), F32),
                        pltpu.VMEM((nb, CHUNK, GROUP), F32),
                        pltpu.VMEM((nb, tm, GROUP), F32),
                        pltpu.VMEM((nb, GROUP, GROUP), F32)],
        compiler_params=pltpu.CompilerParams(dimension_semantics=("arbitrary",),
                                             vmem_limit_bytes=VMEM_LIMIT),
        name="front",
    )(x, nw, w_main, w_rest, lnw, lnb, ws, bs_wide, scw, dcw, alog_n, dtb_n, ea, eb, bd, wg_pad, gb,
      nwc, nwd, ltri, *[w for w, _, _ in cast_jobs])


def _outproj_ffn_kernel(final, n_cast, x_ref, ya_ref, yb_ref, yc_ref, yd_ref, wo_ref, n2_ref, wgu_ref, wd_ref,
                        fn_ref, *rest):
    cast_in, (o_ref,), cast_out, (act_ref,) = _split_refs(rest, n_cast, 1, n_cast, 1)
    _run_casts(cast_in, cast_out)
    x1 = x_ref[...]
    for idx, y_ref in enumerate((ya_ref, yb_ref, yc_ref, yd_ref)):
        x1 = x1 + _dot(y_ref[...], wo_ref[idx * GROUP:(idx + 1) * GROUP, :])
    ms = jnp.mean(x1 * x1, axis=-1, keepdims=True)
    h = (x1 * lax.rsqrt(ms + EPS) * n2_ref[...]).astype(BF16)
    for j in range(D_FF // FF_CHUNK):
        cols = slice(j * FF_CHUNK, (j + 1) * FF_CHUNK)
        gate = _dot(h, wgu_ref[:, cols])
        up = _dot(h, wgu_ref[:, D_FF + j * FF_CHUNK:D_FF + (j + 1) * FF_CHUNK])
        act_ref[:, cols] = (_silu(gate) * up).astype(BF16)
    x2 = x1 + _dot(act_ref[...], wd_ref[...])
    if final:
        ms2 = jnp.mean(x2 * x2, axis=-1, keepdims=True)
        x2 = x2 * lax.rsqrt(ms2 + EPS) * fn_ref[...]
    o_ref[...] = x2


def _outproj_ffn(x, ys, wo, n2, wgu, wd, fn, layer, final, cast_jobs=()):
    m = x.shape[0]
    n_steps = m // ROW_TILE
    assert n_steps >= CAST_BLOCKS or not cast_jobs
    row = lambda i: (i, 0)
    fixed = lambda i: (0, 0)
    cast_in_specs, cast_out_specs, cast_out_shapes = _cast_specs(cast_jobs)
    outs = pl.pallas_call(
        functools.partial(_outproj_ffn_kernel, final, len(cast_jobs)),
        grid=(n_steps,),
        in_specs=[pl.BlockSpec((ROW_TILE, D_MODEL), row)]
                 + [pl.BlockSpec((ROW_TILE, GROUP), row)] * 4
                 + [_resident(wo), _layer_param(n2, layer), _resident(wgu), _resident(wd),
                    pl.BlockSpec((1, D_MODEL), fixed)]
                 + cast_in_specs,
        out_specs=[pl.BlockSpec((ROW_TILE, D_MODEL), row)] + cast_out_specs,
        out_shape=[jax.ShapeDtypeStruct((m, D_MODEL), F32)] + cast_out_shapes,
        scratch_shapes=[pltpu.VMEM((ROW_TILE, D_FF), BF16)],
        compiler_params=pltpu.CompilerParams(dimension_semantics=("arbitrary",),
                                             vmem_limit_bytes=VMEM_LIMIT),
        name="outproj_ffn",
    )(x, *ys, wo, n2, wgu, wd, fn, *[w for w, _, _ in cast_jobs])
    return outs


def _repack_w_in_rest(w):
    c_a = MAIN_COLS
    c_z = c_a + 2 * HEADS
    d0 = c_z + GROUP
    d_g = d0 + 3 * GROUP
    d_z = d_g + GATE_RANK
    pad = jnp.zeros(w.shape[:2] + (GATE_COLS - 2 * HEADS - GATE_RANK,), w.dtype)
    out = jnp.concatenate(
        [w[..., c_z:d_g], w[..., d_z:d_z + GROUP], w[..., c_a:c_z], w[..., d_g:d_z], pad], axis=-1)
    assert out.shape[-1] == REST_COLS
    return out


def _gate_vec(v):
    return jnp.pad(v.astype(F32), ((0, 0), (0, GATE_COLS - HEADS)))[:, None, :]


def _row(v):
    return v[:, None, :]


def _gate_selectors():
    ea = np.zeros((GATE_COLS, GROUP), np.float32)
    eb = np.zeros((GATE_COLS, GROUP), np.float32)
    for h in range(HEADS):
        ea[h, h * HEAD_DIM:(h + 1) * HEAD_DIM] = 1.0
        eb[HEADS + h, h * HEAD_DIM:(h + 1) * HEAD_DIM] = 1.0
    return jnp.asarray(ea, BF16), jnp.asarray(eb, BF16)


def kernel(x, norm1_w, w_in, sgu_ln_w, sgu_ln_b, sgu_w_spatial, sgu_b_spatial, sc_conv_w, dn_conv_w, dn_a_log, dn_dt_bias, dn_norm_w, gla_w_gate2, gla_gate_bias, gla_norm_w, w_out, norm2_w, w_gate_up, w_down, final_norm_w):
    bsz, seq, d = x.shape
    depth = w_in.shape[0]
    assert seq % ROW_TILE == 0 and seq % SEQ_TILE == 0 and d == D_MODEL
    m = bsz * seq
    xf = x.reshape(m, d)
    bd, ltri = _const_inputs()
    ea, eb = _gate_selectors()
    w_main = w_in[0, :, 0:MAIN_COLS].astype(BF16)
    w_in_rest = _repack_w_in_rest(w_in).astype(BF16)
    ffn_jobs = lambda layer: [(w_out, layer, D_MODEL), (w_gate_up, layer, 2 * D_FF), (w_down, layer, D_MODEL)]
    bs_wide = jnp.repeat(jnp.swapaxes(sgu_b_spatial, 1, 2), HEAD_DIM, axis=2)
    wg_pad = jnp.pad(gla_w_gate2.astype(BF16),
                     ((0, 0), (2 * HEADS, GATE_COLS - 2 * HEADS - GATE_RANK), (0, 0)))
    front_params = (_row(sgu_ln_w), _row(sgu_ln_b), sgu_w_spatial, bs_wide, sc_conv_w, dn_conv_w,
                    _gate_vec(dn_a_log), _gate_vec(dn_dt_bias), ea, eb, bd, wg_pad, _row(gla_gate_bias),
                    _row(jnp.tile(dn_norm_w, (1, HEADS))), _row(jnp.tile(gla_norm_w, (1, HEADS))), ltri)
    norm1, norm2 = _row(norm1_w), _row(norm2_w)
    ffn_w = None
    for l in range(depth):
        ya, yb, yc, yd, *cast = _front(xf.reshape(bsz, seq, d), norm1, w_main, w_in_rest, l, *front_params,
                                       cast_jobs=ffn_jobs(0) if l == 0 else ())
        if l == 0:
            ffn_w = cast
        last = l == depth - 1
        xf, *cast = _outproj_ffn(xf, tuple(y.reshape(m, GROUP) for y in (ya, yb, yc, yd)),
                                 ffn_w[0], norm2, ffn_w[1], ffn_w[2], final_norm_w[None, :], layer=l, final=last,
                                 cast_jobs=() if last else [(w_in, l + 1, MAIN_COLS)] + ffn_jobs(l + 1))
        if not last:
            w_main, ffn_w = cast[0], cast[1:]
    return xf.reshape(bsz, seq, d)
```

```python
import functools

import numpy as np
import jax
import jax.numpy as jnp
from jax import lax
from jax.experimental import pallas as pl
from jax.experimental.pallas import tpu as pltpu

F32 = jnp.float32
BF16 = jnp.bfloat16

D_MODEL = 1024
GROUP = 256
HEADS = 4
HEAD_DIM = 64
SGU_CHUNK = 128
SC_WIDTH = 3
DN_CONV_WIDTH = 4
CHUNK = 64
GATE_RANK = 16
GATE_TEMP = 16.0
D_FF = 2816
EPS = 1e-6
GATE_COLS = 128
TAIL = 8

ROW_TILE = 1024
SEQ_TILE = 256
GDN_GROUPS = 1
FF_CHUNK = 256
VMEM_LIMIT = 60 * 1024 * 1024

MAIN_A = 0
MAIN_B = MAIN_A + 2 * GROUP
MAIN_C = MAIN_B + 3 * GROUP
MAIN_COLS = MAIN_C + 3 * GROUP
REST_CZ = 0
REST_D = REST_CZ + GROUP
REST_G = REST_D + 4 * GROUP
REST_COLS = REST_G + GATE_COLS
C_PACK = 5 * GROUP
D_PACK = 4 * GROUP


def _dot(a, b):
    return jnp.dot(a, b, preferred_element_type=F32)


def _dot_nt(a, b):
    return lax.dot_general(a, b, (((1,), (1,)), ((), ())), preferred_element_type=F32)


def _dot_tn(a, b):
    return lax.dot_general(a, b, (((0,), (0,)), ((), ())), preferred_element_type=F32)


def _split2(x):
    hi = x.astype(BF16)
    lo = (x - hi.astype(F32)).astype(BF16)
    return hi, lo


def _dot_sel_rhs(x, sel):
    hi, lo = _split2(x)
    return _dot(hi, sel) + _dot(lo, sel)


def _dot_sel_lhs(sel, x):
    hi, lo = _split2(x)
    return _dot(sel, hi) + _dot(sel, lo)


def _softplus(x):
    return jnp.maximum(x, 0.0) + jnp.log(1.0 + jnp.exp(-jnp.abs(x)))


def _sigmoid(x):
    return 0.5 + 0.5 * jnp.tanh(0.5 * x)


def _silu(x):
    h = 0.5 * x
    return h + h * jnp.tanh(h)


def _gelu_tanh(x):
    h = 0.5 * x
    return h + h * jnp.tanh(x * (0.7978845608028654 + (0.7978845608028654 * 0.044715) * (x * x)))


def _blockdiag(x, mask):
    xb = x.astype(BF16)
    return jnp.concatenate([xb] * HEADS, axis=0) * mask


def _head_mean_sq(x, bones):
    return _dot((x * x).astype(BF16), bones) * (1.0 / HEAD_DIM)


def _causal_conv(x, ext_ref, cw_ref, width):
    tm = x.shape[0]
    ext_ref[TAIL:TAIL + tm, :] = x
    acc = x * cw_ref[width - 1:width, :]
    for j in range(1, width):
        acc = acc + ext_ref[TAIL - j:TAIL - j + tm, :] * cw_ref[width - 1 - j:width - j, :]
    ext_ref[0:TAIL, :] = x[tm - TAIL:tm, :]
    return acc


def _layer_param(stacked, layer):
    zeros = (0,) * (stacked.ndim - 1)
    return pl.BlockSpec((None,) + stacked.shape[1:], lambda i: (layer,) + zeros)


def _layer_block(stacked, layer):
    zeros = (0,) * (stacked.ndim - 1)
    return pl.BlockSpec((None,) + stacked.shape[1:], lambda i: (layer,) + zeros,
                        pipeline_mode=pl.Buffered(1))


def _resident(a):
    return pl.BlockSpec(a.shape, lambda i: (0, 0), pipeline_mode=pl.Buffered(1))


CAST_BLOCKS = 16


def _cast_specs(cast_jobs):
    last = CAST_BLOCKS - 1
    in_specs, out_specs, out_shapes = [], [], []
    for stacked, layer, n_cols in cast_jobs:
        rows = stacked.shape[1]
        assert rows % (16 * CAST_BLOCKS) == 0 and n_cols % 128 == 0
        blk = rows // CAST_BLOCKS
        in_specs.append(pl.BlockSpec((None, blk, n_cols),
                                     lambda i, layer=layer: (layer, jnp.minimum(i, last), 0)))
        out_specs.append(pl.BlockSpec((blk, n_cols), lambda i: (jnp.minimum(i, last), 0)))
        out_shapes.append(jax.ShapeDtypeStruct((rows, n_cols), BF16))
    return in_specs, out_specs, out_shapes


def _run_casts(cast_in, cast_out):
    for src, dst in zip(cast_in, cast_out):
        dst[...] = src[...].astype(dst.dtype)


def _split_refs(refs, *counts):
    assert sum(counts) == len(refs)
    groups, pos = [], 0
    for c in counts:
        groups.append(tuple(refs[pos:pos + c]))
        pos += c
    return groups


def _inproj_stages(x_ref, nw_ref, w_ref, wr_ref, lnw_ref, lnb_ref, ws_ref, bs_ref, scw_ref,
                   dcw_ref, alog_ref, dtb_ref, ea_ref, eb_ref, bd_ref, wg_ref, gb_ref,
                   ya_ref, yb_ref, cpk_ref, cg_ref, dpk_ref, dla_ref, extb_ref, extc_ref):
    nb, tm = x_ref.shape[0], x_ref.shape[1]
    m = nb * tm
    batch_rows = [slice(b * tm, (b + 1) * tm) for b in range(nb)]

    x = x_ref[...].reshape(m, D_MODEL)
    ms = jnp.mean(x * x, axis=-1, keepdims=True)
    h = (x * lax.rsqrt(ms + EPS) * nw_ref[...]).astype(BF16)
    bd = bd_ref[...]

    def project(ref, col0, n_slabs):
        parts = []
        for c in range(col0, col0 + n_slabs * GROUP, GROUP):
            parts.append(_dot(h, ref[:, c:c + GROUP]))
            yield
        return jnp.concatenate(parts, axis=1)

    pa = yield from project(w_ref, MAIN_A, 2)
    pc = yield from project(w_ref, MAIN_C, 3)
    pcz = yield from project(wr_ref, REST_CZ, 1)
    pg = _dot(h, wr_ref[:, REST_G:REST_G + GATE_COLS])
    pb = yield from project(w_ref, MAIN_B, 3)

    tri = (lax.broadcasted_iota(jnp.int32, (SGU_CHUNK, SGU_CHUNK), 0)
           >= lax.broadcasted_iota(jnp.int32, (SGU_CHUNK, SGU_CHUNK), 1))
    head_of_lane = lax.broadcasted_iota(jnp.int32, (SGU_CHUNK, GROUP), 1) // HEAD_DIM
    ws = [jnp.where(tri, ws_ref[hd], 0.0).astype(BF16) for hd in range(HEADS)]
    for b in range(nb):
        for c in range(tm // SGU_CHUNK):
            rows = slice(b * tm + c * SGU_CHUNK, b * tm + (c + 1) * SGU_CHUNK)
            u = _gelu_tanh(pa[rows, 0:GROUP])
            v = _gelu_tanh(pa[rows, GROUP:2 * GROUP])
            mu = jnp.mean(v, axis=-1, keepdims=True)
            vc = v - mu
            var = jnp.mean(vc * vc, axis=-1, keepdims=True)
            vn = vc * lax.rsqrt(var + EPS) * lnw_ref[...] + lnb_ref[...]
            mixed = bs_ref[...]
            for hd in range(HEADS):
                mixed = mixed + _dot(ws[hd], jnp.where(head_of_lane == hd, vn, 0.0).astype(BF16))
            ya_ref[b, c * SGU_CHUNK:(c + 1) * SGU_CHUNK, :] = (u * mixed).astype(ya_ref.dtype)
            yield

    pd = yield from project(wr_ref, REST_D, 4)

    for b, rows in enumerate(batch_rows):
        conv_b = _causal_conv(pb[rows, GROUP:2 * GROUP] * pb[rows, 2 * GROUP:3 * GROUP],
                              extb_ref.at[b], scw_ref, SC_WIDTH)
        yb_ref[b] = (pb[rows, 0:GROUP] * conv_b).astype(yb_ref.dtype)

    qkv = jnp.concatenate(
        [_silu(_causal_conv(pc[rows, 0:3 * GROUP], extc_ref.at[b], dcw_ref, DN_CONV_WIDTH))
         for b, rows in enumerate(batch_rows)], axis=0)
    q = qkv[:, 0:GROUP]
    k = qkv[:, GROUP:2 * GROUP]
    qn = q * lax.rsqrt(_dot((q * q).astype(BF16), bd) + EPS) * (HEAD_DIM ** -0.5)
    kn = k * lax.rsqrt(_dot((k * k).astype(BF16), bd) + EPS)
    yield
    g_narrow = -jnp.exp(alog_ref[...]) * _softplus(pg + dtb_ref[...])
    g_wide = _dot_sel_rhs(g_narrow, ea_ref[...])
    beta = _dot(_sigmoid(pg).astype(BF16), eb_ref[...])
    zg_c = _silu(pcz)
    yield
    pre = _dot(pg.astype(BF16), wg_ref[...]) + gb_ref[...]
    log_a = -_softplus(-pre) * (1.0 / GATE_TEMP)
    for b, rows in enumerate(batch_rows):
        cg_ref[b] = g_wide[rows]
        cpk_ref[b, :, 0:GROUP] = qn[rows].astype(BF16)
        cpk_ref[b, :, GROUP:2 * GROUP] = kn[rows].astype(BF16)
        cpk_ref[b, :, 2 * GROUP:3 * GROUP] = qkv[rows, 2 * GROUP:3 * GROUP].astype(BF16)
        cpk_ref[b, :, 3 * GROUP:4 * GROUP] = beta[rows].astype(BF16)
        cpk_ref[b, :, 4 * GROUP:5 * GROUP] = zg_c[rows].astype(BF16)
        dla_ref[b] = log_a[rows]
        dpk_ref[b, :, 0:3 * GROUP] = pd[rows, 0:3 * GROUP].astype(BF16)
        dpk_ref[b, :, 3 * GROUP:4 * GROUP] = _silu(pd[rows, 3 * GROUP:4 * GROUP]).astype(BF16)


def _chunk_masks():
    i = lax.broadcasted_iota(jnp.int32, (CHUNK, GROUP), 0)
    j = lax.broadcasted_iota(jnp.int32, (CHUNK, GROUP), 1) % CHUNK
    return i == j, i >= j, i > j


def _const_inputs():
    lane_head = np.arange(GROUP) // HEAD_DIM
    bd = (lane_head[:, None] == lane_head[None, :]).astype(np.float32)
    ltri = np.tril(np.ones((CHUNK, CHUNK), np.float32))
    return jnp.asarray(bd, BF16), jnp.asarray(ltri, BF16)


def _chunk_rows(c):
    return slice(c * CHUNK, (c + 1) * CHUNK)


def _gdn_stages(cpk_ref, cg_ref, nw_ref, bd_ref, ltri_ref, y_ref, ac_ref, bb_ref, d_ref, dec_ref, o_ref, s_ref):
    nb, tm = cpk_ref.shape[0], cpk_ref.shape[1]
    cps = tm // CHUNK
    eye_w, causal_w, strict_w = _chunk_masks()
    eye_f = eye_w.astype(F32)
    ltri = ltri_ref[...]
    bd = bd_ref[...]

    all_items = [(b, c) for b in range(nb) for c in range(cps)]
    per_group = len(all_items) // GDN_GROUPS
    for g in range(GDN_GROUPS):
        yield from _gdn_prepare(all_items[g * per_group:(g + 1) * per_group], cps, cpk_ref, cg_ref,
                                ac_ref, bb_ref, d_ref, dec_ref, eye_w, causal_w, strict_w, eye_f, ltri, bd)

    for c in range(cps):
        for b in range(nb):
            i = b * cps + c
            s = s_ref[b]
            r = _dot(ac_ref[i], _blockdiag(s, bd))
            s_ref[b] = s * dec_ref[i, 0:1, :] + r[0:CHUNK] + bb_ref[i]
            o_ref[b, _chunk_rows(c), :] = r[CHUNK:2 * CHUNK] + d_ref[i]
        yield

    for b in range(nb):
        o = o_ref[b]
        zg = cpk_ref[b, :, 4 * GROUP:5 * GROUP].astype(F32)
        y_ref[b] = (o * lax.rsqrt(_head_mean_sq(o, bd) + EPS) * nw_ref[...] * zg).astype(y_ref.dtype)


def _gdn_prepare(items, cps, cpk_ref, cg_ref, ac_ref, bb_ref, d_ref, dec_ref,
                 eye_w, causal_w, strict_w, eye_f, ltri, bd):
    n = len(items)

    def load(col, i):
        b, c = items[i]
        return cpk_ref[b, _chunk_rows(c), col * GROUP:(col + 1) * GROUP]

    qn = [load(0, i) for i in range(n)]
    kn = [load(1, i) for i in range(n)]
    beta = [load(3, i) for i in range(n)]
    gc = [_dot_sel_lhs(ltri, cg_ref[b, _chunk_rows(c), :]) for b, c in items]
    yield
    eg = [jnp.exp(x) for x in gc]
    kb = [kn[i] * beta[i] for i in range(n)]
    eye_b = eye_f.astype(BF16)
    r1 = [_dot_nt(jnp.concatenate([kb[i], qn[i], eye_b], axis=0), _blockdiag(kn[i], bd))
          for i in range(n)]
    yield
    low, attn, g_row = [], [], []
    for i in range(n):
        g_row.append(jnp.sum(jnp.where(eye_w, gc[i], 0.0), axis=0, keepdims=True))
        decay = jnp.where(causal_w, jnp.exp(jnp.minimum(gc[i] - g_row[i], 0.0)), 0.0)
        low.append(jnp.where(strict_w, r1[i][0:CHUNK] * decay, 0.0))
        attn.append((r1[i][CHUNK:2 * CHUNK] * decay).astype(BF16))
    t = [eye_f - x for x in low]
    p = [_dot(x.astype(BF16), _blockdiag(x, bd)) for x in low]
    yield
    n_levels = CHUNK.bit_length() - 1
    for level in range(1, n_levels):
        if level < n_levels - 1:
            r2 = [_dot(jnp.concatenate([p[i], t[i]], axis=0).astype(BF16), _blockdiag(p[i], bd))
                  for i in range(n)]
            p = [x[0:CHUNK] for x in r2]
            t = [t[i] + r2[i][CHUNK:2 * CHUNK] for i in range(n)]
        else:
            t = [t[i] + _dot(t[i].astype(BF16), _blockdiag(p[i], bd)) for i in range(n)]
        yield
    wu = [_dot(t[i].astype(BF16),
               jnp.concatenate([_blockdiag(kb[i] * eg[i], bd), _blockdiag(load(2, i) * beta[i], bd)], axis=1))
          for i in range(n)]
    yield
    gc_last = [x[CHUNK - 1:CHUNK, :] for x in gc]
    kd_t = [(r1[i][2 * CHUNK:3 * CHUNK] * jnp.exp(gc_last[i] - g_row[i])).astype(BF16) for i in range(n)]
    prod = [_dot(jnp.concatenate([attn[i], kd_t[i]], axis=0),
                 jnp.concatenate([_blockdiag(wu[i][:, 0:GROUP], bd),
                                  _blockdiag(wu[i][:, GROUP:2 * GROUP], bd)], axis=1))
            for i in range(n)]
    for i, (b, c) in enumerate(items):
        j = b * cps + c
        ac_ref[j, 0:CHUNK, :] = (-prod[i][CHUNK:2 * CHUNK, 0:GROUP]).astype(BF16)
        bb_ref[j] = prod[i][CHUNK:2 * CHUNK, GROUP:2 * GROUP]
        ac_ref[j, CHUNK:2 * CHUNK, :] = (qn[i] * eg[i] - prod[i][0:CHUNK, 0:GROUP]).astype(BF16)
        d_ref[j] = prod[i][0:CHUNK, GROUP:2 * GROUP]
        dec_ref[j] = jnp.broadcast_to(jnp.exp(gc_last[i]), (TAIL, GROUP))
    yield


def _gla_stages(dpk_ref, dla_ref, nw_ref, bd_ref, ltri_ref, y_ref, o_ref, st_ref):
    nb, tm = dpk_ref.shape[0], dpk_ref.shape[1]
    cps = tm // CHUNK
    _, causal_w, _ = _chunk_masks()
    ltri = ltri_ref[...]
    bd = bd_ref[...]
    bdf = bd.astype(F32)
    mid = CHUNK // 2

    items = [(b, c) for c in range(cps) for b in range(nb)]
    n = len(items)
    q = [dpk_ref[b, _chunk_rows(c), 0:GROUP].astype(F32) * (HEAD_DIM ** -0.5) for b, c in items]
    k = [dpk_ref[b, _chunk_rows(c), GROUP:2 * GROUP].astype(F32) for b, c in items]
    v = [dpk_ref[b, _chunk_rows(c), 2 * GROUP:3 * GROUP] for b, c in items]
    gcum = [_dot_sel_lhs(ltri, dla_ref[b, _chunk_rows(c), :]) for b, c in items]
    yield
    g_mid = [x[mid:mid + 1, :] for x in gcum]
    g_last = [x[CHUNK - 1:CHUNK, :] for x in gcum]
    attn = [jnp.where(causal_w,
                      _dot_nt((q[i] * jnp.exp(gcum[i] - g_mid[i])).astype(BF16),
                              _blockdiag(k[i] * jnp.exp(g_mid[i] - gcum[i]), bd)), 0.0).astype(BF16)
            for i in range(n)]
    yield
    upd = [bdf * _dot_tn(v[i], (k[i] * jnp.exp(g_last[i] - gcum[i])).astype(BF16)) for i in range(n)]
    yield
    o_intra = [_dot(attn[i], _blockdiag(v[i], bd)) for i in range(n)]
    qg = [(q[i] * jnp.exp(gcum[i])).astype(BF16) for i in range(n)]
    yield
    for i, (b, c) in enumerate(items):
        st = st_ref[b]
        o_ref[b, _chunk_rows(c), :] = o_intra[i] + _dot_nt(qg[i], st.astype(BF16))
        st_ref[b] = st * jnp.exp(g_last[i]) + upd[i]
        if b == nb - 1:
            yield

    for b in range(nb):
        o = o_ref[b]
        zg = dpk_ref[b, :, 3 * GROUP:4 * GROUP].astype(F32)
        y_ref[b] = (o * lax.rsqrt(_head_mean_sq(o, bd) + EPS) * nw_ref[...] * zg).astype(y_ref.dtype)


def _front_kernel(n_cast, x_ref, nw_ref, w_ref, wr_ref, lnw_ref, lnb_ref, ws_ref, bs_ref, scw_ref, dcw_ref,
                  alog_ref, dtb_ref, ea_ref, eb_ref, bd_ref, wg_ref, gb_ref, nwc_ref, nwd_ref, ltri_ref, *rest):
    cast_in, (ya_ref, yb_ref, yc_ref, yd_ref), cast_out, scratch = _split_refs(
        rest, n_cast, 4, n_cast, len(rest) - 2 * n_cast - 4)
    (extb_ref, extc_ref, ya_s, yb_s, cpk_s, cg_s, dpk_s, dla_s,
     ac_ref, bb_ref, d_ref, dec_ref, oc_ref, s_ref, od_ref, st_ref) = scratch
    _run_casts(cast_in, cast_out)
    i = pl.program_id(0)
    wslot = lax.rem(i, 2)
    rslot = 1 - wslot

    @pl.when(i == 0)
    def _():
        for ref in (extb_ref, extc_ref, s_ref, st_ref):
            ref[...] = jnp.zeros(ref.shape, ref.dtype)
        for ref in (ya_s, yb_s, cpk_s, cg_s, dpk_s, dla_s):
            ref[1] = jnp.zeros(ref.shape[1:], ref.dtype)

    stages = [_gdn_stages(cpk_s.at[rslot], cg_s.at[rslot], nwc_ref, bd_ref, ltri_ref, yc_ref,
                          ac_ref, bb_ref, d_ref, dec_ref, oc_ref, s_ref),
              _gla_stages(dpk_s.at[rslot], dla_s.at[rslot], nwd_ref, bd_ref, ltri_ref, yd_ref, od_ref, st_ref),
              _inproj_stages(x_ref, nw_ref, w_ref, wr_ref, lnw_ref, lnb_ref, ws_ref, bs_ref, scw_ref, dcw_ref,
                             alog_ref, dtb_ref, ea_ref, eb_ref, bd_ref, wg_ref, gb_ref,
                             ya_s.at[wslot], yb_s.at[wslot], cpk_s.at[wslot], cg_s.at[wslot],
                             dpk_s.at[wslot], dla_s.at[wslot], extb_ref, extc_ref)]
    while stages:
        for gen in list(stages):
            try:
                next(gen)
            except StopIteration:
                stages.remove(gen)
    ya_ref[...] = ya_s[rslot]
    yb_ref[...] = yb_s[rslot]


def _front(x, nw, w_main, w_rest, layer, lnw, lnb, ws, bs_wide, scw, dcw, alog_n, dtb_n, ea, eb, bd, wg_pad, gb,
           nwc, nwd, ltri, cast_jobs=()):
    nb, seq, _ = x.shape
    tm = SEQ_TILE
    n_tiles = seq // tm
    assert n_tiles >= CAST_BLOCKS or not cast_jobs
    n_chunks = nb * tm // CHUNK
    cast_in_specs, cast_out_specs, cast_out_shapes = _cast_specs(cast_jobs)
    tile_in = lambda i: (0, jnp.minimum(i, n_tiles - 1), 0)
    tile_out = lambda i: (0, jnp.maximum(i - 1, 0), 0)
    whole = lambda a: pl.BlockSpec(a.shape, (lambda i: (0,) * a.ndim))
    per_layer = lambda a: _layer_param(a, layer)
    out = jax.ShapeDtypeStruct((nb, seq, GROUP), BF16)
    slot = lambda width, dt: pltpu.VMEM((2, nb, tm, width), dt)
    return pl.pallas_call(
        functools.partial(_front_kernel, len(cast_jobs)),
        grid=(n_tiles + 1,),
        in_specs=[pl.BlockSpec((nb, tm, D_MODEL), tile_in),
                  per_layer(nw),
                  _layer_block(w_main, layer),
                  _layer_block(w_rest, layer),
                  per_layer(lnw), per_layer(lnb), per_layer(ws), per_layer(bs_wide), per_layer(scw),
                  per_layer(dcw), per_layer(alog_n), per_layer(dtb_n), whole(ea), whole(eb), whole(bd),
                  per_layer(wg_pad), per_layer(gb), per_layer(nwc), per_layer(nwd), whole(ltri)]
                 + cast_in_specs,
        out_specs=[pl.BlockSpec((nb, tm, GROUP), tile_out)] * 4 + cast_out_specs,
        out_shape=[out] * 4 + cast_out_shapes,
        scratch_shapes=[pltpu.VMEM((nb, TAIL + tm, GROUP), F32),
                        pltpu.VMEM((nb, TAIL + tm, 3 * GROUP), F32),
                        slot(GROUP, BF16), slot(GROUP, BF16),
                        slot(C_PACK, BF16), slot(GROUP, F32), slot(D_PACK, BF16), slot(GROUP, F32),
                        pltpu.VMEM((n_chunks, 2 * CHUNK, GROUP), BF16),
                        pltpu.VMEM((n_chunks, CHUNK, GROUP), F32),
                        pltpu.VMEM((n_chunks, CHUNK, GROUP), F32),
                        pltpu.VMEM((n_chunks, TAIL, GROUP), F32),
                        pltpu.VMEM((nb, tm, GROUP), F32),
                        pltpu.VMEM((nb, CHUNK, GROUP), F32),
                        pltpu.VMEM((nb, tm, GROUP), F32),
                        pltpu.VMEM((nb, GROUP, GROUP), F32)],
        compiler_params=pltpu.CompilerParams(dimension_semantics=("arbitrary",),
                                             vmem_limit_bytes=VMEM_LIMIT),
        name="front",
    )(x, nw, w_main, w_rest, lnw, lnb, ws, bs_wide, scw, dcw, alog_n, dtb_n, ea, eb, bd, wg_pad, gb,
      nwc, nwd, ltri, *[w for w, _, _ in cast_jobs])


def _outproj_ffn_kernel(final, n_cast, x_ref, ya_ref, yb_ref, yc_ref, yd_ref, wo_ref, n2_ref, wgu_ref, wd_ref,
                        fn_ref, *rest):
    cast_in, (o_ref,), cast_out, (act_ref,) = _split_refs(rest, n_cast, 1, n_cast, 1)
    _run_casts(cast_in, cast_out)
    x1 = x_ref[...]
    for idx, y_ref in enumerate((ya_ref, yb_ref, yc_ref, yd_ref)):
        x1 = x1 + _dot(y_ref[...], wo_ref[idx * GROUP:(idx + 1) * GROUP, :])
    ms = jnp.mean(x1 * x1, axis=-1, keepdims=True)
    h = (x1 * lax.rsqrt(ms + EPS) * n2_ref[...]).astype(BF16)
    for j in range(D_FF // FF_CHUNK):
        cols = slice(j * FF_CHUNK, (j + 1) * FF_CHUNK)
        gate = _dot(h, wgu_ref[:, cols])
        up = _dot(h, wgu_ref[:, D_FF + j * FF_CHUNK:D_FF + (j + 1) * FF_CHUNK])
        act_ref[:, cols] = (_silu(gate) * up).astype(BF16)
    x2 = x1 + _dot(act_ref[...], wd_ref[...])
    if final:
        ms2 = jnp.mean(x2 * x2, axis=-1, keepdims=True)
        x2 = x2 * lax.rsqrt(ms2 + EPS) * fn_ref[...]
    o_ref[...] = x2


def _outproj_ffn(x, ys, wo, n2, wgu, wd, fn, layer, final, cast_jobs=()):
    m = x.shape[0]
    n_steps = m // ROW_TILE
    assert n_steps >= CAST_BLOCKS or not cast_jobs
    row = lambda i: (i, 0)
    fixed = lambda i: (0, 0)
    cast_in_specs, cast_out_specs, cast_out_shapes = _cast_specs(cast_jobs)
    outs = pl.pallas_call(
        functools.partial(_outproj_ffn_kernel, final, len(cast_jobs)),
        grid=(n_steps,),
        in_specs=[pl.BlockSpec((ROW_TILE, D_MODEL), row)]
                 + [pl.BlockSpec((ROW_TILE, GROUP), row)] * 4
                 + [_resident(wo), _layer_param(n2, layer), _resident(wgu), _resident(wd),
                    pl.BlockSpec((1, D_MODEL), fixed)]
                 + cast_in_specs,
        out_specs=[pl.BlockSpec((ROW_TILE, D_MODEL), row)] + cast_out_specs,
        out_shape=[jax.ShapeDtypeStruct((m, D_MODEL), F32)] + cast_out_shapes,
        scratch_shapes=[pltpu.VMEM((ROW_TILE, D_FF), BF16)],
        compiler_params=pltpu.CompilerParams(dimension_semantics=("arbitrary",),
                                             vmem_limit_bytes=VMEM_LIMIT),
        name="outproj_ffn",
    )(x, *ys, wo, n2, wgu, wd, fn, *[w for w, _, _ in cast_jobs])
    return outs


def _repack_w_in_rest(w):
    c_a = MAIN_COLS
    c_z = c_a + 2 * HEADS
    d0 = c_z + GROUP
    d_g = d0 + 3 * GROUP
    d_z = d_g + GATE_RANK
    pad = jnp.zeros(w.shape[:2] + (GATE_COLS - 2 * HEADS - GATE_RANK,), w.dtype)
    out = jnp.concatenate(
        [w[..., c_z:d_g], w[..., d_z:d_z + GROUP], w[..., c_a:c_z], w[..., d_g:d_z], pad], axis=-1)
    assert out.shape[-1] == REST_COLS
    return out


def _gate_vec(v):
    return jnp.pad(v.astype(F32), ((0, 0), (0, GATE_COLS - HEADS)))[:, None, :]


def _row(v):
    return v[:, None, :]


def _gate_selectors():
    ea = np.zeros((GATE_COLS, GROUP), np.float32)
    eb = np.zeros((GATE_COLS, GROUP), np.float32)
    for h in range(HEADS):
        ea[h, h * HEAD_DIM:(h + 1) * HEAD_DIM] = 1.0
        eb[HEADS + h, h * HEAD_DIM:(h + 1) * HEAD_DIM] = 1.0
    return jnp.asarray(ea, BF16), jnp.asarray(eb, BF16)


def kernel(x, norm1_w, w_in, sgu_ln_w, sgu_ln_b, sgu_w_spatial, sgu_b_spatial, sc_conv_w, dn_conv_w, dn_a_log, dn_dt_bias, dn_norm_w, gla_w_gate2, gla_gate_bias, gla_norm_w, w_out, norm2_w, w_gate_up, w_down, final_norm_w):
    bsz, seq, d = x.shape
    depth = w_in.shape[0]
    assert seq % ROW_TILE == 0 and seq % SEQ_TILE == 0 and d == D_MODEL
    m = bsz * seq
    xf = x.reshape(m, d)
    bd, ltri = _const_inputs()
    ea, eb = _gate_selectors()
    w_in_main = w_in[..., 0:MAIN_COLS].astype(BF16)
    w_in_rest = _repack_w_in_rest(w_in).astype(BF16)
    ffn_jobs = lambda layer: [(w_out, layer, D_MODEL), (w_gate_up, layer, 2 * D_FF), (w_down, layer, D_MODEL)]
    bs_wide = jnp.repeat(jnp.swapaxes(sgu_b_spatial, 1, 2), HEAD_DIM, axis=2)
    wg_pad = jnp.pad(gla_w_gate2.astype(BF16),
                     ((0, 0), (2 * HEADS, GATE_COLS - 2 * HEADS - GATE_RANK), (0, 0)))
    front_params = (_row(sgu_ln_w), _row(sgu_ln_b), sgu_w_spatial, bs_wide, sc_conv_w, dn_conv_w,
                    _gate_vec(dn_a_log), _gate_vec(dn_dt_bias), ea, eb, bd, wg_pad, _row(gla_gate_bias),
                    _row(jnp.tile(dn_norm_w, (1, HEADS))), _row(jnp.tile(gla_norm_w, (1, HEADS))), ltri)
    norm1, norm2 = _row(norm1_w), _row(norm2_w)
    ffn_w = None
    for l in range(depth):
        ya, yb, yc, yd, *cast = _front(xf.reshape(bsz, seq, d), norm1, w_in_main, w_in_rest, l, *front_params,
                                       cast_jobs=ffn_jobs(0) if l == 0 else ())
        if l == 0:
            ffn_w = cast
        last = l == depth - 1
        xf, *cast = _outproj_ffn(xf, tuple(y.reshape(m, GROUP) for y in (ya, yb, yc, yd)),
                                 ffn_w[0], norm2, ffn_w[1], ffn_w[2], final_norm_w[None, :], layer=l, final=last,
                                 cast_jobs=() if last else ffn_jobs(l + 1))
        if not last:
            ffn_w = cast
    return xf.reshape(bsz, seq, d)
```

```python
import functools

import numpy as np
import jax
import jax.numpy as jnp
from jax import lax
from jax.experimental import pallas as pl
from jax.experimental.pallas import tpu as pltpu

F32 = jnp.float32
BF16 = jnp.bfloat16

D_MODEL = 1024
GROUP = 256
HEADS = 4
HEAD_DIM = 64
SGU_CHUNK = 128
SC_WIDTH = 3
DN_CONV_WIDTH = 4
CHUNK = 64
GATE_RANK = 16
GATE_TEMP = 16.0
D_FF = 2816
EPS = 1e-6
GATE_COLS = 128
TAIL = 8

ROW_TILE = 1024
SEQ_TILE = 512
PROJ_SLABS = 1
GDN_GROUPS = 1
FF_CHUNK = 256
VMEM_LIMIT = 60 * 1024 * 1024

MAIN_A = 0
MAIN_B = MAIN_A + 2 * GROUP
MAIN_C = MAIN_B + 3 * GROUP
MAIN_COLS = MAIN_C + 3 * GROUP
REST_CZ = 0
REST_D = REST_CZ + GROUP
REST_G = REST_D + 4 * GROUP
REST_COLS = REST_G + GATE_COLS
C_PACK = 5 * GROUP
D_PACK = 4 * GROUP


def _dot(a, b):
    return jnp.dot(a, b, preferred_element_type=F32)


def _dot_nt(a, b):
    return lax.dot_general(a, b, (((1,), (1,)), ((), ())), preferred_element_type=F32)


def _dot_tn(a, b):
    return lax.dot_general(a, b, (((0,), (0,)), ((), ())), preferred_element_type=F32)


def _split2(x):
    hi = x.astype(BF16)
    lo = (x - hi.astype(F32)).astype(BF16)
    return hi, lo


def _dot_sel_rhs(x, sel):
    hi, lo = _split2(x)
    return _dot(hi, sel) + _dot(lo, sel)


def _dot_sel_lhs(sel, x):
    hi, lo = _split2(x)
    return _dot(sel, hi) + _dot(sel, lo)


def _softplus(x):
    return jnp.maximum(x, 0.0) + jnp.log(1.0 + jnp.exp(-jnp.abs(x)))


def _sigmoid(x):
    return 0.5 + 0.5 * jnp.tanh(0.5 * x)


def _silu(x):
    h = 0.5 * x
    return h + h * jnp.tanh(h)


def _gelu_tanh(x):
    h = 0.5 * x
    return h + h * jnp.tanh(x * (0.7978845608028654 + (0.7978845608028654 * 0.044715) * (x * x)))


def _blockdiag(x, mask):
    xb = x.astype(BF16)
    return jnp.concatenate([xb] * HEADS, axis=0) * mask


def _head_mean_sq(x, bones):
    return _dot((x * x).astype(BF16), bones) * (1.0 / HEAD_DIM)


def _causal_conv(x, ext_ref, cw_ref, width):
    tm = x.shape[0]
    ext_ref[TAIL:TAIL + tm, :] = x
    acc = x * cw_ref[width - 1:width, :]
    for j in range(1, width):
        acc = acc + ext_ref[TAIL - j:TAIL - j + tm, :] * cw_ref[width - 1 - j:width - j, :]
    ext_ref[0:TAIL, :] = x[tm - TAIL:tm, :]
    return acc


def _layer_param(stacked, layer):
    zeros = (0,) * (stacked.ndim - 1)
    return pl.BlockSpec((None,) + stacked.shape[1:], lambda i: (layer,) + zeros)


def _layer_block(stacked, layer):
    zeros = (0,) * (stacked.ndim - 1)
    return pl.BlockSpec((None,) + stacked.shape[1:], lambda i: (layer,) + zeros,
                        pipeline_mode=pl.Buffered(1))


def _resident(a):
    return pl.BlockSpec(a.shape, lambda i: (0, 0), pipeline_mode=pl.Buffered(1))


CAST_BLOCKS = 16


def _cast_specs(cast_jobs):
    last = CAST_BLOCKS - 1
    in_specs, out_specs, out_shapes = [], [], []
    for stacked, layer, n_cols in cast_jobs:
        rows = stacked.shape[1]
        assert rows % (16 * CAST_BLOCKS) == 0 and n_cols % 128 == 0
        blk = rows // CAST_BLOCKS
        in_specs.append(pl.BlockSpec((None, blk, n_cols),
                                     lambda i, layer=layer: (layer, jnp.minimum(i, last), 0)))
        out_specs.append(pl.BlockSpec((blk, n_cols), lambda i: (jnp.minimum(i, last), 0)))
        out_shapes.append(jax.ShapeDtypeStruct((rows, n_cols), BF16))
    return in_specs, out_specs, out_shapes


def _run_casts(cast_in, cast_out):
    for src, dst in zip(cast_in, cast_out):
        dst[...] = src[...].astype(dst.dtype)


def _split_refs(refs, *counts):
    assert sum(counts) == len(refs)
    groups, pos = [], 0
    for c in counts:
        groups.append(tuple(refs[pos:pos + c]))
        pos += c
    return groups


def _run_stages(stages):
    stages = list(stages)
    while stages:
        for gen in list(stages):
            try:
                next(gen)
            except StopIteration:
                stages.remove(gen)


def _inproj_stages(x_ref, nw_ref, w_ref, wr_ref, lnw_ref, lnb_ref, ws_ref, bs_ref, scw_ref,
                   dcw_ref, alog_ref, dtb_ref, ea_ref, eb_ref, bd_ref, wg_ref, gb_ref,
                   ya_ref, yb_ref, cpk_ref, cg_ref, dpk_ref, dla_ref, extb_ref, extc_ref):
    nb, tm = x_ref.shape[0], x_ref.shape[1]
    m = nb * tm
    batch_rows = [slice(b * tm, (b + 1) * tm) for b in range(nb)]

    x = x_ref[...].reshape(m, D_MODEL)
    ms = jnp.mean(x * x, axis=-1, keepdims=True)
    h = (x * lax.rsqrt(ms + EPS) * nw_ref[...]).astype(BF16)
    bd = bd_ref[...]

    def project(ref, col0, n_slabs):
        parts = []
        for s in range(0, n_slabs, PROJ_SLABS):
            width = min(PROJ_SLABS, n_slabs - s) * GROUP
            c = col0 + s * GROUP
            parts.append(_dot(h, ref[:, c:c + width]))
            yield
        return jnp.concatenate(parts, axis=1)

    pc = yield from project(w_ref, MAIN_C, 3)
    pcz = yield from project(wr_ref, REST_CZ, 1)
    pg = _dot(h, wr_ref[:, REST_G:REST_G + GATE_COLS])
    pa = yield from project(w_ref, MAIN_A, 2)

    qkv = jnp.concatenate(
        [_silu(_causal_conv(pc[rows, 0:3 * GROUP], extc_ref.at[b], dcw_ref, DN_CONV_WIDTH))
         for b, rows in enumerate(batch_rows)], axis=0)
    q = qkv[:, 0:GROUP]
    k = qkv[:, GROUP:2 * GROUP]
    pb = yield from project(w_ref, MAIN_B, 3)
    qn = q * lax.rsqrt(_dot((q * q).astype(BF16), bd) + EPS) * (HEAD_DIM ** -0.5)
    kn = k * lax.rsqrt(_dot((k * k).astype(BF16), bd) + EPS)
    yield
    g_narrow = -jnp.exp(alog_ref[...]) * _softplus(pg + dtb_ref[...])
    g_wide = _dot_sel_rhs(g_narrow, ea_ref[...])
    beta = _dot(_sigmoid(pg).astype(BF16), eb_ref[...])
    zg_c = _silu(pcz)
    pre = _dot(pg.astype(BF16), wg_ref[...]) + gb_ref[...]
    log_a = -_softplus(-pre) * (1.0 / GATE_TEMP)
    yield

    tri = (lax.broadcasted_iota(jnp.int32, (SGU_CHUNK, SGU_CHUNK), 0)
           >= lax.broadcasted_iota(jnp.int32, (SGU_CHUNK, SGU_CHUNK), 1))
    head_of_lane = lax.broadcasted_iota(jnp.int32, (SGU_CHUNK, GROUP), 1) // HEAD_DIM
    ws = [jnp.where(tri, ws_ref[hd], 0.0).astype(BF16) for hd in range(HEADS)]
    for b in range(nb):
        for c in range(tm // SGU_CHUNK):
            rows = slice(b * tm + c * SGU_CHUNK, b * tm + (c + 1) * SGU_CHUNK)
            u = _gelu_tanh(pa[rows, 0:GROUP])
            v = _gelu_tanh(pa[rows, GROUP:2 * GROUP])
            mu = jnp.mean(v, axis=-1, keepdims=True)
            vc = v - mu
            var = jnp.mean(vc * vc, axis=-1, keepdims=True)
            vn = vc * lax.rsqrt(var + EPS) * lnw_ref[...] + lnb_ref[...]
            mixed = bs_ref[...]
            for hd in range(HEADS):
                mixed = mixed + _dot(ws[hd], jnp.where(head_of_lane == hd, vn, 0.0).astype(BF16))
            ya_ref[b, c * SGU_CHUNK:(c + 1) * SGU_CHUNK, :] = (u * mixed).astype(ya_ref.dtype)
            yield

    pd = yield from project(wr_ref, REST_D, 4)

    for b, rows in enumerate(batch_rows):
        conv_b = _causal_conv(pb[rows, GROUP:2 * GROUP] * pb[rows, 2 * GROUP:3 * GROUP],
                              extb_ref.at[b], scw_ref, SC_WIDTH)
        yb_ref[b] = (pb[rows, 0:GROUP] * conv_b).astype(yb_ref.dtype)

    for b, rows in enumerate(batch_rows):
        cg_ref[b] = g_wide[rows]
        cpk_ref[b, :, 0:GROUP] = qn[rows].astype(BF16)
        cpk_ref[b, :, GROUP:2 * GROUP] = kn[rows].astype(BF16)
        cpk_ref[b, :, 2 * GROUP:3 * GROUP] = qkv[rows, 2 * GROUP:3 * GROUP].astype(BF16)
        cpk_ref[b, :, 3 * GROUP:4 * GROUP] = beta[rows].astype(BF16)
        cpk_ref[b, :, 4 * GROUP:5 * GROUP] = zg_c[rows].astype(BF16)
        dla_ref[b] = log_a[rows]
        dpk_ref[b, :, 0:3 * GROUP] = pd[rows, 0:3 * GROUP].astype(BF16)
        dpk_ref[b, :, 3 * GROUP:4 * GROUP] = _silu(pd[rows, 3 * GROUP:4 * GROUP]).astype(BF16)


def _inproj_kernel(n_cast, x_ref, nw_ref, w_ref, wr_ref, lnw_ref, lnb_ref, ws_ref, bs_ref, scw_ref, dcw_ref,
                   alog_ref, dtb_ref, ea_ref, eb_ref, bd_ref, wg_ref, gb_ref, *rest):
    cast_in, outs, cast_out, (extb_ref, extc_ref) = _split_refs(rest, n_cast, 6, n_cast, 2)
    ya_ref, yb_ref, cpk_ref, cg_ref, dpk_ref, dla_ref = outs
    _run_casts(cast_in, cast_out)

    @pl.when(pl.program_id(0) == 0)
    def _():
        extb_ref[...] = jnp.zeros(extb_ref.shape, F32)
        extc_ref[...] = jnp.zeros(extc_ref.shape, F32)

    _run_stages([_inproj_stages(x_ref, nw_ref, w_ref, wr_ref, lnw_ref, lnb_ref, ws_ref, bs_ref, scw_ref, dcw_ref,
                                alog_ref, dtb_ref, ea_ref, eb_ref, bd_ref, wg_ref, gb_ref,
                                ya_ref, yb_ref, cpk_ref, cg_ref, dpk_ref, dla_ref, extb_ref, extc_ref)])


def _inproj(x, nw, w_main, w_rest, layer, lnw, lnb, ws, bs_wide, scw, dcw, alog_n, dtb_n, ea, eb, bd, wg_pad, gb,
            cast_jobs=()):
    nb, seq, _ = x.shape
    tm = SEQ_TILE
    n_tiles = seq // tm
    assert n_tiles >= CAST_BLOCKS or not cast_jobs
    tile = lambda i: (0, i, 0)
    whole = lambda a: pl.BlockSpec(a.shape, (lambda i: (0,) * a.ndim))
    per_layer = lambda a: _layer_param(a, layer)
    outs = ((GROUP, BF16), (GROUP, BF16), (C_PACK, BF16), (GROUP, F32), (D_PACK, BF16), (GROUP, F32))
    cast_in_specs, cast_out_specs, cast_out_shapes = _cast_specs(cast_jobs)
    return pl.pallas_call(
        functools.partial(_inproj_kernel, len(cast_jobs)),
        grid=(n_tiles,),
        in_specs=[pl.BlockSpec((nb, tm, D_MODEL), tile),
                  per_layer(nw),
                  _layer_block(w_main, layer),
                  _layer_block(w_rest, layer),
                  per_layer(lnw), per_layer(lnb), per_layer(ws), per_layer(bs_wide), per_layer(scw),
                  per_layer(dcw), per_layer(alog_n), per_layer(dtb_n), whole(ea), whole(eb), whole(bd),
                  per_layer(wg_pad), per_layer(gb)] + cast_in_specs,
        out_specs=[pl.BlockSpec((nb, tm, n), tile) for n, _ in outs] + cast_out_specs,
        out_shape=[jax.ShapeDtypeStruct((nb, seq, n), dt) for n, dt in outs] + cast_out_shapes,
        scratch_shapes=[pltpu.VMEM((nb, TAIL + tm, GROUP), F32),
                        pltpu.VMEM((nb, TAIL + tm, 3 * GROUP), F32)],
        compiler_params=pltpu.CompilerParams(dimension_semantics=("arbitrary",),
                                             vmem_limit_bytes=VMEM_LIMIT),
        name="in_proj",
    )(x, nw, w_main, w_rest, lnw, lnb, ws, bs_wide, scw, dcw, alog_n, dtb_n, ea, eb, bd, wg_pad, gb,
      *[w for w, _, _ in cast_jobs])


def _chunk_masks():
    i = lax.broadcasted_iota(jnp.int32, (CHUNK, GROUP), 0)
    j = lax.broadcasted_iota(jnp.int32, (CHUNK, GROUP), 1) % CHUNK
    return i == j, i >= j, i > j


def _const_inputs():
    lane_head = np.arange(GROUP) // HEAD_DIM
    bd = (lane_head[:, None] == lane_head[None, :]).astype(np.float32)
    ltri = np.tril(np.ones((CHUNK, CHUNK), np.float32))
    return jnp.asarray(bd, BF16), jnp.asarray(ltri, BF16)


def _chunk_rows(c):
    return slice(c * CHUNK, (c + 1) * CHUNK)


def _gdn_stages(cpk_ref, cg_ref, nw_ref, bd_ref, ltri_ref, y_ref, ac_ref, bb_ref, d_ref, dec_ref, o_ref, s_ref):
    nb, tm = cpk_ref.shape[0], cpk_ref.shape[1]
    cps = tm // CHUNK
    eye_w, causal_w, strict_w = _chunk_masks()
    eye_f = eye_w.astype(F32)
    ltri = ltri_ref[...]
    bd = bd_ref[...]

    all_items = [(b, c) for b in range(nb) for c in range(cps)]
    per_group = len(all_items) // GDN_GROUPS
    for g in range(GDN_GROUPS):
        yield from _gdn_prepare(all_items[g * per_group:(g + 1) * per_group], cps, cpk_ref, cg_ref,
                                ac_ref, bb_ref, d_ref, dec_ref, eye_w, causal_w, strict_w, eye_f, ltri, bd)

    for c in range(cps):
        for b in range(nb):
            i = b * cps + c
            s = s_ref[b]
            r = _dot(ac_ref[i], _blockdiag(s, bd))
            s_ref[b] = s * dec_ref[i, 0:1, :] + r[0:CHUNK] + bb_ref[i]
            o_ref[b, _chunk_rows(c), :] = r[CHUNK:2 * CHUNK] + d_ref[i]
        yield

    for b in range(nb):
        o = o_ref[b]
        zg = cpk_ref[b, :, 4 * GROUP:5 * GROUP].astype(F32)
        y_ref[b] = (o * lax.rsqrt(_head_mean_sq(o, bd) + EPS) * nw_ref[...] * zg).astype(y_ref.dtype)


def _gdn_prepare(items, cps, cpk_ref, cg_ref, ac_ref, bb_ref, d_ref, dec_ref,
                 eye_w, causal_w, strict_w, eye_f, ltri, bd):
    n = len(items)

    def load(col, i):
        b, c = items[i]
        return cpk_ref[b, _chunk_rows(c), col * GROUP:(col + 1) * GROUP]

    qn = [load(0, i) for i in range(n)]
    kn = [load(1, i) for i in range(n)]
    beta = [load(3, i) for i in range(n)]
    gc = [_dot_sel_lhs(ltri, cg_ref[b, _chunk_rows(c), :]) for b, c in items]
    yield
    eg = [jnp.exp(x) for x in gc]
    kb = [kn[i] * beta[i] for i in range(n)]
    eye_b = eye_f.astype(BF16)
    r1 = [_dot_nt(jnp.concatenate([kb[i], qn[i], eye_b], axis=0), _blockdiag(kn[i], bd))
          for i in range(n)]
    yield
    low, attn, g_row = [], [], []
    for i in range(n):
        g_row.append(jnp.sum(jnp.where(eye_w, gc[i], 0.0), axis=0, keepdims=True))
        decay = jnp.where(causal_w, jnp.exp(jnp.minimum(gc[i] - g_row[i], 0.0)), 0.0)
        low.append(jnp.where(strict_w, r1[i][0:CHUNK] * decay, 0.0))
        attn.append((r1[i][CHUNK:2 * CHUNK] * decay).astype(BF16))
    t = [eye_f - x for x in low]
    p = [_dot(x.astype(BF16), _blockdiag(x, bd)) for x in low]
    yield
    n_levels = CHUNK.bit_length() - 1
    for level in range(1, n_levels):
        if level < n_levels - 1:
            r2 = [_dot(jnp.concatenate([p[i], t[i]], axis=0).astype(BF16), _blockdiag(p[i], bd))
                  for i in range(n)]
            p = [x[0:CHUNK] for x in r2]
            t = [t[i] + r2[i][CHUNK:2 * CHUNK] for i in range(n)]
        else:
            t = [t[i] + _dot(t[i].astype(BF16), _blockdiag(p[i], bd)) for i in range(n)]
        yield
    wu = [_dot(t[i].astype(BF16),
               jnp.concatenate([_blockdiag(kb[i] * eg[i], bd), _blockdiag(load(2, i) * beta[i], bd)], axis=1))
          for i in range(n)]
    yield
    gc_last = [x[CHUNK - 1:CHUNK, :] for x in gc]
    kd_t = [(r1[i][2 * CHUNK:3 * CHUNK] * jnp.exp(gc_last[i] - g_row[i])).astype(BF16) for i in range(n)]
    prod = [_dot(jnp.concatenate([attn[i], kd_t[i]], axis=0),
                 jnp.concatenate([_blockdiag(wu[i][:, 0:GROUP], bd),
                                  _blockdiag(wu[i][:, GROUP:2 * GROUP], bd)], axis=1))
            for i in range(n)]
    for i, (b, c) in enumerate(items):
        j = b * cps + c
        ac_ref[j, 0:CHUNK, :] = (-prod[i][CHUNK:2 * CHUNK, 0:GROUP]).astype(BF16)
        bb_ref[j] = prod[i][CHUNK:2 * CHUNK, GROUP:2 * GROUP]
        ac_ref[j, CHUNK:2 * CHUNK, :] = (qn[i] * eg[i] - prod[i][0:CHUNK, 0:GROUP]).astype(BF16)
        d_ref[j] = prod[i][0:CHUNK, GROUP:2 * GROUP]
        dec_ref[j] = jnp.broadcast_to(jnp.exp(gc_last[i]), (TAIL, GROUP))
    yield


def _gla_stages(dpk_ref, dla_ref, nw_ref, bd_ref, ltri_ref, y_ref, o_ref, st_ref):
    nb, tm = dpk_ref.shape[0], dpk_ref.shape[1]
    cps = tm // CHUNK
    _, causal_w, _ = _chunk_masks()
    ltri = ltri_ref[...]
    bd = bd_ref[...]
    bdf = bd.astype(F32)
    mid = CHUNK // 2

    items = [(b, c) for c in range(cps) for b in range(nb)]
    n = len(items)
    q = [dpk_ref[b, _chunk_rows(c), 0:GROUP].astype(F32) * (HEAD_DIM ** -0.5) for b, c in items]
    k = [dpk_ref[b, _chunk_rows(c), GROUP:2 * GROUP].astype(F32) for b, c in items]
    v = [dpk_ref[b, _chunk_rows(c), 2 * GROUP:3 * GROUP] for b, c in items]
    gcum = [_dot_sel_lhs(ltri, dla_ref[b, _chunk_rows(c), :]) for b, c in items]
    yield
    g_mid = [x[mid:mid + 1, :] for x in gcum]
    g_last = [x[CHUNK - 1:CHUNK, :] for x in gcum]
    attn = [jnp.where(causal_w,
                      _dot_nt((q[i] * jnp.exp(gcum[i] - g_mid[i])).astype(BF16),
                              _blockdiag(k[i] * jnp.exp(g_mid[i] - gcum[i]), bd)), 0.0).astype(BF16)
            for i in range(n)]
    yield
    upd = [bdf * _dot_tn(v[i], (k[i] * jnp.exp(g_last[i] - gcum[i])).astype(BF16)) for i in range(n)]
    yield
    o_intra = [_dot(attn[i], _blockdiag(v[i], bd)) for i in range(n)]
    qg = [(q[i] * jnp.exp(gcum[i])).astype(BF16) for i in range(n)]
    yield
    for i, (b, c) in enumerate(items):
        st = st_ref[b]
        o_ref[b, _chunk_rows(c), :] = o_intra[i] + _dot_nt(qg[i], st.astype(BF16))
        st_ref[b] = st * jnp.exp(g_last[i]) + upd[i]
        if b == nb - 1:
            yield

    for b in range(nb):
        o = o_ref[b]
        zg = dpk_ref[b, :, 3 * GROUP:4 * GROUP].astype(F32)
        y_ref[b] = (o * lax.rsqrt(_head_mean_sq(o, bd) + EPS) * nw_ref[...] * zg).astype(y_ref.dtype)


def _recur_kernel(cpk_ref, cg_ref, dpk_ref, dla_ref, nwc_ref, nwd_ref, bd_ref, ltri_ref, yc_ref, yd_ref,
                  ac_ref, bb_ref, d_ref, dec_ref, oc_ref, s_ref, od_ref, st_ref):
    @pl.when(pl.program_id(0) == 0)
    def _():
        s_ref[...] = jnp.zeros(s_ref.shape, F32)
        st_ref[...] = jnp.zeros(st_ref.shape, F32)

    _run_stages([_gdn_stages(cpk_ref, cg_ref, nwc_ref, bd_ref, ltri_ref, yc_ref,
                             ac_ref, bb_ref, d_ref, dec_ref, oc_ref, s_ref),
                 _gla_stages(dpk_ref, dla_ref, nwd_ref, bd_ref, ltri_ref, yd_ref, od_ref, st_ref)])


def _recur(cpk, cg, dpk, dla, nwc, nwd, bd, ltri, layer):
    nb, seq, _ = cpk.shape
    tm = SEQ_TILE
    n_chunks = nb * tm // CHUNK
    tile = lambda i: (0, i, 0)
    whole = lambda a: pl.BlockSpec(a.shape, (lambda i: (0,) * a.ndim))
    out = jax.ShapeDtypeStruct((nb, seq, GROUP), BF16)
    return pl.pallas_call(
        _recur_kernel,
        grid=(seq // tm,),
        in_specs=[pl.BlockSpec((nb, tm, C_PACK), tile),
                  pl.BlockSpec((nb, tm, GROUP), tile),
                  pl.BlockSpec((nb, tm, D_PACK), tile),
                  pl.BlockSpec((nb, tm, GROUP), tile),
                  _layer_param(nwc, layer), _layer_param(nwd, layer), whole(bd), whole(ltri)],
        out_specs=[pl.BlockSpec((nb, tm, GROUP), tile)] * 2,
        out_shape=[out, out],
        scratch_shapes=[pltpu.VMEM((n_chunks, 2 * CHUNK, GROUP), BF16),
                        pltpu.VMEM((n_chunks, CHUNK, GROUP), F32),
                        pltpu.VMEM((n_chunks, CHUNK, GROUP), F32),
                        pltpu.VMEM((n_chunks, TAIL, GROUP), F32),
                        pltpu.VMEM((nb, tm, GROUP), F32),
                        pltpu.VMEM((nb, CHUNK, GROUP), F32),
                        pltpu.VMEM((nb, tm, GROUP), F32),
                        pltpu.VMEM((nb, GROUP, GROUP), F32)],
        compiler_params=pltpu.CompilerParams(dimension_semantics=("arbitrary",),
                                             vmem_limit_bytes=VMEM_LIMIT),
        name="recur",
    )(cpk, cg, dpk, dla, nwc, nwd, bd, ltri)


def _outproj_ffn_kernel(final, n_cast, x_ref, ya_ref, yb_ref, yc_ref, yd_ref, wo_ref, n2_ref, wgu_ref, wd_ref,
                        fn_ref, *rest):
    cast_in, (o_ref,), cast_out, (act_ref,) = _split_refs(rest, n_cast, 1, n_cast, 1)
    _run_casts(cast_in, cast_out)
    x1 = x_ref[...]
    for idx, y_ref in enumerate((ya_ref, yb_ref, yc_ref, yd_ref)):
        x1 = x1 + _dot(y_ref[...], wo_ref[idx * GROUP:(idx + 1) * GROUP, :])
    ms = jnp.mean(x1 * x1, axis=-1, keepdims=True)
    h = (x1 * lax.rsqrt(ms + EPS) * n2_ref[...]).astype(BF16)
    for j in range(D_FF // FF_CHUNK):
        cols = slice(j * FF_CHUNK, (j + 1) * FF_CHUNK)
        gate = _dot(h, wgu_ref[:, cols])
        up = _dot(h, wgu_ref[:, D_FF + j * FF_CHUNK:D_FF + (j + 1) * FF_CHUNK])
        act_ref[:, cols] = (_silu(gate) * up).astype(BF16)
    x2 = x1 + _dot(act_ref[...], wd_ref[...])
    if final:
        ms2 = jnp.mean(x2 * x2, axis=-1, keepdims=True)
        x2 = x2 * lax.rsqrt(ms2 + EPS) * fn_ref[...]
    o_ref[...] = x2


def _outproj_ffn(x, ys, wo, n2, wgu, wd, fn, layer, final, cast_jobs=()):
    m = x.shape[0]
    n_steps = m // ROW_TILE
    assert n_steps >= CAST_BLOCKS or not cast_jobs
    row = lambda i: (i, 0)
    fixed = lambda i: (0, 0)
    cast_in_specs, cast_out_specs, cast_out_shapes = _cast_specs(cast_jobs)
    outs = pl.pallas_call(
        functools.partial(_outproj_ffn_kernel, final, len(cast_jobs)),
        grid=(n_steps,),
        in_specs=[pl.BlockSpec((ROW_TILE, D_MODEL), row)]
                 + [pl.BlockSpec((ROW_TILE, GROUP), row)] * 4
                 + [_resident(wo), _layer_param(n2, layer), _resident(wgu), _resident(wd),
                    pl.BlockSpec((1, D_MODEL), fixed)]
                 + cast_in_specs,
        out_specs=[pl.BlockSpec((ROW_TILE, D_MODEL), row)] + cast_out_specs,
        out_shape=[jax.ShapeDtypeStruct((m, D_MODEL), F32)] + cast_out_shapes,
        scratch_shapes=[pltpu.VMEM((ROW_TILE, D_FF), BF16)],
        compiler_params=pltpu.CompilerParams(dimension_semantics=("arbitrary",),
                                             vmem_limit_bytes=VMEM_LIMIT),
        name="outproj_ffn",
    )(x, *ys, wo, n2, wgu, wd, fn, *[w for w, _, _ in cast_jobs])
    return outs


def _repack_w_in_rest(w):
    c_a = MAIN_COLS
    c_z = c_a + 2 * HEADS
    d0 = c_z + GROUP
    d_g = d0 + 3 * GROUP
    d_z = d_g + GATE_RANK
    pad = jnp.zeros(w.shape[:2] + (GATE_COLS - 2 * HEADS - GATE_RANK,), w.dtype)
    out = jnp.concatenate(
        [w[..., c_z:d_g], w[..., d_z:d_z + GROUP], w[..., c_a:c_z], w[..., d_g:d_z], pad], axis=-1)
    assert out.shape[-1] == REST_COLS
    return out


def _gate_vec(v):
    return jnp.pad(v.astype(F32), ((0, 0), (0, GATE_COLS - HEADS)))[:, None, :]


def _row(v):
    return v[:, None, :]


def _gate_selectors():
    ea = np.zeros((GATE_COLS, GROUP), np.float32)
    eb = np.zeros((GATE_COLS, GROUP), np.float32)
    for h in range(HEADS):
        ea[h, h * HEAD_DIM:(h + 1) * HEAD_DIM] = 1.0
        eb[HEADS + h, h * HEAD_DIM:(h + 1) * HEAD_DIM] = 1.0
    return jnp.asarray(ea, BF16), jnp.asarray(eb, BF16)


def kernel(x, norm1_w, w_in, sgu_ln_w, sgu_ln_b, sgu_w_spatial, sgu_b_spatial, sc_conv_w, dn_conv_w, dn_a_log, dn_dt_bias, dn_norm_w, gla_w_gate2, gla_gate_bias, gla_norm_w, w_out, norm2_w, w_gate_up, w_down, final_norm_w):
    bsz, seq, d = x.shape
    depth = w_in.shape[0]
    assert seq % ROW_TILE == 0 and seq % SEQ_TILE == 0 and d == D_MODEL
    m = bsz * seq
    xf = x.reshape(m, d)
    bd, ltri = _const_inputs()
    ea, eb = _gate_selectors()
    w_in_main = w_in[..., 0:MAIN_COLS].astype(BF16)
    w_in_rest = _repack_w_in_rest(w_in).astype(BF16)
    ffn_jobs = lambda layer: [(w_out, layer, D_MODEL), (w_gate_up, layer, 2 * D_FF), (w_down, layer, D_MODEL)]
    bs_wide = jnp.repeat(jnp.swapaxes(sgu_b_spatial, 1, 2), HEAD_DIM, axis=2)
    wg_pad = jnp.pad(gla_w_gate2.astype(BF16),
                     ((0, 0), (2 * HEADS, GATE_COLS - 2 * HEADS - GATE_RANK), (0, 0)))
    inproj_params = (_row(sgu_ln_w), _row(sgu_ln_b), sgu_w_spatial, bs_wide, sc_conv_w, dn_conv_w,
                     _gate_vec(dn_a_log), _gate_vec(dn_dt_bias), ea, eb, bd, wg_pad, _row(gla_gate_bias))
    nwc, nwd = _row(jnp.tile(dn_norm_w, (1, HEADS))), _row(jnp.tile(gla_norm_w, (1, HEADS)))
    norm1, norm2 = _row(norm1_w), _row(norm2_w)
    ffn_w = None
    for l in range(depth):
        ya, yb, cpk, cg, dpk, dla, *cast = _inproj(xf.reshape(bsz, seq, d), norm1, w_in_main, w_in_rest, l,
                                                   *inproj_params, cast_jobs=ffn_jobs(0) if l == 0 else ())
        if l == 0:
            ffn_w = cast
        yc, yd = _recur(cpk, cg, dpk, dla, nwc, nwd, bd, ltri, l)
        last = l == depth - 1
        xf, *cast = _outproj_ffn(xf, tuple(y.reshape(m, GROUP) for y in (ya, yb, yc, yd)),
                                 ffn_w[0], norm2, ffn_w[1], ffn_w[2], final_norm_w[None, :], layer=l, final=last,
                                 cast_jobs=() if last else ffn_jobs(l + 1))
        if not last:
            ffn_w = cast
    return xf.reshape(bsz, seq, d)
```

```python
import functools

import numpy as np
import jax
import jax.numpy as jnp
from jax import lax
from jax.experimental import pallas as pl
from jax.experimental.pallas import tpu as pltpu

F32 = jnp.float32
BF16 = jnp.bfloat16

D_MODEL = 1024
GROUP = 256
HEADS = 4
HEAD_DIM = 64
SGU_CHUNK = 128
SC_WIDTH = 3
DN_CONV_WIDTH = 4
CHUNK = 64
GATE_RANK = 16
GATE_TEMP = 16.0
D_FF = 2816
EPS = 1e-6
GATE_COLS = 128
TAIL = 8

ROW_TILE = 1024
SEQ_TILE = 512
PROJ_SLABS = 1
GDN_GROUPS = 1
FF_CHUNK = 256
VMEM_LIMIT = 60 * 1024 * 1024

MAIN_A = 0
MAIN_B = MAIN_A + 2 * GROUP
MAIN_C = MAIN_B + 3 * GROUP
MAIN_COLS = MAIN_C + 3 * GROUP
REST_CZ = 0
REST_D = REST_CZ + GROUP
REST_G = REST_D + 4 * GROUP
REST_COLS = REST_G + GATE_COLS
C_PACK = 5 * GROUP
D_PACK = 4 * GROUP


def _dot(a, b):
    return jnp.dot(a, b, preferred_element_type=F32)


def _dot_nt(a, b):
    return lax.dot_general(a, b, (((1,), (1,)), ((), ())), preferred_element_type=F32)


def _dot_tn(a, b):
    return lax.dot_general(a, b, (((0,), (0,)), ((), ())), preferred_element_type=F32)


def _split2(x):
    hi = x.astype(BF16)
    lo = (x - hi.astype(F32)).astype(BF16)
    return hi, lo


def _dot_sel_rhs(x, sel):
    hi, lo = _split2(x)
    return _dot(hi, sel) + _dot(lo, sel)


def _dot_sel_lhs(sel, x):
    hi, lo = _split2(x)
    return _dot(sel, hi) + _dot(sel, lo)


def _softplus(x):
    return jnp.maximum(x, 0.0) + jnp.log(1.0 + jnp.exp(-jnp.abs(x)))


def _sigmoid(x):
    return 0.5 + 0.5 * jnp.tanh(0.5 * x)


def _silu(x):
    h = 0.5 * x
    return h + h * jnp.tanh(h)


def _gelu_tanh(x):
    h = 0.5 * x
    return h + h * jnp.tanh(x * (0.7978845608028654 + (0.7978845608028654 * 0.044715) * (x * x)))


def _blockdiag(x, mask):
    xb = x.astype(BF16)
    return jnp.concatenate([xb] * HEADS, axis=0) * mask


def _head_mean_sq(x, bones):
    return _dot((x * x).astype(BF16), bones) * (1.0 / HEAD_DIM)


def _causal_conv(x, ext_ref, cw_ref, width):
    tm = x.shape[0]
    ext_ref[TAIL:TAIL + tm, :] = x
    acc = x * cw_ref[width - 1:width, :]
    for j in range(1, width):
        acc = acc + ext_ref[TAIL - j:TAIL - j + tm, :] * cw_ref[width - 1 - j:width - j, :]
    ext_ref[0:TAIL, :] = x[tm - TAIL:tm, :]
    return acc


def _layer_param(stacked, layer):
    zeros = (0,) * (stacked.ndim - 1)
    return pl.BlockSpec((None,) + stacked.shape[1:], lambda i: (layer,) + zeros)


def _layer_block(stacked, layer):
    zeros = (0,) * (stacked.ndim - 1)
    return pl.BlockSpec((None,) + stacked.shape[1:], lambda i: (layer,) + zeros,
                        pipeline_mode=pl.Buffered(1))


def _resident(a):
    return pl.BlockSpec(a.shape, lambda i: (0, 0), pipeline_mode=pl.Buffered(1))


CAST_BLOCKS = 16


def _cast_specs(cast_jobs):
    last = CAST_BLOCKS - 1
    in_specs, out_specs, out_shapes = [], [], []
    for stacked, layer, n_cols in cast_jobs:
        rows = stacked.shape[1]
        assert rows % (16 * CAST_BLOCKS) == 0 and n_cols % 128 == 0
        blk = rows // CAST_BLOCKS
        in_specs.append(pl.BlockSpec((None, blk, n_cols),
                                     lambda i, layer=layer: (layer, jnp.minimum(i, last), 0)))
        out_specs.append(pl.BlockSpec((blk, n_cols), lambda i: (jnp.minimum(i, last), 0)))
        out_shapes.append(jax.ShapeDtypeStruct((rows, n_cols), BF16))
    return in_specs, out_specs, out_shapes


def _run_casts(cast_in, cast_out):
    for src, dst in zip(cast_in, cast_out):
        dst[...] = src[...].astype(dst.dtype)


def _split_refs(refs, *counts):
    assert sum(counts) == len(refs)
    groups, pos = [], 0
    for c in counts:
        groups.append(tuple(refs[pos:pos + c]))
        pos += c
    return groups


def _run_stages(stages):
    stages = list(stages)
    while stages:
        for gen in list(stages):
            try:
                next(gen)
            except StopIteration:
                stages.remove(gen)


def _inproj_stages(x_ref, nw_ref, w_ref, wr_ref, lnw_ref, lnb_ref, ws_ref, bs_ref, scw_ref,
                   dcw_ref, alog_ref, dtb_ref, ea_ref, eb_ref, bd_ref, wg_ref, gb_ref,
                   ya_ref, yb_ref, cpk_ref, cg_ref, dpk_ref, dla_ref, extb_ref, extc_ref):
    nb, tm = x_ref.shape[0], x_ref.shape[1]
    m = nb * tm
    batch_rows = [slice(b * tm, (b + 1) * tm) for b in range(nb)]

    x = x_ref[...].reshape(m, D_MODEL)
    ms = jnp.mean(x * x, axis=-1, keepdims=True)
    h = (x * lax.rsqrt(ms + EPS) * nw_ref[...]).astype(BF16)
    bd = bd_ref[...]

    def project(ref, col0, n_slabs):
        parts = []
        for s in range(0, n_slabs, PROJ_SLABS):
            width = min(PROJ_SLABS, n_slabs - s) * GROUP
            c = col0 + s * GROUP
            parts.append(_dot(h, ref[:, c:c + width]))
            yield
        return jnp.concatenate(parts, axis=1)

    pc = yield from project(w_ref, MAIN_C, 3)
    pcz = yield from project(wr_ref, REST_CZ, 1)
    pg = _dot(h, wr_ref[:, REST_G:REST_G + GATE_COLS])
    pa = yield from project(w_ref, MAIN_A, 2)

    qkv = jnp.concatenate(
        [_silu(_causal_conv(pc[rows, 0:3 * GROUP], extc_ref.at[b], dcw_ref, DN_CONV_WIDTH))
         for b, rows in enumerate(batch_rows)], axis=0)
    q = qkv[:, 0:GROUP]
    k = qkv[:, GROUP:2 * GROUP]
    pb = yield from project(w_ref, MAIN_B, 3)
    qn = q * lax.rsqrt(_dot((q * q).astype(BF16), bd) + EPS) * (HEAD_DIM ** -0.5)
    kn = k * lax.rsqrt(_dot((k * k).astype(BF16), bd) + EPS)
    yield
    g_narrow = -jnp.exp(alog_ref[...]) * _softplus(pg + dtb_ref[...])
    g_wide = _dot_sel_rhs(g_narrow, ea_ref[...])
    beta = _dot(_sigmoid(pg).astype(BF16), eb_ref[...])
    zg_c = _silu(pcz)
    pre = _dot(pg.astype(BF16), wg_ref[...]) + gb_ref[...]
    log_a = -_softplus(-pre) * (1.0 / GATE_TEMP)
    yield

    tri = (lax.broadcasted_iota(jnp.int32, (SGU_CHUNK, SGU_CHUNK), 0)
           >= lax.broadcasted_iota(jnp.int32, (SGU_CHUNK, SGU_CHUNK), 1))
    head_of_lane = lax.broadcasted_iota(jnp.int32, (SGU_CHUNK, GROUP), 1) // HEAD_DIM
    ws = [jnp.where(tri, ws_ref[hd], 0.0).astype(BF16) for hd in range(HEADS)]
    for b in range(nb):
        for c in range(tm // SGU_CHUNK):
            rows = slice(b * tm + c * SGU_CHUNK, b * tm + (c + 1) * SGU_CHUNK)
            u = _gelu_tanh(pa[rows, 0:GROUP])
            v = _gelu_tanh(pa[rows, GROUP:2 * GROUP])
            mu = jnp.mean(v, axis=-1, keepdims=True)
            vc = v - mu
            var = jnp.mean(vc * vc, axis=-1, keepdims=True)
            vn = vc * lax.rsqrt(var + EPS) * lnw_ref[...] + lnb_ref[...]
            mixed = bs_ref[...]
            for hd in range(HEADS):
                mixed = mixed + _dot(ws[hd], jnp.where(head_of_lane == hd, vn, 0.0).astype(BF16))
            ya_ref[b, c * SGU_CHUNK:(c + 1) * SGU_CHUNK, :] = (u * mixed).astype(ya_ref.dtype)
            yield

    pd = yield from project(wr_ref, REST_D, 4)

    for b, rows in enumerate(batch_rows):
        conv_b = _causal_conv(pb[rows, GROUP:2 * GROUP] * pb[rows, 2 * GROUP:3 * GROUP],
                              extb_ref.at[b], scw_ref, SC_WIDTH)
        yb_ref[b] = (pb[rows, 0:GROUP] * conv_b).astype(yb_ref.dtype)

    for b, rows in enumerate(batch_rows):
        cg_ref[b] = g_wide[rows]
        cpk_ref[b, :, 0:GROUP] = qn[rows].astype(BF16)
        cpk_ref[b, :, GROUP:2 * GROUP] = kn[rows].astype(BF16)
        cpk_ref[b, :, 2 * GROUP:3 * GROUP] = qkv[rows, 2 * GROUP:3 * GROUP].astype(BF16)
        cpk_ref[b, :, 3 * GROUP:4 * GROUP] = beta[rows].astype(BF16)
        cpk_ref[b, :, 4 * GROUP:5 * GROUP] = zg_c[rows].astype(BF16)
        dla_ref[b] = log_a[rows]
        dpk_ref[b, :, 0:3 * GROUP] = pd[rows, 0:3 * GROUP].astype(BF16)
        dpk_ref[b, :, 3 * GROUP:4 * GROUP] = _silu(pd[rows, 3 * GROUP:4 * GROUP]).astype(BF16)


def _repack_w_in_rest(w_ref, wr_ref):
    c_a = MAIN_COLS
    c_z = c_a + 2 * HEADS
    d0 = c_z + GROUP
    d_g = d0 + 3 * GROUP
    d_z = d_g + GATE_RANK
    wr_ref[:, REST_CZ:REST_CZ + GROUP] = w_ref[:, c_z:d0]
    wr_ref[:, REST_D:REST_D + 3 * GROUP] = w_ref[:, d0:d_g]
    wr_ref[:, REST_D + 3 * GROUP:REST_D + 4 * GROUP] = w_ref[:, d_z:d_z + GROUP]
    wr_ref[:, REST_G:REST_G + GATE_COLS] = jnp.zeros((wr_ref.shape[0], GATE_COLS), wr_ref.dtype)
    wr_ref[:, REST_G:REST_G + 2 * HEADS] = w_ref[:, c_a:c_z]
    wr_ref[:, REST_G + 2 * HEADS:REST_G + 2 * HEADS + GATE_RANK] = w_ref[:, d_g:d_z]


def _inproj_kernel(n_cast, x_ref, nw_ref, w_ref, lnw_ref, lnb_ref, ws_ref, bs_ref, scw_ref, dcw_ref,
                   alog_ref, dtb_ref, ea_ref, eb_ref, bd_ref, wg_ref, gb_ref, *rest):
    cast_in, outs, cast_out, (extb_ref, extc_ref, wr_ref) = _split_refs(rest, n_cast, 6, n_cast, 3)
    ya_ref, yb_ref, cpk_ref, cg_ref, dpk_ref, dla_ref = outs
    _run_casts(cast_in, cast_out)

    @pl.when(pl.program_id(0) == 0)
    def _():
        extb_ref[...] = jnp.zeros(extb_ref.shape, F32)
        extc_ref[...] = jnp.zeros(extc_ref.shape, F32)
        _repack_w_in_rest(w_ref, wr_ref)

    _run_stages([_inproj_stages(x_ref, nw_ref, w_ref, wr_ref, lnw_ref, lnb_ref, ws_ref, bs_ref, scw_ref, dcw_ref,
                                alog_ref, dtb_ref, ea_ref, eb_ref, bd_ref, wg_ref, gb_ref,
                                ya_ref, yb_ref, cpk_ref, cg_ref, dpk_ref, dla_ref, extb_ref, extc_ref)])


def _inproj(x, nw, w_in_b, layer, lnw, lnb, ws, bs_wide, scw, dcw, alog_n, dtb_n, ea, eb, bd, wg_pad, gb,
            cast_jobs=()):
    nb, seq, _ = x.shape
    tm = SEQ_TILE
    n_tiles = seq // tm
    assert n_tiles >= CAST_BLOCKS or not cast_jobs
    tile = lambda i: (0, i, 0)
    whole = lambda a: pl.BlockSpec(a.shape, (lambda i: (0,) * a.ndim))
    per_layer = lambda a: _layer_param(a, layer)
    outs = ((GROUP, BF16), (GROUP, BF16), (C_PACK, BF16), (GROUP, F32), (D_PACK, BF16), (GROUP, F32))
    cast_in_specs, cast_out_specs, cast_out_shapes = _cast_specs(cast_jobs)
    return pl.pallas_call(
        functools.partial(_inproj_kernel, len(cast_jobs)),
        grid=(n_tiles,),
        in_specs=[pl.BlockSpec((nb, tm, D_MODEL), tile),
                  per_layer(nw),
                  _layer_block(w_in_b, layer),
                  per_layer(lnw), per_layer(lnb), per_layer(ws), per_layer(bs_wide), per_layer(scw),
                  per_layer(dcw), per_layer(alog_n), per_layer(dtb_n), whole(ea), whole(eb), whole(bd),
                  per_layer(wg_pad), per_layer(gb)] + cast_in_specs,
        out_specs=[pl.BlockSpec((nb, tm, n), tile) for n, _ in outs] + cast_out_specs,
        out_shape=[jax.ShapeDtypeStruct((nb, seq, n), dt) for n, dt in outs] + cast_out_shapes,
        scratch_shapes=[pltpu.VMEM((nb, TAIL + tm, GROUP), F32),
                        pltpu.VMEM((nb, TAIL + tm, 3 * GROUP), F32),
                        pltpu.VMEM((D_MODEL, REST_COLS), BF16)],
        compiler_params=pltpu.CompilerParams(dimension_semantics=("arbitrary",),
                                             vmem_limit_bytes=VMEM_LIMIT),
        name="in_proj",
    )(x, nw, w_in_b, lnw, lnb, ws, bs_wide, scw, dcw, alog_n, dtb_n, ea, eb, bd, wg_pad, gb,
      *[w for w, _, _ in cast_jobs])


def _chunk_masks():
    i = lax.broadcasted_iota(jnp.int32, (CHUNK, GROUP), 0)
    j = lax.broadcasted_iota(jnp.int32, (CHUNK, GROUP), 1) % CHUNK
    return i == j, i >= j, i > j


def _const_inputs():
    lane_head = np.arange(GROUP) // HEAD_DIM
    bd = (lane_head[:, None] == lane_head[None, :]).astype(np.float32)
    ltri = np.tril(np.ones((CHUNK, CHUNK), np.float32))
    return jnp.asarray(bd, BF16), jnp.asarray(ltri, BF16)


def _chunk_rows(c):
    return slice(c * CHUNK, (c + 1) * CHUNK)


def _gdn_stages(cpk_ref, cg_ref, nw_ref, bd_ref, ltri_ref, y_ref, ac_ref, bb_ref, d_ref, dec_ref, o_ref, s_ref):
    nb, tm = cpk_ref.shape[0], cpk_ref.shape[1]
    cps = tm // CHUNK
    eye_w, causal_w, strict_w = _chunk_masks()
    eye_f = eye_w.astype(F32)
    ltri = ltri_ref[...]
    bd = bd_ref[...]

    all_items = [(b, c) for b in range(nb) for c in range(cps)]
    per_group = len(all_items) // GDN_GROUPS
    for g in range(GDN_GROUPS):
        yield from _gdn_prepare(all_items[g * per_group:(g + 1) * per_group], cps, cpk_ref, cg_ref,
                                ac_ref, bb_ref, d_ref, dec_ref, eye_w, causal_w, strict_w, eye_f, ltri, bd)

    for c in range(cps):
        for b in range(nb):
            i = b * cps + c
            s = s_ref[b]
            r = _dot(ac_ref[i], _blockdiag(s, bd))
            s_ref[b] = s * dec_ref[i, 0:1, :] + r[0:CHUNK] + bb_ref[i]
            o_ref[b, _chunk_rows(c), :] = r[CHUNK:2 * CHUNK] + d_ref[i]
        yield

    for b in range(nb):
        o = o_ref[b]
        zg = cpk_ref[b, :, 4 * GROUP:5 * GROUP].astype(F32)
        y_ref[b] = (o * lax.rsqrt(_head_mean_sq(o, bd) + EPS) * nw_ref[...] * zg).astype(y_ref.dtype)


def _gdn_prepare(items, cps, cpk_ref, cg_ref, ac_ref, bb_ref, d_ref, dec_ref,
                 eye_w, causal_w, strict_w, eye_f, ltri, bd):
    n = len(items)

    def load(col, i):
        b, c = items[i]
        return cpk_ref[b, _chunk_rows(c), col * GROUP:(col + 1) * GROUP]

    qn = [load(0, i) for i in range(n)]
    kn = [load(1, i) for i in range(n)]
    beta = [load(3, i) for i in range(n)]
    gc = [_dot_sel_lhs(ltri, cg_ref[b, _chunk_rows(c), :]) for b, c in items]
    yield
    eg = [jnp.exp(x) for x in gc]
    kb = [kn[i] * beta[i] for i in range(n)]
    eye_b = eye_f.astype(BF16)
    r1 = [_dot_nt(jnp.concatenate([kb[i], qn[i], eye_b], axis=0), _blockdiag(kn[i], bd))
          for i in range(n)]
    yield
    low, attn, g_row = [], [], []
    for i in range(n):
        g_row.append(jnp.sum(jnp.where(eye_w, gc[i], 0.0), axis=0, keepdims=True))
        decay = jnp.where(causal_w, jnp.exp(jnp.minimum(gc[i] - g_row[i], 0.0)), 0.0)
        low.append(jnp.where(strict_w, r1[i][0:CHUNK] * decay, 0.0))
        attn.append((r1[i][CHUNK:2 * CHUNK] * decay).astype(BF16))
    t = [eye_f - x for x in low]
    p = [_dot(x.astype(BF16), _blockdiag(x, bd)) for x in low]
    yield
    n_levels = CHUNK.bit_length() - 1
    for level in range(1, n_levels):
        if level < n_levels - 1:
            r2 = [_dot(jnp.concatenate([p[i], t[i]], axis=0).astype(BF16), _blockdiag(p[i], bd))
                  for i in range(n)]
            p = [x[0:CHUNK] for x in r2]
            t = [t[i] + r2[i][CHUNK:2 * CHUNK] for i in range(n)]
        else:
            t = [t[i] + _dot(t[i].astype(BF16), _blockdiag(p[i], bd)) for i in range(n)]
        yield
    wu = [_dot(t[i].astype(BF16),
               jnp.concatenate([_blockdiag(kb[i] * eg[i], bd), _blockdiag(load(2, i) * beta[i], bd)], axis=1))
          for i in range(n)]
    yield
    gc_last = [x[CHUNK - 1:CHUNK, :] for x in gc]
    kd_t = [(r1[i][2 * CHUNK:3 * CHUNK] * jnp.exp(gc_last[i] - g_row[i])).astype(BF16) for i in range(n)]
    prod = [_dot(jnp.concatenate([attn[i], kd_t[i]], axis=0),
                 jnp.concatenate([_blockdiag(wu[i][:, 0:GROUP], bd),
                                  _blockdiag(wu[i][:, GROUP:2 * GROUP], bd)], axis=1))
            for i in range(n)]
    for i, (b, c) in enumerate(items):
        j = b * cps + c
        ac_ref[j, 0:CHUNK, :] = (-prod[i][CHUNK:2 * CHUNK, 0:GROUP]).astype(BF16)
        bb_ref[j] = prod[i][CHUNK:2 * CHUNK, GROUP:2 * GROUP]
        ac_ref[j, CHUNK:2 * CHUNK, :] = (qn[i] * eg[i] - prod[i][0:CHUNK, 0:GROUP]).astype(BF16)
        d_ref[j] = prod[i][0:CHUNK, GROUP:2 * GROUP]
        dec_ref[j] = jnp.broadcast_to(jnp.exp(gc_last[i]), (TAIL, GROUP))
    yield


def _gla_stages(dpk_ref, dla_ref, nw_ref, bd_ref, ltri_ref, y_ref, o_ref, st_ref):
    nb, tm = dpk_ref.shape[0], dpk_ref.shape[1]
    cps = tm // CHUNK
    _, causal_w, _ = _chunk_masks()
    ltri = ltri_ref[...]
    bd = bd_ref[...]
    bdf = bd.astype(F32)
    mid = CHUNK // 2

    items = [(b, c) for c in range(cps) for b in range(nb)]
    n = len(items)
    q = [dpk_ref[b, _chunk_rows(c), 0:GROUP].astype(F32) * (HEAD_DIM ** -0.5) for b, c in items]
    k = [dpk_ref[b, _chunk_rows(c), GROUP:2 * GROUP].astype(F32) for b, c in items]
    v = [dpk_ref[b, _chunk_rows(c), 2 * GROUP:3 * GROUP] for b, c in items]
    gcum = [_dot_sel_lhs(ltri, dla_ref[b, _chunk_rows(c), :]) for b, c in items]
    yield
    g_mid = [x[mid:mid + 1, :] for x in gcum]
    g_last = [x[CHUNK - 1:CHUNK, :] for x in gcum]
    attn = [jnp.where(causal_w,
                      _dot_nt((q[i] * jnp.exp(gcum[i] - g_mid[i])).astype(BF16),
                              _blockdiag(k[i] * jnp.exp(g_mid[i] - gcum[i]), bd)), 0.0).astype(BF16)
            for i in range(n)]
    yield
    upd = [bdf * _dot_tn(v[i], (k[i] * jnp.exp(g_last[i] - gcum[i])).astype(BF16)) for i in range(n)]
    yield
    o_intra = [_dot(attn[i], _blockdiag(v[i], bd)) for i in range(n)]
    qg = [(q[i] * jnp.exp(gcum[i])).astype(BF16) for i in range(n)]
    yield
    for i, (b, c) in enumerate(items):
        st = st_ref[b]
        o_ref[b, _chunk_rows(c), :] = o_intra[i] + _dot_nt(qg[i], st.astype(BF16))
        st_ref[b] = st * jnp.exp(g_last[i]) + upd[i]
        if b == nb - 1:
            yield

    for b in range(nb):
        o = o_ref[b]
        zg = dpk_ref[b, :, 3 * GROUP:4 * GROUP].astype(F32)
        y_ref[b] = (o * lax.rsqrt(_head_mean_sq(o, bd) + EPS) * nw_ref[...] * zg).astype(y_ref.dtype)


def _recur_kernel(cpk_ref, cg_ref, dpk_ref, dla_ref, nwc_ref, nwd_ref, bd_ref, ltri_ref, yc_ref, yd_ref,
                  ac_ref, bb_ref, d_ref, dec_ref, oc_ref, s_ref, od_ref, st_ref):
    @pl.when(pl.program_id(0) == 0)
    def _():
        s_ref[...] = jnp.zeros(s_ref.shape, F32)
        st_ref[...] = jnp.zeros(st_ref.shape, F32)

    _run_stages([_gdn_stages(cpk_ref, cg_ref, nwc_ref, bd_ref, ltri_ref, yc_ref,
                             ac_ref, bb_ref, d_ref, dec_ref, oc_ref, s_ref),
                 _gla_stages(dpk_ref, dla_ref, nwd_ref, bd_ref, ltri_ref, yd_ref, od_ref, st_ref)])


def _recur(cpk, cg, dpk, dla, nwc, nwd, bd, ltri, layer):
    nb, seq, _ = cpk.shape
    tm = SEQ_TILE
    n_chunks = nb * tm // CHUNK
    tile = lambda i: (0, i, 0)
    whole = lambda a: pl.BlockSpec(a.shape, (lambda i: (0,) * a.ndim))
    out = jax.ShapeDtypeStruct((nb, seq, GROUP), BF16)
    return pl.pallas_call(
        _recur_kernel,
        grid=(seq // tm,),
        in_specs=[pl.BlockSpec((nb, tm, C_PACK), tile),
                  pl.BlockSpec((nb, tm, GROUP), tile),
                  pl.BlockSpec((nb, tm, D_PACK), tile),
                  pl.BlockSpec((nb, tm, GROUP), tile),
                  _layer_param(nwc, layer), _layer_param(nwd, layer), whole(bd), whole(ltri)],
        out_specs=[pl.BlockSpec((nb, tm, GROUP), tile)] * 2,
        out_shape=[out, out],
        scratch_shapes=[pltpu.VMEM((n_chunks, 2 * CHUNK, GROUP), BF16),
                        pltpu.VMEM((n_chunks, CHUNK, GROUP), F32),
                        pltpu.VMEM((n_chunks, CHUNK, GROUP), F32),
                        pltpu.VMEM((n_chunks, TAIL, GROUP), F32),
                        pltpu.VMEM((nb, tm, GROUP), F32),
                        pltpu.VMEM((nb, CHUNK, GROUP), F32),
                        pltpu.VMEM((nb, tm, GROUP), F32),
                        pltpu.VMEM((nb, GROUP, GROUP), F32)],
        compiler_params=pltpu.CompilerParams(dimension_semantics=("arbitrary",),
                                             vmem_limit_bytes=VMEM_LIMIT),
        name="recur",
    )(cpk, cg, dpk, dla, nwc, nwd, bd, ltri)


def _outproj_ffn_kernel(final, n_cast, x_ref, ya_ref, yb_ref, yc_ref, yd_ref, wo_ref, n2_ref, wgu_ref, wd_ref,
                        fn_ref, *rest):
    cast_in, (o_ref,), cast_out, (act_ref,) = _split_refs(rest, n_cast, 1, n_cast, 1)
    _run_casts(cast_in, cast_out)
    x1 = x_ref[...]
    for idx, y_ref in enumerate((ya_ref, yb_ref, yc_ref, yd_ref)):
        x1 = x1 + _dot(y_ref[...], wo_ref[idx * GROUP:(idx + 1) * GROUP, :])
    ms = jnp.mean(x1 * x1, axis=-1, keepdims=True)
    h = (x1 * lax.rsqrt(ms + EPS) * n2_ref[...]).astype(BF16)
    for j in range(D_FF // FF_CHUNK):
        cols = slice(j * FF_CHUNK, (j + 1) * FF_CHUNK)
        gate = _dot(h, wgu_ref[:, cols])
        up = _dot(h, wgu_ref[:, D_FF + j * FF_CHUNK:D_FF + (j + 1) * FF_CHUNK])
        act_ref[:, cols] = (_silu(gate) * up).astype(BF16)
    x2 = x1 + _dot(act_ref[...], wd_ref[...])
    if final:
        ms2 = jnp.mean(x2 * x2, axis=-1, keepdims=True)
        x2 = x2 * lax.rsqrt(ms2 + EPS) * fn_ref[...]
    o_ref[...] = x2


def _outproj_ffn(x, ys, wo, n2, wgu, wd, fn, layer, final, cast_jobs=()):
    m = x.shape[0]
    n_steps = m // ROW_TILE
    assert n_steps >= CAST_BLOCKS or not cast_jobs
    row = lambda i: (i, 0)
    fixed = lambda i: (0, 0)
    cast_in_specs, cast_out_specs, cast_out_shapes = _cast_specs(cast_jobs)
    outs = pl.pallas_call(
        functools.partial(_outproj_ffn_kernel, final, len(cast_jobs)),
        grid=(n_steps,),
        in_specs=[pl.BlockSpec((ROW_TILE, D_MODEL), row)]
                 + [pl.BlockSpec((ROW_TILE, GROUP), row)] * 4
                 + [_resident(wo), _layer_param(n2, layer), _resident(wgu), _resident(wd),
                    pl.BlockSpec((1, D_MODEL), fixed)]
                 + cast_in_specs,
        out_specs=[pl.BlockSpec((ROW_TILE, D_MODEL), row)] + cast_out_specs,
        out_shape=[jax.ShapeDtypeStruct((m, D_MODEL), F32)] + cast_out_shapes,
        scratch_shapes=[pltpu.VMEM((ROW_TILE, D_FF), BF16)],
        compiler_params=pltpu.CompilerParams(dimension_semantics=("arbitrary",),
                                             vmem_limit_bytes=VMEM_LIMIT),
        name="outproj_ffn",
    )(x, *ys, wo, n2, wgu, wd, fn, *[w for w, _, _ in cast_jobs])
    return outs


def _gate_vec(v):
    return jnp.pad(v.astype(F32), ((0, 0), (0, GATE_COLS - HEADS)))[:, None, :]


def _row(v):
    return v[:, None, :]


def _gate_selectors():
    ea = np.zeros((GATE_COLS, GROUP), np.float32)
    eb = np.zeros((GATE_COLS, GROUP), np.float32)
    for h in range(HEADS):
        ea[h, h * HEAD_DIM:(h + 1) * HEAD_DIM] = 1.0
        eb[HEADS + h, h * HEAD_DIM:(h + 1) * HEAD_DIM] = 1.0
    return jnp.asarray(ea, BF16), jnp.asarray(eb, BF16)


def kernel(x, norm1_w, w_in, sgu_ln_w, sgu_ln_b, sgu_w_spatial, sgu_b_spatial, sc_conv_w, dn_conv_w, dn_a_log, dn_dt_bias, dn_norm_w, gla_w_gate2, gla_gate_bias, gla_norm_w, w_out, norm2_w, w_gate_up, w_down, final_norm_w):
    bsz, seq, d = x.shape
    depth = w_in.shape[0]
    assert seq % ROW_TILE == 0 and seq % SEQ_TILE == 0 and d == D_MODEL
    m = bsz * seq
    xf = x.reshape(m, d)
    bd, ltri = _const_inputs()
    ea, eb = _gate_selectors()
    w_in_b = w_in.astype(BF16)
    ffn_jobs = lambda layer: [(w_out, layer, D_MODEL), (w_gate_up, layer, 2 * D_FF), (w_down, layer, D_MODEL)]
    bs_wide = jnp.repeat(jnp.swapaxes(sgu_b_spatial, 1, 2), HEAD_DIM, axis=2)
    wg_pad = jnp.pad(gla_w_gate2.astype(BF16),
                     ((0, 0), (2 * HEADS, GATE_COLS - 2 * HEADS - GATE_RANK), (0, 0)))
    inproj_params = (_row(sgu_ln_w), _row(sgu_ln_b), sgu_w_spatial, bs_wide, sc_conv_w, dn_conv_w,
                     _gate_vec(dn_a_log), _gate_vec(dn_dt_bias), ea, eb, bd, wg_pad, _row(gla_gate_bias))
    nwc, nwd = _row(jnp.tile(dn_norm_w, (1, HEADS))), _row(jnp.tile(gla_norm_w, (1, HEADS)))
    norm1, norm2 = _row(norm1_w), _row(norm2_w)
    ffn_w = None
    for l in range(depth):
        ya, yb, cpk, cg, dpk, dla, *cast = _inproj(xf.reshape(bsz, seq, d), norm1, w_in_b, l,
                                                   *inproj_params, cast_jobs=ffn_jobs(0) if l == 0 else ())
        if l == 0:
            ffn_w = cast
        yc, yd = _recur(cpk, cg, dpk, dla, nwc, nwd, bd, ltri, l)
        last = l == depth - 1
        xf, *cast = _outproj_ffn(xf, tuple(y.reshape(m, GROUP) for y in (ya, yb, yc, yd)),
                                 ffn_w[0], norm2, ffn_w[1], ffn_w[2], final_norm_w[None, :], layer=l, final=last,
                                 cast_jobs=() if last else ffn_jobs(l + 1))
        if not last:
            ffn_w = cast
    return xf.reshape(bsz, seq, d)
```

```python
import functools

import numpy as np
import jax
import jax.numpy as jnp
from jax import lax
from jax.experimental import pallas as pl
from jax.experimental.pallas import tpu as pltpu

F32 = jnp.float32
BF16 = jnp.bfloat16

D_MODEL = 1024
GROUP = 256
HEADS = 4
HEAD_DIM = 64
SGU_CHUNK = 128
SC_WIDTH = 3
DN_CONV_WIDTH = 4
CHUNK = 64
GATE_RANK = 16
GATE_TEMP = 16.0
D_FF = 2816
EPS = 1e-6
GATE_COLS = 128
TAIL = 8

ROW_TILE = 1024
SEQ_TILE = 512
FF_CHUNK = 256
VMEM_LIMIT = 60 * 1024 * 1024

MAIN_A = 0
MAIN_B = MAIN_A + 2 * GROUP
MAIN_C = MAIN_B + 3 * GROUP
MAIN_COLS = MAIN_C + 3 * GROUP
REST_CZ = 0
REST_D = REST_CZ + GROUP
REST_G = REST_D + 4 * GROUP
REST_COLS = REST_G + GATE_COLS
C_PACK = 5 * GROUP
D_PACK = 4 * GROUP


def _dot(a, b):
    return jnp.dot(a, b, preferred_element_type=F32)


def _dot_nt(a, b):
    return lax.dot_general(a, b, (((1,), (1,)), ((), ())), preferred_element_type=F32)


def _dot_tn(a, b):
    return lax.dot_general(a, b, (((0,), (0,)), ((), ())), preferred_element_type=F32)


def _split2(x):
    hi = x.astype(BF16)
    lo = (x - hi.astype(F32)).astype(BF16)
    return hi, lo


def _dot_sel_rhs(x, sel):
    hi, lo = _split2(x)
    return _dot(hi, sel) + _dot(lo, sel)


def _dot_sel_lhs(sel, x):
    hi, lo = _split2(x)
    return _dot(sel, hi) + _dot(sel, lo)


def _softplus(x):
    return jnp.maximum(x, 0.0) + jnp.log(1.0 + jnp.exp(-jnp.abs(x)))


def _sigmoid(x):
    return 0.5 + 0.5 * jnp.tanh(0.5 * x)


def _silu(x):
    h = 0.5 * x
    return h + h * jnp.tanh(h)


def _gelu_tanh(x):
    h = 0.5 * x
    return h + h * jnp.tanh(x * (0.7978845608028654 + (0.7978845608028654 * 0.044715) * (x * x)))


def _blockdiag(x, mask):
    xb = x.astype(BF16)
    return jnp.concatenate([xb] * HEADS, axis=0) * mask


def _head_mean_sq(x, bones):
    return _dot((x * x).astype(BF16), bones) * (1.0 / HEAD_DIM)


LANES = 128


def _causal_conv(x, ext_ref, out_ref, cw_ref, width):
    tm, ch = x.shape
    half = tm // 2
    n_slabs = ch // LANES
    for s in range(n_slabs):
        ext_ref[s, TAIL:TAIL + tm, :] = x[:, s * LANES:(s + 1) * LANES]
    for s in range(n_slabs):
        lanes = slice(s * LANES, (s + 1) * LANES)
        for parity in (0, 1):
            acc = None
            for j in range(width):
                tap = (ext_ref[s, pl.ds(TAIL + parity - j, half, stride=2), :]
                       * cw_ref[width - 1 - j:width - j, lanes])
                acc = tap if acc is None else acc + tap
            out_ref[s, pl.ds(parity, half, stride=2), :] = acc
    for s in range(n_slabs):
        ext_ref[s, 0:TAIL, :] = x[tm - TAIL:tm, s * LANES:(s + 1) * LANES]
    return jnp.concatenate([out_ref[s] for s in range(n_slabs)], axis=1)


def _layer_param(stacked, layer):
    zeros = (0,) * (stacked.ndim - 1)
    return pl.BlockSpec((None,) + stacked.shape[1:], lambda i: (layer,) + zeros)


def _layer_block(stacked, layer):
    zeros = (0,) * (stacked.ndim - 1)
    return pl.BlockSpec((None,) + stacked.shape[1:], lambda i: (layer,) + zeros,
                        pipeline_mode=pl.Buffered(1))


def _resident(a):
    return pl.BlockSpec(a.shape, lambda i: (0, 0), pipeline_mode=pl.Buffered(1))


CAST_BLOCKS = 16


def _cast_specs(cast_jobs):
    last = CAST_BLOCKS - 1
    in_specs, out_specs, out_shapes = [], [], []
    for stacked, layer, n_cols in cast_jobs:
        rows = stacked.shape[1]
        assert rows % (16 * CAST_BLOCKS) == 0 and n_cols % 128 == 0
        blk = rows // CAST_BLOCKS
        in_specs.append(pl.BlockSpec((None, blk, n_cols),
                                     lambda i, layer=layer: (layer, jnp.minimum(i, last), 0)))
        out_specs.append(pl.BlockSpec((blk, n_cols), lambda i: (jnp.minimum(i, last), 0)))
        out_shapes.append(jax.ShapeDtypeStruct((rows, n_cols), BF16))
    return in_specs, out_specs, out_shapes


def _run_casts(cast_in, cast_out):
    for src, dst in zip(cast_in, cast_out):
        dst[...] = src[...].astype(dst.dtype)


def _split_refs(refs, *counts):
    assert sum(counts) == len(refs)
    groups, pos = [], 0
    for c in counts:
        groups.append(tuple(refs[pos:pos + c]))
        pos += c
    return groups


def _run_stages(stages):
    stages = list(stages)
    while stages:
        for gen in list(stages):
            try:
                next(gen)
            except StopIteration:
                stages.remove(gen)


def _inproj_stages(x_ref, nw_ref, w_ref, wr_ref, lnw_ref, lnb_ref, ws_ref, bs_ref, scw_ref,
                   dcw_ref, alog_ref, dtb_ref, ea_ref, eb_ref, bd_ref, wg_ref, gb_ref,
                   ya_ref, yb_ref, cpk_ref, cg_ref, dpk_ref, dla_ref, extb_ref, extc_ref, convb_ref, convc_ref):
    nb, tm = x_ref.shape[0], x_ref.shape[1]
    m = nb * tm
    batch_rows = [slice(b * tm, (b + 1) * tm) for b in range(nb)]

    x = x_ref[...].reshape(m, D_MODEL)
    ms = jnp.mean(x * x, axis=-1, keepdims=True)
    h = (x * lax.rsqrt(ms + EPS) * nw_ref[...]).astype(BF16)
    bd = bd_ref[...]

    def project(ref, col0, n_slabs):
        parts = []
        for c in range(col0, col0 + n_slabs * GROUP, GROUP):
            parts.append(_dot(h, ref[:, c:c + GROUP]))
            yield
        return jnp.concatenate(parts, axis=1)

    pc = yield from project(w_ref, MAIN_C, 3)
    pcz = yield from project(wr_ref, REST_CZ, 1)
    pg = _dot(h, wr_ref[:, REST_G:REST_G + GATE_COLS])
    pa = yield from project(w_ref, MAIN_A, 2)

    qkv = jnp.concatenate(
        [_silu(_causal_conv(pc[rows, 0:3 * GROUP], extc_ref.at[b], convc_ref, dcw_ref, DN_CONV_WIDTH))
         for b, rows in enumerate(batch_rows)], axis=0)
    q = qkv[:, 0:GROUP]
    k = qkv[:, GROUP:2 * GROUP]
    pb = yield from project(w_ref, MAIN_B, 3)
    qn = q * lax.rsqrt(_dot((q * q).astype(BF16), bd) + EPS) * (HEAD_DIM ** -0.5)
    kn = k * lax.rsqrt(_dot((k * k).astype(BF16), bd) + EPS)
    yield
    g_narrow = -jnp.exp(alog_ref[...]) * _softplus(pg + dtb_ref[...])
    g_wide = _dot_sel_rhs(g_narrow, ea_ref[...])
    beta = _dot(_sigmoid(pg).astype(BF16), eb_ref[...])
    zg_c = _silu(pcz)
    pre = _dot(pg.astype(BF16), wg_ref[...]) + gb_ref[...]
    log_a = -_softplus(-pre) * (1.0 / GATE_TEMP)
    yield

    tri = (lax.broadcasted_iota(jnp.int32, (SGU_CHUNK, SGU_CHUNK), 0)
           >= lax.broadcasted_iota(jnp.int32, (SGU_CHUNK, SGU_CHUNK), 1))
    head_of_lane = lax.broadcasted_iota(jnp.int32, (SGU_CHUNK, GROUP), 1) // HEAD_DIM
    ws = [jnp.where(tri, ws_ref[hd], 0.0).astype(BF16) for hd in range(HEADS)]
    for b in range(nb):
        for c in range(tm // SGU_CHUNK):
            rows = slice(b * tm + c * SGU_CHUNK, b * tm + (c + 1) * SGU_CHUNK)
            u = _gelu_tanh(pa[rows, 0:GROUP])
            v = _gelu_tanh(pa[rows, GROUP:2 * GROUP])
            mu = jnp.mean(v, axis=-1, keepdims=True)
            vc = v - mu
            var = jnp.mean(vc * vc, axis=-1, keepdims=True)
            vn = vc * lax.rsqrt(var + EPS) * lnw_ref[...] + lnb_ref[...]
            mixed = bs_ref[...]
            for hd in range(HEADS):
                mixed = mixed + _dot(ws[hd], jnp.where(head_of_lane == hd, vn, 0.0).astype(BF16))
            ya_ref[b, c * SGU_CHUNK:(c + 1) * SGU_CHUNK, :] = (u * mixed).astype(ya_ref.dtype)
            yield

    pd = yield from project(wr_ref, REST_D, 4)

    for b, rows in enumerate(batch_rows):
        conv_b = _causal_conv(pb[rows, GROUP:2 * GROUP] * pb[rows, 2 * GROUP:3 * GROUP],
                              extb_ref.at[b], convb_ref, scw_ref, SC_WIDTH)
        yb_ref[b] = (pb[rows, 0:GROUP] * conv_b).astype(yb_ref.dtype)

    for b, rows in enumerate(batch_rows):
        cg_ref[b] = g_wide[rows]
        cpk_ref[b, :, 0:GROUP] = qn[rows].astype(BF16)
        cpk_ref[b, :, GROUP:2 * GROUP] = kn[rows].astype(BF16)
        cpk_ref[b, :, 2 * GROUP:3 * GROUP] = qkv[rows, 2 * GROUP:3 * GROUP].astype(BF16)
        cpk_ref[b, :, 3 * GROUP:4 * GROUP] = beta[rows].astype(BF16)
        cpk_ref[b, :, 4 * GROUP:5 * GROUP] = zg_c[rows].astype(BF16)
        dla_ref[b] = log_a[rows]
        dpk_ref[b, :, 0:3 * GROUP] = pd[rows, 0:3 * GROUP].astype(BF16)
        dpk_ref[b, :, 3 * GROUP:4 * GROUP] = _silu(pd[rows, 3 * GROUP:4 * GROUP]).astype(BF16)


def _repack_w_in_rest(w_ref, wr_ref):
    c_a = MAIN_COLS
    c_z = c_a + 2 * HEADS
    d0 = c_z + GROUP
    d_g = d0 + 3 * GROUP
    d_z = d_g + GATE_RANK
    wr_ref[:, REST_CZ:REST_CZ + GROUP] = w_ref[:, c_z:d0]
    wr_ref[:, REST_D:REST_D + 3 * GROUP] = w_ref[:, d0:d_g]
    wr_ref[:, REST_D + 3 * GROUP:REST_D + 4 * GROUP] = w_ref[:, d_z:d_z + GROUP]
    wr_ref[:, REST_G:REST_G + GATE_COLS] = jnp.zeros((wr_ref.shape[0], GATE_COLS), wr_ref.dtype)
    wr_ref[:, REST_G:REST_G + 2 * HEADS] = w_ref[:, c_a:c_z]
    wr_ref[:, REST_G + 2 * HEADS:REST_G + 2 * HEADS + GATE_RANK] = w_ref[:, d_g:d_z]


def _inproj_kernel(n_cast, x_ref, nw_ref, w_ref, lnw_ref, lnb_ref, ws_ref, bs_ref, scw_ref, dcw_ref,
                   alog_ref, dtb_ref, ea_ref, eb_ref, bd_ref, wg_ref, gb_ref, *rest):
    cast_in, outs, cast_out, (extb_ref, extc_ref, convb_ref, convc_ref, wr_ref) = _split_refs(
        rest, n_cast, 6, n_cast, 5)
    ya_ref, yb_ref, cpk_ref, cg_ref, dpk_ref, dla_ref = outs
    _run_casts(cast_in, cast_out)

    @pl.when(pl.program_id(0) == 0)
    def _():
        extb_ref[...] = jnp.zeros(extb_ref.shape, F32)
        extc_ref[...] = jnp.zeros(extc_ref.shape, F32)
        _repack_w_in_rest(w_ref, wr_ref)

    _run_stages([_inproj_stages(x_ref, nw_ref, w_ref, wr_ref, lnw_ref, lnb_ref, ws_ref, bs_ref, scw_ref, dcw_ref,
                                alog_ref, dtb_ref, ea_ref, eb_ref, bd_ref, wg_ref, gb_ref,
                                ya_ref, yb_ref, cpk_ref, cg_ref, dpk_ref, dla_ref, extb_ref, extc_ref,
                                convb_ref, convc_ref)])


def _inproj(x, nw, w_in_b, layer, lnw, lnb, ws, bs_wide, scw, dcw, alog_n, dtb_n, ea, eb, bd, wg_pad, gb,
            cast_jobs=()):
    nb, seq, _ = x.shape
    tm = SEQ_TILE
    n_tiles = seq // tm
    assert n_tiles >= CAST_BLOCKS or not cast_jobs
    tile = lambda i: (0, i, 0)
    whole = lambda a: pl.BlockSpec(a.shape, (lambda i: (0,) * a.ndim))
    per_layer = lambda a: _layer_param(a, layer)
    outs = ((GROUP, BF16), (GROUP, BF16), (C_PACK, BF16), (GROUP, F32), (D_PACK, BF16), (GROUP, F32))
    cast_in_specs, cast_out_specs, cast_out_shapes = _cast_specs(cast_jobs)
    return pl.pallas_call(
        functools.partial(_inproj_kernel, len(cast_jobs)),
        grid=(n_tiles,),
        in_specs=[pl.BlockSpec((nb, tm, D_MODEL), tile),
                  per_layer(nw),
                  _layer_block(w_in_b, layer),
                  per_layer(lnw), per_layer(lnb), per_layer(ws), per_layer(bs_wide), per_layer(scw),
                  per_layer(dcw), per_layer(alog_n), per_layer(dtb_n), whole(ea), whole(eb), whole(bd),
                  per_layer(wg_pad), per_layer(gb)] + cast_in_specs,
        out_specs=[pl.BlockSpec((nb, tm, n), tile) for n, _ in outs] + cast_out_specs,
        out_shape=[jax.ShapeDtypeStruct((nb, seq, n), dt) for n, dt in outs] + cast_out_shapes,
        scratch_shapes=[pltpu.VMEM((nb, GROUP // LANES, TAIL + tm, LANES), F32),
                        pltpu.VMEM((nb, 3 * GROUP // LANES, TAIL + tm, LANES), F32),
                        pltpu.VMEM((GROUP // LANES, tm, LANES), F32),
                        pltpu.VMEM((3 * GROUP // LANES, tm, LANES), F32),
                        pltpu.VMEM((D_MODEL, REST_COLS), BF16)],
        compiler_params=pltpu.CompilerParams(dimension_semantics=("arbitrary",),
                                             vmem_limit_bytes=VMEM_LIMIT),
        name="in_proj",
    )(x, nw, w_in_b, lnw, lnb, ws, bs_wide, scw, dcw, alog_n, dtb_n, ea, eb, bd, wg_pad, gb,
      *[w for w, _, _ in cast_jobs])


def _chunk_masks():
    i = lax.broadcasted_iota(jnp.int32, (CHUNK, GROUP), 0)
    j = lax.broadcasted_iota(jnp.int32, (CHUNK, GROUP), 1) % CHUNK
    return i == j, i >= j, i > j


def _const_inputs():
    lane_head = np.arange(GROUP) // HEAD_DIM
    bd = (lane_head[:, None] == lane_head[None, :]).astype(np.float32)
    ltri = np.tril(np.ones((CHUNK, CHUNK), np.float32))
    return jnp.asarray(bd, BF16), jnp.asarray(ltri, BF16)


def _chunk_rows(c):
    return slice(c * CHUNK, (c + 1) * CHUNK)


def _gdn_stages(cpk_ref, cg_ref, nw_ref, bd_ref, ltri_ref, y_ref, ac_ref, bb_ref, d_ref, dec_ref, o_ref, s_ref):
    nb, tm = cpk_ref.shape[0], cpk_ref.shape[1]
    cps = tm // CHUNK
    eye_w, causal_w, strict_w = _chunk_masks()
    eye_f = eye_w.astype(F32)
    ltri = ltri_ref[...]
    bd = bd_ref[...]

    yield from _gdn_prepare([(b, c) for b in range(nb) for c in range(cps)], cps, cpk_ref, cg_ref,
                            ac_ref, bb_ref, d_ref, dec_ref, eye_w, causal_w, strict_w, eye_f, ltri, bd)

    for c in range(cps):
        for b in range(nb):
            i = b * cps + c
            s = s_ref[b]
            r = _dot(ac_ref[i], _blockdiag(s, bd))
            s_ref[b] = s * dec_ref[i, 0:1, :] + r[0:CHUNK] + bb_ref[i]
            o_ref[b, _chunk_rows(c), :] = r[CHUNK:2 * CHUNK] + d_ref[i]
        yield

    for b in range(nb):
        o = o_ref[b]
        zg = cpk_ref[b, :, 4 * GROUP:5 * GROUP].astype(F32)
        y_ref[b] = (o * lax.rsqrt(_head_mean_sq(o, bd) + EPS) * nw_ref[...] * zg).astype(y_ref.dtype)


def _gdn_prepare(items, cps, cpk_ref, cg_ref, ac_ref, bb_ref, d_ref, dec_ref,
                 eye_w, causal_w, strict_w, eye_f, ltri, bd):
    n = len(items)

    def load(col, i):
        b, c = items[i]
        return cpk_ref[b, _chunk_rows(c), col * GROUP:(col + 1) * GROUP]

    qn = [load(0, i) for i in range(n)]
    kn = [load(1, i) for i in range(n)]
    beta = [load(3, i) for i in range(n)]
    gc = [_dot_sel_lhs(ltri, cg_ref[b, _chunk_rows(c), :]) for b, c in items]
    yield
    eg = [jnp.exp(x) for x in gc]
    kb = [kn[i] * beta[i] for i in range(n)]
    eye_b = eye_f.astype(BF16)
    r1 = [_dot_nt(jnp.concatenate([kb[i], qn[i], eye_b], axis=0), _blockdiag(kn[i], bd))
          for i in range(n)]
    yield
    low, attn, g_row = [], [], []
    for i in range(n):
        g_row.append(jnp.sum(jnp.where(eye_w, gc[i], 0.0), axis=0, keepdims=True))
        decay = jnp.where(causal_w, jnp.exp(jnp.minimum(gc[i] - g_row[i], 0.0)), 0.0)
        low.append(jnp.where(strict_w, r1[i][0:CHUNK] * decay, 0.0))
        attn.append((r1[i][CHUNK:2 * CHUNK] * decay).astype(BF16))
    t = [eye_f - x for x in low]
    p = [_dot(x.astype(BF16), _blockdiag(x, bd)) for x in low]
    yield
    n_levels = CHUNK.bit_length() - 1
    for level in range(1, n_levels):
        if level < n_levels - 1:
            r2 = [_dot(jnp.concatenate([p[i], t[i]], axis=0).astype(BF16), _blockdiag(p[i], bd))
                  for i in range(n)]
            p = [x[0:CHUNK] for x in r2]
            t = [t[i] + r2[i][CHUNK:2 * CHUNK] for i in range(n)]
        else:
            t = [t[i] + _dot(t[i].astype(BF16), _blockdiag(p[i], bd)) for i in range(n)]
        yield
    wu = [_dot(t[i].astype(BF16),
               jnp.concatenate([_blockdiag(kb[i] * eg[i], bd), _blockdiag(load(2, i) * beta[i], bd)], axis=1))
          for i in range(n)]
    yield
    gc_last = [x[CHUNK - 1:CHUNK, :] for x in gc]
    kd_t = [(r1[i][2 * CHUNK:3 * CHUNK] * jnp.exp(gc_last[i] - g_row[i])).astype(BF16) for i in range(n)]
    prod = [_dot(jnp.concatenate([attn[i], kd_t[i]], axis=0),
                 jnp.concatenate([_blockdiag(wu[i][:, 0:GROUP], bd),
                                  _blockdiag(wu[i][:, GROUP:2 * GROUP], bd)], axis=1))
            for i in range(n)]
    for i, (b, c) in enumerate(items):
        j = b * cps + c
        ac_ref[j, 0:CHUNK, :] = (-prod[i][CHUNK:2 * CHUNK, 0:GROUP]).astype(BF16)
        bb_ref[j] = prod[i][CHUNK:2 * CHUNK, GROUP:2 * GROUP]
        ac_ref[j, CHUNK:2 * CHUNK, :] = (qn[i] * eg[i] - prod[i][0:CHUNK, 0:GROUP]).astype(BF16)
        d_ref[j] = prod[i][0:CHUNK, GROUP:2 * GROUP]
        dec_ref[j] = jnp.broadcast_to(jnp.exp(gc_last[i]), (TAIL, GROUP))
    yield


def _gla_stages(dpk_ref, dla_ref, nw_ref, bd_ref, ltri_ref, y_ref, o_ref, st_ref):
    nb, tm = dpk_ref.shape[0], dpk_ref.shape[1]
    cps = tm // CHUNK
    _, causal_w, _ = _chunk_masks()
    ltri = ltri_ref[...]
    bd = bd_ref[...]
    bdf = bd.astype(F32)
    mid = CHUNK // 2

    items = [(b, c) for c in range(cps) for b in range(nb)]
    n = len(items)
    q = [dpk_ref[b, _chunk_rows(c), 0:GROUP].astype(F32) * (HEAD_DIM ** -0.5) for b, c in items]
    k = [dpk_ref[b, _chunk_rows(c), GROUP:2 * GROUP].astype(F32) for b, c in items]
    v = [dpk_ref[b, _chunk_rows(c), 2 * GROUP:3 * GROUP] for b, c in items]
    gcum = [_dot_sel_lhs(ltri, dla_ref[b, _chunk_rows(c), :]) for b, c in items]
    yield
    g_mid = [x[mid:mid + 1, :] for x in gcum]
    g_last = [x[CHUNK - 1:CHUNK, :] for x in gcum]
    attn = [jnp.where(causal_w,
                      _dot_nt((q[i] * jnp.exp(gcum[i] - g_mid[i])).astype(BF16),
                              _blockdiag(k[i] * jnp.exp(g_mid[i] - gcum[i]), bd)), 0.0).astype(BF16)
            for i in range(n)]
    yield
    upd = [bdf * _dot_tn(v[i], (k[i] * jnp.exp(g_last[i] - gcum[i])).astype(BF16)) for i in range(n)]
    yield
    o_intra = [_dot(attn[i], _blockdiag(v[i], bd)) for i in range(n)]
    qg = [(q[i] * jnp.exp(gcum[i])).astype(BF16) for i in range(n)]
    yield
    for i, (b, c) in enumerate(items):
        st = st_ref[b]
        o_ref[b, _chunk_rows(c), :] = o_intra[i] + _dot_nt(qg[i], st.astype(BF16))
        st_ref[b] = st * jnp.exp(g_last[i]) + upd[i]
        if b == nb - 1:
            yield

    for b in range(nb):
        o = o_ref[b]
        zg = dpk_ref[b, :, 3 * GROUP:4 * GROUP].astype(F32)
        y_ref[b] = (o * lax.rsqrt(_head_mean_sq(o, bd) + EPS) * nw_ref[...] * zg).astype(y_ref.dtype)


def _recur_kernel(cpk_ref, cg_ref, dpk_ref, dla_ref, nwc_ref, nwd_ref, bd_ref, ltri_ref, yc_ref, yd_ref,
                  ac_ref, bb_ref, d_ref, dec_ref, oc_ref, s_ref, od_ref, st_ref):
    @pl.when(pl.program_id(0) == 0)
    def _():
        s_ref[...] = jnp.zeros(s_ref.shape, F32)
        st_ref[...] = jnp.zeros(st_ref.shape, F32)

    _run_stages([_gla_stages(dpk_ref, dla_ref, nwd_ref, bd_ref, ltri_ref, yd_ref, od_ref, st_ref),
                 _gdn_stages(cpk_ref, cg_ref, nwc_ref, bd_ref, ltri_ref, yc_ref,
                             ac_ref, bb_ref, d_ref, dec_ref, oc_ref, s_ref)])


def _recur(cpk, cg, dpk, dla, nwc, nwd, bd, ltri, layer):
    nb, seq, _ = cpk.shape
    tm = SEQ_TILE
    n_chunks = nb * tm // CHUNK
    tile = lambda i: (0, i, 0)
    whole = lambda a: pl.BlockSpec(a.shape, (lambda i: (0,) * a.ndim))
    out = jax.ShapeDtypeStruct((nb, seq, GROUP), BF16)
    return pl.pallas_call(
        _recur_kernel,
        grid=(seq // tm,),
        in_specs=[pl.BlockSpec((nb, tm, C_PACK), tile),
                  pl.BlockSpec((nb, tm, GROUP), tile),
                  pl.BlockSpec((nb, tm, D_PACK), tile),
                  pl.BlockSpec((nb, tm, GROUP), tile),
                  _layer_param(nwc, layer), _layer_param(nwd, layer), whole(bd), whole(ltri)],
        out_specs=[pl.BlockSpec((nb, tm, GROUP), tile)] * 2,
        out_shape=[out, out],
        scratch_shapes=[pltpu.VMEM((n_chunks, 2 * CHUNK, GROUP), BF16),
                        pltpu.VMEM((n_chunks, CHUNK, GROUP), F32),
                        pltpu.VMEM((n_chunks, CHUNK, GROUP), F32),
                        pltpu.VMEM((n_chunks, TAIL, GROUP), F32),
                        pltpu.VMEM((nb, tm, GROUP), F32),
                        pltpu.VMEM((nb, CHUNK, GROUP), F32),
                        pltpu.VMEM((nb, tm, GROUP), F32),
                        pltpu.VMEM((nb, GROUP, GROUP), F32)],
        compiler_params=pltpu.CompilerParams(dimension_semantics=("arbitrary",),
                                             vmem_limit_bytes=VMEM_LIMIT),
        name="recur",
    )(cpk, cg, dpk, dla, nwc, nwd, bd, ltri)


def _outproj_ffn_kernel(final, n_cast, x_ref, ya_ref, yb_ref, yc_ref, yd_ref, wo_ref, n2_ref, wgu_ref, wd_ref,
                        fn_ref, *rest):
    cast_in, (o_ref,), cast_out, (act_ref,) = _split_refs(rest, n_cast, 1, n_cast, 1)
    _run_casts(cast_in, cast_out)
    x1 = x_ref[...]
    for idx, y_ref in enumerate((ya_ref, yb_ref, yc_ref, yd_ref)):
        x1 = x1 + _dot(y_ref[...], wo_ref[idx * GROUP:(idx + 1) * GROUP, :])
    ms = jnp.mean(x1 * x1, axis=-1, keepdims=True)
    h = (x1 * lax.rsqrt(ms + EPS) * n2_ref[...]).astype(BF16)
    for j in range(D_FF // FF_CHUNK):
        cols = slice(j * FF_CHUNK, (j + 1) * FF_CHUNK)
        gate = _dot(h, wgu_ref[:, cols])
        up = _dot(h, wgu_ref[:, D_FF + j * FF_CHUNK:D_FF + (j + 1) * FF_CHUNK])
        act_ref[:, cols] = (_silu(gate) * up).astype(BF16)
    x2 = x1 + _dot(act_ref[...], wd_ref[...])
    if final:
        ms2 = jnp.mean(x2 * x2, axis=-1, keepdims=True)
        x2 = x2 * lax.rsqrt(ms2 + EPS) * fn_ref[...]
    o_ref[...] = x2


def _outproj_ffn(x, ys, wo, n2, wgu, wd, fn, layer, final, cast_jobs=()):
    m = x.shape[0]
    n_steps = m // ROW_TILE
    assert n_steps >= CAST_BLOCKS or not cast_jobs
    row = lambda i: (i, 0)
    fixed = lambda i: (0, 0)
    cast_in_specs, cast_out_specs, cast_out_shapes = _cast_specs(cast_jobs)
    outs = pl.pallas_call(
        functools.partial(_outproj_ffn_kernel, final, len(cast_jobs)),
        grid=(n_steps,),
        in_specs=[pl.BlockSpec((ROW_TILE, D_MODEL), row)]
                 + [pl.BlockSpec((ROW_TILE, GROUP), row)] * 4
                 + [_resident(wo), _layer_param(n2, layer), _resident(wgu), _resident(wd),
                    pl.BlockSpec((1, D_MODEL), fixed)]
                 + cast_in_specs,
        out_specs=[pl.BlockSpec((ROW_TILE, D_MODEL), row)] + cast_out_specs,
        out_shape=[jax.ShapeDtypeStruct((m, D_MODEL), F32)] + cast_out_shapes,
        scratch_shapes=[pltpu.VMEM((ROW_TILE, D_FF), BF16)],
        compiler_params=pltpu.CompilerParams(dimension_semantics=("arbitrary",),
                                             vmem_limit_bytes=VMEM_LIMIT),
        name="outproj_ffn",
    )(x, *ys, wo, n2, wgu, wd, fn, *[w for w, _, _ in cast_jobs])
    return outs


def _gate_vec(v):
    return jnp.pad(v.astype(F32), ((0, 0), (0, GATE_COLS - HEADS)))[:, None, :]


def _row(v):
    return v[:, None, :]


def _gate_selectors():
    ea = np.zeros((GATE_COLS, GROUP), np.float32)
    eb = np.zeros((GATE_COLS, GROUP), np.float32)
    for h in range(HEADS):
        ea[h, h * HEAD_DIM:(h + 1) * HEAD_DIM] = 1.0
        eb[HEADS + h, h * HEAD_DIM:(h + 1) * HEAD_DIM] = 1.0
    return jnp.asarray(ea, BF16), jnp.asarray(eb, BF16)


def kernel(x, norm1_w, w_in, sgu_ln_w, sgu_ln_b, sgu_w_spatial, sgu_b_spatial, sc_conv_w, dn_conv_w, dn_a_log, dn_dt_bias, dn_norm_w, gla_w_gate2, gla_gate_bias, gla_norm_w, w_out, norm2_w, w_gate_up, w_down, final_norm_w):
    bsz, seq, d = x.shape
    depth = w_in.shape[0]
    assert seq % ROW_TILE == 0 and seq % SEQ_TILE == 0 and d == D_MODEL
    m = bsz * seq
    xf = x.reshape(m, d)
    bd, ltri = _const_inputs()
    ea, eb = _gate_selectors()
    w_in_b = w_in.astype(BF16)
    ffn_jobs = lambda layer: [(w_out, layer, D_MODEL), (w_gate_up, layer, 2 * D_FF), (w_down, layer, D_MODEL)]
    bs_wide = jnp.repeat(jnp.swapaxes(sgu_b_spatial, 1, 2), HEAD_DIM, axis=2)
    wg_pad = jnp.pad(gla_w_gate2.astype(BF16),
                     ((0, 0), (2 * HEADS, GATE_COLS - 2 * HEADS - GATE_RANK), (0, 0)))
    inproj_params = (_row(sgu_ln_w), _row(sgu_ln_b), sgu_w_spatial, bs_wide, sc_conv_w, dn_conv_w,
                     _gate_vec(dn_a_log), _gate_vec(dn_dt_bias), ea, eb, bd, wg_pad, _row(gla_gate_bias))
    nwc, nwd = _row(jnp.tile(dn_norm_w, (1, HEADS))), _row(jnp.tile(gla_norm_w, (1, HEADS)))
    norm1, norm2 = _row(norm1_w), _row(norm2_w)
    ffn_w = None
    for l in range(depth):
        ya, yb, cpk, cg, dpk, dla, *cast = _inproj(xf.reshape(bsz, seq, d), norm1, w_in_b, l,
                                                   *inproj_params, cast_jobs=ffn_jobs(0) if l == 0 else ())
        if l == 0:
            ffn_w = cast
        yc, yd = _recur(cpk, cg, dpk, dla, nwc, nwd, bd, ltri, l)
        last = l == depth - 1
        xf, *cast = _outproj_ffn(xf, tuple(y.reshape(m, GROUP) for y in (ya, yb, yc, yd)),
                                 ffn_w[0], norm2, ffn_w[1], ffn_w[2], final_norm_w[None, :], layer=l, final=last,
                                 cast_jobs=() if last else ffn_jobs(l + 1))
        if not last:
            ffn_w = cast
    return xf.reshape(bsz, seq, d)
```

```python
import functools

import numpy as np
import jax
import jax.numpy as jnp
from jax import lax
from jax.experimental import pallas as pl
from jax.experimental.pallas import tpu as pltpu

F32 = jnp.float32
BF16 = jnp.bfloat16

D_MODEL = 1024
GROUP = 256
HEADS = 4
HEAD_DIM = 64
SGU_CHUNK = 128
SC_WIDTH = 3
DN_CONV_WIDTH = 4
CHUNK = 64
GATE_RANK = 16
GATE_TEMP = 16.0
D_FF = 2816
EPS = 1e-6
GATE_COLS = 128
TAIL = 8

ROW_TILE = 1024
SEQ_TILE = 512
FF_CHUNK = 256
VMEM_LIMIT = 60 * 1024 * 1024

MAIN_A = 0
MAIN_B = MAIN_A + 2 * GROUP
MAIN_C = MAIN_B + 3 * GROUP
MAIN_COLS = MAIN_C + 3 * GROUP
REST_CZ = 0
REST_D = REST_CZ + GROUP
REST_G = REST_D + 4 * GROUP
REST_COLS = REST_G + GATE_COLS
C_PACK = 5 * GROUP
D_PACK = 4 * GROUP


def _dot(a, b):
    return jnp.dot(a, b, preferred_element_type=F32)


def _dot_nt(a, b):
    return lax.dot_general(a, b, (((1,), (1,)), ((), ())), preferred_element_type=F32)


def _dot_tn(a, b):
    return lax.dot_general(a, b, (((0,), (0,)), ((), ())), preferred_element_type=F32)


def _split2(x):
    hi = x.astype(BF16)
    lo = (x - hi.astype(F32)).astype(BF16)
    return hi, lo


def _dot_sel_rhs(x, sel):
    hi, lo = _split2(x)
    return _dot(hi, sel) + _dot(lo, sel)


def _dot_sel_lhs(sel, x):
    hi, lo = _split2(x)
    return _dot(sel, hi) + _dot(sel, lo)


def _softplus(x):
    return jnp.maximum(x, 0.0) + jnp.log(1.0 + jnp.exp(-jnp.abs(x)))


def _sigmoid(x):
    return 0.5 + 0.5 * jnp.tanh(0.5 * x)


def _silu(x):
    h = 0.5 * x
    return h + h * jnp.tanh(h)


def _gelu_tanh(x):
    h = 0.5 * x
    return h + h * jnp.tanh(x * (0.7978845608028654 + (0.7978845608028654 * 0.044715) * (x * x)))


def _blockdiag(x, mask):
    xb = x.astype(BF16)
    return jnp.concatenate([xb] * HEADS, axis=0) * mask


def _head_mean_sq(x, bones):
    return _dot((x * x).astype(BF16), bones) * (1.0 / HEAD_DIM)


LANES = 128


def _causal_conv(x, ext_ref, out_ref, cw_ref, width):
    tm, ch = x.shape
    half = tm // 2
    n_slabs = ch // LANES
    for s in range(n_slabs):
        ext_ref[s, TAIL:TAIL + tm, :] = x[:, s * LANES:(s + 1) * LANES]
    for s in range(n_slabs):
        lanes = slice(s * LANES, (s + 1) * LANES)
        for parity in (0, 1):
            acc = None
            for j in range(width):
                tap = (ext_ref[s, pl.ds(TAIL + parity - j, half, stride=2), :]
                       * cw_ref[width - 1 - j:width - j, lanes])
                acc = tap if acc is None else acc + tap
            out_ref[s, pl.ds(parity, half, stride=2), :] = acc
    for s in range(n_slabs):
        ext_ref[s, 0:TAIL, :] = x[tm - TAIL:tm, s * LANES:(s + 1) * LANES]
    return jnp.concatenate([out_ref[s] for s in range(n_slabs)], axis=1)


def _layer_param(stacked, layer):
    zeros = (0,) * (stacked.ndim - 1)
    return pl.BlockSpec((None,) + stacked.shape[1:], lambda i: (layer,) + zeros)


def _layer_block(stacked, layer):
    zeros = (0,) * (stacked.ndim - 1)
    return pl.BlockSpec((None,) + stacked.shape[1:], lambda i: (layer,) + zeros,
                        pipeline_mode=pl.Buffered(1))


def _resident(a):
    return pl.BlockSpec(a.shape, lambda i: (0, 0), pipeline_mode=pl.Buffered(1))


CAST_BLOCKS = 16


def _cast_specs(cast_jobs):
    last = CAST_BLOCKS - 1
    in_specs, out_specs, out_shapes = [], [], []
    for stacked, layer, n_cols in cast_jobs:
        rows = stacked.shape[1]
        assert rows % (16 * CAST_BLOCKS) == 0 and n_cols % 128 == 0
        blk = rows // CAST_BLOCKS
        in_specs.append(pl.BlockSpec((None, blk, n_cols),
                                     lambda i, layer=layer: (layer, jnp.minimum(i, last), 0)))
        out_specs.append(pl.BlockSpec((blk, n_cols), lambda i: (jnp.minimum(i, last), 0)))
        out_shapes.append(jax.ShapeDtypeStruct((rows, n_cols), BF16))
    return in_specs, out_specs, out_shapes


def _run_casts(cast_in, cast_out):
    for src, dst in zip(cast_in, cast_out):
        dst[...] = src[...].astype(dst.dtype)


def _split_refs(refs, *counts):
    assert sum(counts) == len(refs)
    groups, pos = [], 0
    for c in counts:
        groups.append(tuple(refs[pos:pos + c]))
        pos += c
    return groups


def _run_stages(stages):
    stages = list(stages)
    while stages:
        for gen in list(stages):
            try:
                next(gen)
            except StopIteration:
                stages.remove(gen)


def _inproj_stages(x_ref, nw_ref, w_ref, wr_ref, lnw_ref, lnb_ref, ws_ref, bs_ref, scw_ref,
                   dcw_ref, alog_ref, dtb_ref, ea_ref, eb_ref, bd_ref, wg_ref, gb_ref,
                   ya_ref, yb_ref, cpk_ref, cg_ref, dpk_ref, dla_ref, extb_ref, extc_ref, convb_ref, convc_ref):
    nb, tm = x_ref.shape[0], x_ref.shape[1]
    m = nb * tm
    batch_rows = [slice(b * tm, (b + 1) * tm) for b in range(nb)]

    x = x_ref[...].reshape(m, D_MODEL)
    ms = jnp.mean(x * x, axis=-1, keepdims=True)
    h = (x * lax.rsqrt(ms + EPS) * nw_ref[...]).astype(BF16)
    bd = bd_ref[...]

    def project(ref, col0, n_slabs):
        parts = []
        for c in range(col0, col0 + n_slabs * GROUP, GROUP):
            parts.append(_dot(h, ref[:, c:c + GROUP]))
            yield
        return jnp.concatenate(parts, axis=1)

    pc = yield from project(w_ref, MAIN_C, 3)
    pcz = yield from project(wr_ref, REST_CZ, 1)
    pg = _dot(h, wr_ref[:, REST_G:REST_G + GATE_COLS])
    pa = yield from project(w_ref, MAIN_A, 2)

    qkv = jnp.concatenate(
        [_silu(_causal_conv(pc[rows, 0:3 * GROUP], extc_ref.at[b], convc_ref, dcw_ref, DN_CONV_WIDTH))
         for b, rows in enumerate(batch_rows)], axis=0)
    q = qkv[:, 0:GROUP]
    k = qkv[:, GROUP:2 * GROUP]
    pb = yield from project(w_ref, MAIN_B, 3)
    qn = q * lax.rsqrt(_dot((q * q).astype(BF16), bd) + EPS) * (HEAD_DIM ** -0.5)
    kn = k * lax.rsqrt(_dot((k * k).astype(BF16), bd) + EPS)
    yield
    g_narrow = -jnp.exp(alog_ref[...]) * _softplus(pg + dtb_ref[...])
    g_wide = _dot_sel_rhs(g_narrow, ea_ref[...])
    beta = _dot(_sigmoid(pg).astype(BF16), eb_ref[...])
    zg_c = _silu(pcz)
    pre = _dot(pg.astype(BF16), wg_ref[...]) + gb_ref[...]
    log_a = -_softplus(-pre) * (1.0 / GATE_TEMP)
    yield

    tri = (lax.broadcasted_iota(jnp.int32, (SGU_CHUNK, SGU_CHUNK), 0)
           >= lax.broadcasted_iota(jnp.int32, (SGU_CHUNK, SGU_CHUNK), 1))
    head_of_lane = lax.broadcasted_iota(jnp.int32, (SGU_CHUNK, GROUP), 1) // HEAD_DIM
    ws = jnp.concatenate([jnp.where(tri, ws_ref[hd], 0.0).astype(BF16) for hd in range(HEADS)],
                         axis=1)
    for b in range(nb):
        for c in range(tm // SGU_CHUNK):
            rows = slice(b * tm + c * SGU_CHUNK, b * tm + (c + 1) * SGU_CHUNK)
            u = _gelu_tanh(pa[rows, 0:GROUP])
            v = _gelu_tanh(pa[rows, GROUP:2 * GROUP])
            mu = jnp.mean(v, axis=-1, keepdims=True)
            vc = v - mu
            var = jnp.mean(vc * vc, axis=-1, keepdims=True)
            vn = vc * lax.rsqrt(var + EPS) * lnw_ref[...] + lnb_ref[...]
            vn_heads = jnp.concatenate([jnp.where(head_of_lane == hd, vn, 0.0).astype(BF16)
                                        for hd in range(HEADS)], axis=0)
            mixed = bs_ref[...] + _dot(ws, vn_heads)
            ya_ref[b, c * SGU_CHUNK:(c + 1) * SGU_CHUNK, :] = (u * mixed).astype(ya_ref.dtype)
            yield

    pd = yield from project(wr_ref, REST_D, 4)

    for b, rows in enumerate(batch_rows):
        conv_b = _causal_conv(pb[rows, GROUP:2 * GROUP] * pb[rows, 2 * GROUP:3 * GROUP],
                              extb_ref.at[b], convb_ref, scw_ref, SC_WIDTH)
        yb_ref[b] = (pb[rows, 0:GROUP] * conv_b).astype(yb_ref.dtype)

    for b, rows in enumerate(batch_rows):
        cg_ref[b] = g_wide[rows]
        cpk_ref[b, :, 0:GROUP] = qn[rows].astype(BF16)
        cpk_ref[b, :, GROUP:2 * GROUP] = kn[rows].astype(BF16)
        cpk_ref[b, :, 2 * GROUP:3 * GROUP] = qkv[rows, 2 * GROUP:3 * GROUP].astype(BF16)
        cpk_ref[b, :, 3 * GROUP:4 * GROUP] = beta[rows].astype(BF16)
        cpk_ref[b, :, 4 * GROUP:5 * GROUP] = zg_c[rows].astype(BF16)
        dla_ref[b] = log_a[rows]
        dpk_ref[b, :, 0:3 * GROUP] = pd[rows, 0:3 * GROUP].astype(BF16)
        dpk_ref[b, :, 3 * GROUP:4 * GROUP] = _silu(pd[rows, 3 * GROUP:4 * GROUP]).astype(BF16)


def _repack_w_in_rest(w_ref, wr_ref):
    c_a = MAIN_COLS
    c_z = c_a + 2 * HEADS
    d0 = c_z + GROUP
    d_g = d0 + 3 * GROUP
    d_z = d_g + GATE_RANK
    wr_ref[:, REST_CZ:REST_CZ + GROUP] = w_ref[:, c_z:d0]
    wr_ref[:, REST_D:REST_D + 3 * GROUP] = w_ref[:, d0:d_g]
    wr_ref[:, REST_D + 3 * GROUP:REST_D + 4 * GROUP] = w_ref[:, d_z:d_z + GROUP]
    wr_ref[:, REST_G:REST_G + GATE_COLS] = jnp.zeros((wr_ref.shape[0], GATE_COLS), wr_ref.dtype)
    wr_ref[:, REST_G:REST_G + 2 * HEADS] = w_ref[:, c_a:c_z]
    wr_ref[:, REST_G + 2 * HEADS:REST_G + 2 * HEADS + GATE_RANK] = w_ref[:, d_g:d_z]


def _inproj_kernel(n_cast, x_ref, nw_ref, w_ref, lnw_ref, lnb_ref, ws_ref, bs_ref, scw_ref, dcw_ref,
                   alog_ref, dtb_ref, ea_ref, eb_ref, bd_ref, wg_ref, gb_ref, *rest):
    cast_in, outs, cast_out, (extb_ref, extc_ref, convb_ref, convc_ref, wr_ref) = _split_refs(
        rest, n_cast, 6, n_cast, 5)
    ya_ref, yb_ref, cpk_ref, cg_ref, dpk_ref, dla_ref = outs
    _run_casts(cast_in, cast_out)

    @pl.when(pl.program_id(0) == 0)
    def _():
        extb_ref[...] = jnp.zeros(extb_ref.shape, F32)
        extc_ref[...] = jnp.zeros(extc_ref.shape, F32)
        _repack_w_in_rest(w_ref, wr_ref)

    _run_stages([_inproj_stages(x_ref, nw_ref, w_ref, wr_ref, lnw_ref, lnb_ref, ws_ref, bs_ref, scw_ref, dcw_ref,
                                alog_ref, dtb_ref, ea_ref, eb_ref, bd_ref, wg_ref, gb_ref,
                                ya_ref, yb_ref, cpk_ref, cg_ref, dpk_ref, dla_ref, extb_ref, extc_ref,
                                convb_ref, convc_ref)])


def _inproj(x, nw, w_in_b, layer, lnw, lnb, ws, bs_wide, scw, dcw, alog_n, dtb_n, ea, eb, bd, wg_pad, gb,
            cast_jobs=()):
    nb, seq, _ = x.shape
    tm = SEQ_TILE
    n_tiles = seq // tm
    assert n_tiles >= CAST_BLOCKS or not cast_jobs
    tile = lambda i: (0, i, 0)
    whole = lambda a: pl.BlockSpec(a.shape, (lambda i: (0,) * a.ndim))
    per_layer = lambda a: _layer_param(a, layer)
    outs = ((GROUP, BF16), (GROUP, BF16), (C_PACK, BF16), (GROUP, F32), (D_PACK, BF16), (GROUP, F32))
    cast_in_specs, cast_out_specs, cast_out_shapes = _cast_specs(cast_jobs)
    return pl.pallas_call(
        functools.partial(_inproj_kernel, len(cast_jobs)),
        grid=(n_tiles,),
        in_specs=[pl.BlockSpec((nb, tm, D_MODEL), tile),
                  per_layer(nw),
                  _layer_block(w_in_b, layer),
                  per_layer(lnw), per_layer(lnb), per_layer(ws), per_layer(bs_wide), per_layer(scw),
                  per_layer(dcw), per_layer(alog_n), per_layer(dtb_n), whole(ea), whole(eb), whole(bd),
                  per_layer(wg_pad), per_layer(gb)] + cast_in_specs,
        out_specs=[pl.BlockSpec((nb, tm, n), tile) for n, _ in outs] + cast_out_specs,
        out_shape=[jax.ShapeDtypeStruct((nb, seq, n), dt) for n, dt in outs] + cast_out_shapes,
        scratch_shapes=[pltpu.VMEM((nb, GROUP // LANES, TAIL + tm, LANES), F32),
                        pltpu.VMEM((nb, 3 * GROUP // LANES, TAIL + tm, LANES), F32),
                        pltpu.VMEM((GROUP // LANES, tm, LANES), F32),
                        pltpu.VMEM((3 * GROUP // LANES, tm, LANES), F32),
                        pltpu.VMEM((D_MODEL, REST_COLS), BF16)],
        compiler_params=pltpu.CompilerParams(dimension_semantics=("arbitrary",),
                                             vmem_limit_bytes=VMEM_LIMIT),
        name="in_proj",
    )(x, nw, w_in_b, lnw, lnb, ws, bs_wide, scw, dcw, alog_n, dtb_n, ea, eb, bd, wg_pad, gb,
      *[w for w, _, _ in cast_jobs])


def _chunk_masks():
    i = lax.broadcasted_iota(jnp.int32, (CHUNK, GROUP), 0)
    j = lax.broadcasted_iota(jnp.int32, (CHUNK, GROUP), 1) % CHUNK
    return i == j, i >= j, i > j


def _const_inputs():
    lane_head = np.arange(GROUP) // HEAD_DIM
    bd = (lane_head[:, None] == lane_head[None, :]).astype(np.float32)
    ltri = np.tril(np.ones((CHUNK, CHUNK), np.float32))
    return jnp.asarray(bd, BF16), jnp.asarray(ltri, BF16)


def _chunk_rows(c):
    return slice(c * CHUNK, (c + 1) * CHUNK)


def _gdn_stages(cpk_ref, cg_ref, nw_ref, bd_ref, ltri_ref, y_ref, ac_ref, bb_ref, d_ref, dec_ref, o_ref, s_ref):
    nb, tm = cpk_ref.shape[0], cpk_ref.shape[1]
    cps = tm // CHUNK
    eye_w, causal_w, strict_w = _chunk_masks()
    eye_f = eye_w.astype(F32)
    ltri = ltri_ref[...]
    bd = bd_ref[...]

    yield from _gdn_prepare([(b, c) for b in range(nb) for c in range(cps)], cps, cpk_ref, cg_ref,
                            ac_ref, bb_ref, d_ref, dec_ref, eye_w, causal_w, strict_w, eye_f, ltri, bd)

    for c in range(cps):
        for b in range(nb):
            i = b * cps + c
            s = s_ref[b]
            r = _dot(ac_ref[i], _blockdiag(s, bd))
            s_ref[b] = s * dec_ref[i, 0:1, :] + r[0:CHUNK] + bb_ref[i]
            o_ref[b, _chunk_rows(c), :] = r[CHUNK:2 * CHUNK] + d_ref[i]
        yield

    for b in range(nb):
        o = o_ref[b]
        zg = cpk_ref[b, :, 4 * GROUP:5 * GROUP].astype(F32)
        y_ref[b] = (o * lax.rsqrt(_head_mean_sq(o, bd) + EPS) * nw_ref[...] * zg).astype(y_ref.dtype)


def _gdn_prepare(items, cps, cpk_ref, cg_ref, ac_ref, bb_ref, d_ref, dec_ref,
                 eye_w, causal_w, strict_w, eye_f, ltri, bd):
    n = len(items)

    def load(col, i):
        b, c = items[i]
        return cpk_ref[b, _chunk_rows(c), col * GROUP:(col + 1) * GROUP]

    qn = [load(0, i) for i in range(n)]
    kn = [load(1, i) for i in range(n)]
    beta = [load(3, i) for i in range(n)]
    gc = [_dot_sel_lhs(ltri, cg_ref[b, _chunk_rows(c), :]) for b, c in items]
    yield
    eg = [jnp.exp(x) for x in gc]
    kb = [kn[i] * beta[i] for i in range(n)]
    eye_b = eye_f.astype(BF16)
    r1 = [_dot_nt(jnp.concatenate([kb[i], qn[i], eye_b], axis=0), _blockdiag(kn[i], bd))
          for i in range(n)]
    yield
    low, attn, g_row = [], [], []
    for i in range(n):
        g_row.append(jnp.sum(jnp.where(eye_w, gc[i], 0.0), axis=0, keepdims=True))
        decay = jnp.where(causal_w, jnp.exp(jnp.minimum(gc[i] - g_row[i], 0.0)), 0.0)
        low.append(jnp.where(strict_w, r1[i][0:CHUNK] * decay, 0.0))
        attn.append((r1[i][CHUNK:2 * CHUNK] * decay).astype(BF16))
    t = [eye_f - x for x in low]
    p = [_dot(x.astype(BF16), _blockdiag(x, bd)) for x in low]
    yield
    n_levels = CHUNK.bit_length() - 1
    for level in range(1, n_levels):
        if level < n_levels - 1:
            r2 = [_dot(jnp.concatenate([p[i], t[i]], axis=0).astype(BF16), _blockdiag(p[i], bd))
                  for i in range(n)]
            p = [x[0:CHUNK] for x in r2]
            t = [t[i] + r2[i][CHUNK:2 * CHUNK] for i in range(n)]
        else:
            t = [t[i] + _dot(t[i].astype(BF16), _blockdiag(p[i], bd)) for i in range(n)]
        yield
    gc_last = [x[CHUNK - 1:CHUNK, :] for x in gc]
    kd_t = [(r1[i][2 * CHUNK:3 * CHUNK] * jnp.exp(gc_last[i] - g_row[i])).astype(BF16) for i in range(n)]
    lhs_t = [_dot(jnp.concatenate([attn[i], kd_t[i]], axis=0), _blockdiag(t[i], bd)) for i in range(n)]
    yield
    prod = [_dot(lhs_t[i].astype(BF16),
                 jnp.concatenate([_blockdiag(kb[i] * eg[i], bd), _blockdiag(load(2, i) * beta[i], bd)], axis=1))
            for i in range(n)]
    for i, (b, c) in enumerate(items):
        j = b * cps + c
        ac_ref[j, 0:CHUNK, :] = (-prod[i][CHUNK:2 * CHUNK, 0:GROUP]).astype(BF16)
        bb_ref[j] = prod[i][CHUNK:2 * CHUNK, GROUP:2 * GROUP]
        ac_ref[j, CHUNK:2 * CHUNK, :] = (qn[i] * eg[i] - prod[i][0:CHUNK, 0:GROUP]).astype(BF16)
        d_ref[j] = prod[i][0:CHUNK, GROUP:2 * GROUP]
        dec_ref[j] = jnp.broadcast_to(jnp.exp(gc_last[i]), (TAIL, GROUP))
    yield


def _gla_stages(dpk_ref, dla_ref, nw_ref, bd_ref, ltri_ref, y_ref, o_ref, st_ref):
    nb, tm = dpk_ref.shape[0], dpk_ref.shape[1]
    cps = tm // CHUNK
    _, causal_w, _ = _chunk_masks()
    ltri = ltri_ref[...]
    bd = bd_ref[...]
    bdf = bd.astype(F32)
    mid = CHUNK // 2

    items = [(b, c) for c in range(cps) for b in range(nb)]
    n = len(items)
    q = [dpk_ref[b, _chunk_rows(c), 0:GROUP].astype(F32) * (HEAD_DIM ** -0.5) for b, c in items]
    k = [dpk_ref[b, _chunk_rows(c), GROUP:2 * GROUP].astype(F32) for b, c in items]
    v = [dpk_ref[b, _chunk_rows(c), 2 * GROUP:3 * GROUP] for b, c in items]
    gcum = [_dot_sel_lhs(ltri, dla_ref[b, _chunk_rows(c), :]) for b, c in items]
    yield
    g_mid = [x[mid:mid + 1, :] for x in gcum]
    g_last = [x[CHUNK - 1:CHUNK, :] for x in gcum]
    attn = [jnp.where(causal_w,
                      _dot_nt((q[i] * jnp.exp(gcum[i] - g_mid[i])).astype(BF16),
                              _blockdiag(k[i] * jnp.exp(g_mid[i] - gcum[i]), bd)), 0.0).astype(BF16)
            for i in range(n)]
    yield
    upd = [bdf * _dot_tn((k[i] * jnp.exp(g_last[i] - gcum[i])).astype(BF16), v[i]) for i in range(n)]
    yield
    o_intra = [_dot(attn[i], _blockdiag(v[i], bd)) for i in range(n)]
    qg = [(q[i] * jnp.exp(gcum[i])).astype(BF16) for i in range(n)]
    yield
    for i, (b, c) in enumerate(items):
        st = st_ref[b]
        o_ref[b, _chunk_rows(c), :] = o_intra[i] + _dot(qg[i], st.astype(BF16))
        dec_col = jnp.transpose(jnp.broadcast_to(jnp.exp(g_last[i]), (GROUP, GROUP)))
        st_ref[b] = st * dec_col + upd[i]
        if b == nb - 1:
            yield

    for b in range(nb):
        o = o_ref[b]
        zg = dpk_ref[b, :, 3 * GROUP:4 * GROUP].astype(F32)
        y_ref[b] = (o * lax.rsqrt(_head_mean_sq(o, bd) + EPS) * nw_ref[...] * zg).astype(y_ref.dtype)


def _recur_kernel(cpk_ref, cg_ref, dpk_ref, dla_ref, nwc_ref, nwd_ref, bd_ref, ltri_ref, yc_ref, yd_ref,
                  ac_ref, bb_ref, d_ref, dec_ref, oc_ref, s_ref, od_ref, st_ref):
    @pl.when(pl.program_id(0) == 0)
    def _():
        s_ref[...] = jnp.zeros(s_ref.shape, F32)
        st_ref[...] = jnp.zeros(st_ref.shape, F32)

    _run_stages([_gla_stages(dpk_ref, dla_ref, nwd_ref, bd_ref, ltri_ref, yd_ref, od_ref, st_ref),
                 _gdn_stages(cpk_ref, cg_ref, nwc_ref, bd_ref, ltri_ref, yc_ref,
                             ac_ref, bb_ref, d_ref, dec_ref, oc_ref, s_ref)])


def _recur(cpk, cg, dpk, dla, nwc, nwd, bd, ltri, layer):
    nb, seq, _ = cpk.shape
    tm = SEQ_TILE
    n_chunks = nb * tm // CHUNK
    tile = lambda i: (0, i, 0)
    whole = lambda a: pl.BlockSpec(a.shape, (lambda i: (0,) * a.ndim))
    out = jax.ShapeDtypeStruct((nb, seq, GROUP), BF16)
    return pl.pallas_call(
        _recur_kernel,
        grid=(seq // tm,),
        in_specs=[pl.BlockSpec((nb, tm, C_PACK), tile),
                  pl.BlockSpec((nb, tm, GROUP), tile),
                  pl.BlockSpec((nb, tm, D_PACK), tile),
                  pl.BlockSpec((nb, tm, GROUP), tile),
                  _layer_param(nwc, layer), _layer_param(nwd, layer), whole(bd), whole(ltri)],
        out_specs=[pl.BlockSpec((nb, tm, GROUP), tile)] * 2,
        out_shape=[out, out],
        scratch_shapes=[pltpu.VMEM((n_chunks, 2 * CHUNK, GROUP), BF16),
                        pltpu.VMEM((n_chunks, CHUNK, GROUP), F32),
                        pltpu.VMEM((n_chunks, CHUNK, GROUP), F32),
                        pltpu.VMEM((n_chunks, TAIL, GROUP), F32),
                        pltpu.VMEM((nb, tm, GROUP), F32),
                        pltpu.VMEM((nb, CHUNK, GROUP), F32),
                        pltpu.VMEM((nb, tm, GROUP), F32),
                        pltpu.VMEM((nb, GROUP, GROUP), F32)],
        compiler_params=pltpu.CompilerParams(dimension_semantics=("arbitrary",),
                                             vmem_limit_bytes=VMEM_LIMIT),
        name="recur",
    )(cpk, cg, dpk, dla, nwc, nwd, bd, ltri)


def _outproj_ffn_kernel(final, n_cast, x_ref, ya_ref, yb_ref, yc_ref, yd_ref, wo_ref, n2_ref, wgu_ref, wd_ref,
                        fn_ref, *rest):
    cast_in, (o_ref,), cast_out, (act_ref,) = _split_refs(rest, n_cast, 1, n_cast, 1)
    _run_casts(cast_in, cast_out)
    x1 = x_ref[...]
    for idx, y_ref in enumerate((ya_ref, yb_ref, yc_ref, yd_ref)):
        x1 = x1 + _dot(y_ref[...], wo_ref[idx * GROUP:(idx + 1) * GROUP, :])
    ms = jnp.mean(x1 * x1, axis=-1, keepdims=True)
    h = (x1 * lax.rsqrt(ms + EPS) * n2_ref[...]).astype(BF16)
    for j in range(D_FF // FF_CHUNK):
        cols = slice(j * FF_CHUNK, (j + 1) * FF_CHUNK)
        gate = _dot(h, wgu_ref[:, cols])
        up = _dot(h, wgu_ref[:, D_FF + j * FF_CHUNK:D_FF + (j + 1) * FF_CHUNK])
        act_ref[:, cols] = (_silu(gate) * up).astype(BF16)
    x2 = x1 + _dot(act_ref[...], wd_ref[...])
    if final:
        ms2 = jnp.mean(x2 * x2, axis=-1, keepdims=True)
        x2 = x2 * lax.rsqrt(ms2 + EPS) * fn_ref[...]
    o_ref[...] = x2


def _outproj_ffn(x, ys, wo, n2, wgu, wd, fn, layer, final, cast_jobs=()):
    m = x.shape[0]
    n_steps = m // ROW_TILE
    assert n_steps >= CAST_BLOCKS or not cast_jobs
    row = lambda i: (i, 0)
    fixed = lambda i: (0, 0)
    cast_in_specs, cast_out_specs, cast_out_shapes = _cast_specs(cast_jobs)
    outs = pl.pallas_call(
        functools.partial(_outproj_ffn_kernel, final, len(cast_jobs)),
        grid=(n_steps,),
        in_specs=[pl.BlockSpec((ROW_TILE, D_MODEL), row)]
                 + [pl.BlockSpec((ROW_TILE, GROUP), row)] * 4
                 + [_resident(wo), _layer_param(n2, layer), _resident(wgu), _resident(wd),
                    pl.BlockSpec((1, D_MODEL), fixed)]
                 + cast_in_specs,
        out_specs=[pl.BlockSpec((ROW_TILE, D_MODEL), row)] + cast_out_specs,
        out_shape=[jax.ShapeDtypeStruct((m, D_MODEL), F32)] + cast_out_shapes,
        scratch_shapes=[pltpu.VMEM((ROW_TILE, D_FF), BF16)],
        compiler_params=pltpu.CompilerParams(dimension_semantics=("arbitrary",),
                                             vmem_limit_bytes=VMEM_LIMIT),
        name="outproj_ffn",
    )(x, *ys, wo, n2, wgu, wd, fn, *[w for w, _, _ in cast_jobs])
    return outs


def _gate_vec(v):
    return jnp.pad(v.astype(F32), ((0, 0), (0, GATE_COLS - HEADS)))[:, None, :]


def _row(v):
    return v[:, None, :]


def _gate_selectors():
    ea = np.zeros((GATE_COLS, GROUP), np.float32)
    eb = np.zeros((GATE_COLS, GROUP), np.float32)
    for h in range(HEADS):
        ea[h, h * HEAD_DIM:(h + 1) * HEAD_DIM] = 1.0
        eb[HEADS + h, h * HEAD_DIM:(h + 1) * HEAD_DIM] = 1.0
    return jnp.asarray(ea, BF16), jnp.asarray(eb, BF16)


def kernel(x, norm1_w, w_in, sgu_ln_w, sgu_ln_b, sgu_w_spatial, sgu_b_spatial, sc_conv_w, dn_conv_w, dn_a_log, dn_dt_bias, dn_norm_w, gla_w_gate2, gla_gate_bias, gla_norm_w, w_out, norm2_w, w_gate_up, w_down, final_norm_w):
    bsz, seq, d = x.shape
    depth = w_in.shape[0]
    assert seq % ROW_TILE == 0 and seq % SEQ_TILE == 0 and d == D_MODEL
    m = bsz * seq
    xf = x.reshape(m, d)
    bd, ltri = _const_inputs()
    ea, eb = _gate_selectors()
    w_in_b = w_in.astype(BF16)
    ffn_jobs = lambda layer: [(w_out, layer, D_MODEL), (w_gate_up, layer, 2 * D_FF), (w_down, layer, D_MODEL)]
    bs_wide = jnp.repeat(jnp.swapaxes(sgu_b_spatial, 1, 2), HEAD_DIM, axis=2)
    wg_pad = jnp.pad(gla_w_gate2.astype(BF16),
                     ((0, 0), (2 * HEADS, GATE_COLS - 2 * HEADS - GATE_RANK), (0, 0)))
    inproj_params = (_row(sgu_ln_w), _row(sgu_ln_b), sgu_w_spatial, bs_wide, sc_conv_w, dn_conv_w,
                     _gate_vec(dn_a_log), _gate_vec(dn_dt_bias), ea, eb, bd, wg_pad, _row(gla_gate_bias))
    nwc, nwd = _row(jnp.tile(dn_norm_w, (1, HEADS))), _row(jnp.tile(gla_norm_w, (1, HEADS)))
    norm1, norm2 = _row(norm1_w), _row(norm2_w)
    ffn_w = None
    for l in range(depth):
        ya, yb, cpk, cg, dpk, dla, *cast = _inproj(xf.reshape(bsz, seq, d), norm1, w_in_b, l,
                                                   *inproj_params, cast_jobs=ffn_jobs(0) if l == 0 else ())
        if l == 0:
            ffn_w = cast
        yc, yd = _recur(cpk, cg, dpk, dla, nwc, nwd, bd, ltri, l)
        last = l == depth - 1
        xf, *cast = _outproj_ffn(xf, tuple(y.reshape(m, GROUP) for y in (ya, yb, yc, yd)),
                                 ffn_w[0], norm2, ffn_w[1], ffn_w[2], final_norm_w[None, :], layer=l, final=last,
                                 cast_jobs=() if last else ffn_jobs(l + 1))
        if not last:
            ffn_w = cast
    return xf.reshape(bsz, seq, d)
```

```python
import functools

import numpy as np
import jax
import jax.numpy as jnp
from jax import lax
from jax.experimental import pallas as pl
from jax.experimental.pallas import tpu as pltpu

F32 = jnp.float32
BF16 = jnp.bfloat16

D_MODEL = 1024
GROUP = 256
HEADS = 4
HEAD_DIM = 64
SGU_CHUNK = 128
SC_WIDTH = 3
DN_CONV_WIDTH = 4
CHUNK = 64
GATE_RANK = 16
GATE_TEMP = 16.0
D_FF = 2816
EPS = 1e-6
GATE_COLS = 128
TAIL = 8

ROW_TILE = 1024
SEQ_TILE = 512
FF_CHUNK = 256
VMEM_LIMIT = 60 * 1024 * 1024

MAIN_A = 0
MAIN_B = MAIN_A + 2 * GROUP
MAIN_C = MAIN_B + 3 * GROUP
MAIN_COLS = MAIN_C + 3 * GROUP
REST_CZ = 0
REST_D = REST_CZ + GROUP
REST_G = REST_D + 4 * GROUP
REST_COLS = REST_G + GATE_COLS
C_PACK = 5 * GROUP
D_PACK = 4 * GROUP


def _dot(a, b):
    return jnp.dot(a, b, preferred_element_type=F32)


def _dot_nt(a, b):
    return lax.dot_general(a, b, (((1,), (1,)), ((), ())), preferred_element_type=F32)


def _dot_tn(a, b):
    return lax.dot_general(a, b, (((0,), (0,)), ((), ())), preferred_element_type=F32)


def _split2(x):
    hi = x.astype(BF16)
    lo = (x - hi.astype(F32)).astype(BF16)
    return hi, lo


def _dot_sel_rhs(x, sel):
    hi, lo = _split2(x)
    return _dot(hi, sel) + _dot(lo, sel)


def _dot_sel_lhs(sel, x):
    hi, lo = _split2(x)
    return _dot(sel, hi) + _dot(sel, lo)


def _softplus(x):
    return jnp.maximum(x, 0.0) + jnp.log(1.0 + jnp.exp(-jnp.abs(x)))


def _sigmoid(x):
    return 0.5 + 0.5 * jnp.tanh(0.5 * x)


def _silu(x):
    h = 0.5 * x
    return h + h * jnp.tanh(h)


def _gelu_tanh(x):
    h = 0.5 * x
    return h + h * jnp.tanh(x * (0.7978845608028654 + (0.7978845608028654 * 0.044715) * (x * x)))


def _rmsnorm_bf16(x, w):
    ms = jnp.mean(x * x, axis=-1, keepdims=True)
    return (x * lax.rsqrt(ms + EPS) * w).astype(BF16)


def _blockdiag(x, mask):
    xb = x.astype(BF16)
    return jnp.concatenate([xb] * HEADS, axis=0) * mask


def _head_mean_sq(x, bones):
    return _dot((x * x).astype(BF16), bones) * (1.0 / HEAD_DIM)


LANES = 128


def _causal_conv(x, ext_ref, out_ref, cw_ref, width):
    tm, ch = x.shape
    half = tm // 2
    n_slabs = ch // LANES
    for s in range(n_slabs):
        ext_ref[s, TAIL:TAIL + tm, :] = x[:, s * LANES:(s + 1) * LANES]
    for s in range(n_slabs):
        lanes = slice(s * LANES, (s + 1) * LANES)
        for parity in (0, 1):
            acc = None
            for j in range(width):
                tap = (ext_ref[s, pl.ds(TAIL + parity - j, half, stride=2), :]
                       * cw_ref[width - 1 - j:width - j, lanes])
                acc = tap if acc is None else acc + tap
            out_ref[s, pl.ds(parity, half, stride=2), :] = acc
    for s in range(n_slabs):
        ext_ref[s, 0:TAIL, :] = x[tm - TAIL:tm, s * LANES:(s + 1) * LANES]
    return jnp.concatenate([out_ref[s] for s in range(n_slabs)], axis=1)


def _layer_param(stacked, layer):
    zeros = (0,) * (stacked.ndim - 1)
    return pl.BlockSpec((None,) + stacked.shape[1:], lambda i: (layer,) + zeros)


def _layer_block(stacked, layer):
    zeros = (0,) * (stacked.ndim - 1)
    return pl.BlockSpec((None,) + stacked.shape[1:], lambda i: (layer,) + zeros,
                        pipeline_mode=pl.Buffered(1))


def _resident(a):
    return pl.BlockSpec(a.shape, lambda i: (0, 0), pipeline_mode=pl.Buffered(1))


CAST_BLOCKS = 16


def _cast_specs(cast_jobs):
    last = CAST_BLOCKS - 1
    in_specs, out_specs, out_shapes = [], [], []
    for stacked, layer, n_cols in cast_jobs:
        rows = stacked.shape[1]
        assert rows % (16 * CAST_BLOCKS) == 0 and n_cols % 128 == 0
        blk = rows // CAST_BLOCKS
        in_specs.append(pl.BlockSpec((None, blk, n_cols),
                                     lambda i, layer=layer: (layer, jnp.minimum(i, last), 0)))
        out_specs.append(pl.BlockSpec((blk, n_cols), lambda i: (jnp.minimum(i, last), 0)))
        out_shapes.append(jax.ShapeDtypeStruct((rows, n_cols), BF16))
    return in_specs, out_specs, out_shapes


def _run_casts(cast_in, cast_out):
    for src, dst in zip(cast_in, cast_out):
        dst[...] = src[...].astype(dst.dtype)


def _split_refs(refs, *counts):
    assert sum(counts) == len(refs)
    groups, pos = [], 0
    for c in counts:
        groups.append(tuple(refs[pos:pos + c]))
        pos += c
    return groups


def _run_stages(stages):
    stages = list(stages)
    while stages:
        for gen in list(stages):
            try:
                next(gen)
            except StopIteration:
                stages.remove(gen)


def _inproj_stages(x_ref, nw_ref, w_ref, wr_ref, lnw_ref, lnb_ref, ws_ref, bs_ref, scw_ref,
                   dcw_ref, alog_ref, dtb_ref, ea_ref, eb_ref, bd_ref, wg_ref, gb_ref,
                   ya_ref, yb_ref, cpk_ref, cg_ref, dpk_ref, dla_ref, extb_ref, extc_ref, convb_ref, convc_ref):
    nb, tm = x_ref.shape[0], x_ref.shape[1]
    m = nb * tm
    batch_rows = [slice(b * tm, (b + 1) * tm) for b in range(nb)]

    x = x_ref[...].reshape(m, D_MODEL)
    if x.dtype == BF16:
        h = x
    else:
        h = _rmsnorm_bf16(x, nw_ref[...])
    bd = bd_ref[...]

    def project(ref, col0, n_slabs):
        parts = []
        for c in range(col0, col0 + n_slabs * GROUP, GROUP):
            parts.append(_dot(h, ref[:, c:c + GROUP]))
            yield
        return jnp.concatenate(parts, axis=1)

    pc = yield from project(w_ref, MAIN_C, 3)
    pcz = yield from project(wr_ref, REST_CZ, 1)
    pg = _dot(h, wr_ref[:, REST_G:REST_G + GATE_COLS])
    pa = yield from project(w_ref, MAIN_A, 2)

    qkv = jnp.concatenate(
        [_silu(_causal_conv(pc[rows, 0:3 * GROUP], extc_ref.at[b], convc_ref, dcw_ref, DN_CONV_WIDTH))
         for b, rows in enumerate(batch_rows)], axis=0)
    q = qkv[:, 0:GROUP]
    k = qkv[:, GROUP:2 * GROUP]
    pb = yield from project(w_ref, MAIN_B, 3)
    qn = q * lax.rsqrt(_dot((q * q).astype(BF16), bd) + EPS) * (HEAD_DIM ** -0.5)
    kn = k * lax.rsqrt(_dot((k * k).astype(BF16), bd) + EPS)
    yield
    g_narrow = -jnp.exp(alog_ref[...]) * _softplus(pg + dtb_ref[...])
    g_wide = _dot_sel_rhs(g_narrow, ea_ref[...])
    beta = _dot(_sigmoid(pg).astype(BF16), eb_ref[...])
    zg_c = _silu(pcz)
    pre = _dot(pg.astype(BF16), wg_ref[...]) + gb_ref[...]
    log_a = -_softplus(-pre) * (1.0 / GATE_TEMP)
    yield

    tri = (lax.broadcasted_iota(jnp.int32, (SGU_CHUNK, SGU_CHUNK), 0)
           >= lax.broadcasted_iota(jnp.int32, (SGU_CHUNK, SGU_CHUNK), 1))
    head_of_lane = lax.broadcasted_iota(jnp.int32, (SGU_CHUNK, GROUP), 1) // HEAD_DIM
    ws = jnp.concatenate([jnp.where(tri, ws_ref[hd], 0.0).astype(BF16) for hd in range(HEADS)],
                         axis=1)
    for b in range(nb):
        for c in range(tm // SGU_CHUNK):
            rows = slice(b * tm + c * SGU_CHUNK, b * tm + (c + 1) * SGU_CHUNK)
            u = _gelu_tanh(pa[rows, 0:GROUP])
            v = _gelu_tanh(pa[rows, GROUP:2 * GROUP])
            mu = jnp.mean(v, axis=-1, keepdims=True)
            vc = v - mu
            var = jnp.mean(vc * vc, axis=-1, keepdims=True)
            vn = vc * lax.rsqrt(var + EPS) * lnw_ref[...] + lnb_ref[...]
            vn_heads = jnp.concatenate([jnp.where(head_of_lane == hd, vn, 0.0).astype(BF16)
                                        for hd in range(HEADS)], axis=0)
            mixed = bs_ref[...] + _dot(ws, vn_heads)
            ya_ref[b, c * SGU_CHUNK:(c + 1) * SGU_CHUNK, :] = (u * mixed).astype(ya_ref.dtype)
            yield

    pd = yield from project(wr_ref, REST_D, 4)

    for b, rows in enumerate(batch_rows):
        conv_b = _causal_conv(pb[rows, GROUP:2 * GROUP] * pb[rows, 2 * GROUP:3 * GROUP],
                              extb_ref.at[b], convb_ref, scw_ref, SC_WIDTH)
        yb_ref[b] = (pb[rows, 0:GROUP] * conv_b).astype(yb_ref.dtype)

    for b, rows in enumerate(batch_rows):
        cg_ref[b] = g_wide[rows]
        cpk_ref[b, :, 0:GROUP] = qn[rows].astype(BF16)
        cpk_ref[b, :, GROUP:2 * GROUP] = kn[rows].astype(BF16)
        cpk_ref[b, :, 2 * GROUP:3 * GROUP] = qkv[rows, 2 * GROUP:3 * GROUP].astype(BF16)
        cpk_ref[b, :, 3 * GROUP:4 * GROUP] = beta[rows].astype(BF16)
        cpk_ref[b, :, 4 * GROUP:5 * GROUP] = zg_c[rows].astype(BF16)
        dla_ref[b] = log_a[rows]
        dpk_ref[b, :, 0:3 * GROUP] = pd[rows, 0:3 * GROUP].astype(BF16)
        dpk_ref[b, :, 3 * GROUP:4 * GROUP] = _silu(pd[rows, 3 * GROUP:4 * GROUP]).astype(BF16)


def _repack_w_in_rest(w_ref, wr_ref):
    c_a = MAIN_COLS
    c_z = c_a + 2 * HEADS
    d0 = c_z + GROUP
    d_g = d0 + 3 * GROUP
    d_z = d_g + GATE_RANK
    wr_ref[:, REST_CZ:REST_CZ + GROUP] = w_ref[:, c_z:d0]
    wr_ref[:, REST_D:REST_D + 3 * GROUP] = w_ref[:, d0:d_g]
    wr_ref[:, REST_D + 3 * GROUP:REST_D + 4 * GROUP] = w_ref[:, d_z:d_z + GROUP]
    wr_ref[:, REST_G:REST_G + GATE_COLS] = jnp.zeros((wr_ref.shape[0], GATE_COLS), wr_ref.dtype)
    wr_ref[:, REST_G:REST_G + 2 * HEADS] = w_ref[:, c_a:c_z]
    wr_ref[:, REST_G + 2 * HEADS:REST_G + 2 * HEADS + GATE_RANK] = w_ref[:, d_g:d_z]


def _inproj_kernel(x_ref, nw_ref, w_ref, lnw_ref, lnb_ref, ws_ref, bs_ref, scw_ref, dcw_ref,
                   alog_ref, dtb_ref, ea_ref, eb_ref, bd_ref, wg_ref, gb_ref,
                   ya_ref, yb_ref, cpk_ref, cg_ref, dpk_ref, dla_ref,
                   extb_ref, extc_ref, convb_ref, convc_ref, wr_ref):
    @pl.when(pl.program_id(0) == 0)
    def _():
        extb_ref[...] = jnp.zeros(extb_ref.shape, F32)
        extc_ref[...] = jnp.zeros(extc_ref.shape, F32)
        _repack_w_in_rest(w_ref, wr_ref)

    _run_stages([_inproj_stages(x_ref, nw_ref, w_ref, wr_ref, lnw_ref, lnb_ref, ws_ref, bs_ref, scw_ref, dcw_ref,
                                alog_ref, dtb_ref, ea_ref, eb_ref, bd_ref, wg_ref, gb_ref,
                                ya_ref, yb_ref, cpk_ref, cg_ref, dpk_ref, dla_ref, extb_ref, extc_ref,
                                convb_ref, convc_ref)])


def _inproj(x, nw, w_in_b, layer, lnw, lnb, ws, bs_wide, scw, dcw, alog_n, dtb_n, ea, eb, bd, wg_pad, gb):
    nb, seq, _ = x.shape
    tm = SEQ_TILE
    n_tiles = seq // tm
    tile = lambda i: (0, i, 0)
    whole = lambda a: pl.BlockSpec(a.shape, (lambda i: (0,) * a.ndim))
    per_layer = lambda a: _layer_param(a, layer)
    outs = ((GROUP, BF16), (GROUP, BF16), (C_PACK, BF16), (GROUP, F32), (D_PACK, BF16), (GROUP, F32))
    return pl.pallas_call(
        _inproj_kernel,
        grid=(n_tiles,),
        in_specs=[pl.BlockSpec((nb, tm, D_MODEL), tile),
                  per_layer(nw),
                  _layer_block(w_in_b, layer),
                  per_layer(lnw), per_layer(lnb), per_layer(ws), per_layer(bs_wide), per_layer(scw),
                  per_layer(dcw), per_layer(alog_n), per_layer(dtb_n), whole(ea), whole(eb), whole(bd),
                  per_layer(wg_pad), per_layer(gb)],
        out_specs=[pl.BlockSpec((nb, tm, n), tile) for n, _ in outs],
        out_shape=[jax.ShapeDtypeStruct((nb, seq, n), dt) for n, dt in outs],
        scratch_shapes=[pltpu.VMEM((nb, GROUP // LANES, TAIL + tm, LANES), F32),
                        pltpu.VMEM((nb, 3 * GROUP // LANES, TAIL + tm, LANES), F32),
                        pltpu.VMEM((GROUP // LANES, tm, LANES), F32),
                        pltpu.VMEM((3 * GROUP // LANES, tm, LANES), F32),
                        pltpu.VMEM((D_MODEL, REST_COLS), BF16)],
        compiler_params=pltpu.CompilerParams(dimension_semantics=("arbitrary",),
                                             vmem_limit_bytes=VMEM_LIMIT),
        name="in_proj",
    )(x, nw, w_in_b, lnw, lnb, ws, bs_wide, scw, dcw, alog_n, dtb_n, ea, eb, bd, wg_pad, gb)


def _chunk_masks():
    i = lax.broadcasted_iota(jnp.int32, (CHUNK, GROUP), 0)
    j = lax.broadcasted_iota(jnp.int32, (CHUNK, GROUP), 1) % CHUNK
    return i == j, i >= j, i > j


def _const_inputs():
    lane_head = np.arange(GROUP) // HEAD_DIM
    bd = (lane_head[:, None] == lane_head[None, :]).astype(np.float32)
    ltri = np.tril(np.ones((CHUNK, CHUNK), np.float32))
    return jnp.asarray(bd, BF16), jnp.asarray(ltri, BF16)


def _chunk_rows(c):
    return slice(c * CHUNK, (c + 1) * CHUNK)


def _gdn_stages(cpk_ref, cg_ref, nw_ref, bd_ref, ltri_ref, y_ref, ac_ref, bb_ref, d_ref, dec_ref, o_ref, s_ref):
    nb, tm = cpk_ref.shape[0], cpk_ref.shape[1]
    cps = tm // CHUNK
    eye_w, causal_w, strict_w = _chunk_masks()
    eye_f = eye_w.astype(F32)
    ltri = ltri_ref[...]
    bd = bd_ref[...]

    yield from _gdn_prepare([(b, c) for b in range(nb) for c in range(cps)], cps, cpk_ref, cg_ref,
                            ac_ref, bb_ref, d_ref, dec_ref, eye_w, causal_w, strict_w, eye_f, ltri, bd)

    for c in range(cps):
        for b in range(nb):
            i = b * cps + c
            s = s_ref[b]
            r = _dot(ac_ref[i], _blockdiag(s, bd))
            s_ref[b] = s * dec_ref[i, 0:1, :] + r[0:CHUNK] + bb_ref[i]
            o_ref[b, _chunk_rows(c), :] = r[CHUNK:2 * CHUNK] + d_ref[i]
        yield

    for b in range(nb):
        o = o_ref[b]
        zg = cpk_ref[b, :, 4 * GROUP:5 * GROUP].astype(F32)
        y_ref[b] = (o * lax.rsqrt(_head_mean_sq(o, bd) + EPS) * nw_ref[...] * zg).astype(y_ref.dtype)


def _gdn_prepare(items, cps, cpk_ref, cg_ref, ac_ref, bb_ref, d_ref, dec_ref,
                 eye_w, causal_w, strict_w, eye_f, ltri, bd):
    n = len(items)

    def load(col, i):
        b, c = items[i]
        return cpk_ref[b, _chunk_rows(c), col * GROUP:(col + 1) * GROUP]

    qn = [load(0, i) for i in range(n)]
    kn = [load(1, i) for i in range(n)]
    beta = [load(3, i) for i in range(n)]
    gc = [_dot_sel_lhs(ltri, cg_ref[b, _chunk_rows(c), :]) for b, c in items]
    yield
    eg = [jnp.exp(x) for x in gc]
    kb = [kn[i] * beta[i] for i in range(n)]
    eye_b = eye_f.astype(BF16)
    r1 = [_dot_nt(jnp.concatenate([kb[i], qn[i], eye_b], axis=0), _blockdiag(kn[i], bd))
          for i in range(n)]
    yield
    low, attn, g_row = [], [], []
    for i in range(n):
        g_row.append(jnp.sum(jnp.where(eye_w, gc[i], 0.0), axis=0, keepdims=True))
        decay = jnp.where(causal_w, jnp.exp(jnp.minimum(gc[i] - g_row[i], 0.0)), 0.0)
        low.append(jnp.where(strict_w, r1[i][0:CHUNK] * decay, 0.0))
        attn.append((r1[i][CHUNK:2 * CHUNK] * decay).astype(BF16))
    t = [eye_f - x for x in low]
    p = [_dot(x.astype(BF16), _blockdiag(x, bd)) for x in low]
    yield
    n_levels = CHUNK.bit_length() - 1
    for level in range(1, n_levels):
        if level < n_levels - 1:
            r2 = [_dot(jnp.concatenate([p[i], t[i]], axis=0).astype(BF16), _blockdiag(p[i], bd))
                  for i in range(n)]
            p = [x[0:CHUNK] for x in r2]
            t = [t[i] + r2[i][CHUNK:2 * CHUNK] for i in range(n)]
        else:
            t = [t[i] + _dot(t[i].astype(BF16), _blockdiag(p[i], bd)) for i in range(n)]
        yield
    gc_last = [x[CHUNK - 1:CHUNK, :] for x in gc]
    kd_t = [(r1[i][2 * CHUNK:3 * CHUNK] * jnp.exp(gc_last[i] - g_row[i])).astype(BF16) for i in range(n)]
    lhs_t = [_dot(jnp.concatenate([attn[i], kd_t[i]], axis=0), _blockdiag(t[i], bd)) for i in range(n)]
    yield
    prod = [_dot(lhs_t[i].astype(BF16),
                 jnp.concatenate([_blockdiag(kb[i] * eg[i], bd), _blockdiag(load(2, i) * beta[i], bd)], axis=1))
            for i in range(n)]
    for i, (b, c) in enumerate(items):
        j = b * cps + c
        ac_ref[j, 0:CHUNK, :] = (-prod[i][CHUNK:2 * CHUNK, 0:GROUP]).astype(BF16)
        bb_ref[j] = prod[i][CHUNK:2 * CHUNK, GROUP:2 * GROUP]
        ac_ref[j, CHUNK:2 * CHUNK, :] = (qn[i] * eg[i] - prod[i][0:CHUNK, 0:GROUP]).astype(BF16)
        d_ref[j] = prod[i][0:CHUNK, GROUP:2 * GROUP]
        dec_ref[j] = jnp.broadcast_to(jnp.exp(gc_last[i]), (TAIL, GROUP))
    yield


def _gla_stages(dpk_ref, dla_ref, nw_ref, bd_ref, ltri_ref, y_ref, o_ref, st_ref):
    nb, tm = dpk_ref.shape[0], dpk_ref.shape[1]
    cps = tm // CHUNK
    _, causal_w, _ = _chunk_masks()
    ltri = ltri_ref[...]
    bd = bd_ref[...]
    bdf = bd.astype(F32)
    mid = CHUNK // 2

    items = [(b, c) for c in range(cps) for b in range(nb)]
    n = len(items)
    q = [dpk_ref[b, _chunk_rows(c), 0:GROUP].astype(F32) * (HEAD_DIM ** -0.5) for b, c in items]
    k = [dpk_ref[b, _chunk_rows(c), GROUP:2 * GROUP].astype(F32) for b, c in items]
    v = [dpk_ref[b, _chunk_rows(c), 2 * GROUP:3 * GROUP] for b, c in items]
    gcum = [_dot_sel_lhs(ltri, dla_ref[b, _chunk_rows(c), :]) for b, c in items]
    yield
    g_mid = [x[mid:mid + 1, :] for x in gcum]
    g_last = [x[CHUNK - 1:CHUNK, :] for x in gcum]
    attn = [jnp.where(causal_w,
                      _dot_nt((q[i] * jnp.exp(gcum[i] - g_mid[i])).astype(BF16),
                              _blockdiag(k[i] * jnp.exp(g_mid[i] - gcum[i]), bd)), 0.0).astype(BF16)
            for i in range(n)]
    yield
    upd = [bdf * _dot_tn((k[i] * jnp.exp(g_last[i] - gcum[i])).astype(BF16), v[i]) for i in range(n)]
    yield
    o_intra = [_dot(attn[i], _blockdiag(v[i], bd)) for i in range(n)]
    qg = [(q[i] * jnp.exp(gcum[i])).astype(BF16) for i in range(n)]
    yield
    for i, (b, c) in enumerate(items):
        st = st_ref[b]
        o_ref[b, _chunk_rows(c), :] = o_intra[i] + _dot(qg[i], st.astype(BF16))
        dec_col = jnp.transpose(jnp.broadcast_to(jnp.exp(g_last[i]), (GROUP, GROUP)))
        st_ref[b] = st * dec_col + upd[i]
        if b == nb - 1:
            yield

    for b in range(nb):
        o = o_ref[b]
        zg = dpk_ref[b, :, 3 * GROUP:4 * GROUP].astype(F32)
        y_ref[b] = (o * lax.rsqrt(_head_mean_sq(o, bd) + EPS) * nw_ref[...] * zg).astype(y_ref.dtype)


def _recur_kernel(n_cast, cpk_ref, cg_ref, dpk_ref, dla_ref, nwc_ref, nwd_ref, bd_ref, ltri_ref, *rest):
    cast_in, (yc_ref, yd_ref), cast_out, scratch = _split_refs(rest, n_cast, 2, n_cast, 8)
    ac_ref, bb_ref, d_ref, dec_ref, oc_ref, s_ref, od_ref, st_ref = scratch
    _run_casts(cast_in, cast_out)

    @pl.when(pl.program_id(0) == 0)
    def _():
        s_ref[...] = jnp.zeros(s_ref.shape, F32)
        st_ref[...] = jnp.zeros(st_ref.shape, F32)

    _run_stages([_gla_stages(dpk_ref, dla_ref, nwd_ref, bd_ref, ltri_ref, yd_ref, od_ref, st_ref),
                 _gdn_stages(cpk_ref, cg_ref, nwc_ref, bd_ref, ltri_ref, yc_ref,
                             ac_ref, bb_ref, d_ref, dec_ref, oc_ref, s_ref)])


def _recur(cpk, cg, dpk, dla, nwc, nwd, bd, ltri, layer, cast_jobs=()):
    nb, seq, _ = cpk.shape
    tm = SEQ_TILE
    n_chunks = nb * tm // CHUNK
    assert seq // tm >= CAST_BLOCKS or not cast_jobs
    tile = lambda i: (0, i, 0)
    whole = lambda a: pl.BlockSpec(a.shape, (lambda i: (0,) * a.ndim))
    out = jax.ShapeDtypeStruct((nb, seq, GROUP), BF16)
    cast_in_specs, cast_out_specs, cast_out_shapes = _cast_specs(cast_jobs)
    return pl.pallas_call(
        functools.partial(_recur_kernel, len(cast_jobs)),
        grid=(seq // tm,),
        in_specs=[pl.BlockSpec((nb, tm, C_PACK), tile),
                  pl.BlockSpec((nb, tm, GROUP), tile),
                  pl.BlockSpec((nb, tm, D_PACK), tile),
                  pl.BlockSpec((nb, tm, GROUP), tile),
                  _layer_param(nwc, layer), _layer_param(nwd, layer), whole(bd), whole(ltri)]
                 + cast_in_specs,
        out_specs=[pl.BlockSpec((nb, tm, GROUP), tile)] * 2 + cast_out_specs,
        out_shape=[out, out] + cast_out_shapes,
        scratch_shapes=[pltpu.VMEM((n_chunks, 2 * CHUNK, GROUP), BF16),
                        pltpu.VMEM((n_chunks, CHUNK, GROUP), F32),
                        pltpu.VMEM((n_chunks, CHUNK, GROUP), F32),
                        pltpu.VMEM((n_chunks, TAIL, GROUP), F32),
                        pltpu.VMEM((nb, tm, GROUP), F32),
                        pltpu.VMEM((nb, CHUNK, GROUP), F32),
                        pltpu.VMEM((nb, tm, GROUP), F32),
                        pltpu.VMEM((nb, GROUP, GROUP), F32)],
        compiler_params=pltpu.CompilerParams(dimension_semantics=("arbitrary",),
                                             vmem_limit_bytes=VMEM_LIMIT),
        name="recur",
    )(cpk, cg, dpk, dla, nwc, nwd, bd, ltri, *[w for w, _, _ in cast_jobs])


def _outproj_ffn_kernel(final, x_ref, ya_ref, yb_ref, yc_ref, yd_ref, wo_ref, n2_ref, wgu_ref, wd_ref,
                        fn_ref, *rest):
    if final:
        o_ref, act_ref = rest
    else:
        o_ref, h_ref, act_ref = rest
    x1 = x_ref[...]
    for idx, y_ref in enumerate((ya_ref, yb_ref, yc_ref, yd_ref)):
        x1 = x1 + _dot(y_ref[...], wo_ref[idx * GROUP:(idx + 1) * GROUP, :])
    ms = jnp.mean(x1 * x1, axis=-1, keepdims=True)
    h = (x1 * lax.rsqrt(ms + EPS) * n2_ref[...]).astype(BF16)
    for j in range(D_FF // FF_CHUNK):
        cols = slice(j * FF_CHUNK, (j + 1) * FF_CHUNK)
        gate = _dot(h, wgu_ref[:, cols])
        up = _dot(h, wgu_ref[:, D_FF + j * FF_CHUNK:D_FF + (j + 1) * FF_CHUNK])
        act_ref[:, cols] = (_silu(gate) * up).astype(BF16)
    x2 = x1 + _dot(act_ref[...], wd_ref[...])
    if final:
        ms2 = jnp.mean(x2 * x2, axis=-1, keepdims=True)
        x2 = x2 * lax.rsqrt(ms2 + EPS) * fn_ref[...]
    else:
        h_ref[...] = _rmsnorm_bf16(x2, fn_ref[...])
    o_ref[...] = x2


def _outproj_ffn(x, ys, wo, n2, wgu, wd, fn, layer, final):
    m = x.shape[0]
    n_steps = m // ROW_TILE
    row = lambda i: (i, 0)
    fixed = lambda i: (0, 0)
    out_specs = [pl.BlockSpec((ROW_TILE, D_MODEL), row)]
    out_shape = [jax.ShapeDtypeStruct((m, D_MODEL), F32)]
    if not final:
        out_specs.append(pl.BlockSpec((ROW_TILE, D_MODEL), row))
        out_shape.append(jax.ShapeDtypeStruct((m, D_MODEL), BF16))
    return pl.pallas_call(
        functools.partial(_outproj_ffn_kernel, final),
        grid=(n_steps,),
        in_specs=[pl.BlockSpec((ROW_TILE, D_MODEL), row)]
                 + [pl.BlockSpec((ROW_TILE, GROUP), row)] * 4
                 + [_resident(wo), _layer_param(n2, layer), _resident(wgu), _resident(wd),
                    pl.BlockSpec((1, D_MODEL), fixed)],
        out_specs=out_specs,
        out_shape=out_shape,
        scratch_shapes=[pltpu.VMEM((ROW_TILE, D_FF), BF16)],
        compiler_params=pltpu.CompilerParams(dimension_semantics=("arbitrary",),
                                             vmem_limit_bytes=VMEM_LIMIT),
        name="outproj_ffn",
    )(x, *ys, wo, n2, wgu, wd, fn)


def _gate_vec(v):
    return jnp.pad(v.astype(F32), ((0, 0), (0, GATE_COLS - HEADS)))[:, None, :]


def _row(v):
    return v[:, None, :]


def _gate_selectors():
    ea = np.zeros((GATE_COLS, GROUP), np.float32)
    eb = np.zeros((GATE_COLS, GROUP), np.float32)
    for h in range(HEADS):
        ea[h, h * HEAD_DIM:(h + 1) * HEAD_DIM] = 1.0
        eb[HEADS + h, h * HEAD_DIM:(h + 1) * HEAD_DIM] = 1.0
    return jnp.asarray(ea, BF16), jnp.asarray(eb, BF16)


def kernel(x, norm1_w, w_in, sgu_ln_w, sgu_ln_b, sgu_w_spatial, sgu_b_spatial, sc_conv_w, dn_conv_w, dn_a_log, dn_dt_bias, dn_norm_w, gla_w_gate2, gla_gate_bias, gla_norm_w, w_out, norm2_w, w_gate_up, w_down, final_norm_w):
    bsz, seq, d = x.shape
    depth = w_in.shape[0]
    assert seq % ROW_TILE == 0 and seq % SEQ_TILE == 0 and d == D_MODEL
    m = bsz * seq
    xf = x.reshape(m, d)
    bd, ltri = _const_inputs()
    ea, eb = _gate_selectors()
    w_in_b = w_in.astype(BF16)
    ffn_jobs = lambda layer: [(w_out, layer, D_MODEL), (w_gate_up, layer, 2 * D_FF), (w_down, layer, D_MODEL)]
    bs_wide = jnp.repeat(jnp.swapaxes(sgu_b_spatial, 1, 2), HEAD_DIM, axis=2)
    wg_pad = jnp.pad(gla_w_gate2.astype(BF16),
                     ((0, 0), (2 * HEADS, GATE_COLS - 2 * HEADS - GATE_RANK), (0, 0)))
    inproj_params = (_row(sgu_ln_w), _row(sgu_ln_b), sgu_w_spatial, bs_wide, sc_conv_w, dn_conv_w,
                     _gate_vec(dn_a_log), _gate_vec(dn_dt_bias), ea, eb, bd, wg_pad, _row(gla_gate_bias))
    nwc, nwd = _row(jnp.tile(dn_norm_w, (1, HEADS))), _row(jnp.tile(gla_norm_w, (1, HEADS)))
    norm1, norm2 = _row(norm1_w), _row(norm2_w)
    x_in = xf.reshape(bsz, seq, d)
    for l in range(depth):
        ya, yb, cpk, cg, dpk, dla = _inproj(x_in, norm1, w_in_b, l, *inproj_params)
        yc, yd, wo_b, wgu_b, wd_b = _recur(cpk, cg, dpk, dla, nwc, nwd, bd, ltri, l, cast_jobs=ffn_jobs(l))
        last = l == depth - 1
        fn = final_norm_w[None, :] if last else norm1[l + 1]
        outs = _outproj_ffn(xf, tuple(y.reshape(m, GROUP) for y in (ya, yb, yc, yd)),
                            wo_b, norm2, wgu_b, wd_b, fn, layer=l, final=last)
        xf = outs[0]
        if not last:
            x_in = outs[1].reshape(bsz, seq, d)
    return xf.reshape(bsz, seq, d)
```

```python
import functools

import numpy as np
import jax
import jax.numpy as jnp
from jax import lax
from jax.experimental import pallas as pl
from jax.experimental.pallas import tpu as pltpu

F32 = jnp.float32
BF16 = jnp.bfloat16

D_MODEL = 1024
GROUP = 256
HEADS = 4
HEAD_DIM = 64
SGU_CHUNK = 128
SC_WIDTH = 3
DN_CONV_WIDTH = 4
CHUNK = 64
GATE_RANK = 16
GATE_TEMP = 16.0
D_FF = 2816
EPS = 1e-6
GATE_COLS = 128
TAIL = 8

ROW_TILE = 1024
SEQ_TILE = 512
FF_CHUNK = 256
VMEM_LIMIT = 60 * 1024 * 1024

MAIN_A = 0
MAIN_B = MAIN_A + 2 * GROUP
MAIN_C = MAIN_B + 3 * GROUP
MAIN_COLS = MAIN_C + 3 * GROUP
REST_CZ = 0
REST_D = REST_CZ + GROUP
REST_G = REST_D + 4 * GROUP
REST_COLS = REST_G + GATE_COLS
C_PACK = 5 * GROUP
D_PACK = 4 * GROUP


def _dot(a, b):
    return jnp.dot(a, b, preferred_element_type=F32)


def _dot_nt(a, b):
    return lax.dot_general(a, b, (((1,), (1,)), ((), ())), preferred_element_type=F32)


def _dot_tn(a, b):
    return lax.dot_general(a, b, (((0,), (0,)), ((), ())), preferred_element_type=F32)


def _split2(x):
    hi = x.astype(BF16)
    lo = (x - hi.astype(F32)).astype(BF16)
    return hi, lo


def _dot_sel_rhs(x, sel):
    hi, lo = _split2(x)
    return _dot(hi, sel) + _dot(lo, sel)


def _dot_sel_lhs(sel, x):
    hi, lo = _split2(x)
    return _dot(sel, hi) + _dot(sel, lo)


def _softplus(x):
    return jnp.maximum(x, 0.0) + jnp.log(1.0 + jnp.exp(-jnp.abs(x)))


def _sigmoid(x):
    return 0.5 + 0.5 * jnp.tanh(0.5 * x)


def _silu(x):
    h = 0.5 * x
    return h + h * jnp.tanh(h)


def _gelu_tanh(x):
    h = 0.5 * x
    return h + h * jnp.tanh(x * (0.7978845608028654 + (0.7978845608028654 * 0.044715) * (x * x)))


def _rmsnorm_bf16(x, w):
    ms = jnp.mean(x * x, axis=-1, keepdims=True)
    return (x * lax.rsqrt(ms + EPS) * w).astype(BF16)


def _blockdiag(x, mask):
    xb = x.astype(BF16)
    return jnp.concatenate([xb] * HEADS, axis=0) * mask


def _head_mean_sq(x, bones):
    return _dot((x * x).astype(BF16), bones) * (1.0 / HEAD_DIM)


LANES = 128


def _causal_conv(x, ext_ref, out_ref, cw_ref, width):
    tm, ch = x.shape
    half = tm // 2
    n_slabs = ch // LANES
    for s in range(n_slabs):
        ext_ref[s, TAIL:TAIL + tm, :] = x[:, s * LANES:(s + 1) * LANES]
    for s in range(n_slabs):
        lanes = slice(s * LANES, (s + 1) * LANES)
        for parity in (0, 1):
            acc = None
            for j in range(width):
                tap = (ext_ref[s, pl.ds(TAIL + parity - j, half, stride=2), :]
                       * cw_ref[width - 1 - j:width - j, lanes])
                acc = tap if acc is None else acc + tap
            out_ref[s, pl.ds(parity, half, stride=2), :] = acc
    for s in range(n_slabs):
        ext_ref[s, 0:TAIL, :] = x[tm - TAIL:tm, s * LANES:(s + 1) * LANES]
    return jnp.concatenate([out_ref[s] for s in range(n_slabs)], axis=1)


def _layer_param(stacked, layer):
    zeros = (0,) * (stacked.ndim - 1)
    return pl.BlockSpec((None,) + stacked.shape[1:], lambda i: (layer,) + zeros)


def _layer_block(stacked, layer):
    zeros = (0,) * (stacked.ndim - 1)
    return pl.BlockSpec((None,) + stacked.shape[1:], lambda i: (layer,) + zeros,
                        pipeline_mode=pl.Buffered(1))


def _resident(a):
    return pl.BlockSpec(a.shape, lambda i: (0, 0), pipeline_mode=pl.Buffered(1))


CAST_BLOCKS = 16
(V_NORM1, V_NORM2, V_POST, V_LNW, V_LNB, V_ALOG, V_DTB, V_GB, V_NWC, V_NWD) = range(10)
VEC_ROWS = 16


def _vec(vec_ref, row, n):
    return vec_ref[row:row + 1, 0:n]


def _cast_specs(cast_jobs):
    last = CAST_BLOCKS - 1
    in_specs, out_specs, out_shapes = [], [], []
    for stacked, layer, n_cols in cast_jobs:
        rows = stacked.shape[1]
        assert rows % (16 * CAST_BLOCKS) == 0 and n_cols % 128 == 0
        blk = rows // CAST_BLOCKS
        in_specs.append(pl.BlockSpec((None, blk, n_cols),
                                     lambda i, layer=layer: (layer, jnp.minimum(i, last), 0)))
        out_specs.append(pl.BlockSpec((blk, n_cols), lambda i: (jnp.minimum(i, last), 0)))
        out_shapes.append(jax.ShapeDtypeStruct((rows, n_cols), BF16))
    return in_specs, out_specs, out_shapes


def _run_casts(cast_in, cast_out):
    for src, dst in zip(cast_in, cast_out):
        dst[...] = src[...].astype(dst.dtype)


def _split_refs(refs, *counts):
    assert sum(counts) == len(refs)
    groups, pos = [], 0
    for c in counts:
        groups.append(tuple(refs[pos:pos + c]))
        pos += c
    return groups


def _run_stages(stages):
    stages = list(stages)
    while stages:
        for gen in list(stages):
            try:
                next(gen)
            except StopIteration:
                stages.remove(gen)


def _inproj_stages(x_ref, vec_ref, w_ref, wr_ref, ws_ref, bs_ref, scw_ref,
                   dcw_ref, ea_ref, eb_ref, bd_ref, wg_ref,
                   ya_ref, yb_ref, cpk_ref, cg_ref, dpk_ref, dla_ref, extb_ref, extc_ref, convb_ref, convc_ref):
    nb, tm = x_ref.shape[0], x_ref.shape[1]
    m = nb * tm
    batch_rows = [slice(b * tm, (b + 1) * tm) for b in range(nb)]

    x = x_ref[...].reshape(m, D_MODEL)
    if x.dtype == BF16:
        h = x
    else:
        h = _rmsnorm_bf16(x, _vec(vec_ref, V_NORM1, D_MODEL))
    bd = bd_ref[...]

    def project(ref, col0, n_slabs):
        parts = []
        for c in range(col0, col0 + n_slabs * GROUP, GROUP):
            parts.append(_dot(h, ref[:, c:c + GROUP]))
            yield
        return jnp.concatenate(parts, axis=1)

    pc = yield from project(w_ref, MAIN_C, 3)
    pcz = yield from project(wr_ref, REST_CZ, 1)
    pg = _dot(h, wr_ref[:, REST_G:REST_G + GATE_COLS])
    pa = yield from project(w_ref, MAIN_A, 2)

    qkv = jnp.concatenate(
        [_silu(_causal_conv(pc[rows, 0:3 * GROUP], extc_ref.at[b], convc_ref, dcw_ref, DN_CONV_WIDTH))
         for b, rows in enumerate(batch_rows)], axis=0)
    q = qkv[:, 0:GROUP]
    k = qkv[:, GROUP:2 * GROUP]
    pb = yield from project(w_ref, MAIN_B, 3)
    qn = q * lax.rsqrt(_dot((q * q).astype(BF16), bd) + EPS) * (HEAD_DIM ** -0.5)
    kn = k * lax.rsqrt(_dot((k * k).astype(BF16), bd) + EPS)
    yield
    g_narrow = -jnp.exp(_vec(vec_ref, V_ALOG, GATE_COLS)) * _softplus(pg + _vec(vec_ref, V_DTB, GATE_COLS))
    g_wide = _dot_sel_rhs(g_narrow, ea_ref[...])
    beta = _dot(_sigmoid(pg).astype(BF16), eb_ref[...])
    zg_c = _silu(pcz)
    pre = _dot(pg.astype(BF16), wg_ref[...]) + _vec(vec_ref, V_GB, GROUP)
    log_a = -_softplus(-pre) * (1.0 / GATE_TEMP)
    yield

    tri = (lax.broadcasted_iota(jnp.int32, (SGU_CHUNK, SGU_CHUNK), 0)
           >= lax.broadcasted_iota(jnp.int32, (SGU_CHUNK, SGU_CHUNK), 1))
    head_of_lane = lax.broadcasted_iota(jnp.int32, (SGU_CHUNK, GROUP), 1) // HEAD_DIM
    ws = jnp.concatenate([jnp.where(tri, ws_ref[hd], 0.0).astype(BF16) for hd in range(HEADS)],
                         axis=1)
    lnw, lnb = _vec(vec_ref, V_LNW, GROUP), _vec(vec_ref, V_LNB, GROUP)
    for b in range(nb):
        for c in range(tm // SGU_CHUNK):
            rows = slice(b * tm + c * SGU_CHUNK, b * tm + (c + 1) * SGU_CHUNK)
            u = _gelu_tanh(pa[rows, 0:GROUP])
            v = _gelu_tanh(pa[rows, GROUP:2 * GROUP])
            mu = jnp.mean(v, axis=-1, keepdims=True)
            vc = v - mu
            var = jnp.mean(vc * vc, axis=-1, keepdims=True)
            vn = vc * lax.rsqrt(var + EPS) * lnw + lnb
            vn_heads = jnp.concatenate([jnp.where(head_of_lane == hd, vn, 0.0).astype(BF16)
                                        for hd in range(HEADS)], axis=0)
            mixed = bs_ref[...] + _dot(ws, vn_heads)
            ya_ref[b, c * SGU_CHUNK:(c + 1) * SGU_CHUNK, :] = (u * mixed).astype(ya_ref.dtype)
            yield

    pd = yield from project(wr_ref, REST_D, 4)

    for b, rows in enumerate(batch_rows):
        conv_b = _causal_conv(pb[rows, GROUP:2 * GROUP] * pb[rows, 2 * GROUP:3 * GROUP],
                              extb_ref.at[b], convb_ref, scw_ref, SC_WIDTH)
        yb_ref[b] = (pb[rows, 0:GROUP] * conv_b).astype(yb_ref.dtype)

    for b, rows in enumerate(batch_rows):
        cg_ref[b] = g_wide[rows]
        cpk_ref[b, :, 0:GROUP] = qn[rows].astype(BF16)
        cpk_ref[b, :, GROUP:2 * GROUP] = kn[rows].astype(BF16)
        cpk_ref[b, :, 2 * GROUP:3 * GROUP] = qkv[rows, 2 * GROUP:3 * GROUP].astype(BF16)
        cpk_ref[b, :, 3 * GROUP:4 * GROUP] = beta[rows].astype(BF16)
        cpk_ref[b, :, 4 * GROUP:5 * GROUP] = zg_c[rows].astype(BF16)
        dla_ref[b] = log_a[rows]
        dpk_ref[b, :, 0:3 * GROUP] = pd[rows, 0:3 * GROUP].astype(BF16)
        dpk_ref[b, :, 3 * GROUP:4 * GROUP] = _silu(pd[rows, 3 * GROUP:4 * GROUP]).astype(BF16)


def _repack_w_in_rest(w_ref, wr_ref):
    c_a = MAIN_COLS
    c_z = c_a + 2 * HEADS
    d0 = c_z + GROUP
    d_g = d0 + 3 * GROUP
    d_z = d_g + GATE_RANK
    wr_ref[:, REST_CZ:REST_CZ + GROUP] = w_ref[:, c_z:d0]
    wr_ref[:, REST_D:REST_D + 3 * GROUP] = w_ref[:, d0:d_g]
    wr_ref[:, REST_D + 3 * GROUP:REST_D + 4 * GROUP] = w_ref[:, d_z:d_z + GROUP]
    wr_ref[:, REST_G:REST_G + GATE_COLS] = jnp.zeros((wr_ref.shape[0], GATE_COLS), wr_ref.dtype)
    wr_ref[:, REST_G:REST_G + 2 * HEADS] = w_ref[:, c_a:c_z]
    wr_ref[:, REST_G + 2 * HEADS:REST_G + 2 * HEADS + GATE_RANK] = w_ref[:, d_g:d_z]


def _inproj_kernel(n_cast, x_ref, vec_ref, w_ref, ws_ref, bs_ref, scw_ref, dcw_ref,
                   ea_ref, eb_ref, bd_ref, wg_ref, *rest):
    cast_in, outs, cast_out, scratch = _split_refs(rest, n_cast, 6, n_cast, 5)
    ya_ref, yb_ref, cpk_ref, cg_ref, dpk_ref, dla_ref = outs
    extb_ref, extc_ref, convb_ref, convc_ref, wr_ref = scratch
    _run_casts(cast_in, cast_out)

    @pl.when(pl.program_id(0) == 0)
    def _():
        extb_ref[...] = jnp.zeros(extb_ref.shape, F32)
        extc_ref[...] = jnp.zeros(extc_ref.shape, F32)
        _repack_w_in_rest(w_ref, wr_ref)

    _run_stages([_inproj_stages(x_ref, vec_ref, w_ref, wr_ref, ws_ref, bs_ref, scw_ref, dcw_ref,
                                ea_ref, eb_ref, bd_ref, wg_ref,
                                ya_ref, yb_ref, cpk_ref, cg_ref, dpk_ref, dla_ref, extb_ref, extc_ref,
                                convb_ref, convc_ref)])


def _inproj(x, vecs, w_in_b, layer, ws, bs_wide, scw, dcw, ea, eb, bd, wg_pad, cast_jobs=()):
    nb, seq, _ = x.shape
    tm = SEQ_TILE
    n_tiles = seq // tm
    assert n_tiles >= CAST_BLOCKS or not cast_jobs
    cast_in_specs, cast_out_specs, cast_out_shapes = _cast_specs(cast_jobs)
    tile = lambda i: (0, i, 0)
    whole = lambda a: pl.BlockSpec(a.shape, (lambda i: (0,) * a.ndim))
    per_layer = lambda a: _layer_param(a, layer)
    outs = ((GROUP, BF16), (GROUP, BF16), (C_PACK, BF16), (GROUP, F32), (D_PACK, BF16), (GROUP, F32))
    return pl.pallas_call(
        functools.partial(_inproj_kernel, len(cast_jobs)),
        grid=(n_tiles,),
        in_specs=[pl.BlockSpec((nb, tm, D_MODEL), tile),
                  per_layer(vecs),
                  _layer_block(w_in_b, layer),
                  per_layer(ws), per_layer(bs_wide), per_layer(scw),
                  per_layer(dcw), whole(ea), whole(eb), whole(bd),
                  per_layer(wg_pad)]
                 + cast_in_specs,
        out_specs=[pl.BlockSpec((nb, tm, n), tile) for n, _ in outs] + cast_out_specs,
        out_shape=[jax.ShapeDtypeStruct((nb, seq, n), dt) for n, dt in outs] + cast_out_shapes,
        scratch_shapes=[pltpu.VMEM((nb, GROUP // LANES, TAIL + tm, LANES), F32),
                        pltpu.VMEM((nb, 3 * GROUP // LANES, TAIL + tm, LANES), F32),
                        pltpu.VMEM((GROUP // LANES, tm, LANES), F32),
                        pltpu.VMEM((3 * GROUP // LANES, tm, LANES), F32),
                        pltpu.VMEM((D_MODEL, REST_COLS), BF16)],
        compiler_params=pltpu.CompilerParams(dimension_semantics=("arbitrary",),
                                             vmem_limit_bytes=VMEM_LIMIT),
        name="in_proj",
    )(x, vecs, w_in_b, ws, bs_wide, scw, dcw, ea, eb, bd, wg_pad, *[w for w, _, _ in cast_jobs])


def _chunk_masks():
    i = lax.broadcasted_iota(jnp.int32, (CHUNK, GROUP), 0)
    j = lax.broadcasted_iota(jnp.int32, (CHUNK, GROUP), 1) % CHUNK
    return i == j, i >= j, i > j


def _const_inputs():
    lane_head = np.arange(GROUP) // HEAD_DIM
    bd = (lane_head[:, None] == lane_head[None, :]).astype(np.float32)
    ltri = np.tril(np.ones((CHUNK, CHUNK), np.float32))
    return jnp.asarray(bd, BF16), jnp.asarray(ltri, BF16)


def _chunk_rows(c):
    return slice(c * CHUNK, (c + 1) * CHUNK)


def _gdn_stages(cpk_ref, cg_ref, nw_ref, bd_ref, ltri_ref, y_ref, ac_ref, bb_ref, d_ref, dec_ref, o_ref, s_ref):
    nb, tm = cpk_ref.shape[0], cpk_ref.shape[1]
    cps = tm // CHUNK
    eye_w, causal_w, strict_w = _chunk_masks()
    eye_f = eye_w.astype(F32)
    ltri = ltri_ref[...]
    bd = bd_ref[...]

    yield from _gdn_prepare([(b, c) for b in range(nb) for c in range(cps)], cps, cpk_ref, cg_ref,
                            ac_ref, bb_ref, d_ref, dec_ref, eye_w, causal_w, strict_w, eye_f, ltri, bd)

    for c in range(cps):
        for b in range(nb):
            i = b * cps + c
            s = s_ref[b]
            r = _dot(ac_ref[i], _blockdiag(s, bd))
            s_ref[b] = s * dec_ref[i, 0:1, :] + r[0:CHUNK] + bb_ref[i]
            o_ref[b, _chunk_rows(c), :] = r[CHUNK:2 * CHUNK] + d_ref[i]
        yield

    for b in range(nb):
        o = o_ref[b]
        zg = cpk_ref[b, :, 4 * GROUP:5 * GROUP].astype(F32)
        y_ref[b] = (o * lax.rsqrt(_head_mean_sq(o, bd) + EPS) * nw_ref[...] * zg).astype(y_ref.dtype)


def _gdn_prepare(items, cps, cpk_ref, cg_ref, ac_ref, bb_ref, d_ref, dec_ref,
                 eye_w, causal_w, strict_w, eye_f, ltri, bd):
    n = len(items)

    def load(col, i):
        b, c = items[i]
        return cpk_ref[b, _chunk_rows(c), col * GROUP:(col + 1) * GROUP]

    qn = [load(0, i) for i in range(n)]
    kn = [load(1, i) for i in range(n)]
    beta = [load(3, i) for i in range(n)]
    gc = [_dot_sel_lhs(ltri, cg_ref[b, _chunk_rows(c), :]) for b, c in items]
    yield
    eg = [jnp.exp(x) for x in gc]
    kb = [kn[i] * beta[i] for i in range(n)]
    eye_b = eye_f.astype(BF16)
    r1 = [_dot_nt(jnp.concatenate([kb[i], qn[i], eye_b], axis=0), _blockdiag(kn[i], bd))
          for i in range(n)]
    yield
    low, attn, g_row = [], [], []
    for i in range(n):
        g_row.append(jnp.sum(jnp.where(eye_w, gc[i], 0.0), axis=0, keepdims=True))
        decay = jnp.where(causal_w, jnp.exp(jnp.minimum(gc[i] - g_row[i], 0.0)), 0.0)
        low.append(jnp.where(strict_w, r1[i][0:CHUNK] * decay, 0.0))
        attn.append((r1[i][CHUNK:2 * CHUNK] * decay).astype(BF16))
    t = [eye_f - x for x in low]
    p = [_dot(x.astype(BF16), _blockdiag(x, bd)) for x in low]
    yield
    n_levels = CHUNK.bit_length() - 1
    for level in range(1, n_levels):
        if level < n_levels - 1:
            r2 = [_dot(jnp.concatenate([p[i], t[i]], axis=0).astype(BF16), _blockdiag(p[i], bd))
                  for i in range(n)]
            p = [x[0:CHUNK] for x in r2]
            t = [t[i] + r2[i][CHUNK:2 * CHUNK] for i in range(n)]
        else:
            t = [t[i] + _dot(t[i].astype(BF16), _blockdiag(p[i], bd)) for i in range(n)]
        yield
    gc_last = [x[CHUNK - 1:CHUNK, :] for x in gc]
    kd_t = [(r1[i][2 * CHUNK:3 * CHUNK] * jnp.exp(gc_last[i] - g_row[i])).astype(BF16) for i in range(n)]
    lhs_t = [_dot(jnp.concatenate([attn[i], kd_t[i]], axis=0), _blockdiag(t[i], bd)) for i in range(n)]
    yield
    prod = [_dot(lhs_t[i].astype(BF16),
                 jnp.concatenate([_blockdiag(kb[i] * eg[i], bd), _blockdiag(load(2, i) * beta[i], bd)], axis=1))
            for i in range(n)]
    for i, (b, c) in enumerate(items):
        j = b * cps + c
        ac_ref[j, 0:CHUNK, :] = (-prod[i][CHUNK:2 * CHUNK, 0:GROUP]).astype(BF16)
        bb_ref[j] = prod[i][CHUNK:2 * CHUNK, GROUP:2 * GROUP]
        ac_ref[j, CHUNK:2 * CHUNK, :] = (qn[i] * eg[i] - prod[i][0:CHUNK, 0:GROUP]).astype(BF16)
        d_ref[j] = prod[i][0:CHUNK, GROUP:2 * GROUP]
        dec_ref[j] = jnp.broadcast_to(jnp.exp(gc_last[i]), (TAIL, GROUP))
    yield


def _gla_stages(dpk_ref, dla_ref, nw_ref, bd_ref, ltri_ref, y_ref, o_ref, st_ref):
    nb, tm = dpk_ref.shape[0], dpk_ref.shape[1]
    cps = tm // CHUNK
    _, causal_w, _ = _chunk_masks()
    ltri = ltri_ref[...]
    bd = bd_ref[...]
    bdf = bd.astype(F32)
    mid = CHUNK // 2

    items = [(b, c) for c in range(cps) for b in range(nb)]
    n = len(items)
    q = [dpk_ref[b, _chunk_rows(c), 0:GROUP].astype(F32) * (HEAD_DIM ** -0.5) for b, c in items]
    k = [dpk_ref[b, _chunk_rows(c), GROUP:2 * GROUP].astype(F32) for b, c in items]
    v = [dpk_ref[b, _chunk_rows(c), 2 * GROUP:3 * GROUP] for b, c in items]
    gcum = [_dot_sel_lhs(ltri, dla_ref[b, _chunk_rows(c), :]) for b, c in items]
    yield
    g_mid = [x[mid:mid + 1, :] for x in gcum]
    g_last = [x[CHUNK - 1:CHUNK, :] for x in gcum]
    attn = [jnp.where(causal_w,
                      _dot_nt((q[i] * jnp.exp(gcum[i] - g_mid[i])).astype(BF16),
                              _blockdiag(k[i] * jnp.exp(g_mid[i] - gcum[i]), bd)), 0.0).astype(BF16)
            for i in range(n)]
    yield
    upd = [bdf * _dot_tn((k[i] * jnp.exp(g_last[i] - gcum[i])).astype(BF16), v[i]) for i in range(n)]
    yield
    o_intra = [_dot(attn[i], _blockdiag(v[i], bd)) for i in range(n)]
    qg = [(q[i] * jnp.exp(gcum[i])).astype(BF16) for i in range(n)]
    yield
    for i, (b, c) in enumerate(items):
        st = st_ref[b]
        o_ref[b, _chunk_rows(c), :] = o_intra[i] + _dot(qg[i], st.astype(BF16))
        dec_col = jnp.transpose(jnp.broadcast_to(jnp.exp(g_last[i]), (GROUP, GROUP)))
        st_ref[b] = st * dec_col + upd[i]
        if b == nb - 1:
            yield

    for b in range(nb):
        o = o_ref[b]
        zg = dpk_ref[b, :, 3 * GROUP:4 * GROUP].astype(F32)
        y_ref[b] = (o * lax.rsqrt(_head_mean_sq(o, bd) + EPS) * nw_ref[...] * zg).astype(y_ref.dtype)


def _recur_kernel(cpk_ref, cg_ref, dpk_ref, dla_ref, vec_ref, bd_ref, ltri_ref, yc_ref, yd_ref,
                  ac_ref, bb_ref, d_ref, dec_ref, oc_ref, s_ref, od_ref, st_ref):
    nwc_ref = vec_ref.at[pl.ds(V_NWC, 1), pl.ds(0, GROUP)]
    nwd_ref = vec_ref.at[pl.ds(V_NWD, 1), pl.ds(0, GROUP)]

    @pl.when(pl.program_id(0) == 0)
    def _():
        s_ref[...] = jnp.zeros(s_ref.shape, F32)
        st_ref[...] = jnp.zeros(st_ref.shape, F32)

    _run_stages([_gla_stages(dpk_ref, dla_ref, nwd_ref, bd_ref, ltri_ref, yd_ref, od_ref, st_ref),
                 _gdn_stages(cpk_ref, cg_ref, nwc_ref, bd_ref, ltri_ref, yc_ref,
                             ac_ref, bb_ref, d_ref, dec_ref, oc_ref, s_ref)])


def _recur(cpk, cg, dpk, dla, vecs, bd, ltri, layer):
    nb, seq, _ = cpk.shape
    tm = SEQ_TILE
    n_chunks = nb * tm // CHUNK
    tile = lambda i: (0, i, 0)
    whole = lambda a: pl.BlockSpec(a.shape, (lambda i: (0,) * a.ndim))
    out = jax.ShapeDtypeStruct((nb, seq, GROUP), BF16)
    return pl.pallas_call(
        _recur_kernel,
        grid=(seq // tm,),
        in_specs=[pl.BlockSpec((nb, tm, C_PACK), tile),
                  pl.BlockSpec((nb, tm, GROUP), tile),
                  pl.BlockSpec((nb, tm, D_PACK), tile),
                  pl.BlockSpec((nb, tm, GROUP), tile),
                  _layer_param(vecs, layer), whole(bd), whole(ltri)],
        out_specs=[pl.BlockSpec((nb, tm, GROUP), tile)] * 2,
        out_shape=[out, out],
        scratch_shapes=[pltpu.VMEM((n_chunks, 2 * CHUNK, GROUP), BF16),
                        pltpu.VMEM((n_chunks, CHUNK, GROUP), F32),
                        pltpu.VMEM((n_chunks, CHUNK, GROUP), F32),
                        pltpu.VMEM((n_chunks, TAIL, GROUP), F32),
                        pltpu.VMEM((nb, tm, GROUP), F32),
                        pltpu.VMEM((nb, CHUNK, GROUP), F32),
                        pltpu.VMEM((nb, tm, GROUP), F32),
                        pltpu.VMEM((nb, GROUP, GROUP), F32)],
        compiler_params=pltpu.CompilerParams(dimension_semantics=("arbitrary",),
                                             vmem_limit_bytes=VMEM_LIMIT),
        name="recur",
    )(cpk, cg, dpk, dla, vecs, bd, ltri)


def _outproj_ffn_kernel(final, x_ref, ya_ref, yb_ref, yc_ref, yd_ref, wo_ref, vec_ref, wgu_ref, wd_ref, *rest):
    post_w = _vec(vec_ref, V_POST, D_MODEL)
    if final:
        o_ref, act_ref = rest
    else:
        o_ref, h_ref, act_ref = rest
    x1 = x_ref[...]
    for idx, y_ref in enumerate((ya_ref, yb_ref, yc_ref, yd_ref)):
        x1 = x1 + _dot(y_ref[...], wo_ref[idx * GROUP:(idx + 1) * GROUP, :])
    ms = jnp.mean(x1 * x1, axis=-1, keepdims=True)
    h = (x1 * lax.rsqrt(ms + EPS) * _vec(vec_ref, V_NORM2, D_MODEL)).astype(BF16)
    for j in range(D_FF // FF_CHUNK):
        cols = slice(j * FF_CHUNK, (j + 1) * FF_CHUNK)
        gate = _dot(h, wgu_ref[:, cols])
        up = _dot(h, wgu_ref[:, D_FF + j * FF_CHUNK:D_FF + (j + 1) * FF_CHUNK])
        act_ref[:, cols] = (_silu(gate) * up).astype(BF16)
    x2 = x1 + _dot(act_ref[...], wd_ref[...])
    if final:
        ms2 = jnp.mean(x2 * x2, axis=-1, keepdims=True)
        x2 = x2 * lax.rsqrt(ms2 + EPS) * post_w
    else:
        h_ref[...] = _rmsnorm_bf16(x2, post_w)
    o_ref[...] = x2


def _outproj_ffn(x, ys, wo, vecs, wgu, wd, layer, final):
    m = x.shape[0]
    n_steps = m // ROW_TILE
    row = lambda i: (i, 0)
    out_specs = [pl.BlockSpec((ROW_TILE, D_MODEL), row)]
    out_shape = [jax.ShapeDtypeStruct((m, D_MODEL), F32)]
    if not final:
        out_specs.append(pl.BlockSpec((ROW_TILE, D_MODEL), row))
        out_shape.append(jax.ShapeDtypeStruct((m, D_MODEL), BF16))
    return pl.pallas_call(
        functools.partial(_outproj_ffn_kernel, final),
        grid=(n_steps,),
        in_specs=[pl.BlockSpec((ROW_TILE, D_MODEL), row)]
                 + [pl.BlockSpec((ROW_TILE, GROUP), row)] * 4
                 + [_resident(wo), _layer_param(vecs, layer), _resident(wgu), _resident(wd)],
        out_specs=out_specs,
        out_shape=out_shape,
        scratch_shapes=[pltpu.VMEM((ROW_TILE, D_FF), BF16)],
        compiler_params=pltpu.CompilerParams(dimension_semantics=("arbitrary",),
                                             vmem_limit_bytes=VMEM_LIMIT),
        name="outproj_ffn",
    )(x, *ys, wo, vecs, wgu, wd)


def _pack_vectors(rows):
    depth = next(iter(rows.values())).shape[0]
    blank = jnp.zeros((depth, D_MODEL), F32)
    table = [blank] * VEC_ROWS
    for r, v in rows.items():
        table[r] = jnp.pad(v.astype(F32), ((0, 0), (0, D_MODEL - v.shape[1])))
    return jnp.stack(table, axis=1)


def _gate_selectors():
    ea = np.zeros((GATE_COLS, GROUP), np.float32)
    eb = np.zeros((GATE_COLS, GROUP), np.float32)
    for h in range(HEADS):
        ea[h, h * HEAD_DIM:(h + 1) * HEAD_DIM] = 1.0
        eb[HEADS + h, h * HEAD_DIM:(h + 1) * HEAD_DIM] = 1.0
    return jnp.asarray(ea, BF16), jnp.asarray(eb, BF16)


def kernel(x, norm1_w, w_in, sgu_ln_w, sgu_ln_b, sgu_w_spatial, sgu_b_spatial, sc_conv_w, dn_conv_w, dn_a_log, dn_dt_bias, dn_norm_w, gla_w_gate2, gla_gate_bias, gla_norm_w, w_out, norm2_w, w_gate_up, w_down, final_norm_w):
    bsz, seq, d = x.shape
    depth = w_in.shape[0]
    assert seq % ROW_TILE == 0 and seq % SEQ_TILE == 0 and d == D_MODEL
    m = bsz * seq
    xf = x.reshape(m, d)
    bd, ltri = _const_inputs()
    ea, eb = _gate_selectors()
    w_in_b = w_in.astype(BF16)
    ffn_jobs = lambda layer: [(w_out, layer, D_MODEL), (w_gate_up, layer, 2 * D_FF), (w_down, layer, D_MODEL)]
    bs_wide = jnp.repeat(jnp.swapaxes(sgu_b_spatial, 1, 2), HEAD_DIM, axis=2)
    wg_pad = jnp.pad(gla_w_gate2.astype(BF16),
                     ((0, 0), (2 * HEADS, GATE_COLS - 2 * HEADS - GATE_RANK), (0, 0)))
    inproj_params = (sgu_w_spatial, bs_wide, sc_conv_w, dn_conv_w, ea, eb, bd, wg_pad)
    vecs = _pack_vectors({
        V_NORM1: norm1_w, V_NORM2: norm2_w,
        V_POST: jnp.concatenate([norm1_w[1:], final_norm_w[None, :]], axis=0),
        V_LNW: sgu_ln_w, V_LNB: sgu_ln_b, V_ALOG: dn_a_log, V_DTB: dn_dt_bias, V_GB: gla_gate_bias,
        V_NWC: jnp.tile(dn_norm_w, (1, HEADS)), V_NWD: jnp.tile(gla_norm_w, (1, HEADS))})
    x_in = xf.reshape(bsz, seq, d)
    for l in range(depth):
        ya, yb, cpk, cg, dpk, dla, wo_b, wgu_b, wd_b = _inproj(x_in, vecs, w_in_b, l, *inproj_params,
                                                               cast_jobs=ffn_jobs(l))
        yc, yd = _recur(cpk, cg, dpk, dla, vecs, bd, ltri, l)
        last = l == depth - 1
        outs = _outproj_ffn(xf, tuple(y.reshape(m, GROUP) for y in (ya, yb, yc, yd)),
                            wo_b, vecs, wgu_b, wd_b, layer=l, final=last)
        xf = outs[0]
        if not last:
            x_in = outs[1].reshape(bsz, seq, d)
    return xf.reshape(bsz, seq, d)
```

```python
import functools

import numpy as np
import jax
import jax.numpy as jnp
from jax import lax
from jax.experimental import pallas as pl
from jax.experimental.pallas import tpu as pltpu

F32 = jnp.float32
BF16 = jnp.bfloat16

D_MODEL = 1024
GROUP = 256
HEADS = 4
HEAD_DIM = 64
SGU_CHUNK = 128
SC_WIDTH = 3
DN_CONV_WIDTH = 4
CHUNK = 64
GATE_RANK = 16
GATE_TEMP = 16.0
D_FF = 2816
EPS = 1e-6
GATE_COLS = 128
TAIL = 8

ROW_TILE = 1024
SEQ_TILE = 512
FF_CHUNK = 256
VMEM_LIMIT = 60 * 1024 * 1024

MAIN_A = 0
MAIN_B = MAIN_A + 2 * GROUP
MAIN_C = MAIN_B + 3 * GROUP
MAIN_COLS = MAIN_C + 3 * GROUP
REST_CZ = 0
REST_D = REST_CZ + GROUP
REST_G = REST_D + 4 * GROUP
REST_COLS = REST_G + GATE_COLS
C_PACK = 5 * GROUP
D_PACK = 4 * GROUP


def _dot(a, b):
    return jnp.dot(a, b, preferred_element_type=F32)


def _dot_nt(a, b):
    return lax.dot_general(a, b, (((1,), (1,)), ((), ())), preferred_element_type=F32)


def _dot_tn(a, b):
    return lax.dot_general(a, b, (((0,), (0,)), ((), ())), preferred_element_type=F32)


def _split2(x):
    hi = x.astype(BF16)
    lo = (x - hi.astype(F32)).astype(BF16)
    return hi, lo


def _dot_sel_rhs(x, sel):
    hi, lo = _split2(x)
    return _dot(hi, sel) + _dot(lo, sel)


def _dot_sel_lhs(sel, x):
    hi, lo = _split2(x)
    return _dot(sel, hi) + _dot(sel, lo)


def _softplus(x):
    return jnp.maximum(x, 0.0) + jnp.log(1.0 + jnp.exp(-jnp.abs(x)))


def _sigmoid(x):
    return 0.5 + 0.5 * jnp.tanh(0.5 * x)


def _silu(x):
    h = 0.5 * x
    return h + h * jnp.tanh(h)


def _gelu_tanh(x):
    h = 0.5 * x
    return h + h * jnp.tanh(x * (0.7978845608028654 + (0.7978845608028654 * 0.044715) * (x * x)))


def _rmsnorm_bf16(x, w):
    ms = jnp.mean(x * x, axis=-1, keepdims=True)
    return (x * lax.rsqrt(ms + EPS) * w).astype(BF16)


def _blockdiag(x, mask):
    xb = x.astype(BF16)
    return jnp.concatenate([xb] * HEADS, axis=0) * mask


def _head_mean_sq(x, bones):
    return _dot((x * x).astype(BF16), bones) * (1.0 / HEAD_DIM)


LANES = 128


def _causal_conv(x, ext_ref, out_ref, cw_ref, width):
    tm, ch = x.shape
    half = tm // 2
    n_slabs = ch // LANES
    for s in range(n_slabs):
        ext_ref[s, TAIL:TAIL + tm, :] = x[:, s * LANES:(s + 1) * LANES]
    for s in range(n_slabs):
        lanes = slice(s * LANES, (s + 1) * LANES)
        for parity in (0, 1):
            acc = None
            for j in range(width):
                tap = (ext_ref[s, pl.ds(TAIL + parity - j, half, stride=2), :]
                       * cw_ref[width - 1 - j:width - j, lanes])
                acc = tap if acc is None else acc + tap
            out_ref[s, pl.ds(parity, half, stride=2), :] = acc
    for s in range(n_slabs):
        ext_ref[s, 0:TAIL, :] = x[tm - TAIL:tm, s * LANES:(s + 1) * LANES]
    return jnp.concatenate([out_ref[s] for s in range(n_slabs)], axis=1)


def _layer_param(stacked, layer):
    zeros = (0,) * (stacked.ndim - 1)
    return pl.BlockSpec((None,) + stacked.shape[1:], lambda i: (layer,) + zeros)


def _layer_block(stacked, layer):
    zeros = (0,) * (stacked.ndim - 1)
    return pl.BlockSpec((None,) + stacked.shape[1:], lambda i: (layer,) + zeros,
                        pipeline_mode=pl.Buffered(1))


def _resident(a):
    return pl.BlockSpec(a.shape, lambda i: (0, 0), pipeline_mode=pl.Buffered(1))


CAST_BLOCKS = 16
(V_NORM1, V_LNW, V_LNB, V_ALOG, V_DTB, V_GB) = range(6)


def _vec(table, key):
    return table[key]()


def _param_row(ref, layer):
    return lambda: ref[layer:layer + 1, :].astype(F32)


def _head_vec(ref, layer):
    def load():
        lane = lax.broadcasted_iota(jnp.int32, (1, GATE_COLS), 1)
        out = jnp.zeros((1, GATE_COLS), F32)
        for hd in range(HEADS):
            out = jnp.where(lane == hd, ref[layer, hd].astype(F32), out)
        return out
    return load


def _cast_specs(cast_jobs):
    last = CAST_BLOCKS - 1
    in_specs, out_specs, out_shapes = [], [], []
    for stacked, layer, n_cols in cast_jobs:
        rows = stacked.shape[1]
        assert rows % (16 * CAST_BLOCKS) == 0 and n_cols % 128 == 0
        blk = rows // CAST_BLOCKS
        in_specs.append(pl.BlockSpec((None, blk, n_cols),
                                     lambda i, layer=layer: (layer, jnp.minimum(i, last), 0)))
        out_specs.append(pl.BlockSpec((blk, n_cols), lambda i: (jnp.minimum(i, last), 0)))
        out_shapes.append(jax.ShapeDtypeStruct((rows, n_cols), BF16))
    return in_specs, out_specs, out_shapes


def _run_casts(cast_in, cast_out):
    for src, dst in zip(cast_in, cast_out):
        dst[...] = src[...].astype(dst.dtype)


def _split_refs(refs, *counts):
    assert sum(counts) == len(refs)
    groups, pos = [], 0
    for c in counts:
        groups.append(tuple(refs[pos:pos + c]))
        pos += c
    return groups


def _run_stages(stages):
    stages = list(stages)
    while stages:
        for gen in list(stages):
            try:
                next(gen)
            except StopIteration:
                stages.remove(gen)


def _inproj_stages(x_ref, vec, w_ref, wr_ref, ws_ref, bs_ref, scw_ref,
                   dcw_ref, ea_ref, eb_ref, bd_ref, wg_ref,
                   ya_ref, yb_ref, cpk_ref, cg_ref, dpk_ref, dla_ref, extb_ref, extc_ref, convb_ref, convc_ref):
    nb, tm = x_ref.shape[0], x_ref.shape[1]
    m = nb * tm
    batch_rows = [slice(b * tm, (b + 1) * tm) for b in range(nb)]

    x = x_ref[...].reshape(m, D_MODEL)
    if x.dtype == BF16:
        h = x
    else:
        h = _rmsnorm_bf16(x, _vec(vec, V_NORM1))
    bd = bd_ref[...]

    def project(ref, col0, n_slabs):
        parts = []
        for c in range(col0, col0 + n_slabs * GROUP, GROUP):
            parts.append(_dot(h, ref[:, c:c + GROUP]))
            yield
        return jnp.concatenate(parts, axis=1)

    pc = yield from project(w_ref, MAIN_C, 3)
    pcz = yield from project(wr_ref, REST_CZ, 1)
    pg = _dot(h, wr_ref[:, REST_G:REST_G + GATE_COLS])
    pa = yield from project(w_ref, MAIN_A, 2)

    qkv = jnp.concatenate(
        [_silu(_causal_conv(pc[rows, 0:3 * GROUP], extc_ref.at[b], convc_ref, dcw_ref, DN_CONV_WIDTH))
         for b, rows in enumerate(batch_rows)], axis=0)
    q = qkv[:, 0:GROUP]
    k = qkv[:, GROUP:2 * GROUP]
    pb = yield from project(w_ref, MAIN_B, 3)
    qn = q * lax.rsqrt(_dot((q * q).astype(BF16), bd) + EPS) * (HEAD_DIM ** -0.5)
    kn = k * lax.rsqrt(_dot((k * k).astype(BF16), bd) + EPS)
    yield
    g_narrow = -jnp.exp(_vec(vec, V_ALOG)) * _softplus(pg + _vec(vec, V_DTB))
    g_wide = _dot_sel_rhs(g_narrow, ea_ref[...])
    beta = _dot(_sigmoid(pg).astype(BF16), eb_ref[...])
    zg_c = _silu(pcz)
    pre = _dot(pg.astype(BF16), wg_ref[...]) + _vec(vec, V_GB)
    log_a = -_softplus(-pre) * (1.0 / GATE_TEMP)
    yield

    tri = (lax.broadcasted_iota(jnp.int32, (SGU_CHUNK, SGU_CHUNK), 0)
           >= lax.broadcasted_iota(jnp.int32, (SGU_CHUNK, SGU_CHUNK), 1))
    head_of_lane = lax.broadcasted_iota(jnp.int32, (SGU_CHUNK, GROUP), 1) // HEAD_DIM
    ws = jnp.concatenate([jnp.where(tri, ws_ref[hd], 0.0).astype(BF16) for hd in range(HEADS)],
                         axis=1)
    lnw, lnb = _vec(vec, V_LNW), _vec(vec, V_LNB)
    for b in range(nb):
        for c in range(tm // SGU_CHUNK):
            rows = slice(b * tm + c * SGU_CHUNK, b * tm + (c + 1) * SGU_CHUNK)
            u = _gelu_tanh(pa[rows, 0:GROUP])
            v = _gelu_tanh(pa[rows, GROUP:2 * GROUP])
            mu = jnp.mean(v, axis=-1, keepdims=True)
            vc = v - mu
            var = jnp.mean(vc * vc, axis=-1, keepdims=True)
            vn = vc * lax.rsqrt(var + EPS) * lnw + lnb
            vn_heads = jnp.concatenate([jnp.where(head_of_lane == hd, vn, 0.0).astype(BF16)
                                        for hd in range(HEADS)], axis=0)
            mixed = bs_ref[...] + _dot(ws, vn_heads)
            ya_ref[b, c * SGU_CHUNK:(c + 1) * SGU_CHUNK, :] = (u * mixed).astype(ya_ref.dtype)
            yield

    pd = yield from project(wr_ref, REST_D, 4)

    for b, rows in enumerate(batch_rows):
        conv_b = _causal_conv(pb[rows, GROUP:2 * GROUP] * pb[rows, 2 * GROUP:3 * GROUP],
                              extb_ref.at[b], convb_ref, scw_ref, SC_WIDTH)
        yb_ref[b] = (pb[rows, 0:GROUP] * conv_b).astype(yb_ref.dtype)

    for b, rows in enumerate(batch_rows):
        cg_ref[b] = g_wide[rows]
        cpk_ref[b, :, 0:GROUP] = qn[rows].astype(BF16)
        cpk_ref[b, :, GROUP:2 * GROUP] = kn[rows].astype(BF16)
        cpk_ref[b, :, 2 * GROUP:3 * GROUP] = qkv[rows, 2 * GROUP:3 * GROUP].astype(BF16)
        cpk_ref[b, :, 3 * GROUP:4 * GROUP] = beta[rows].astype(BF16)
        cpk_ref[b, :, 4 * GROUP:5 * GROUP] = zg_c[rows].astype(BF16)
        dla_ref[b] = log_a[rows]
        dpk_ref[b, :, 0:3 * GROUP] = pd[rows, 0:3 * GROUP].astype(BF16)
        dpk_ref[b, :, 3 * GROUP:4 * GROUP] = _silu(pd[rows, 3 * GROUP:4 * GROUP]).astype(BF16)


def _repack_w_in_rest(w_ref, wr_ref):
    c_a = MAIN_COLS
    c_z = c_a + 2 * HEADS
    d0 = c_z + GROUP
    d_g = d0 + 3 * GROUP
    d_z = d_g + GATE_RANK
    wr_ref[:, REST_CZ:REST_CZ + GROUP] = w_ref[:, c_z:d0]
    wr_ref[:, REST_D:REST_D + 3 * GROUP] = w_ref[:, d0:d_g]
    wr_ref[:, REST_D + 3 * GROUP:REST_D + 4 * GROUP] = w_ref[:, d_z:d_z + GROUP]
    wr_ref[:, REST_G:REST_G + GATE_COLS] = jnp.zeros((wr_ref.shape[0], GATE_COLS), wr_ref.dtype)
    wr_ref[:, REST_G:REST_G + 2 * HEADS] = w_ref[:, c_a:c_z]
    wr_ref[:, REST_G + 2 * HEADS:REST_G + 2 * HEADS + GATE_RANK] = w_ref[:, d_g:d_z]


def _inproj_kernel(layer, n_cast, x_ref, n1_ref, lnw_ref, lnb_ref, gb_ref, alog_ref, dtb_ref, w_ref, ws_ref, bs_ref,
                   scw_ref, dcw_ref, ea_ref, eb_ref, bd_ref, wg_ref, *rest):
    cast_in, outs, cast_out, scratch = _split_refs(rest, n_cast, 6, n_cast, 5)
    vec = {V_NORM1: _param_row(n1_ref, layer), V_LNW: _param_row(lnw_ref, layer), V_LNB: _param_row(lnb_ref, layer),
           V_GB: _param_row(gb_ref, layer), V_ALOG: _head_vec(alog_ref, layer), V_DTB: _head_vec(dtb_ref, layer)}
    ya_ref, yb_ref, cpk_ref, cg_ref, dpk_ref, dla_ref = outs
    extb_ref, extc_ref, convb_ref, convc_ref, wr_ref = scratch
    _run_casts(cast_in, cast_out)

    @pl.when(pl.program_id(0) == 0)
    def _():
        extb_ref[...] = jnp.zeros(extb_ref.shape, F32)
        extc_ref[...] = jnp.zeros(extc_ref.shape, F32)
        _repack_w_in_rest(w_ref, wr_ref)

    _run_stages([_inproj_stages(x_ref, vec, w_ref, wr_ref, ws_ref, bs_ref, scw_ref, dcw_ref,
                                ea_ref, eb_ref, bd_ref, wg_ref,
                                ya_ref, yb_ref, cpk_ref, cg_ref, dpk_ref, dla_ref, extb_ref, extc_ref,
                                convb_ref, convc_ref)])


def _inproj(x, n1, lnw, lnb, gb, alog, dtb, w_in_b, layer, ws, bs_wide, scw, dcw, ea, eb, bd, wg_pad, cast_jobs=()):
    nb, seq, _ = x.shape
    tm = SEQ_TILE
    n_tiles = seq // tm
    assert n_tiles >= CAST_BLOCKS or not cast_jobs
    cast_in_specs, cast_out_specs, cast_out_shapes = _cast_specs(cast_jobs)
    tile = lambda i: (0, i, 0)
    whole = lambda a: pl.BlockSpec(a.shape, (lambda i: (0,) * a.ndim))
    per_layer = lambda a: _layer_param(a, layer)
    outs = ((GROUP, BF16), (GROUP, BF16), (C_PACK, BF16), (GROUP, F32), (D_PACK, BF16), (GROUP, F32))
    return pl.pallas_call(
        functools.partial(_inproj_kernel, layer, len(cast_jobs)),
        grid=(n_tiles,),
        in_specs=[pl.BlockSpec((nb, tm, D_MODEL), tile),
                  whole(n1), whole(lnw), whole(lnb), whole(gb),
                  pl.BlockSpec(memory_space=pltpu.SMEM), pl.BlockSpec(memory_space=pltpu.SMEM),
                  _layer_block(w_in_b, layer),
                  per_layer(ws), per_layer(bs_wide), per_layer(scw),
                  per_layer(dcw), whole(ea), whole(eb), whole(bd),
                  per_layer(wg_pad)]
                 + cast_in_specs,
        out_specs=[pl.BlockSpec((nb, tm, n), tile) for n, _ in outs] + cast_out_specs,
        out_shape=[jax.ShapeDtypeStruct((nb, seq, n), dt) for n, dt in outs] + cast_out_shapes,
        scratch_shapes=[pltpu.VMEM((nb, GROUP // LANES, TAIL + tm, LANES), F32),
                        pltpu.VMEM((nb, 3 * GROUP // LANES, TAIL + tm, LANES), F32),
                        pltpu.VMEM((GROUP // LANES, tm, LANES), F32),
                        pltpu.VMEM((3 * GROUP // LANES, tm, LANES), F32),
                        pltpu.VMEM((D_MODEL, REST_COLS), BF16)],
        compiler_params=pltpu.CompilerParams(dimension_semantics=("arbitrary",),
                                             vmem_limit_bytes=VMEM_LIMIT),
        name="in_proj",
    )(x, n1, lnw, lnb, gb, alog, dtb, w_in_b, ws, bs_wide, scw, dcw, ea, eb, bd, wg_pad,
      *[w for w, _, _ in cast_jobs])


def _chunk_masks():
    i = lax.broadcasted_iota(jnp.int32, (CHUNK, GROUP), 0)
    j = lax.broadcasted_iota(jnp.int32, (CHUNK, GROUP), 1) % CHUNK
    return i == j, i >= j, i > j


def _const_inputs():
    lane_head = np.arange(GROUP) // HEAD_DIM
    bd = (lane_head[:, None] == lane_head[None, :]).astype(np.float32)
    ltri = np.tril(np.ones((CHUNK, CHUNK), np.float32))
    return jnp.asarray(bd, BF16), jnp.asarray(ltri, BF16)


def _chunk_rows(c):
    return slice(c * CHUNK, (c + 1) * CHUNK)


def _gdn_stages(cpk_ref, cg_ref, nw_ref, bd_ref, ltri_ref, y_ref, ac_ref, bb_ref, d_ref, dec_ref, o_ref, s_ref):
    nb, tm = cpk_ref.shape[0], cpk_ref.shape[1]
    cps = tm // CHUNK
    eye_w, causal_w, strict_w = _chunk_masks()
    eye_f = eye_w.astype(F32)
    ltri = ltri_ref[...]
    bd = bd_ref[...]

    yield from _gdn_prepare([(b, c) for b in range(nb) for c in range(cps)], cps, cpk_ref, cg_ref,
                            ac_ref, bb_ref, d_ref, dec_ref, eye_w, causal_w, strict_w, eye_f, ltri, bd)

    for c in range(cps):
        for b in range(nb):
            i = b * cps + c
            s = s_ref[b]
            r = _dot(ac_ref[i], _blockdiag(s, bd))
            s_ref[b] = s * dec_ref[i, 0:1, :] + r[0:CHUNK] + bb_ref[i]
            o_ref[b, _chunk_rows(c), :] = r[CHUNK:2 * CHUNK] + d_ref[i]
        yield

    for b in range(nb):
        o = o_ref[b]
        zg = cpk_ref[b, :, 4 * GROUP:5 * GROUP].astype(F32)
        y_ref[b] = (o * lax.rsqrt(_head_mean_sq(o, bd) + EPS) * nw_ref[...] * zg).astype(y_ref.dtype)


def _gdn_prepare(items, cps, cpk_ref, cg_ref, ac_ref, bb_ref, d_ref, dec_ref,
                 eye_w, causal_w, strict_w, eye_f, ltri, bd):
    n = len(items)

    def load(col, i):
        b, c = items[i]
        return cpk_ref[b, _chunk_rows(c), col * GROUP:(col + 1) * GROUP]

    qn = [load(0, i) for i in range(n)]
    kn = [load(1, i) for i in range(n)]
    beta = [load(3, i) for i in range(n)]
    gc = [_dot_sel_lhs(ltri, cg_ref[b, _chunk_rows(c), :]) for b, c in items]
    yield
    eg = [jnp.exp(x) for x in gc]
    kb = [kn[i] * beta[i] for i in range(n)]
    eye_b = eye_f.astype(BF16)
    r1 = [_dot_nt(jnp.concatenate([kb[i], qn[i], eye_b], axis=0), _blockdiag(kn[i], bd))
          for i in range(n)]
    yield
    low, attn, g_row = [], [], []
    for i in range(n):
        g_row.append(jnp.sum(jnp.where(eye_w, gc[i], 0.0), axis=0, keepdims=True))
        decay = jnp.where(causal_w, jnp.exp(jnp.minimum(gc[i] - g_row[i], 0.0)), 0.0)
        low.append(jnp.where(strict_w, r1[i][0:CHUNK] * decay, 0.0))
        attn.append((r1[i][CHUNK:2 * CHUNK] * decay).astype(BF16))
    t = [eye_f - x for x in low]
    p = [_dot(x.astype(BF16), _blockdiag(x, bd)) for x in low]
    yield
    n_levels = CHUNK.bit_length() - 1
    for level in range(1, n_levels):
        if level < n_levels - 1:
            r2 = [_dot(jnp.concatenate([p[i], t[i]], axis=0).astype(BF16), _blockdiag(p[i], bd))
                  for i in range(n)]
            p = [x[0:CHUNK] for x in r2]
            t = [t[i] + r2[i][CHUNK:2 * CHUNK] for i in range(n)]
        else:
            t = [t[i] + _dot(t[i].astype(BF16), _blockdiag(p[i], bd)) for i in range(n)]
        yield
    gc_last = [x[CHUNK - 1:CHUNK, :] for x in gc]
    kd_t = [(r1[i][2 * CHUNK:3 * CHUNK] * jnp.exp(gc_last[i] - g_row[i])).astype(BF16) for i in range(n)]
    lhs_t = [_dot(jnp.concatenate([attn[i], kd_t[i]], axis=0), _blockdiag(t[i], bd)) for i in range(n)]
    yield
    prod = [_dot(lhs_t[i].astype(BF16),
                 jnp.concatenate([_blockdiag(kb[i] * eg[i], bd), _blockdiag(load(2, i) * beta[i], bd)], axis=1))
            for i in range(n)]
    for i, (b, c) in enumerate(items):
        j = b * cps + c
        ac_ref[j, 0:CHUNK, :] = (-prod[i][CHUNK:2 * CHUNK, 0:GROUP]).astype(BF16)
        bb_ref[j] = prod[i][CHUNK:2 * CHUNK, GROUP:2 * GROUP]
        ac_ref[j, CHUNK:2 * CHUNK, :] = (qn[i] * eg[i] - prod[i][0:CHUNK, 0:GROUP]).astype(BF16)
        d_ref[j] = prod[i][0:CHUNK, GROUP:2 * GROUP]
        dec_ref[j] = jnp.broadcast_to(jnp.exp(gc_last[i]), (TAIL, GROUP))
    yield


def _gla_stages(dpk_ref, dla_ref, nw_ref, bd_ref, ltri_ref, y_ref, o_ref, st_ref):
    nb, tm = dpk_ref.shape[0], dpk_ref.shape[1]
    cps = tm // CHUNK
    _, causal_w, _ = _chunk_masks()
    ltri = ltri_ref[...]
    bd = bd_ref[...]
    bdf = bd.astype(F32)
    mid = CHUNK // 2

    items = [(b, c) for c in range(cps) for b in range(nb)]
    n = len(items)
    q = [dpk_ref[b, _chunk_rows(c), 0:GROUP].astype(F32) * (HEAD_DIM ** -0.5) for b, c in items]
    k = [dpk_ref[b, _chunk_rows(c), GROUP:2 * GROUP].astype(F32) for b, c in items]
    v = [dpk_ref[b, _chunk_rows(c), 2 * GROUP:3 * GROUP] for b, c in items]
    gcum = [_dot_sel_lhs(ltri, dla_ref[b, _chunk_rows(c), :]) for b, c in items]
    yield
    g_mid = [x[mid:mid + 1, :] for x in gcum]
    g_last = [x[CHUNK - 1:CHUNK, :] for x in gcum]
    attn = [jnp.where(causal_w,
                      _dot_nt((q[i] * jnp.exp(gcum[i] - g_mid[i])).astype(BF16),
                              _blockdiag(k[i] * jnp.exp(g_mid[i] - gcum[i]), bd)), 0.0).astype(BF16)
            for i in range(n)]
    yield
    upd = [bdf * _dot_tn((k[i] * jnp.exp(g_last[i] - gcum[i])).astype(BF16), v[i]) for i in range(n)]
    yield
    o_intra = [_dot(attn[i], _blockdiag(v[i], bd)) for i in range(n)]
    qg = [(q[i] * jnp.exp(gcum[i])).astype(BF16) for i in range(n)]
    yield
    for i, (b, c) in enumerate(items):
        st = st_ref[b]
        o_ref[b, _chunk_rows(c), :] = o_intra[i] + _dot(qg[i], st.astype(BF16))
        dec_col = jnp.transpose(jnp.broadcast_to(jnp.exp(g_last[i]), (GROUP, GROUP)))
        st_ref[b] = st * dec_col + upd[i]
        if b == nb - 1:
            yield

    for b in range(nb):
        o = o_ref[b]
        zg = dpk_ref[b, :, 3 * GROUP:4 * GROUP].astype(F32)
        y_ref[b] = (o * lax.rsqrt(_head_mean_sq(o, bd) + EPS) * nw_ref[...] * zg).astype(y_ref.dtype)


def _recur_kernel(cpk_ref, cg_ref, dpk_ref, dla_ref, nw_ref, bd_ref, ltri_ref, yc_ref, yd_ref,
                  ac_ref, bb_ref, d_ref, dec_ref, oc_ref, s_ref, od_ref, st_ref):
    nwc_ref = nw_ref.at[pl.ds(0, 1), pl.ds(0, GROUP)]
    nwd_ref = nw_ref.at[pl.ds(1, 1), pl.ds(0, GROUP)]

    @pl.when(pl.program_id(0) == 0)
    def _():
        s_ref[...] = jnp.zeros(s_ref.shape, F32)
        st_ref[...] = jnp.zeros(st_ref.shape, F32)

    _run_stages([_gla_stages(dpk_ref, dla_ref, nwd_ref, bd_ref, ltri_ref, yd_ref, od_ref, st_ref),
                 _gdn_stages(cpk_ref, cg_ref, nwc_ref, bd_ref, ltri_ref, yc_ref,
                             ac_ref, bb_ref, d_ref, dec_ref, oc_ref, s_ref)])


def _recur(cpk, cg, dpk, dla, nw, bd, ltri, layer):
    nb, seq, _ = cpk.shape
    tm = SEQ_TILE
    n_chunks = nb * tm // CHUNK
    tile = lambda i: (0, i, 0)
    whole = lambda a: pl.BlockSpec(a.shape, (lambda i: (0,) * a.ndim))
    out = jax.ShapeDtypeStruct((nb, seq, GROUP), BF16)
    return pl.pallas_call(
        _recur_kernel,
        grid=(seq // tm,),
        in_specs=[pl.BlockSpec((nb, tm, C_PACK), tile),
                  pl.BlockSpec((nb, tm, GROUP), tile),
                  pl.BlockSpec((nb, tm, D_PACK), tile),
                  pl.BlockSpec((nb, tm, GROUP), tile),
                  _layer_param(nw, layer), whole(bd), whole(ltri)],
        out_specs=[pl.BlockSpec((nb, tm, GROUP), tile)] * 2,
        out_shape=[out, out],
        scratch_shapes=[pltpu.VMEM((n_chunks, 2 * CHUNK, GROUP), BF16),
                        pltpu.VMEM((n_chunks, CHUNK, GROUP), F32),
                        pltpu.VMEM((n_chunks, CHUNK, GROUP), F32),
                        pltpu.VMEM((n_chunks, TAIL, GROUP), F32),
                        pltpu.VMEM((nb, tm, GROUP), F32),
                        pltpu.VMEM((nb, CHUNK, GROUP), F32),
                        pltpu.VMEM((nb, tm, GROUP), F32),
                        pltpu.VMEM((nb, GROUP, GROUP), F32)],
        compiler_params=pltpu.CompilerParams(dimension_semantics=("arbitrary",),
                                             vmem_limit_bytes=VMEM_LIMIT),
        name="recur",
    )(cpk, cg, dpk, dla, nw, bd, ltri)


def _outproj_ffn_kernel(final, layer, x_ref, ya_ref, yb_ref, yc_ref, yd_ref, wo_ref, n2_ref, post_ref, wgu_ref,
                        wd_ref, *rest):
    post_w = post_ref[0:1, :] if final else post_ref[layer + 1:layer + 2, :]
    if final:
        o_ref, act_ref = rest
    else:
        o_ref, h_ref, act_ref = rest
    x1 = x_ref[...]
    for idx, y_ref in enumerate((ya_ref, yb_ref, yc_ref, yd_ref)):
        x1 = x1 + _dot(y_ref[...], wo_ref[idx * GROUP:(idx + 1) * GROUP, :])
    ms = jnp.mean(x1 * x1, axis=-1, keepdims=True)
    h = (x1 * lax.rsqrt(ms + EPS) * n2_ref[layer:layer + 1, :]).astype(BF16)
    for j in range(D_FF // FF_CHUNK):
        cols = slice(j * FF_CHUNK, (j + 1) * FF_CHUNK)
        gate = _dot(h, wgu_ref[:, cols])
        up = _dot(h, wgu_ref[:, D_FF + j * FF_CHUNK:D_FF + (j + 1) * FF_CHUNK])
        act_ref[:, cols] = (_silu(gate) * up).astype(BF16)
    x2 = x1 + _dot(act_ref[...], wd_ref[...])
    if final:
        ms2 = jnp.mean(x2 * x2, axis=-1, keepdims=True)
        x2 = x2 * lax.rsqrt(ms2 + EPS) * post_w
    else:
        h_ref[...] = _rmsnorm_bf16(x2, post_w)
    o_ref[...] = x2


def _outproj_ffn(x, ys, wo, n2, post, wgu, wd, layer, final):
    m = x.shape[0]
    n_steps = m // ROW_TILE
    row = lambda i: (i, 0)
    whole = lambda a: pl.BlockSpec(a.shape, (lambda i: (0,) * a.ndim))
    out_specs = [pl.BlockSpec((ROW_TILE, D_MODEL), row)]
    out_shape = [jax.ShapeDtypeStruct((m, D_MODEL), F32)]
    if not final:
        out_specs.append(pl.BlockSpec((ROW_TILE, D_MODEL), row))
        out_shape.append(jax.ShapeDtypeStruct((m, D_MODEL), BF16))
    return pl.pallas_call(
        functools.partial(_outproj_ffn_kernel, final, layer),
        grid=(n_steps,),
        in_specs=[pl.BlockSpec((ROW_TILE, D_MODEL), row)]
                 + [pl.BlockSpec((ROW_TILE, GROUP), row)] * 4
                 + [_resident(wo), whole(n2), whole(post), _resident(wgu), _resident(wd)],
        out_specs=out_specs,
        out_shape=out_shape,
        scratch_shapes=[pltpu.VMEM((ROW_TILE, D_FF), BF16)],
        compiler_params=pltpu.CompilerParams(dimension_semantics=("arbitrary",),
                                             vmem_limit_bytes=VMEM_LIMIT),
        name="outproj_ffn",
    )(x, *ys, wo, n2, post, wgu, wd)


def _gate_selectors():
    ea = np.zeros((GATE_COLS, GROUP), np.float32)
    eb = np.zeros((GATE_COLS, GROUP), np.float32)
    for h in range(HEADS):
        ea[h, h * HEAD_DIM:(h + 1) * HEAD_DIM] = 1.0
        eb[HEADS + h, h * HEAD_DIM:(h + 1) * HEAD_DIM] = 1.0
    return jnp.asarray(ea, BF16), jnp.asarray(eb, BF16)


def kernel(x, norm1_w, w_in, sgu_ln_w, sgu_ln_b, sgu_w_spatial, sgu_b_spatial, sc_conv_w, dn_conv_w, dn_a_log, dn_dt_bias, dn_norm_w, gla_w_gate2, gla_gate_bias, gla_norm_w, w_out, norm2_w, w_gate_up, w_down, final_norm_w):
    bsz, seq, d = x.shape
    depth = w_in.shape[0]
    assert seq % ROW_TILE == 0 and seq % SEQ_TILE == 0 and d == D_MODEL
    m = bsz * seq
    xf = x.reshape(m, d)
    bd, ltri = _const_inputs()
    ea, eb = _gate_selectors()
    w_in_b = w_in.astype(BF16)
    ffn_jobs = lambda layer: [(w_out, layer, D_MODEL), (w_gate_up, layer, 2 * D_FF), (w_down, layer, D_MODEL)]
    bs_wide = jnp.repeat(jnp.swapaxes(sgu_b_spatial, 1, 2), HEAD_DIM, axis=2)
    wg_pad = jnp.pad(gla_w_gate2.astype(BF16),
                     ((0, 0), (2 * HEADS, GATE_COLS - 2 * HEADS - GATE_RANK), (0, 0)))
    inproj_params = (sgu_w_spatial, bs_wide, sc_conv_w, dn_conv_w, ea, eb, bd, wg_pad)
    inproj_vecs = (norm1_w, sgu_ln_w, sgu_ln_b, gla_gate_bias, dn_a_log, dn_dt_bias)
    nw = jnp.tile(jnp.stack([dn_norm_w, gla_norm_w], axis=1).astype(F32), (1, 1, HEADS))
    x_in = xf.reshape(bsz, seq, d)
    for l in range(depth):
        ya, yb, cpk, cg, dpk, dla, wo_b, wgu_b, wd_b = _inproj(x_in, *inproj_vecs, w_in_b, l, *inproj_params,
                                                               cast_jobs=ffn_jobs(l))
        yc, yd = _recur(cpk, cg, dpk, dla, nw, bd, ltri, l)
        last = l == depth - 1
        outs = _outproj_ffn(xf, tuple(y.reshape(m, GROUP) for y in (ya, yb, yc, yd)),
                            wo_b, norm2_w, final_norm_w[None, :] if last else norm1_w, wgu_b, wd_b,
                            layer=l, final=last)
        xf = outs[0]
        if not last:
            x_in = outs[1].reshape(bsz, seq, d)
    return xf.reshape(bsz, seq, d)
```

```python
import functools

import numpy as np
import jax
import jax.numpy as jnp
from jax import lax
from jax.experimental import pallas as pl
from jax.experimental.pallas import tpu as pltpu

F32 = jnp.float32
BF16 = jnp.bfloat16

D_MODEL = 1024
GROUP = 256
HEADS = 4
HEAD_DIM = 64
SGU_CHUNK = 128
SC_WIDTH = 3
DN_CONV_WIDTH = 4
CHUNK = 64
GATE_RANK = 16
GATE_TEMP = 16.0
D_FF = 2816
EPS = 1e-6
GATE_COLS = 128
TAIL = 8

ROW_TILE = 1024
SEQ_TILE = 512
FF_CHUNK = 256
VMEM_LIMIT = 60 * 1024 * 1024

MAIN_A = 0
MAIN_B = MAIN_A + 2 * GROUP
MAIN_C = MAIN_B + 3 * GROUP
MAIN_COLS = MAIN_C + 3 * GROUP
REST_CZ = 0
REST_D = REST_CZ + GROUP
REST_G = REST_D + 4 * GROUP
REST_COLS = REST_G + GATE_COLS
C_PACK = 5 * GROUP
D_PACK = 4 * GROUP


def _dot(a, b):
    return jnp.dot(a, b, preferred_element_type=F32)


def _dot_nt(a, b):
    return lax.dot_general(a, b, (((1,), (1,)), ((), ())), preferred_element_type=F32)


def _dot_tn(a, b):
    return lax.dot_general(a, b, (((0,), (0,)), ((), ())), preferred_element_type=F32)


def _split2(x):
    hi = x.astype(BF16)
    lo = (x - hi.astype(F32)).astype(BF16)
    return hi, lo


def _dot_sel_rhs(x, sel):
    hi, lo = _split2(x)
    return _dot(hi, sel) + _dot(lo, sel)


def _dot_sel_lhs(sel, x):
    hi, lo = _split2(x)
    return _dot(sel, hi) + _dot(sel, lo)


def _softplus(x):
    return jnp.maximum(x, 0.0) + jnp.log(1.0 + jnp.exp(-jnp.abs(x)))


def _sigmoid(x):
    return 0.5 + 0.5 * jnp.tanh(0.5 * x)


def _silu(x):
    h = 0.5 * x
    return h + h * jnp.tanh(h)


def _gelu_tanh(x):
    h = 0.5 * x
    return h + h * jnp.tanh(x * (0.7978845608028654 + (0.7978845608028654 * 0.044715) * (x * x)))


def _rmsnorm_bf16(x, w):
    ms = jnp.mean(x * x, axis=-1, keepdims=True)
    return (x * lax.rsqrt(ms + EPS) * w).astype(BF16)


def _blockdiag(x, mask):
    xb = x.astype(BF16)
    return jnp.concatenate([xb] * HEADS, axis=0) * mask


def _head_mean_sq(x, bones):
    return _dot((x * x).astype(BF16), bones) * (1.0 / HEAD_DIM)


LANES = 128


def _causal_conv(x, ext_ref, out_ref, cw_ref, width):
    tm, ch = x.shape
    half = tm // 2
    n_slabs = ch // LANES
    for s in range(n_slabs):
        ext_ref[s, TAIL:TAIL + tm, :] = x[:, s * LANES:(s + 1) * LANES]
    for s in range(n_slabs):
        lanes = slice(s * LANES, (s + 1) * LANES)
        for parity in (0, 1):
            acc = None
            for j in range(width):
                tap = (ext_ref[s, pl.ds(TAIL + parity - j, half, stride=2), :]
                       * cw_ref[width - 1 - j:width - j, lanes])
                acc = tap if acc is None else acc + tap
            out_ref[s, pl.ds(parity, half, stride=2), :] = acc
    for s in range(n_slabs):
        ext_ref[s, 0:TAIL, :] = x[tm - TAIL:tm, s * LANES:(s + 1) * LANES]
    return jnp.concatenate([out_ref[s] for s in range(n_slabs)], axis=1)


def _layer_param(stacked, layer):
    zeros = (0,) * (stacked.ndim - 1)
    return pl.BlockSpec((None,) + stacked.shape[1:], lambda i: (layer,) + zeros)


def _layer_block(stacked, layer):
    zeros = (0,) * (stacked.ndim - 1)
    return pl.BlockSpec((None,) + stacked.shape[1:], lambda i: (layer,) + zeros,
                        pipeline_mode=pl.Buffered(1))


def _resident(a):
    return pl.BlockSpec(a.shape, lambda i: (0, 0), pipeline_mode=pl.Buffered(1))


CAST_BLOCKS = 16
(V_NORM1, V_LNW, V_LNB, V_ALOG, V_DTB, V_GB) = range(6)


def _vec(table, key):
    return table[key]()


def _param_row(ref, layer):
    return lambda: ref[layer:layer + 1, :].astype(F32)


def _head_vec(ref, layer):
    def load():
        lane = lax.broadcasted_iota(jnp.int32, (1, GATE_COLS), 1)
        out = jnp.zeros((1, GATE_COLS), F32)
        for hd in range(HEADS):
            out = jnp.where(lane == hd, ref[layer, hd].astype(F32), out)
        return out
    return load


def _cast_specs(cast_jobs):
    last = CAST_BLOCKS - 1
    in_specs, out_specs, out_shapes = [], [], []
    for stacked, layer, n_cols in cast_jobs:
        rows = stacked.shape[1]
        assert rows % (16 * CAST_BLOCKS) == 0 and n_cols % 128 == 0
        blk = rows // CAST_BLOCKS
        in_specs.append(pl.BlockSpec((None, blk, n_cols),
                                     lambda i, layer=layer: (layer, jnp.minimum(i, last), 0)))
        out_specs.append(pl.BlockSpec((blk, n_cols), lambda i: (jnp.minimum(i, last), 0)))
        out_shapes.append(jax.ShapeDtypeStruct((rows, n_cols), BF16))
    return in_specs, out_specs, out_shapes


def _run_casts(cast_in, cast_out):
    for src, dst in zip(cast_in, cast_out):
        dst[...] = src[...].astype(dst.dtype)


def _split_refs(refs, *counts):
    assert sum(counts) == len(refs)
    groups, pos = [], 0
    for c in counts:
        groups.append(tuple(refs[pos:pos + c]))
        pos += c
    return groups


def _run_stages(stages):
    stages = list(stages)
    while stages:
        for gen in list(stages):
            try:
                next(gen)
            except StopIteration:
                stages.remove(gen)


def _inproj_stages(x_ref, vec, w_ref, wr_ref, ws_ref, bs_ref, scw_ref,
                   dcw_ref, ea_ref, eb_ref, bd_ref, wg_ref,
                   ya_ref, yb_ref, cpk_ref, cg_ref, dpk_ref, dla_ref, extb_ref, extc_ref, convb_ref, convc_ref):
    nb, tm = x_ref.shape[0], x_ref.shape[1]
    m = nb * tm
    batch_rows = [slice(b * tm, (b + 1) * tm) for b in range(nb)]

    x = x_ref[...].reshape(m, D_MODEL)
    if x.dtype == BF16:
        h = x
    else:
        h = _rmsnorm_bf16(x, _vec(vec, V_NORM1))
    bd = bd_ref[...]

    def project(ref, col0, n_slabs):
        parts = []
        for c in range(col0, col0 + n_slabs * GROUP, GROUP):
            parts.append(_dot(h, ref[:, c:c + GROUP]))
            yield
        return jnp.concatenate(parts, axis=1)

    pc = yield from project(w_ref, MAIN_C, 3)
    pcz = yield from project(wr_ref, REST_CZ, 1)
    pg = _dot(h, wr_ref[:, REST_G:REST_G + GATE_COLS])
    pa = yield from project(w_ref, MAIN_A, 2)

    qkv = jnp.concatenate(
        [_silu(_causal_conv(pc[rows, 0:3 * GROUP], extc_ref.at[b], convc_ref, dcw_ref, DN_CONV_WIDTH))
         for b, rows in enumerate(batch_rows)], axis=0)
    q = qkv[:, 0:GROUP]
    k = qkv[:, GROUP:2 * GROUP]
    pb = yield from project(w_ref, MAIN_B, 3)
    qn = q * lax.rsqrt(_dot((q * q).astype(BF16), bd) + EPS) * (HEAD_DIM ** -0.5)
    kn = k * lax.rsqrt(_dot((k * k).astype(BF16), bd) + EPS)
    yield
    g_narrow = -jnp.exp(_vec(vec, V_ALOG)) * _softplus(pg + _vec(vec, V_DTB))
    g_wide = _dot_sel_rhs(g_narrow, ea_ref[...])
    beta = _dot(_sigmoid(pg).astype(BF16), eb_ref[...])
    zg_c = _silu(pcz)
    pre = _dot(pg.astype(BF16), _gate_weight(wg_ref)) + _vec(vec, V_GB)
    log_a = -_softplus(-pre) * (1.0 / GATE_TEMP)
    yield

    tri = (lax.broadcasted_iota(jnp.int32, (SGU_CHUNK, SGU_CHUNK), 0)
           >= lax.broadcasted_iota(jnp.int32, (SGU_CHUNK, SGU_CHUNK), 1))
    head_of_lane = lax.broadcasted_iota(jnp.int32, (SGU_CHUNK, GROUP), 1) // HEAD_DIM
    ws = jnp.concatenate([jnp.where(tri, ws_ref[hd], 0.0).astype(BF16) for hd in range(HEADS)],
                         axis=1)
    lnw, lnb = _vec(vec, V_LNW), _vec(vec, V_LNB)
    for b in range(nb):
        for c in range(tm // SGU_CHUNK):
            rows = slice(b * tm + c * SGU_CHUNK, b * tm + (c + 1) * SGU_CHUNK)
            u = _gelu_tanh(pa[rows, 0:GROUP])
            v = _gelu_tanh(pa[rows, GROUP:2 * GROUP])
            mu = jnp.mean(v, axis=-1, keepdims=True)
            vc = v - mu
            var = jnp.mean(vc * vc, axis=-1, keepdims=True)
            vn = vc * lax.rsqrt(var + EPS) * lnw + lnb
            vn_heads = jnp.concatenate([jnp.where(head_of_lane == hd, vn, 0.0).astype(BF16)
                                        for hd in range(HEADS)], axis=0)
            mixed = bs_ref[...] + _dot(ws, vn_heads)
            ya_ref[b, c * SGU_CHUNK:(c + 1) * SGU_CHUNK, :] = (u * mixed).astype(ya_ref.dtype)
            yield

    pd = yield from project(wr_ref, REST_D, 4)

    for b, rows in enumerate(batch_rows):
        conv_b = _causal_conv(pb[rows, GROUP:2 * GROUP] * pb[rows, 2 * GROUP:3 * GROUP],
                              extb_ref.at[b], convb_ref, scw_ref, SC_WIDTH)
        yb_ref[b] = (pb[rows, 0:GROUP] * conv_b).astype(yb_ref.dtype)

    for b, rows in enumerate(batch_rows):
        cg_ref[b] = g_wide[rows]
        cpk_ref[b, :, 0:GROUP] = qn[rows].astype(BF16)
        cpk_ref[b, :, GROUP:2 * GROUP] = kn[rows].astype(BF16)
        cpk_ref[b, :, 2 * GROUP:3 * GROUP] = qkv[rows, 2 * GROUP:3 * GROUP].astype(BF16)
        cpk_ref[b, :, 3 * GROUP:4 * GROUP] = beta[rows].astype(BF16)
        cpk_ref[b, :, 4 * GROUP:5 * GROUP] = zg_c[rows].astype(BF16)
        dla_ref[b] = log_a[rows]
        dpk_ref[b, :, 0:3 * GROUP] = pd[rows, 0:3 * GROUP].astype(BF16)
        dpk_ref[b, :, 3 * GROUP:4 * GROUP] = _silu(pd[rows, 3 * GROUP:4 * GROUP]).astype(BF16)


def _repack_w_in_rest(w_ref, wr_ref):
    c_a = MAIN_COLS
    c_z = c_a + 2 * HEADS
    d0 = c_z + GROUP
    d_g = d0 + 3 * GROUP
    d_z = d_g + GATE_RANK
    wr_ref[:, REST_CZ:REST_CZ + GROUP] = w_ref[:, c_z:d0]
    wr_ref[:, REST_D:REST_D + 3 * GROUP] = w_ref[:, d0:d_g]
    wr_ref[:, REST_D + 3 * GROUP:REST_D + 4 * GROUP] = w_ref[:, d_z:d_z + GROUP]
    wr_ref[:, REST_G:REST_G + GATE_COLS] = jnp.zeros((wr_ref.shape[0], GATE_COLS), wr_ref.dtype)
    wr_ref[:, REST_G:REST_G + 2 * HEADS] = w_ref[:, c_a:c_z]
    wr_ref[:, REST_G + 2 * HEADS:REST_G + 2 * HEADS + GATE_RANK] = w_ref[:, d_g:d_z]


def _gate_weight(wg_ref):
    top = jnp.zeros((2 * HEADS, GROUP), F32)
    bottom = jnp.zeros((GATE_COLS - 2 * HEADS - GATE_RANK, GROUP), F32)
    return jnp.concatenate([top, wg_ref[...].astype(F32), bottom], axis=0).astype(BF16)


def _spread_spatial_bias(bs_ref, bsw_ref):
    eye = (lax.broadcasted_iota(jnp.int32, (SGU_CHUNK, SGU_CHUNK), 0)
           == lax.broadcasted_iota(jnp.int32, (SGU_CHUNK, SGU_CHUNK), 1))
    head_of_lane = lax.broadcasted_iota(jnp.int32, (SGU_CHUNK, GROUP), 1) // HEAD_DIM
    out = jnp.zeros((SGU_CHUNK, GROUP), F32)
    for hd in range(HEADS):
        col = jnp.sum(jnp.where(eye, bs_ref[hd:hd + 1, :].astype(F32), 0.0), axis=1, keepdims=True)
        out = jnp.where(head_of_lane == hd, col, out)
    bsw_ref[...] = out


def _inproj_kernel(layer, n_cast, x_ref, n1_ref, lnw_ref, lnb_ref, gb_ref, alog_ref, dtb_ref, w_ref, ws_ref, bs_ref,
                   scw_ref, dcw_ref, ea_ref, eb_ref, bd_ref, wg_ref, *rest):
    cast_in, outs, cast_out, scratch = _split_refs(rest, n_cast, 6, n_cast, 6)
    vec = {V_NORM1: _param_row(n1_ref, layer), V_LNW: _param_row(lnw_ref, layer), V_LNB: _param_row(lnb_ref, layer),
           V_GB: _param_row(gb_ref, layer), V_ALOG: _head_vec(alog_ref, layer), V_DTB: _head_vec(dtb_ref, layer)}
    ya_ref, yb_ref, cpk_ref, cg_ref, dpk_ref, dla_ref = outs
    extb_ref, extc_ref, convb_ref, convc_ref, wr_ref, bsw_ref = scratch
    _run_casts(cast_in, cast_out)

    @pl.when(pl.program_id(0) == 0)
    def _():
        extb_ref[...] = jnp.zeros(extb_ref.shape, F32)
        extc_ref[...] = jnp.zeros(extc_ref.shape, F32)
        _repack_w_in_rest(w_ref, wr_ref)
        _spread_spatial_bias(bs_ref, bsw_ref)

    _run_stages([_inproj_stages(x_ref, vec, w_ref, wr_ref, ws_ref, bsw_ref, scw_ref, dcw_ref,
                                ea_ref, eb_ref, bd_ref, wg_ref,
                                ya_ref, yb_ref, cpk_ref, cg_ref, dpk_ref, dla_ref, extb_ref, extc_ref,
                                convb_ref, convc_ref)])


def _inproj(x, n1, lnw, lnb, gb, alog, dtb, w_in_b, layer, ws, bs, scw, dcw, ea, eb, bd, wg, cast_jobs=()):
    nb, seq, _ = x.shape
    tm = SEQ_TILE
    n_tiles = seq // tm
    assert n_tiles >= CAST_BLOCKS or not cast_jobs
    cast_in_specs, cast_out_specs, cast_out_shapes = _cast_specs(cast_jobs)
    tile = lambda i: (0, i, 0)
    whole = lambda a: pl.BlockSpec(a.shape, (lambda i: (0,) * a.ndim))
    per_layer = lambda a: _layer_param(a, layer)
    outs = ((GROUP, BF16), (GROUP, BF16), (C_PACK, BF16), (GROUP, F32), (D_PACK, BF16), (GROUP, F32))
    return pl.pallas_call(
        functools.partial(_inproj_kernel, layer, len(cast_jobs)),
        grid=(n_tiles,),
        in_specs=[pl.BlockSpec((nb, tm, D_MODEL), tile),
                  whole(n1), whole(lnw), whole(lnb), whole(gb),
                  pl.BlockSpec(memory_space=pltpu.SMEM), pl.BlockSpec(memory_space=pltpu.SMEM),
                  _layer_block(w_in_b, layer),
                  per_layer(ws), per_layer(bs), per_layer(scw),
                  per_layer(dcw), whole(ea), whole(eb), whole(bd),
                  per_layer(wg)]
                 + cast_in_specs,
        out_specs=[pl.BlockSpec((nb, tm, n), tile) for n, _ in outs] + cast_out_specs,
        out_shape=[jax.ShapeDtypeStruct((nb, seq, n), dt) for n, dt in outs] + cast_out_shapes,
        scratch_shapes=[pltpu.VMEM((nb, GROUP // LANES, TAIL + tm, LANES), F32),
                        pltpu.VMEM((nb, 3 * GROUP // LANES, TAIL + tm, LANES), F32),
                        pltpu.VMEM((GROUP // LANES, tm, LANES), F32),
                        pltpu.VMEM((3 * GROUP // LANES, tm, LANES), F32),
                        pltpu.VMEM((D_MODEL, REST_COLS), BF16),
                        pltpu.VMEM((SGU_CHUNK, GROUP), F32)],
        compiler_params=pltpu.CompilerParams(dimension_semantics=("arbitrary",),
                                             vmem_limit_bytes=VMEM_LIMIT),
        name="in_proj",
    )(x, n1, lnw, lnb, gb, alog, dtb, w_in_b, ws, bs, scw, dcw, ea, eb, bd, wg,
      *[w for w, _, _ in cast_jobs])


def _chunk_masks():
    i = lax.broadcasted_iota(jnp.int32, (CHUNK, GROUP), 0)
    j = lax.broadcasted_iota(jnp.int32, (CHUNK, GROUP), 1) % CHUNK
    return i == j, i >= j, i > j


def _const_inputs():
    lane_head = np.arange(GROUP) // HEAD_DIM
    bd = (lane_head[:, None] == lane_head[None, :]).astype(np.float32)
    ltri = np.tril(np.ones((CHUNK, CHUNK), np.float32))
    return jnp.asarray(bd, BF16), jnp.asarray(ltri, BF16)


def _chunk_rows(c):
    return slice(c * CHUNK, (c + 1) * CHUNK)


def _gdn_stages(cpk_ref, cg_ref, nw_ref, bd_ref, ltri_ref, y_ref, ac_ref, bb_ref, d_ref, dec_ref, o_ref, s_ref):
    nb, tm = cpk_ref.shape[0], cpk_ref.shape[1]
    cps = tm // CHUNK
    eye_w, causal_w, strict_w = _chunk_masks()
    eye_f = eye_w.astype(F32)
    ltri = ltri_ref[...]
    bd = bd_ref[...]

    yield from _gdn_prepare([(b, c) for b in range(nb) for c in range(cps)], cps, cpk_ref, cg_ref,
                            ac_ref, bb_ref, d_ref, dec_ref, eye_w, causal_w, strict_w, eye_f, ltri, bd)

    for c in range(cps):
        for b in range(nb):
            i = b * cps + c
            s = s_ref[b]
            r = _dot(ac_ref[i], _blockdiag(s, bd))
            s_ref[b] = s * dec_ref[i, 0:1, :] + r[0:CHUNK] + bb_ref[i]
            o_ref[b, _chunk_rows(c), :] = r[CHUNK:2 * CHUNK] + d_ref[i]
        yield

    for b in range(nb):
        o = o_ref[b]
        zg = cpk_ref[b, :, 4 * GROUP:5 * GROUP].astype(F32)
        y_ref[b] = (o * lax.rsqrt(_head_mean_sq(o, bd) + EPS) * nw_ref[...] * zg).astype(y_ref.dtype)


def _gdn_prepare(items, cps, cpk_ref, cg_ref, ac_ref, bb_ref, d_ref, dec_ref,
                 eye_w, causal_w, strict_w, eye_f, ltri, bd):
    n = len(items)

    def load(col, i):
        b, c = items[i]
        return cpk_ref[b, _chunk_rows(c), col * GROUP:(col + 1) * GROUP]

    qn = [load(0, i) for i in range(n)]
    kn = [load(1, i) for i in range(n)]
    beta = [load(3, i) for i in range(n)]
    gc = [_dot_sel_lhs(ltri, cg_ref[b, _chunk_rows(c), :]) for b, c in items]
    yield
    eg = [jnp.exp(x) for x in gc]
    kb = [kn[i] * beta[i] for i in range(n)]
    eye_b = eye_f.astype(BF16)
    r1 = [_dot_nt(jnp.concatenate([kb[i], qn[i], eye_b], axis=0), _blockdiag(kn[i], bd))
          for i in range(n)]
    yield
    low, attn, g_row = [], [], []
    for i in range(n):
        g_row.append(jnp.sum(jnp.where(eye_w, gc[i], 0.0), axis=0, keepdims=True))
        decay = jnp.where(causal_w, jnp.exp(jnp.minimum(gc[i] - g_row[i], 0.0)), 0.0)
        low.append(jnp.where(strict_w, r1[i][0:CHUNK] * decay, 0.0))
        attn.append((r1[i][CHUNK:2 * CHUNK] * decay).astype(BF16))
    t = [eye_f - x for x in low]
    p = [_dot(x.astype(BF16), _blockdiag(x, bd)) for x in low]
    yield
    n_levels = CHUNK.bit_length() - 1
    for level in range(1, n_levels):
        if level < n_levels - 1:
            r2 = [_dot(jnp.concatenate([p[i], t[i]], axis=0).astype(BF16), _blockdiag(p[i], bd))
                  for i in range(n)]
            p = [x[0:CHUNK] for x in r2]
            t = [t[i] + r2[i][CHUNK:2 * CHUNK] for i in range(n)]
        else:
            t = [t[i] + _dot(t[i].astype(BF16), _blockdiag(p[i], bd)) for i in range(n)]
        yield
    gc_last = [x[CHUNK - 1:CHUNK, :] for x in gc]
    kd_t = [(r1[i][2 * CHUNK:3 * CHUNK] * jnp.exp(gc_last[i] - g_row[i])).astype(BF16) for i in range(n)]
    lhs_t = [_dot(jnp.concatenate([attn[i], kd_t[i]], axis=0), _blockdiag(t[i], bd)) for i in range(n)]
    yield
    prod = [_dot(lhs_t[i].astype(BF16),
                 jnp.concatenate([_blockdiag(kb[i] * eg[i], bd), _blockdiag(load(2, i) * beta[i], bd)], axis=1))
            for i in range(n)]
    for i, (b, c) in enumerate(items):
        j = b * cps + c
        ac_ref[j, 0:CHUNK, :] = (-prod[i][CHUNK:2 * CHUNK, 0:GROUP]).astype(BF16)
        bb_ref[j] = prod[i][CHUNK:2 * CHUNK, GROUP:2 * GROUP]
        ac_ref[j, CHUNK:2 * CHUNK, :] = (qn[i] * eg[i] - prod[i][0:CHUNK, 0:GROUP]).astype(BF16)
        d_ref[j] = prod[i][0:CHUNK, GROUP:2 * GROUP]
        dec_ref[j] = jnp.broadcast_to(jnp.exp(gc_last[i]), (TAIL, GROUP))
    yield


def _gla_stages(dpk_ref, dla_ref, nw_ref, bd_ref, ltri_ref, y_ref, o_ref, st_ref):
    nb, tm = dpk_ref.shape[0], dpk_ref.shape[1]
    cps = tm // CHUNK
    _, causal_w, _ = _chunk_masks()
    ltri = ltri_ref[...]
    bd = bd_ref[...]
    bdf = bd.astype(F32)
    mid = CHUNK // 2

    items = [(b, c) for c in range(cps) for b in range(nb)]
    n = len(items)
    q = [dpk_ref[b, _chunk_rows(c), 0:GROUP].astype(F32) * (HEAD_DIM ** -0.5) for b, c in items]
    k = [dpk_ref[b, _chunk_rows(c), GROUP:2 * GROUP].astype(F32) for b, c in items]
    v = [dpk_ref[b, _chunk_rows(c), 2 * GROUP:3 * GROUP] for b, c in items]
    gcum = [_dot_sel_lhs(ltri, dla_ref[b, _chunk_rows(c), :]) for b, c in items]
    yield
    g_mid = [x[mid:mid + 1, :] for x in gcum]
    g_last = [x[CHUNK - 1:CHUNK, :] for x in gcum]
    attn = [jnp.where(causal_w,
                      _dot_nt((q[i] * jnp.exp(gcum[i] - g_mid[i])).astype(BF16),
                              _blockdiag(k[i] * jnp.exp(g_mid[i] - gcum[i]), bd)), 0.0).astype(BF16)
            for i in range(n)]
    yield
    upd = [bdf * _dot_tn((k[i] * jnp.exp(g_last[i] - gcum[i])).astype(BF16), v[i]) for i in range(n)]
    yield
    o_intra = [_dot(attn[i], _blockdiag(v[i], bd)) for i in range(n)]
    qg = [(q[i] * jnp.exp(gcum[i])).astype(BF16) for i in range(n)]
    yield
    for i, (b, c) in enumerate(items):
        st = st_ref[b]
        o_ref[b, _chunk_rows(c), :] = o_intra[i] + _dot(qg[i], st.astype(BF16))
        dec_col = jnp.transpose(jnp.broadcast_to(jnp.exp(g_last[i]), (GROUP, GROUP)))
        st_ref[b] = st * dec_col + upd[i]
        if b == nb - 1:
            yield

    for b in range(nb):
        o = o_ref[b]
        zg = dpk_ref[b, :, 3 * GROUP:4 * GROUP].astype(F32)
        y_ref[b] = (o * lax.rsqrt(_head_mean_sq(o, bd) + EPS) * nw_ref[...] * zg).astype(y_ref.dtype)


def _recur_kernel(cpk_ref, cg_ref, dpk_ref, dla_ref, nw_ref, bd_ref, ltri_ref, yc_ref, yd_ref,
                  ac_ref, bb_ref, d_ref, dec_ref, oc_ref, s_ref, od_ref, st_ref):
    nwc_ref = nw_ref.at[pl.ds(0, 1), pl.ds(0, GROUP)]
    nwd_ref = nw_ref.at[pl.ds(1, 1), pl.ds(0, GROUP)]

    @pl.when(pl.program_id(0) == 0)
    def _():
        s_ref[...] = jnp.zeros(s_ref.shape, F32)
        st_ref[...] = jnp.zeros(st_ref.shape, F32)

    _run_stages([_gla_stages(dpk_ref, dla_ref, nwd_ref, bd_ref, ltri_ref, yd_ref, od_ref, st_ref),
                 _gdn_stages(cpk_ref, cg_ref, nwc_ref, bd_ref, ltri_ref, yc_ref,
                             ac_ref, bb_ref, d_ref, dec_ref, oc_ref, s_ref)])


def _recur(cpk, cg, dpk, dla, nw, bd, ltri, layer):
    nb, seq, _ = cpk.shape
    tm = SEQ_TILE
    n_chunks = nb * tm // CHUNK
    tile = lambda i: (0, i, 0)
    whole = lambda a: pl.BlockSpec(a.shape, (lambda i: (0,) * a.ndim))
    out = jax.ShapeDtypeStruct((nb, seq, GROUP), BF16)
    return pl.pallas_call(
        _recur_kernel,
        grid=(seq // tm,),
        in_specs=[pl.BlockSpec((nb, tm, C_PACK), tile),
                  pl.BlockSpec((nb, tm, GROUP), tile),
                  pl.BlockSpec((nb, tm, D_PACK), tile),
                  pl.BlockSpec((nb, tm, GROUP), tile),
                  _layer_param(nw, layer), whole(bd), whole(ltri)],
        out_specs=[pl.BlockSpec((nb, tm, GROUP), tile)] * 2,
        out_shape=[out, out],
        scratch_shapes=[pltpu.VMEM((n_chunks, 2 * CHUNK, GROUP), BF16),
                        pltpu.VMEM((n_chunks, CHUNK, GROUP), F32),
                        pltpu.VMEM((n_chunks, CHUNK, GROUP), F32),
                        pltpu.VMEM((n_chunks, TAIL, GROUP), F32),
                        pltpu.VMEM((nb, tm, GROUP), F32),
                        pltpu.VMEM((nb, CHUNK, GROUP), F32),
                        pltpu.VMEM((nb, tm, GROUP), F32),
                        pltpu.VMEM((nb, GROUP, GROUP), F32)],
        compiler_params=pltpu.CompilerParams(dimension_semantics=("arbitrary",),
                                             vmem_limit_bytes=VMEM_LIMIT),
        name="recur",
    )(cpk, cg, dpk, dla, nw, bd, ltri)


def _outproj_ffn_kernel(final, layer, x_ref, ya_ref, yb_ref, yc_ref, yd_ref, wo_ref, n2_ref, post_ref, wgu_ref,
                        wd_ref, *rest):
    post_w = post_ref[0:1, :] if final else post_ref[layer + 1:layer + 2, :]
    if final:
        o_ref, act_ref = rest
    else:
        o_ref, h_ref, act_ref = rest
    x1 = x_ref[...]
    for idx, y_ref in enumerate((ya_ref, yb_ref, yc_ref, yd_ref)):
        x1 = x1 + _dot(y_ref[...], wo_ref[idx * GROUP:(idx + 1) * GROUP, :])
    ms = jnp.mean(x1 * x1, axis=-1, keepdims=True)
    h = (x1 * lax.rsqrt(ms + EPS) * n2_ref[layer:layer + 1, :]).astype(BF16)
    for j in range(D_FF // FF_CHUNK):
        cols = slice(j * FF_CHUNK, (j + 1) * FF_CHUNK)
        gate = _dot(h, wgu_ref[:, cols])
        up = _dot(h, wgu_ref[:, D_FF + j * FF_CHUNK:D_FF + (j + 1) * FF_CHUNK])
        act_ref[:, cols] = (_silu(gate) * up).astype(BF16)
    x2 = x1 + _dot(act_ref[...], wd_ref[...])
    if final:
        ms2 = jnp.mean(x2 * x2, axis=-1, keepdims=True)
        x2 = x2 * lax.rsqrt(ms2 + EPS) * post_w
    else:
        h_ref[...] = _rmsnorm_bf16(x2, post_w)
    o_ref[...] = x2


def _outproj_ffn(x, ys, wo, n2, post, wgu, wd, layer, final):
    m = x.shape[0]
    n_steps = m // ROW_TILE
    row = lambda i: (i, 0)
    whole = lambda a: pl.BlockSpec(a.shape, (lambda i: (0,) * a.ndim))
    out_specs = [pl.BlockSpec((ROW_TILE, D_MODEL), row)]
    out_shape = [jax.ShapeDtypeStruct((m, D_MODEL), F32)]
    if not final:
        out_specs.append(pl.BlockSpec((ROW_TILE, D_MODEL), row))
        out_shape.append(jax.ShapeDtypeStruct((m, D_MODEL), BF16))
    return pl.pallas_call(
        functools.partial(_outproj_ffn_kernel, final, layer),
        grid=(n_steps,),
        in_specs=[pl.BlockSpec((ROW_TILE, D_MODEL), row)]
                 + [pl.BlockSpec((ROW_TILE, GROUP), row)] * 4
                 + [_resident(wo), whole(n2), whole(post), _resident(wgu), _resident(wd)],
        out_specs=out_specs,
        out_shape=out_shape,
        scratch_shapes=[pltpu.VMEM((ROW_TILE, D_FF), BF16)],
        compiler_params=pltpu.CompilerParams(dimension_semantics=("arbitrary",),
                                             vmem_limit_bytes=VMEM_LIMIT),
        name="outproj_ffn",
    )(x, *ys, wo, n2, post, wgu, wd)


def _gate_selectors():
    ea = np.zeros((GATE_COLS, GROUP), np.float32)
    eb = np.zeros((GATE_COLS, GROUP), np.float32)
    for h in range(HEADS):
        ea[h, h * HEAD_DIM:(h + 1) * HEAD_DIM] = 1.0
        eb[HEADS + h, h * HEAD_DIM:(h + 1) * HEAD_DIM] = 1.0
    return jnp.asarray(ea, BF16), jnp.asarray(eb, BF16)


def kernel(x, norm1_w, w_in, sgu_ln_w, sgu_ln_b, sgu_w_spatial, sgu_b_spatial, sc_conv_w, dn_conv_w, dn_a_log, dn_dt_bias, dn_norm_w, gla_w_gate2, gla_gate_bias, gla_norm_w, w_out, norm2_w, w_gate_up, w_down, final_norm_w):
    bsz, seq, d = x.shape
    depth = w_in.shape[0]
    assert seq % ROW_TILE == 0 and seq % SEQ_TILE == 0 and d == D_MODEL
    m = bsz * seq
    xf = x.reshape(m, d)
    bd, ltri = _const_inputs()
    ea, eb = _gate_selectors()
    w_in_b = w_in.astype(BF16)
    ffn_jobs = lambda layer: [(w_out, layer, D_MODEL), (w_gate_up, layer, 2 * D_FF), (w_down, layer, D_MODEL)]
    inproj_params = (sgu_w_spatial, sgu_b_spatial, sc_conv_w, dn_conv_w, ea, eb, bd, gla_w_gate2)
    inproj_vecs = (norm1_w, sgu_ln_w, sgu_ln_b, gla_gate_bias, dn_a_log, dn_dt_bias)
    nw = jnp.tile(jnp.stack([dn_norm_w, gla_norm_w], axis=1).astype(F32), (1, 1, HEADS))
    x_in = xf.reshape(bsz, seq, d)
    for l in range(depth):
        ya, yb, cpk, cg, dpk, dla, wo_b, wgu_b, wd_b = _inproj(x_in, *inproj_vecs, w_in_b, l, *inproj_params,
                                                               cast_jobs=ffn_jobs(l))
        yc, yd = _recur(cpk, cg, dpk, dla, nw, bd, ltri, l)
        last = l == depth - 1
        outs = _outproj_ffn(xf, tuple(y.reshape(m, GROUP) for y in (ya, yb, yc, yd)),
                            wo_b, norm2_w, final_norm_w[None, :] if last else norm1_w, wgu_b, wd_b,
                            layer=l, final=last)
        xf = outs[0]
        if not last:
            x_in = outs[1].reshape(bsz, seq, d)
    return xf.reshape(bsz, seq, d)
```

```python
import functools

import numpy as np
import jax
import jax.numpy as jnp
from jax import lax
from jax.experimental import pallas as pl
from jax.experimental.pallas import tpu as pltpu

F32 = jnp.float32
BF16 = jnp.bfloat16

D_MODEL = 1024
GROUP = 256
HEADS = 4
HEAD_DIM = 64
SGU_CHUNK = 128
SC_WIDTH = 3
DN_CONV_WIDTH = 4
CHUNK = 64
GATE_RANK = 16
GATE_TEMP = 16.0
D_FF = 2816
EPS = 1e-6
GATE_COLS = 128
TAIL = 8

ROW_TILE = 1024
SEQ_TILE = 512
FF_CHUNK = 256
VMEM_LIMIT = 60 * 1024 * 1024

MAIN_A = 0
MAIN_B = MAIN_A + 2 * GROUP
MAIN_C = MAIN_B + 3 * GROUP
MAIN_COLS = MAIN_C + 3 * GROUP
REST_CZ = 0
REST_D = REST_CZ + GROUP
REST_G = REST_D + 4 * GROUP
REST_COLS = REST_G + GATE_COLS
C_PACK = 5 * GROUP
D_PACK = 4 * GROUP


def _dot(a, b):
    return jnp.dot(a, b, preferred_element_type=F32)


def _dot_nt(a, b):
    return lax.dot_general(a, b, (((1,), (1,)), ((), ())), preferred_element_type=F32)


def _dot_tn(a, b):
    return lax.dot_general(a, b, (((0,), (0,)), ((), ())), preferred_element_type=F32)


def _split2(x):
    hi = x.astype(BF16)
    lo = (x - hi.astype(F32)).astype(BF16)
    return hi, lo


def _dot_sel_rhs(x, sel):
    hi, lo = _split2(x)
    return _dot(hi, sel) + _dot(lo, sel)


def _dot_sel_lhs(sel, x):
    hi, lo = _split2(x)
    return _dot(sel, hi) + _dot(sel, lo)


def _softplus(x):
    return jnp.maximum(x, 0.0) + jnp.log(1.0 + jnp.exp(-jnp.abs(x)))


def _sigmoid(x):
    return 0.5 + 0.5 * jnp.tanh(0.5 * x)


def _silu(x):
    h = 0.5 * x
    return h + h * jnp.tanh(h)


def _gelu_tanh(x):
    h = 0.5 * x
    return h + h * jnp.tanh(x * (0.7978845608028654 + (0.7978845608028654 * 0.044715) * (x * x)))


def _rmsnorm_bf16(x, w):
    ms = jnp.mean(x * x, axis=-1, keepdims=True)
    return (x * lax.rsqrt(ms + EPS) * w).astype(BF16)


def _blockdiag(x, mask):
    xb = x.astype(BF16)
    return jnp.concatenate([xb] * HEADS, axis=0) * mask


def _head_mean_sq(x, bones):
    return _dot((x * x).astype(BF16), bones) * (1.0 / HEAD_DIM)


LANES = 128


def _causal_conv(x, ext_ref, out_ref, cw_ref, width):
    tm, ch = x.shape
    half = tm // 2
    n_slabs = ch // LANES
    for s in range(n_slabs):
        ext_ref[s, TAIL:TAIL + tm, :] = x[:, s * LANES:(s + 1) * LANES]
    for s in range(n_slabs):
        lanes = slice(s * LANES, (s + 1) * LANES)
        for parity in (0, 1):
            acc = None
            for j in range(width):
                tap = (ext_ref[s, pl.ds(TAIL + parity - j, half, stride=2), :]
                       * cw_ref[width - 1 - j:width - j, lanes])
                acc = tap if acc is None else acc + tap
            out_ref[s, pl.ds(parity, half, stride=2), :] = acc
    for s in range(n_slabs):
        ext_ref[s, 0:TAIL, :] = x[tm - TAIL:tm, s * LANES:(s + 1) * LANES]
    return jnp.concatenate([out_ref[s] for s in range(n_slabs)], axis=1)


def _layer_param(stacked, layer):
    zeros = (0,) * (stacked.ndim - 1)
    return pl.BlockSpec((None,) + stacked.shape[1:], lambda i: (layer,) + zeros)


def _layer_block(stacked, layer):
    zeros = (0,) * (stacked.ndim - 1)
    return pl.BlockSpec((None,) + stacked.shape[1:], lambda i: (layer,) + zeros,
                        pipeline_mode=pl.Buffered(1))


def _resident(a):
    return pl.BlockSpec(a.shape, lambda i: (0, 0), pipeline_mode=pl.Buffered(1))


CAST_BLOCKS = 16
(V_NORM1, V_LNW, V_LNB, V_ALOG, V_DTB, V_GB) = range(6)


def _vec(table, key):
    return table[key]()


def _param_row(ref, layer):
    return lambda: ref[layer:layer + 1, :].astype(F32)


def _head_vec(ref, layer):
    def load():
        lane = lax.broadcasted_iota(jnp.int32, (1, GATE_COLS), 1)
        out = jnp.zeros((1, GATE_COLS), F32)
        for hd in range(HEADS):
            out = jnp.where(lane == hd, ref[layer, hd].astype(F32), out)
        return out
    return load


def _cast_specs(cast_jobs):
    last = CAST_BLOCKS - 1
    in_specs, out_specs, out_shapes = [], [], []
    for stacked, layer, n_cols in cast_jobs:
        rows = stacked.shape[1]
        assert rows % (16 * CAST_BLOCKS) == 0 and n_cols % 128 == 0
        blk = rows // CAST_BLOCKS
        in_specs.append(pl.BlockSpec((None, blk, n_cols),
                                     lambda i, layer=layer: (layer, jnp.minimum(i, last), 0)))
        out_specs.append(pl.BlockSpec((blk, n_cols), lambda i: (jnp.minimum(i, last), 0)))
        out_shapes.append(jax.ShapeDtypeStruct((rows, n_cols), BF16))
    return in_specs, out_specs, out_shapes


def _run_casts(cast_in, cast_out):
    for src, dst in zip(cast_in, cast_out):
        dst[...] = src[...].astype(dst.dtype)


def _split_refs(refs, *counts):
    assert sum(counts) == len(refs)
    groups, pos = [], 0
    for c in counts:
        groups.append(tuple(refs[pos:pos + c]))
        pos += c
    return groups


def _run_stages(stages):
    stages = list(stages)
    while stages:
        for gen in list(stages):
            try:
                next(gen)
            except StopIteration:
                stages.remove(gen)


def _inproj_stages(x_ref, vec, w_ref, wr_ref, ws_ref, bs_ref, scw_ref,
                   dcw_ref, ea_ref, eb_ref, bd_ref, wg_ref,
                   ya_ref, yb_ref, cpk_ref, cg_ref, dpk_ref, dla_ref, extb_ref, extc_ref, convb_ref, convc_ref):
    nb, tm = x_ref.shape[0], x_ref.shape[1]
    m = nb * tm
    batch_rows = [slice(b * tm, (b + 1) * tm) for b in range(nb)]

    x = x_ref[...].reshape(m, D_MODEL)
    if x.dtype == BF16:
        h = x
    else:
        h = _rmsnorm_bf16(x, _vec(vec, V_NORM1))
    bd = bd_ref[...]

    def project(ref, col0, n_slabs):
        parts = []
        for c in range(col0, col0 + n_slabs * GROUP, GROUP):
            parts.append(_dot(h, ref[:, c:c + GROUP]))
            yield
        return jnp.concatenate(parts, axis=1)

    pc = yield from project(w_ref, MAIN_C, 3)
    pcz = yield from project(wr_ref, REST_CZ, 1)
    pg = _dot(h, wr_ref[:, REST_G:REST_G + GATE_COLS])
    pa = yield from project(w_ref, MAIN_A, 2)

    qkv = jnp.concatenate(
        [_silu(_causal_conv(pc[rows, 0:3 * GROUP], extc_ref.at[b], convc_ref, dcw_ref, DN_CONV_WIDTH))
         for b, rows in enumerate(batch_rows)], axis=0)
    q = qkv[:, 0:GROUP]
    k = qkv[:, GROUP:2 * GROUP]
    pb = yield from project(w_ref, MAIN_B, 3)
    qn = q * lax.rsqrt(_dot((q * q).astype(BF16), bd) + EPS) * (HEAD_DIM ** -0.5)
    kn = k * lax.rsqrt(_dot((k * k).astype(BF16), bd) + EPS)
    yield
    g_narrow = -jnp.exp(_vec(vec, V_ALOG)) * _softplus(pg + _vec(vec, V_DTB))
    g_wide = _dot_sel_rhs(g_narrow, ea_ref[...])
    beta = _dot(_sigmoid(pg).astype(BF16), eb_ref[...])
    zg_c = _silu(pcz)
    pre = _dot(pg.astype(BF16), _gate_weight(wg_ref)) + _vec(vec, V_GB)
    log_a = -_softplus(-pre) * (1.0 / GATE_TEMP)
    yield

    tri = (lax.broadcasted_iota(jnp.int32, (SGU_CHUNK, SGU_CHUNK), 0)
           >= lax.broadcasted_iota(jnp.int32, (SGU_CHUNK, SGU_CHUNK), 1))
    head_of_lane = lax.broadcasted_iota(jnp.int32, (SGU_CHUNK, GROUP), 1) // HEAD_DIM
    ws = jnp.concatenate([jnp.where(tri, ws_ref[hd], 0.0).astype(BF16) for hd in range(HEADS)],
                         axis=1)
    lnw, lnb = _vec(vec, V_LNW), _vec(vec, V_LNB)
    for b in range(nb):
        for c in range(tm // SGU_CHUNK):
            rows = slice(b * tm + c * SGU_CHUNK, b * tm + (c + 1) * SGU_CHUNK)
            u = _gelu_tanh(pa[rows, 0:GROUP])
            v = _gelu_tanh(pa[rows, GROUP:2 * GROUP])
            mu = jnp.mean(v, axis=-1, keepdims=True)
            vc = v - mu
            var = jnp.mean(vc * vc, axis=-1, keepdims=True)
            vn = vc * lax.rsqrt(var + EPS) * lnw + lnb
            vn_heads = jnp.concatenate([jnp.where(head_of_lane == hd, vn, 0.0).astype(BF16)
                                        for hd in range(HEADS)], axis=0)
            mixed = bs_ref[...] + _dot(ws, vn_heads)
            ya_ref[b, c * SGU_CHUNK:(c + 1) * SGU_CHUNK, :] = (u * mixed).astype(ya_ref.dtype)
            yield

    pd = yield from project(wr_ref, REST_D, 4)

    for b, rows in enumerate(batch_rows):
        conv_b = _causal_conv(pb[rows, GROUP:2 * GROUP] * pb[rows, 2 * GROUP:3 * GROUP],
                              extb_ref.at[b], convb_ref, scw_ref, SC_WIDTH)
        yb_ref[b] = (pb[rows, 0:GROUP] * conv_b).astype(yb_ref.dtype)

    for b, rows in enumerate(batch_rows):
        cg_ref[b] = g_wide[rows]
        cpk_ref[b, :, 0:GROUP] = qn[rows].astype(BF16)
        cpk_ref[b, :, GROUP:2 * GROUP] = kn[rows].astype(BF16)
        cpk_ref[b, :, 2 * GROUP:3 * GROUP] = qkv[rows, 2 * GROUP:3 * GROUP].astype(BF16)
        cpk_ref[b, :, 3 * GROUP:4 * GROUP] = beta[rows].astype(BF16)
        cpk_ref[b, :, 4 * GROUP:5 * GROUP] = zg_c[rows].astype(BF16)
        dla_ref[b] = log_a[rows]
        dpk_ref[b, :, 0:3 * GROUP] = pd[rows, 0:3 * GROUP].astype(BF16)
        dpk_ref[b, :, 3 * GROUP:4 * GROUP] = _silu(pd[rows, 3 * GROUP:4 * GROUP]).astype(BF16)


def _repack_w_in_rest(w_ref, wr_ref):
    c_a = MAIN_COLS
    c_z = c_a + 2 * HEADS
    d0 = c_z + GROUP
    d_g = d0 + 3 * GROUP
    d_z = d_g + GATE_RANK
    wr_ref[:, REST_CZ:REST_CZ + GROUP] = w_ref[:, c_z:d0]
    wr_ref[:, REST_D:REST_D + 3 * GROUP] = w_ref[:, d0:d_g]
    wr_ref[:, REST_D + 3 * GROUP:REST_D + 4 * GROUP] = w_ref[:, d_z:d_z + GROUP]
    wr_ref[:, REST_G:REST_G + GATE_COLS] = jnp.zeros((wr_ref.shape[0], GATE_COLS), wr_ref.dtype)
    wr_ref[:, REST_G:REST_G + 2 * HEADS] = w_ref[:, c_a:c_z]
    wr_ref[:, REST_G + 2 * HEADS:REST_G + 2 * HEADS + GATE_RANK] = w_ref[:, d_g:d_z]


def _gate_weight(wg_ref):
    top = jnp.zeros((2 * HEADS, GROUP), F32)
    bottom = jnp.zeros((GATE_COLS - 2 * HEADS - GATE_RANK, GROUP), F32)
    return jnp.concatenate([top, wg_ref[...].astype(F32), bottom], axis=0).astype(BF16)


def _spread_spatial_bias(bs_ref, bsw_ref):
    eye = (lax.broadcasted_iota(jnp.int32, (SGU_CHUNK, SGU_CHUNK), 0)
           == lax.broadcasted_iota(jnp.int32, (SGU_CHUNK, SGU_CHUNK), 1))
    head_of_lane = lax.broadcasted_iota(jnp.int32, (SGU_CHUNK, GROUP), 1) // HEAD_DIM
    out = jnp.zeros((SGU_CHUNK, GROUP), F32)
    for hd in range(HEADS):
        col = jnp.sum(jnp.where(eye, bs_ref[hd:hd + 1, :].astype(F32), 0.0), axis=1, keepdims=True)
        out = jnp.where(head_of_lane == hd, col, out)
    bsw_ref[...] = out


def _inproj_kernel(layer, n_cast, x_ref, n1_ref, lnw_ref, lnb_ref, gb_ref, alog_ref, dtb_ref, w_ref, ws_ref, bs_ref,
                   scw_ref, dcw_ref, ea_ref, eb_ref, bd_ref, wg_ref, *rest):
    cast_in, outs, cast_out, scratch = _split_refs(rest, n_cast, 6, n_cast, 6)
    vec = {V_NORM1: _param_row(n1_ref, layer), V_LNW: _param_row(lnw_ref, layer), V_LNB: _param_row(lnb_ref, layer),
           V_GB: _param_row(gb_ref, layer), V_ALOG: _head_vec(alog_ref, layer), V_DTB: _head_vec(dtb_ref, layer)}
    ya_ref, yb_ref, cpk_ref, cg_ref, dpk_ref, dla_ref = outs
    extb_ref, extc_ref, convb_ref, convc_ref, wr_ref, bsw_ref = scratch
    _run_casts(cast_in, cast_out)

    @pl.when(pl.program_id(0) == 0)
    def _():
        extb_ref[...] = jnp.zeros(extb_ref.shape, F32)
        extc_ref[...] = jnp.zeros(extc_ref.shape, F32)
        _repack_w_in_rest(w_ref, wr_ref)
        _spread_spatial_bias(bs_ref, bsw_ref)

    _run_stages([_inproj_stages(x_ref, vec, w_ref, wr_ref, ws_ref, bsw_ref, scw_ref, dcw_ref,
                                ea_ref, eb_ref, bd_ref, wg_ref,
                                ya_ref, yb_ref, cpk_ref, cg_ref, dpk_ref, dla_ref, extb_ref, extc_ref,
                                convb_ref, convc_ref)])


def _inproj(x, n1, lnw, lnb, gb, alog, dtb, w_in_b, layer, ws, bs, scw, dcw, ea, eb, bd, wg, cast_jobs=()):
    nb, seq, _ = x.shape
    tm = SEQ_TILE
    n_tiles = seq // tm
    assert n_tiles >= CAST_BLOCKS or not cast_jobs
    cast_in_specs, cast_out_specs, cast_out_shapes = _cast_specs(cast_jobs)
    tile = lambda i: (0, i, 0)
    whole = lambda a: pl.BlockSpec(a.shape, (lambda i: (0,) * a.ndim))
    per_layer = lambda a: _layer_param(a, layer)
    outs = ((GROUP, BF16), (GROUP, BF16), (C_PACK, BF16), (GROUP, F32), (D_PACK, BF16), (GROUP, F32))
    return pl.pallas_call(
        functools.partial(_inproj_kernel, layer, len(cast_jobs)),
        grid=(n_tiles,),
        in_specs=[pl.BlockSpec((nb, tm, D_MODEL), tile),
                  whole(n1), whole(lnw), whole(lnb), whole(gb),
                  pl.BlockSpec(memory_space=pltpu.SMEM), pl.BlockSpec(memory_space=pltpu.SMEM),
                  _layer_block(w_in_b, layer),
                  per_layer(ws), per_layer(bs), per_layer(scw),
                  per_layer(dcw), whole(ea), whole(eb), whole(bd),
                  per_layer(wg)]
                 + cast_in_specs,
        out_specs=[pl.BlockSpec((nb, tm, n), tile) for n, _ in outs] + cast_out_specs,
        out_shape=[jax.ShapeDtypeStruct((nb, seq, n), dt) for n, dt in outs] + cast_out_shapes,
        scratch_shapes=[pltpu.VMEM((nb, GROUP // LANES, TAIL + tm, LANES), F32),
                        pltpu.VMEM((nb, 3 * GROUP // LANES, TAIL + tm, LANES), F32),
                        pltpu.VMEM((GROUP // LANES, tm, LANES), F32),
                        pltpu.VMEM((3 * GROUP // LANES, tm, LANES), F32),
                        pltpu.VMEM((D_MODEL, REST_COLS), BF16),
                        pltpu.VMEM((SGU_CHUNK, GROUP), F32)],
        compiler_params=pltpu.CompilerParams(dimension_semantics=("arbitrary",),
                                             vmem_limit_bytes=VMEM_LIMIT),
        name="in_proj",
    )(x, n1, lnw, lnb, gb, alog, dtb, w_in_b, ws, bs, scw, dcw, ea, eb, bd, wg,
      *[w for w, _, _ in cast_jobs])


def _chunk_masks():
    i = lax.broadcasted_iota(jnp.int32, (CHUNK, GROUP), 0)
    j = lax.broadcasted_iota(jnp.int32, (CHUNK, GROUP), 1) % CHUNK
    return i == j, i >= j, i > j


def _const_inputs():
    lane_head = np.arange(GROUP) // HEAD_DIM
    bd = (lane_head[:, None] == lane_head[None, :]).astype(np.float32)
    ltri = np.tril(np.ones((CHUNK, CHUNK), np.float32))
    return jnp.asarray(bd, BF16), jnp.asarray(ltri, BF16)


def _chunk_rows(c):
    return slice(c * CHUNK, (c + 1) * CHUNK)


def _gdn_stages(cpk_ref, cg_ref, nw_ref, bd_ref, ltri_ref, y_ref, ac_ref, bb_ref, d_ref, dec_ref, o_ref, s_ref):
    nb, tm = cpk_ref.shape[0], cpk_ref.shape[1]
    cps = tm // CHUNK
    eye_w, causal_w, strict_w = _chunk_masks()
    eye_f = eye_w.astype(F32)
    ltri = ltri_ref[...]
    bd = bd_ref[...]

    yield from _gdn_prepare([(b, c) for b in range(nb) for c in range(cps)], cps, cpk_ref, cg_ref,
                            ac_ref, bb_ref, d_ref, dec_ref, eye_w, causal_w, strict_w, eye_f, ltri, bd)

    for c in range(cps):
        for b in range(nb):
            i = b * cps + c
            s = s_ref[b]
            r = _dot(ac_ref[i], _blockdiag(s, bd))
            s_ref[b] = s * dec_ref[i, 0:1, :] + r[0:CHUNK] + bb_ref[i]
            o_ref[b, _chunk_rows(c), :] = r[CHUNK:2 * CHUNK] + d_ref[i]
        yield

    for b in range(nb):
        o = o_ref[b]
        zg = cpk_ref[b, :, 4 * GROUP:5 * GROUP].astype(F32)
        y_ref[b] = (o * lax.rsqrt(_head_mean_sq(o, bd) + EPS) * nw_ref[...] * zg).astype(y_ref.dtype)


def _gdn_prepare(items, cps, cpk_ref, cg_ref, ac_ref, bb_ref, d_ref, dec_ref,
                 eye_w, causal_w, strict_w, eye_f, ltri, bd):
    n = len(items)

    def load(col, i):
        b, c = items[i]
        return cpk_ref[b, _chunk_rows(c), col * GROUP:(col + 1) * GROUP]

    qn = [load(0, i) for i in range(n)]
    kn = [load(1, i) for i in range(n)]
    beta = [load(3, i) for i in range(n)]
    gc = [_dot_sel_lhs(ltri, cg_ref[b, _chunk_rows(c), :]) for b, c in items]
    yield
    eg = [jnp.exp(x) for x in gc]
    kb = [kn[i] * beta[i] for i in range(n)]
    eye_b = eye_f.astype(BF16)
    r1 = [_dot_nt(jnp.concatenate([kb[i], qn[i], eye_b], axis=0), _blockdiag(kn[i], bd))
          for i in range(n)]
    yield
    low, attn, g_row = [], [], []
    for i in range(n):
        g_row.append(jnp.sum(jnp.where(eye_w, gc[i], 0.0), axis=0, keepdims=True))
        decay = jnp.where(causal_w, jnp.exp(jnp.minimum(gc[i] - g_row[i], 0.0)), 0.0)
        low.append(jnp.where(strict_w, r1[i][0:CHUNK] * decay, 0.0))
        attn.append((r1[i][CHUNK:2 * CHUNK] * decay).astype(BF16))
    t = [eye_f - x for x in low]
    p = [_dot(x.astype(BF16), _blockdiag(x, bd)) for x in low]
    yield
    n_levels = CHUNK.bit_length() - 1
    for level in range(1, n_levels):
        if level < n_levels - 1:
            r2 = [_dot(jnp.concatenate([p[i], t[i]], axis=0).astype(BF16), _blockdiag(p[i], bd))
                  for i in range(n)]
            p = [x[0:CHUNK] for x in r2]
            t = [t[i] + r2[i][CHUNK:2 * CHUNK] for i in range(n)]
        else:
            t = [t[i] + _dot(t[i].astype(BF16), _blockdiag(p[i], bd)) for i in range(n)]
        yield
    gc_last = [x[CHUNK - 1:CHUNK, :] for x in gc]
    kd_t = [(r1[i][2 * CHUNK:3 * CHUNK] * jnp.exp(gc_last[i] - g_row[i])).astype(BF16) for i in range(n)]
    lhs_t = [_dot(jnp.concatenate([attn[i], kd_t[i]], axis=0), _blockdiag(t[i], bd)) for i in range(n)]
    yield
    prod = [_dot(lhs_t[i].astype(BF16),
                 jnp.concatenate([_blockdiag(kb[i] * eg[i], bd), _blockdiag(load(2, i) * beta[i], bd)], axis=1))
            for i in range(n)]
    for i, (b, c) in enumerate(items):
        j = b * cps + c
        ac_ref[j, 0:CHUNK, :] = (-prod[i][CHUNK:2 * CHUNK, 0:GROUP]).astype(BF16)
        bb_ref[j] = prod[i][CHUNK:2 * CHUNK, GROUP:2 * GROUP]
        ac_ref[j, CHUNK:2 * CHUNK, :] = (qn[i] * eg[i] - prod[i][0:CHUNK, 0:GROUP]).astype(BF16)
        d_ref[j] = prod[i][0:CHUNK, GROUP:2 * GROUP]
        dec_ref[j] = jnp.broadcast_to(jnp.exp(gc_last[i]), (TAIL, GROUP))
    yield


def _gla_stages(dpk_ref, dla_ref, nw_ref, bd_ref, ltri_ref, y_ref, o_ref, st_ref):
    nb, tm = dpk_ref.shape[0], dpk_ref.shape[1]
    cps = tm // CHUNK
    _, causal_w, _ = _chunk_masks()
    ltri = ltri_ref[...]
    bd = bd_ref[...]
    bdf = bd.astype(F32)
    mid = CHUNK // 2

    items = [(b, c) for c in range(cps) for b in range(nb)]
    n = len(items)
    q = [dpk_ref[b, _chunk_rows(c), 0:GROUP].astype(F32) * (HEAD_DIM ** -0.5) for b, c in items]
    k = [dpk_ref[b, _chunk_rows(c), GROUP:2 * GROUP].astype(F32) for b, c in items]
    v = [dpk_ref[b, _chunk_rows(c), 2 * GROUP:3 * GROUP] for b, c in items]
    gcum = [_dot_sel_lhs(ltri, dla_ref[b, _chunk_rows(c), :]) for b, c in items]
    yield
    g_mid = [x[mid:mid + 1, :] for x in gcum]
    g_last = [x[CHUNK - 1:CHUNK, :] for x in gcum]
    attn = [jnp.where(causal_w,
                      _dot_nt((q[i] * jnp.exp(gcum[i] - g_mid[i])).astype(BF16),
                              _blockdiag(k[i] * jnp.exp(g_mid[i] - gcum[i]), bd)), 0.0).astype(BF16)
            for i in range(n)]
    yield
    upd = [bdf * _dot_tn((k[i] * jnp.exp(g_last[i] - gcum[i])).astype(BF16), v[i]) for i in range(n)]
    yield
    o_intra = [_dot(attn[i], _blockdiag(v[i], bd)) for i in range(n)]
    qg = [(q[i] * jnp.exp(gcum[i])).astype(BF16) for i in range(n)]
    yield
    for i, (b, c) in enumerate(items):
        st = st_ref[b]
        o_ref[b, _chunk_rows(c), :] = o_intra[i] + _dot(qg[i], st.astype(BF16))
        dec_col = jnp.transpose(jnp.broadcast_to(jnp.exp(g_last[i]), (GROUP, GROUP)))
        st_ref[b] = st * dec_col + upd[i]
        if b == nb - 1:
            yield

    for b in range(nb):
        o = o_ref[b]
        zg = dpk_ref[b, :, 3 * GROUP:4 * GROUP].astype(F32)
        y_ref[b] = (o * lax.rsqrt(_head_mean_sq(o, bd) + EPS) * nw_ref[...] * zg).astype(y_ref.dtype)


def _recur_kernel(cpk_ref, cg_ref, dpk_ref, dla_ref, nw_ref, bd_ref, ltri_ref, yc_ref, yd_ref,
                  ac_ref, bb_ref, d_ref, dec_ref, oc_ref, s_ref, od_ref, st_ref):
    nwc_ref = nw_ref.at[pl.ds(0, 1), pl.ds(0, GROUP)]
    nwd_ref = nw_ref.at[pl.ds(1, 1), pl.ds(0, GROUP)]

    @pl.when(pl.program_id(0) == 0)
    def _():
        s_ref[...] = jnp.zeros(s_ref.shape, F32)
        st_ref[...] = jnp.zeros(st_ref.shape, F32)

    _run_stages([_gla_stages(dpk_ref, dla_ref, nwd_ref, bd_ref, ltri_ref, yd_ref, od_ref, st_ref),
                 _gdn_stages(cpk_ref, cg_ref, nwc_ref, bd_ref, ltri_ref, yc_ref,
                             ac_ref, bb_ref, d_ref, dec_ref, oc_ref, s_ref)])


def _recur(cpk, cg, dpk, dla, nw, bd, ltri, layer):
    nb, seq, _ = cpk.shape
    tm = SEQ_TILE
    n_chunks = nb * tm // CHUNK
    tile = lambda i: (0, i, 0)
    whole = lambda a: pl.BlockSpec(a.shape, (lambda i: (0,) * a.ndim))
    out = jax.ShapeDtypeStruct((nb, seq, GROUP), BF16)
    return pl.pallas_call(
        _recur_kernel,
        grid=(seq // tm,),
        in_specs=[pl.BlockSpec((nb, tm, C_PACK), tile),
                  pl.BlockSpec((nb, tm, GROUP), tile),
                  pl.BlockSpec((nb, tm, D_PACK), tile),
                  pl.BlockSpec((nb, tm, GROUP), tile),
                  _layer_param(nw, layer), whole(bd), whole(ltri)],
        out_specs=[pl.BlockSpec((nb, tm, GROUP), tile)] * 2,
        out_shape=[out, out],
        scratch_shapes=[pltpu.VMEM((n_chunks, 2 * CHUNK, GROUP), BF16),
                        pltpu.VMEM((n_chunks, CHUNK, GROUP), F32),
                        pltpu.VMEM((n_chunks, CHUNK, GROUP), F32),
                        pltpu.VMEM((n_chunks, TAIL, GROUP), F32),
                        pltpu.VMEM((nb, tm, GROUP), F32),
                        pltpu.VMEM((nb, CHUNK, GROUP), F32),
                        pltpu.VMEM((nb, tm, GROUP), F32),
                        pltpu.VMEM((nb, GROUP, GROUP), F32)],
        compiler_params=pltpu.CompilerParams(dimension_semantics=("arbitrary",),
                                             vmem_limit_bytes=VMEM_LIMIT),
        name="recur",
    )(cpk, cg, dpk, dla, nw, bd, ltri)


def _outproj_ffn_kernel(final, layer, x_ref, ya_ref, yb_ref, yc_ref, yd_ref, wo_hbm, n2_ref, post_ref, wgu_hbm,
                        wd_hbm, *rest):
    post_w = post_ref[0:1, :] if final else post_ref[layer + 1:layer + 2, :]
    if final:
        o_ref, act_ref, wo_ref, wgu_ref, wd_ref, sem = rest
    else:
        o_ref, h_ref, act_ref, wo_ref, wgu_ref, wd_ref, sem = rest

    first = pl.program_id(0) == 0
    fetch_wo, fetch_wgu, fetch_wd = [pltpu.make_async_copy(src, dst, sem.at[k]) for k, (src, dst) in
                                     enumerate(((wo_hbm, wo_ref), (wgu_hbm, wgu_ref), (wd_hbm, wd_ref)))]

    @pl.when(first)
    def _():
        fetch_wo.start()
        fetch_wgu.start()
        fetch_wd.start()
        fetch_wo.wait()

    x1 = x_ref[...]
    for idx, y_ref in enumerate((ya_ref, yb_ref, yc_ref, yd_ref)):
        x1 = x1 + _dot(y_ref[...], wo_ref[idx * GROUP:(idx + 1) * GROUP, :])
    ms = jnp.mean(x1 * x1, axis=-1, keepdims=True)
    h = (x1 * lax.rsqrt(ms + EPS) * n2_ref[layer:layer + 1, :]).astype(BF16)

    @pl.when(first)
    def _():
        fetch_wgu.wait()

    for j in range(D_FF // FF_CHUNK):
        cols = slice(j * FF_CHUNK, (j + 1) * FF_CHUNK)
        gate = _dot(h, wgu_ref[:, cols])
        up = _dot(h, wgu_ref[:, D_FF + j * FF_CHUNK:D_FF + (j + 1) * FF_CHUNK])
        act_ref[:, cols] = (_silu(gate) * up).astype(BF16)

    @pl.when(first)
    def _():
        fetch_wd.wait()

    x2 = x1 + _dot(act_ref[...], wd_ref[...])
    if final:
        ms2 = jnp.mean(x2 * x2, axis=-1, keepdims=True)
        x2 = x2 * lax.rsqrt(ms2 + EPS) * post_w
    else:
        h_ref[...] = _rmsnorm_bf16(x2, post_w)
    o_ref[...] = x2


def _outproj_ffn(x, ys, wo, n2, post, wgu, wd, layer, final):
    m = x.shape[0]
    n_steps = m // ROW_TILE
    row = lambda i: (i, 0)
    whole = lambda a: pl.BlockSpec(a.shape, (lambda i: (0,) * a.ndim))
    in_hbm = pl.BlockSpec(memory_space=pl.ANY)
    out_specs = [pl.BlockSpec((ROW_TILE, D_MODEL), row)]
    out_shape = [jax.ShapeDtypeStruct((m, D_MODEL), F32)]
    if not final:
        out_specs.append(pl.BlockSpec((ROW_TILE, D_MODEL), row))
        out_shape.append(jax.ShapeDtypeStruct((m, D_MODEL), BF16))
    return pl.pallas_call(
        functools.partial(_outproj_ffn_kernel, final, layer),
        grid=(n_steps,),
        in_specs=[pl.BlockSpec((ROW_TILE, D_MODEL), row)]
                 + [pl.BlockSpec((ROW_TILE, GROUP), row)] * 4
                 + [in_hbm, whole(n2), whole(post), in_hbm, in_hbm],
        out_specs=out_specs,
        out_shape=out_shape,
        scratch_shapes=[pltpu.VMEM((ROW_TILE, D_FF), BF16),
                        pltpu.VMEM(wo.shape, wo.dtype), pltpu.VMEM(wgu.shape, wgu.dtype),
                        pltpu.VMEM(wd.shape, wd.dtype), pltpu.SemaphoreType.DMA((3,))],
        compiler_params=pltpu.CompilerParams(dimension_semantics=("arbitrary",),
                                             vmem_limit_bytes=VMEM_LIMIT),
        name="outproj_ffn",
    )(x, *ys, wo, n2, post, wgu, wd)


def _gate_selectors():
    ea = np.zeros((GATE_COLS, GROUP), np.float32)
    eb = np.zeros((GATE_COLS, GROUP), np.float32)
    for h in range(HEADS):
        ea[h, h * HEAD_DIM:(h + 1) * HEAD_DIM] = 1.0
        eb[HEADS + h, h * HEAD_DIM:(h + 1) * HEAD_DIM] = 1.0
    return jnp.asarray(ea, BF16), jnp.asarray(eb, BF16)


def kernel(x, norm1_w, w_in, sgu_ln_w, sgu_ln_b, sgu_w_spatial, sgu_b_spatial, sc_conv_w, dn_conv_w, dn_a_log, dn_dt_bias, dn_norm_w, gla_w_gate2, gla_gate_bias, gla_norm_w, w_out, norm2_w, w_gate_up, w_down, final_norm_w):
    bsz, seq, d = x.shape
    depth = w_in.shape[0]
    assert seq % ROW_TILE == 0 and seq % SEQ_TILE == 0 and d == D_MODEL
    m = bsz * seq
    xf = x.reshape(m, d)
    bd, ltri = _const_inputs()
    ea, eb = _gate_selectors()
    w_in_b = w_in.astype(BF16)
    ffn_jobs = lambda layer: [(w_out, layer, D_MODEL), (w_gate_up, layer, 2 * D_FF), (w_down, layer, D_MODEL)]
    inproj_params = (sgu_w_spatial, sgu_b_spatial, sc_conv_w, dn_conv_w, ea, eb, bd, gla_w_gate2)
    inproj_vecs = (norm1_w, sgu_ln_w, sgu_ln_b, gla_gate_bias, dn_a_log, dn_dt_bias)
    nw = jnp.tile(jnp.stack([dn_norm_w, gla_norm_w], axis=1).astype(F32), (1, 1, HEADS))
    x_in = xf.reshape(bsz, seq, d)
    for l in range(depth):
        ya, yb, cpk, cg, dpk, dla, wo_b, wgu_b, wd_b = _inproj(x_in, *inproj_vecs, w_in_b, l, *inproj_params,
                                                               cast_jobs=ffn_jobs(l))
        yc, yd = _recur(cpk, cg, dpk, dla, nw, bd, ltri, l)
        last = l == depth - 1
        outs = _outproj_ffn(xf, tuple(y.reshape(m, GROUP) for y in (ya, yb, yc, yd)),
                            wo_b, norm2_w, final_norm_w[None, :] if last else norm1_w, wgu_b, wd_b,
                            layer=l, final=last)
        xf = outs[0]
        if not last:
            x_in = outs[1].reshape(bsz, seq, d)
    return xf.reshape(bsz, seq, d)
```

```python
import functools

import numpy as np
import jax
import jax.numpy as jnp
from jax import lax
from jax.experimental import pallas as pl
from jax.experimental.pallas import tpu as pltpu

F32 = jnp.float32
BF16 = jnp.bfloat16

D_MODEL = 1024
GROUP = 256
HEADS = 4
HEAD_DIM = 64
SGU_CHUNK = 128
SC_WIDTH = 3
DN_CONV_WIDTH = 4
CHUNK = 64
GATE_RANK = 16
GATE_TEMP = 16.0
D_FF = 2816
EPS = 1e-6
GATE_COLS = 128
TAIL = 8

ROW_TILE = 1024
SEQ_TILE = 512
FF_CHUNK = 256
VMEM_LIMIT = 60 * 1024 * 1024

MAIN_A = 0
MAIN_B = MAIN_A + 2 * GROUP
MAIN_C = MAIN_B + 3 * GROUP
MAIN_COLS = MAIN_C + 3 * GROUP
REST_CZ = 0
REST_D = REST_CZ + GROUP
REST_G = REST_D + 4 * GROUP
REST_COLS = REST_G + GATE_COLS
C_PACK = 5 * GROUP
D_PACK = 4 * GROUP


def _dot(a, b):
    return jnp.dot(a, b, preferred_element_type=F32)


def _dot_nt(a, b):
    return lax.dot_general(a, b, (((1,), (1,)), ((), ())), preferred_element_type=F32)


def _dot_tn(a, b):
    return lax.dot_general(a, b, (((0,), (0,)), ((), ())), preferred_element_type=F32)


def _split2(x):
    hi = x.astype(BF16)
    lo = (x - hi.astype(F32)).astype(BF16)
    return hi, lo


def _dot_sel_rhs(x, sel):
    hi, lo = _split2(x)
    return _dot(hi, sel) + _dot(lo, sel)


def _dot_sel_lhs(sel, x):
    hi, lo = _split2(x)
    return _dot(sel, hi) + _dot(sel, lo)


def _softplus(x):
    return jnp.maximum(x, 0.0) + jnp.log(1.0 + jnp.exp(-jnp.abs(x)))


def _sigmoid(x):
    return 0.5 + 0.5 * jnp.tanh(0.5 * x)


def _silu(x):
    h = 0.5 * x
    return h + h * jnp.tanh(h)


def _gelu_tanh(x):
    h = 0.5 * x
    return h + h * jnp.tanh(x * (0.7978845608028654 + (0.7978845608028654 * 0.044715) * (x * x)))


def _rmsnorm_bf16(x, w):
    ms = jnp.mean(x * x, axis=-1, keepdims=True)
    return (x * lax.rsqrt(ms + EPS) * w).astype(BF16)


def _blockdiag(x, mask):
    xb = x.astype(BF16)
    return jnp.concatenate([xb] * HEADS, axis=0) * mask


def _head_mean_sq(x, bones):
    return _dot((x * x).astype(BF16), bones) * (1.0 / HEAD_DIM)


LANES = 128


def _causal_conv(x, ext_ref, out_ref, cw_ref, width):
    tm, ch = x.shape
    half = tm // 2
    n_slabs = ch // LANES
    for s in range(n_slabs):
        ext_ref[s, TAIL:TAIL + tm, :] = x[:, s * LANES:(s + 1) * LANES]
    for s in range(n_slabs):
        lanes = slice(s * LANES, (s + 1) * LANES)
        for parity in (0, 1):
            acc = None
            for j in range(width):
                tap = (ext_ref[s, pl.ds(TAIL + parity - j, half, stride=2), :]
                       * cw_ref[width - 1 - j:width - j, lanes])
                acc = tap if acc is None else acc + tap
            out_ref[s, pl.ds(parity, half, stride=2), :] = acc
    for s in range(n_slabs):
        ext_ref[s, 0:TAIL, :] = x[tm - TAIL:tm, s * LANES:(s + 1) * LANES]
    return jnp.concatenate([out_ref[s] for s in range(n_slabs)], axis=1)


def _layer_param(stacked, layer):
    zeros = (0,) * (stacked.ndim - 1)
    return pl.BlockSpec((None,) + stacked.shape[1:], lambda i: (layer,) + zeros)


def _layer_block(stacked, layer):
    zeros = (0,) * (stacked.ndim - 1)
    return pl.BlockSpec((None,) + stacked.shape[1:], lambda i: (layer,) + zeros,
                        pipeline_mode=pl.Buffered(1))


def _resident(a):
    return pl.BlockSpec(a.shape, lambda i: (0, 0), pipeline_mode=pl.Buffered(1))


CAST_BLOCKS = 16
(V_NORM1, V_LNW, V_LNB, V_ALOG, V_DTB, V_GB) = range(6)


def _vec(table, key):
    return table[key]()


def _param_row(ref, layer):
    return lambda: ref[layer:layer + 1, :].astype(F32)


def _head_vec(ref, layer):
    def load():
        lane = lax.broadcasted_iota(jnp.int32, (1, GATE_COLS), 1)
        out = jnp.zeros((1, GATE_COLS), F32)
        for hd in range(HEADS):
            out = jnp.where(lane == hd, ref[layer, hd].astype(F32), out)
        return out
    return load


def _cast_specs(cast_jobs):
    last = CAST_BLOCKS - 1
    in_specs, out_specs, out_shapes = [], [], []
    for stacked, layer, n_cols in cast_jobs:
        rows = stacked.shape[1]
        assert rows % (16 * CAST_BLOCKS) == 0 and n_cols % 128 == 0
        blk = rows // CAST_BLOCKS
        in_specs.append(pl.BlockSpec((None, blk, n_cols),
                                     lambda i, layer=layer: (layer, jnp.minimum(i, last), 0)))
        out_specs.append(pl.BlockSpec((blk, n_cols), lambda i: (jnp.minimum(i, last), 0)))
        out_shapes.append(jax.ShapeDtypeStruct((rows, n_cols), BF16))
    return in_specs, out_specs, out_shapes


def _run_casts(cast_in, cast_out):
    for src, dst in zip(cast_in, cast_out):
        dst[...] = src[...].astype(dst.dtype)


def _split_refs(refs, *counts):
    assert sum(counts) == len(refs)
    groups, pos = [], 0
    for c in counts:
        groups.append(tuple(refs[pos:pos + c]))
        pos += c
    return groups


def _run_stages(stages):
    stages = list(stages)
    while stages:
        for gen in list(stages):
            try:
                next(gen)
            except StopIteration:
                stages.remove(gen)


def _inproj_stages(x_ref, vec, w_ref, wr_ref, ws_ref, bs_ref, scw_ref,
                   dcw_ref, ea_ref, eb_ref, bd_ref, wg_ref,
                   ya_ref, yb_ref, cpk_ref, cg_ref, dpk_ref, dla_ref, extb_ref, extc_ref, convb_ref, convc_ref):
    nb, tm = x_ref.shape[0], x_ref.shape[1]
    m = nb * tm
    batch_rows = [slice(b * tm, (b + 1) * tm) for b in range(nb)]

    x = x_ref[...].reshape(m, D_MODEL)
    if x.dtype == BF16:
        h = x
    else:
        h = _rmsnorm_bf16(x, _vec(vec, V_NORM1))
    bd = bd_ref[...]

    def project(ref, col0, n_slabs):
        parts = []
        for c in range(col0, col0 + n_slabs * GROUP, GROUP):
            parts.append(_dot(h, ref[:, c:c + GROUP]))
            yield
        return jnp.concatenate(parts, axis=1)

    pc = yield from project(w_ref, MAIN_C, 3)
    pcz = yield from project(wr_ref, REST_CZ, 1)
    pg = _dot(h, wr_ref[:, REST_G:REST_G + GATE_COLS])
    pa = yield from project(w_ref, MAIN_A, 2)

    qkv = jnp.concatenate(
        [_silu(_causal_conv(pc[rows, 0:3 * GROUP], extc_ref.at[b], convc_ref, dcw_ref, DN_CONV_WIDTH))
         for b, rows in enumerate(batch_rows)], axis=0)
    q = qkv[:, 0:GROUP]
    k = qkv[:, GROUP:2 * GROUP]
    pb = yield from project(w_ref, MAIN_B, 3)
    qn = q * lax.rsqrt(_dot((q * q).astype(BF16), bd) + EPS) * (HEAD_DIM ** -0.5)
    kn = k * lax.rsqrt(_dot((k * k).astype(BF16), bd) + EPS)
    yield
    g_narrow = -jnp.exp(_vec(vec, V_ALOG)) * _softplus(pg + _vec(vec, V_DTB))
    g_wide = _dot_sel_rhs(g_narrow, ea_ref[...])
    beta = _dot(_sigmoid(pg).astype(BF16), eb_ref[...])
    zg_c = _silu(pcz)
    pre = _dot(pg.astype(BF16), _gate_weight(wg_ref)) + _vec(vec, V_GB)
    log_a = -_softplus(-pre) * (1.0 / GATE_TEMP)
    yield

    tri = (lax.broadcasted_iota(jnp.int32, (SGU_CHUNK, SGU_CHUNK), 0)
           >= lax.broadcasted_iota(jnp.int32, (SGU_CHUNK, SGU_CHUNK), 1))
    head_of_lane = lax.broadcasted_iota(jnp.int32, (SGU_CHUNK, GROUP), 1) // HEAD_DIM
    ws = jnp.concatenate([jnp.where(tri, ws_ref[hd], 0.0).astype(BF16) for hd in range(HEADS)],
                         axis=1)
    lnw, lnb = _vec(vec, V_LNW), _vec(vec, V_LNB)
    for b in range(nb):
        for c in range(tm // SGU_CHUNK):
            rows = slice(b * tm + c * SGU_CHUNK, b * tm + (c + 1) * SGU_CHUNK)
            u = _gelu_tanh(pa[rows, 0:GROUP])
            v = _gelu_tanh(pa[rows, GROUP:2 * GROUP])
            mu = jnp.mean(v, axis=-1, keepdims=True)
            vc = v - mu
            var = jnp.mean(vc * vc, axis=-1, keepdims=True)
            vn = vc * lax.rsqrt(var + EPS) * lnw + lnb
            vn_heads = jnp.concatenate([jnp.where(head_of_lane == hd, vn, 0.0).astype(BF16)
                                        for hd in range(HEADS)], axis=0)
            mixed = bs_ref[...] + _dot(ws, vn_heads)
            ya_ref[b, c * SGU_CHUNK:(c + 1) * SGU_CHUNK, :] = (u * mixed).astype(ya_ref.dtype)
            yield

    pd = yield from project(wr_ref, REST_D, 4)

    for b, rows in enumerate(batch_rows):
        conv_b = _causal_conv(pb[rows, GROUP:2 * GROUP] * pb[rows, 2 * GROUP:3 * GROUP],
                              extb_ref.at[b], convb_ref, scw_ref, SC_WIDTH)
        yb_ref[b] = (pb[rows, 0:GROUP] * conv_b).astype(yb_ref.dtype)

    for b, rows in enumerate(batch_rows):
        cg_ref[b] = g_wide[rows]
        cpk_ref[b, :, 0:GROUP] = qn[rows].astype(BF16)
        cpk_ref[b, :, GROUP:2 * GROUP] = kn[rows].astype(BF16)
        cpk_ref[b, :, 2 * GROUP:3 * GROUP] = qkv[rows, 2 * GROUP:3 * GROUP].astype(BF16)
        cpk_ref[b, :, 3 * GROUP:4 * GROUP] = beta[rows].astype(BF16)
        cpk_ref[b, :, 4 * GROUP:5 * GROUP] = zg_c[rows].astype(BF16)
        dla_ref[b] = log_a[rows]
        dpk_ref[b, :, 0:3 * GROUP] = pd[rows, 0:3 * GROUP].astype(BF16)
        dpk_ref[b, :, 3 * GROUP:4 * GROUP] = _silu(pd[rows, 3 * GROUP:4 * GROUP]).astype(BF16)


def _repack_w_in_rest(w_ref, wr_ref):
    c_a = MAIN_COLS
    c_z = c_a + 2 * HEADS
    d0 = c_z + GROUP
    d_g = d0 + 3 * GROUP
    d_z = d_g + GATE_RANK
    wr_ref[:, REST_CZ:REST_CZ + GROUP] = w_ref[:, c_z:d0]
    wr_ref[:, REST_D:REST_D + 3 * GROUP] = w_ref[:, d0:d_g]
    wr_ref[:, REST_D + 3 * GROUP:REST_D + 4 * GROUP] = w_ref[:, d_z:d_z + GROUP]
    wr_ref[:, REST_G:REST_G + GATE_COLS] = jnp.zeros((wr_ref.shape[0], GATE_COLS), wr_ref.dtype)
    wr_ref[:, REST_G:REST_G + 2 * HEADS] = w_ref[:, c_a:c_z]
    wr_ref[:, REST_G + 2 * HEADS:REST_G + 2 * HEADS + GATE_RANK] = w_ref[:, d_g:d_z]


def _gate_weight(wg_ref):
    top = jnp.zeros((2 * HEADS, GROUP), F32)
    bottom = jnp.zeros((GATE_COLS - 2 * HEADS - GATE_RANK, GROUP), F32)
    return jnp.concatenate([top, wg_ref[...].astype(F32), bottom], axis=0).astype(BF16)


def _spread_spatial_bias(bs_ref, bsw_ref):
    eye = (lax.broadcasted_iota(jnp.int32, (SGU_CHUNK, SGU_CHUNK), 0)
           == lax.broadcasted_iota(jnp.int32, (SGU_CHUNK, SGU_CHUNK), 1))
    head_of_lane = lax.broadcasted_iota(jnp.int32, (SGU_CHUNK, GROUP), 1) // HEAD_DIM
    out = jnp.zeros((SGU_CHUNK, GROUP), F32)
    for hd in range(HEADS):
        col = jnp.sum(jnp.where(eye, bs_ref[hd:hd + 1, :].astype(F32), 0.0), axis=1, keepdims=True)
        out = jnp.where(head_of_lane == hd, col, out)
    bsw_ref[...] = out


def _inproj_kernel(layer, n_cast, x_ref, n1_ref, lnw_ref, lnb_ref, gb_ref, alog_ref, dtb_ref, w_ref, ws_ref, bs_ref,
                   scw_ref, dcw_ref, ea_ref, eb_ref, bd_ref, wg_ref, *rest):
    cast_in, outs, cast_out, scratch = _split_refs(rest, n_cast, 6, n_cast, 6)
    vec = {V_NORM1: _param_row(n1_ref, layer), V_LNW: _param_row(lnw_ref, layer), V_LNB: _param_row(lnb_ref, layer),
           V_GB: _param_row(gb_ref, layer), V_ALOG: _head_vec(alog_ref, layer), V_DTB: _head_vec(dtb_ref, layer)}
    ya_ref, yb_ref, cpk_ref, cg_ref, dpk_ref, dla_ref = outs
    extb_ref, extc_ref, convb_ref, convc_ref, wr_ref, bsw_ref = scratch
    _run_casts(cast_in, cast_out)

    @pl.when(pl.program_id(0) == 0)
    def _():
        extb_ref[...] = jnp.zeros(extb_ref.shape, F32)
        extc_ref[...] = jnp.zeros(extc_ref.shape, F32)
        _repack_w_in_rest(w_ref, wr_ref)
        _spread_spatial_bias(bs_ref, bsw_ref)

    _run_stages([_inproj_stages(x_ref, vec, w_ref, wr_ref, ws_ref, bsw_ref, scw_ref, dcw_ref,
                                ea_ref, eb_ref, bd_ref, wg_ref,
                                ya_ref, yb_ref, cpk_ref, cg_ref, dpk_ref, dla_ref, extb_ref, extc_ref,
                                convb_ref, convc_ref)])


def _inproj(x, n1, lnw, lnb, gb, alog, dtb, w_in_b, layer, ws, bs, scw, dcw, ea, eb, bd, wg, cast_jobs=()):
    nb, seq, _ = x.shape
    tm = SEQ_TILE
    n_tiles = seq // tm
    assert n_tiles >= CAST_BLOCKS or not cast_jobs
    cast_in_specs, cast_out_specs, cast_out_shapes = _cast_specs(cast_jobs)
    tile = lambda i: (0, i, 0)
    whole = lambda a: pl.BlockSpec(a.shape, (lambda i: (0,) * a.ndim))
    per_layer = lambda a: _layer_param(a, layer)
    outs = ((GROUP, BF16), (GROUP, BF16), (C_PACK, BF16), (GROUP, F32), (D_PACK, BF16), (GROUP, F32))
    return pl.pallas_call(
        functools.partial(_inproj_kernel, layer, len(cast_jobs)),
        grid=(n_tiles,),
        in_specs=[pl.BlockSpec((nb, tm, D_MODEL), tile),
                  whole(n1), whole(lnw), whole(lnb), whole(gb),
                  pl.BlockSpec(memory_space=pltpu.SMEM), pl.BlockSpec(memory_space=pltpu.SMEM),
                  _layer_block(w_in_b, layer),
                  per_layer(ws), per_layer(bs), per_layer(scw),
                  per_layer(dcw), whole(ea), whole(eb), whole(bd),
                  per_layer(wg)]
                 + cast_in_specs,
        out_specs=[pl.BlockSpec((nb, tm, n), tile) for n, _ in outs] + cast_out_specs,
        out_shape=[jax.ShapeDtypeStruct((nb, seq, n), dt) for n, dt in outs] + cast_out_shapes,
        scratch_shapes=[pltpu.VMEM((nb, GROUP // LANES, TAIL + tm, LANES), F32),
                        pltpu.VMEM((nb, 3 * GROUP // LANES, TAIL + tm, LANES), F32),
                        pltpu.VMEM((GROUP // LANES, tm, LANES), F32),
                        pltpu.VMEM((3 * GROUP // LANES, tm, LANES), F32),
                        pltpu.VMEM((D_MODEL, REST_COLS), BF16),
                        pltpu.VMEM((SGU_CHUNK, GROUP), F32)],
        compiler_params=pltpu.CompilerParams(dimension_semantics=("arbitrary",),
                                             vmem_limit_bytes=VMEM_LIMIT),
        name="in_proj",
    )(x, n1, lnw, lnb, gb, alog, dtb, w_in_b, ws, bs, scw, dcw, ea, eb, bd, wg,
      *[w for w, _, _ in cast_jobs])


def _chunk_masks():
    i = lax.broadcasted_iota(jnp.int32, (CHUNK, GROUP), 0)
    j = lax.broadcasted_iota(jnp.int32, (CHUNK, GROUP), 1) % CHUNK
    return i == j, i >= j, i > j


def _const_inputs():
    lane_head = np.arange(GROUP) // HEAD_DIM
    bd = (lane_head[:, None] == lane_head[None, :]).astype(np.float32)
    ltri = np.tril(np.ones((CHUNK, CHUNK), np.float32))
    return jnp.asarray(bd, BF16), jnp.asarray(ltri, BF16)


def _chunk_rows(c):
    return slice(c * CHUNK, (c + 1) * CHUNK)


def _gdn_stages(cpk_ref, cg_ref, nw_ref, bd_ref, ltri_ref, y_ref, ac_ref, bb_ref, d_ref, dec_ref, o_ref, s_ref):
    nb, tm = cpk_ref.shape[0], cpk_ref.shape[1]
    cps = tm // CHUNK
    eye_w, causal_w, strict_w = _chunk_masks()
    eye_f = eye_w.astype(F32)
    ltri = ltri_ref[...]
    bd = bd_ref[...]

    yield from _gdn_prepare([(b, c) for b in range(nb) for c in range(cps)], cps, cpk_ref, cg_ref,
                            ac_ref, bb_ref, d_ref, dec_ref, eye_w, causal_w, strict_w, eye_f, ltri, bd)

    for c in range(cps):
        for b in range(nb):
            i = b * cps + c
            s = s_ref[b]
            r = _dot(ac_ref[i], _blockdiag(s, bd))
            s_ref[b] = s * dec_ref[i, 0:1, :] + r[0:CHUNK] + bb_ref[i]
            o_ref[b, _chunk_rows(c), :] = r[CHUNK:2 * CHUNK] + d_ref[i]
        yield

    for b in range(nb):
        o = o_ref[b]
        zg = cpk_ref[b, :, 4 * GROUP:5 * GROUP].astype(F32)
        y_ref[b] = (o * lax.rsqrt(_head_mean_sq(o, bd) + EPS) * nw_ref[...] * zg).astype(y_ref.dtype)


def _gdn_prepare(items, cps, cpk_ref, cg_ref, ac_ref, bb_ref, d_ref, dec_ref,
                 eye_w, causal_w, strict_w, eye_f, ltri, bd):
    n = len(items)

    def load(col, i):
        b, c = items[i]
        return cpk_ref[b, _chunk_rows(c), col * GROUP:(col + 1) * GROUP]

    qn = [load(0, i) for i in range(n)]
    kn = [load(1, i) for i in range(n)]
    beta = [load(3, i) for i in range(n)]
    gc = [_dot_sel_lhs(ltri, cg_ref[b, _chunk_rows(c), :]) for b, c in items]
    yield
    eg = [jnp.exp(x) for x in gc]
    kb = [kn[i] * beta[i] for i in range(n)]
    eye_b = eye_f.astype(BF16)
    r1 = [_dot_nt(jnp.concatenate([kb[i], qn[i], eye_b], axis=0), _blockdiag(kn[i], bd))
          for i in range(n)]
    yield
    low, attn, g_row = [], [], []
    for i in range(n):
        g_row.append(jnp.sum(jnp.where(eye_w, gc[i], 0.0), axis=0, keepdims=True))
        decay = jnp.where(causal_w, jnp.exp(jnp.minimum(gc[i] - g_row[i], 0.0)), 0.0)
        low.append(jnp.where(strict_w, r1[i][0:CHUNK] * decay, 0.0))
        attn.append((r1[i][CHUNK:2 * CHUNK] * decay).astype(BF16))
    t = [eye_f - x for x in low]
    p = [_dot(x.astype(BF16), _blockdiag(x, bd)) for x in low]
    yield
    n_levels = CHUNK.bit_length() - 1
    for level in range(1, n_levels):
        if level < n_levels - 1:
            r2 = [_dot(jnp.concatenate([p[i], t[i]], axis=0).astype(BF16), _blockdiag(p[i], bd))
                  for i in range(n)]
            p = [x[0:CHUNK] for x in r2]
            t = [t[i] + r2[i][CHUNK:2 * CHUNK] for i in range(n)]
        else:
            t = [t[i] + _dot(t[i].astype(BF16), _blockdiag(p[i], bd)) for i in range(n)]
        yield
    gc_last = [x[CHUNK - 1:CHUNK, :] for x in gc]
    kd_t = [(r1[i][2 * CHUNK:3 * CHUNK] * jnp.exp(gc_last[i] - g_row[i])).astype(BF16) for i in range(n)]
    lhs_t = [_dot(jnp.concatenate([attn[i], kd_t[i]], axis=0), _blockdiag(t[i], bd)) for i in range(n)]
    yield
    prod = [_dot(lhs_t[i].astype(BF16),
                 jnp.concatenate([_blockdiag(kb[i] * eg[i], bd), _blockdiag(load(2, i) * beta[i], bd)], axis=1))
            for i in range(n)]
    for i, (b, c) in enumerate(items):
        j = b * cps + c
        ac_ref[j, 0:CHUNK, :] = (-prod[i][CHUNK:2 * CHUNK, 0:GROUP]).astype(BF16)
        bb_ref[j] = prod[i][CHUNK:2 * CHUNK, GROUP:2 * GROUP]
        ac_ref[j, CHUNK:2 * CHUNK, :] = (qn[i] * eg[i] - prod[i][0:CHUNK, 0:GROUP]).astype(BF16)
        d_ref[j] = prod[i][0:CHUNK, GROUP:2 * GROUP]
        dec_ref[j] = jnp.broadcast_to(jnp.exp(gc_last[i]), (TAIL, GROUP))
    yield


def _gla_stages(dpk_ref, dla_ref, nw_ref, bd_ref, ltri_ref, y_ref, o_ref, st_ref):
    nb, tm = dpk_ref.shape[0], dpk_ref.shape[1]
    cps = tm // CHUNK
    _, causal_w, _ = _chunk_masks()
    ltri = ltri_ref[...]
    bd = bd_ref[...]
    bdf = bd.astype(F32)
    mid = CHUNK // 2

    items = [(b, c) for c in range(cps) for b in range(nb)]
    n = len(items)
    q = [dpk_ref[b, _chunk_rows(c), 0:GROUP].astype(F32) * (HEAD_DIM ** -0.5) for b, c in items]
    k = [dpk_ref[b, _chunk_rows(c), GROUP:2 * GROUP].astype(F32) for b, c in items]
    v = [dpk_ref[b, _chunk_rows(c), 2 * GROUP:3 * GROUP] for b, c in items]
    gcum = [_dot_sel_lhs(ltri, dla_ref[b, _chunk_rows(c), :]) for b, c in items]
    yield
    g_mid = [x[mid:mid + 1, :] for x in gcum]
    g_last = [x[CHUNK - 1:CHUNK, :] for x in gcum]
    attn = [jnp.where(causal_w,
                      _dot_nt((q[i] * jnp.exp(gcum[i] - g_mid[i])).astype(BF16),
                              _blockdiag(k[i] * jnp.exp(g_mid[i] - gcum[i]), bd)), 0.0).astype(BF16)
            for i in range(n)]
    yield
    upd = [bdf * _dot_tn((k[i] * jnp.exp(g_last[i] - gcum[i])).astype(BF16), v[i]) for i in range(n)]
    yield
    o_intra = [_dot(attn[i], _blockdiag(v[i], bd)) for i in range(n)]
    qg = [(q[i] * jnp.exp(gcum[i])).astype(BF16) for i in range(n)]
    yield
    for i, (b, c) in enumerate(items):
        st = st_ref[b]
        o_ref[b, _chunk_rows(c), :] = o_intra[i] + _dot(qg[i], st.astype(BF16))
        dec_col = jnp.transpose(jnp.broadcast_to(jnp.exp(g_last[i]), (GROUP, GROUP)))
        st_ref[b] = st * dec_col + upd[i]
        if b == nb - 1:
            yield

    for b in range(nb):
        o = o_ref[b]
        zg = dpk_ref[b, :, 3 * GROUP:4 * GROUP].astype(F32)
        y_ref[b] = (o * lax.rsqrt(_head_mean_sq(o, bd) + EPS) * nw_ref[...] * zg).astype(y_ref.dtype)


def _tile_heads(row):
    eye = (lax.broadcasted_iota(jnp.int32, (HEAD_DIM, HEAD_DIM), 0)
           == lax.broadcasted_iota(jnp.int32, (HEAD_DIM, HEAD_DIM), 1))
    col = jnp.sum(jnp.where(eye, row, 0.0), axis=1, keepdims=True)
    hit = (lax.broadcasted_iota(jnp.int32, (HEAD_DIM, GROUP), 0)
           == lax.broadcasted_iota(jnp.int32, (HEAD_DIM, GROUP), 1) % HEAD_DIM)
    return jnp.sum(jnp.where(hit, col, 0.0), axis=0, keepdims=True)


def _recur_kernel(layer, cpk_ref, cg_ref, dpk_ref, dla_ref, nwc_raw_ref, nwd_raw_ref, bd_ref, ltri_ref,
                  yc_ref, yd_ref, ac_ref, bb_ref, d_ref, dec_ref, oc_ref, s_ref, od_ref, st_ref, nw_ref):
    nwc_ref = nw_ref.at[pl.ds(0, 1), pl.ds(0, GROUP)]
    nwd_ref = nw_ref.at[pl.ds(1, 1), pl.ds(0, GROUP)]

    @pl.when(pl.program_id(0) == 0)
    def _():
        s_ref[...] = jnp.zeros(s_ref.shape, F32)
        st_ref[...] = jnp.zeros(st_ref.shape, F32)
        nw_ref[...] = jnp.zeros(nw_ref.shape, F32)
        nw_ref[0:1, :] = _tile_heads(nwc_raw_ref[layer:layer + 1, :].astype(F32))
        nw_ref[1:2, :] = _tile_heads(nwd_raw_ref[layer:layer + 1, :].astype(F32))

    _run_stages([_gla_stages(dpk_ref, dla_ref, nwd_ref, bd_ref, ltri_ref, yd_ref, od_ref, st_ref),
                 _gdn_stages(cpk_ref, cg_ref, nwc_ref, bd_ref, ltri_ref, yc_ref,
                             ac_ref, bb_ref, d_ref, dec_ref, oc_ref, s_ref)])


def _recur(cpk, cg, dpk, dla, nwc, nwd, bd, ltri, layer):
    nb, seq, _ = cpk.shape
    tm = SEQ_TILE
    n_chunks = nb * tm // CHUNK
    tile = lambda i: (0, i, 0)
    whole = lambda a: pl.BlockSpec(a.shape, (lambda i: (0,) * a.ndim))
    out = jax.ShapeDtypeStruct((nb, seq, GROUP), BF16)
    return pl.pallas_call(
        functools.partial(_recur_kernel, layer),
        grid=(seq // tm,),
        in_specs=[pl.BlockSpec((nb, tm, C_PACK), tile),
                  pl.BlockSpec((nb, tm, GROUP), tile),
                  pl.BlockSpec((nb, tm, D_PACK), tile),
                  pl.BlockSpec((nb, tm, GROUP), tile),
                  whole(nwc), whole(nwd), whole(bd), whole(ltri)],
        out_specs=[pl.BlockSpec((nb, tm, GROUP), tile)] * 2,
        out_shape=[out, out],
        scratch_shapes=[pltpu.VMEM((n_chunks, 2 * CHUNK, GROUP), BF16),
                        pltpu.VMEM((n_chunks, CHUNK, GROUP), F32),
                        pltpu.VMEM((n_chunks, CHUNK, GROUP), F32),
                        pltpu.VMEM((n_chunks, TAIL, GROUP), F32),
                        pltpu.VMEM((nb, tm, GROUP), F32),
                        pltpu.VMEM((nb, CHUNK, GROUP), F32),
                        pltpu.VMEM((nb, tm, GROUP), F32),
                        pltpu.VMEM((nb, GROUP, GROUP), F32),
                        pltpu.VMEM((TAIL, GROUP), F32)],
        compiler_params=pltpu.CompilerParams(dimension_semantics=("arbitrary",),
                                             vmem_limit_bytes=VMEM_LIMIT),
        name="recur",
    )(cpk, cg, dpk, dla, nwc, nwd, bd, ltri)


def _outproj_ffn_kernel(final, layer, x_ref, ya_ref, yb_ref, yc_ref, yd_ref, wo_ref, n2_ref, post_ref, wgu_ref,
                        wd_ref, *rest):
    post_w = post_ref[0:1, :] if final else post_ref[layer + 1:layer + 2, :]
    if final:
        o_ref, act_ref = rest
    else:
        o_ref, h_ref, act_ref = rest
    x1 = x_ref[...]
    for idx, y_ref in enumerate((ya_ref, yb_ref, yc_ref, yd_ref)):
        x1 = x1 + _dot(y_ref[...], wo_ref[idx * GROUP:(idx + 1) * GROUP, :])
    ms = jnp.mean(x1 * x1, axis=-1, keepdims=True)
    h = (x1 * lax.rsqrt(ms + EPS) * n2_ref[layer:layer + 1, :]).astype(BF16)
    for j in range(D_FF // FF_CHUNK):
        cols = slice(j * FF_CHUNK, (j + 1) * FF_CHUNK)
        gate = _dot(h, wgu_ref[:, cols])
        up = _dot(h, wgu_ref[:, D_FF + j * FF_CHUNK:D_FF + (j + 1) * FF_CHUNK])
        act_ref[:, cols] = (_silu(gate) * up).astype(BF16)
    x2 = x1 + _dot(act_ref[...], wd_ref[...])
    if final:
        ms2 = jnp.mean(x2 * x2, axis=-1, keepdims=True)
        x2 = x2 * lax.rsqrt(ms2 + EPS) * post_w
    else:
        h_ref[...] = _rmsnorm_bf16(x2, post_w)
    o_ref[...] = x2


def _outproj_ffn(x, ys, wo, n2, post, wgu, wd, layer, final):
    m = x.shape[0]
    n_steps = m // ROW_TILE
    row = lambda i: (i, 0)
    whole = lambda a: pl.BlockSpec(a.shape, (lambda i: (0,) * a.ndim))
    out_specs = [pl.BlockSpec((ROW_TILE, D_MODEL), row)]
    out_shape = [jax.ShapeDtypeStruct((m, D_MODEL), F32)]
    if not final:
        out_specs.append(pl.BlockSpec((ROW_TILE, D_MODEL), row))
        out_shape.append(jax.ShapeDtypeStruct((m, D_MODEL), BF16))
    return pl.pallas_call(
        functools.partial(_outproj_ffn_kernel, final, layer),
        grid=(n_steps,),
        in_specs=[pl.BlockSpec((ROW_TILE, D_MODEL), row)]
                 + [pl.BlockSpec((ROW_TILE, GROUP), row)] * 4
                 + [_resident(wo), whole(n2), whole(post), _resident(wgu), _resident(wd)],
        out_specs=out_specs,
        out_shape=out_shape,
        scratch_shapes=[pltpu.VMEM((ROW_TILE, D_FF), BF16)],
        compiler_params=pltpu.CompilerParams(dimension_semantics=("arbitrary",),
                                             vmem_limit_bytes=VMEM_LIMIT),
        name="outproj_ffn",
    )(x, *ys, wo, n2, post, wgu, wd)


def _gate_selectors():
    ea = np.zeros((GATE_COLS, GROUP), np.float32)
    eb = np.zeros((GATE_COLS, GROUP), np.float32)
    for h in range(HEADS):
        ea[h, h * HEAD_DIM:(h + 1) * HEAD_DIM] = 1.0
        eb[HEADS + h, h * HEAD_DIM:(h + 1) * HEAD_DIM] = 1.0
    return jnp.asarray(ea, BF16), jnp.asarray(eb, BF16)


def kernel(x, norm1_w, w_in, sgu_ln_w, sgu_ln_b, sgu_w_spatial, sgu_b_spatial, sc_conv_w, dn_conv_w, dn_a_log, dn_dt_bias, dn_norm_w, gla_w_gate2, gla_gate_bias, gla_norm_w, w_out, norm2_w, w_gate_up, w_down, final_norm_w):
    bsz, seq, d = x.shape
    depth = w_in.shape[0]
    assert seq % ROW_TILE == 0 and seq % SEQ_TILE == 0 and d == D_MODEL
    m = bsz * seq
    xf = x.reshape(m, d)
    bd, ltri = _const_inputs()
    ea, eb = _gate_selectors()
    w_in_b = w_in.astype(BF16)
    ffn_jobs = lambda layer: [(w_out, layer, D_MODEL), (w_gate_up, layer, 2 * D_FF), (w_down, layer, D_MODEL)]
    inproj_params = (sgu_w_spatial, sgu_b_spatial, sc_conv_w, dn_conv_w, ea, eb, bd, gla_w_gate2)
    inproj_vecs = (norm1_w, sgu_ln_w, sgu_ln_b, gla_gate_bias, dn_a_log, dn_dt_bias)
    x_in = xf.reshape(bsz, seq, d)
    for l in range(depth):
        ya, yb, cpk, cg, dpk, dla, wo_b, wgu_b, wd_b = _inproj(x_in, *inproj_vecs, w_in_b, l, *inproj_params,
                                                               cast_jobs=ffn_jobs(l))
        yc, yd = _recur(cpk, cg, dpk, dla, dn_norm_w, gla_norm_w, bd, ltri, l)
        last = l == depth - 1
        outs = _outproj_ffn(xf, tuple(y.reshape(m, GROUP) for y in (ya, yb, yc, yd)),
                            wo_b, norm2_w, final_norm_w[None, :] if last else norm1_w, wgu_b, wd_b,
                            layer=l, final=last)
        xf = outs[0]
        if not last:
            x_in = outs[1].reshape(bsz, seq, d)
    return xf.reshape(bsz, seq, d)
```

```python
import functools

import numpy as np
import jax
import jax.numpy as jnp
from jax import lax
from jax.experimental import pallas as pl
from jax.experimental.pallas import tpu as pltpu

F32 = jnp.float32
BF16 = jnp.bfloat16

D_MODEL = 1024
GROUP = 256
HEADS = 4
HEAD_DIM = 64
SGU_CHUNK = 128
SC_WIDTH = 3
DN_CONV_WIDTH = 4
CHUNK = 64
GATE_RANK = 16
GATE_TEMP = 16.0
D_FF = 2816
EPS = 1e-6
GATE_COLS = 128
TAIL = 8

ROW_TILE = 1024
SEQ_TILE = 512
FF_CHUNK = 256
VMEM_LIMIT = 60 * 1024 * 1024

MAIN_A = 0
MAIN_B = MAIN_A + 2 * GROUP
MAIN_C = MAIN_B + 3 * GROUP
MAIN_COLS = MAIN_C + 3 * GROUP
REST_CZ = 0
REST_D = REST_CZ + GROUP
REST_G = REST_D + 4 * GROUP
REST_COLS = REST_G + GATE_COLS
C_PACK = 5 * GROUP
D_PACK = 4 * GROUP


def _dot(a, b):
    return jnp.dot(a, b, preferred_element_type=F32)


def _dot_nt(a, b):
    return lax.dot_general(a, b, (((1,), (1,)), ((), ())), preferred_element_type=F32)


def _dot_tn(a, b):
    return lax.dot_general(a, b, (((0,), (0,)), ((), ())), preferred_element_type=F32)


def _split2(x):
    hi = x.astype(BF16)
    lo = (x - hi.astype(F32)).astype(BF16)
    return hi, lo


def _dot_sel_rhs(x, sel):
    hi, lo = _split2(x)
    return _dot(hi, sel) + _dot(lo, sel)


def _dot_sel_lhs(sel, x):
    hi, lo = _split2(x)
    return _dot(sel, hi) + _dot(sel, lo)


def _softplus(x):
    return jnp.maximum(x, 0.0) + jnp.log(1.0 + jnp.exp(-jnp.abs(x)))


def _sigmoid(x):
    return 0.5 + 0.5 * jnp.tanh(0.5 * x)


def _silu(x):
    h = 0.5 * x
    return h + h * jnp.tanh(h)


def _gelu_tanh(x):
    h = 0.5 * x
    return h + h * jnp.tanh(x * (0.7978845608028654 + (0.7978845608028654 * 0.044715) * (x * x)))


def _rmsnorm_bf16(x, w):
    ms = jnp.mean(x * x, axis=-1, keepdims=True)
    return (x * lax.rsqrt(ms + EPS) * w).astype(BF16)


def _blockdiag(x, mask):
    xb = x.astype(BF16)
    return jnp.concatenate([xb] * HEADS, axis=0) * mask


def _head_mean_sq(x, bones):
    return _dot((x * x).astype(BF16), bones) * (1.0 / HEAD_DIM)


LANES = 128


def _causal_conv(x, ext_ref, out_ref, cw_ref, width):
    tm, ch = x.shape
    half = tm // 2
    n_slabs = ch // LANES
    for s in range(n_slabs):
        ext_ref[s, TAIL:TAIL + tm, :] = x[:, s * LANES:(s + 1) * LANES]
    for s in range(n_slabs):
        lanes = slice(s * LANES, (s + 1) * LANES)
        for parity in (0, 1):
            acc = None
            for j in range(width):
                tap = (ext_ref[s, pl.ds(TAIL + parity - j, half, stride=2), :]
                       * cw_ref[width - 1 - j:width - j, lanes])
                acc = tap if acc is None else acc + tap
            out_ref[s, pl.ds(parity, half, stride=2), :] = acc
    for s in range(n_slabs):
        ext_ref[s, 0:TAIL, :] = x[tm - TAIL:tm, s * LANES:(s + 1) * LANES]
    return jnp.concatenate([out_ref[s] for s in range(n_slabs)], axis=1)


def _layer_param(stacked, layer):
    zeros = (0,) * (stacked.ndim - 1)
    return pl.BlockSpec((None,) + stacked.shape[1:], lambda i: (layer,) + zeros)


def _layer_block(stacked, layer):
    zeros = (0,) * (stacked.ndim - 1)
    return pl.BlockSpec((None,) + stacked.shape[1:], lambda i: (layer,) + zeros,
                        pipeline_mode=pl.Buffered(1))


def _resident(a):
    return pl.BlockSpec(a.shape, lambda i: (0, 0), pipeline_mode=pl.Buffered(1))


CAST_BLOCKS = 16
(V_NORM1, V_LNW, V_LNB, V_ALOG, V_DTB, V_GB) = range(6)


def _vec(table, key):
    return table[key]()


def _param_row(ref, layer):
    return lambda: ref[layer:layer + 1, :].astype(F32)


def _head_vec(ref, layer):
    def load():
        lane = lax.broadcasted_iota(jnp.int32, (1, GATE_COLS), 1)
        out = jnp.zeros((1, GATE_COLS), F32)
        for hd in range(HEADS):
            out = jnp.where(lane == hd, ref[layer, hd].astype(F32), out)
        return out
    return load


def _cast_specs(cast_jobs):
    last = CAST_BLOCKS - 1
    in_specs, out_specs, out_shapes = [], [], []
    for stacked, layer, n_cols in cast_jobs:
        rows = stacked.shape[1]
        assert rows % (16 * CAST_BLOCKS) == 0 and n_cols % 128 == 0
        blk = rows // CAST_BLOCKS
        in_specs.append(pl.BlockSpec((None, blk, n_cols),
                                     lambda i, layer=layer: (layer, jnp.minimum(i, last), 0)))
        out_specs.append(pl.BlockSpec((blk, n_cols), lambda i: (jnp.minimum(i, last), 0)))
        out_shapes.append(jax.ShapeDtypeStruct((rows, n_cols), BF16))
    return in_specs, out_specs, out_shapes


def _run_casts(cast_in, cast_out):
    for src, dst in zip(cast_in, cast_out):
        dst[...] = src[...].astype(dst.dtype)


def _split_refs(refs, *counts):
    assert sum(counts) == len(refs)
    groups, pos = [], 0
    for c in counts:
        groups.append(tuple(refs[pos:pos + c]))
        pos += c
    return groups


def _run_stages(stages):
    stages = list(stages)
    while stages:
        for gen in list(stages):
            try:
                next(gen)
            except StopIteration:
                stages.remove(gen)


def _inproj_stages(x_ref, vec, w_ref, wr_ref, ws_ref, bs_ref, scw_ref,
                   dcw_ref, ea_ref, eb_ref, bd_ref, wg_ref,
                   ya_ref, yb_ref, cpk_ref, cg_ref, dpk_ref, dla_ref, extb_ref, extc_ref, convb_ref, convc_ref):
    nb, tm = x_ref.shape[0], x_ref.shape[1]
    m = nb * tm
    batch_rows = [slice(b * tm, (b + 1) * tm) for b in range(nb)]

    x = x_ref[...].reshape(m, D_MODEL)
    if x.dtype == BF16:
        h = x
    else:
        h = _rmsnorm_bf16(x, _vec(vec, V_NORM1))
    bd = bd_ref[...]

    def project(ref, col0, n_slabs):
        parts = []
        for c in range(col0, col0 + n_slabs * GROUP, GROUP):
            parts.append(_dot(h, ref[:, c:c + GROUP]))
            yield
        return jnp.concatenate(parts, axis=1)

    pc = yield from project(w_ref, MAIN_C, 3)
    pcz = yield from project(wr_ref, REST_CZ, 1)
    pg = _dot(h, wr_ref[:, REST_G:REST_G + GATE_COLS])
    pa = yield from project(w_ref, MAIN_A, 2)

    qkv = jnp.concatenate(
        [_silu(_causal_conv(pc[rows, 0:3 * GROUP], extc_ref.at[b], convc_ref, dcw_ref, DN_CONV_WIDTH))
         for b, rows in enumerate(batch_rows)], axis=0)
    q = qkv[:, 0:GROUP]
    k = qkv[:, GROUP:2 * GROUP]
    pb = yield from project(w_ref, MAIN_B, 3)
    qn = q * lax.rsqrt(_dot((q * q).astype(BF16), bd) + EPS) * (HEAD_DIM ** -0.5)
    kn = k * lax.rsqrt(_dot((k * k).astype(BF16), bd) + EPS)
    yield
    g_narrow = -jnp.exp(_vec(vec, V_ALOG)) * _softplus(pg + _vec(vec, V_DTB))
    g_wide = _dot_sel_rhs(g_narrow, ea_ref[...])
    beta = _dot(_sigmoid(pg).astype(BF16), eb_ref[...])
    zg_c = _silu(pcz)
    pre = _dot(pg.astype(BF16), _gate_weight(wg_ref)) + _vec(vec, V_GB)
    log_a = -_softplus(-pre) * (1.0 / GATE_TEMP)
    yield

    tri = (lax.broadcasted_iota(jnp.int32, (SGU_CHUNK, SGU_CHUNK), 0)
           >= lax.broadcasted_iota(jnp.int32, (SGU_CHUNK, SGU_CHUNK), 1))
    head_of_lane = lax.broadcasted_iota(jnp.int32, (SGU_CHUNK, GROUP), 1) // HEAD_DIM
    ws = jnp.concatenate([jnp.where(tri, ws_ref[hd], 0.0).astype(BF16) for hd in range(HEADS)],
                         axis=1)
    lnw, lnb = _vec(vec, V_LNW), _vec(vec, V_LNB)
    for b in range(nb):
        for c in range(tm // SGU_CHUNK):
            rows = slice(b * tm + c * SGU_CHUNK, b * tm + (c + 1) * SGU_CHUNK)
            u = _gelu_tanh(pa[rows, 0:GROUP])
            v = _gelu_tanh(pa[rows, GROUP:2 * GROUP])
            mu = jnp.mean(v, axis=-1, keepdims=True)
            vc = v - mu
            var = jnp.mean(vc * vc, axis=-1, keepdims=True)
            vn = vc * lax.rsqrt(var + EPS) * lnw + lnb
            vn_heads = jnp.concatenate([jnp.where(head_of_lane == hd, vn, 0.0).astype(BF16)
                                        for hd in range(HEADS)], axis=0)
            mixed = bs_ref[...] + _dot(ws, vn_heads)
            ya_ref[b, c * SGU_CHUNK:(c + 1) * SGU_CHUNK, :] = (u * mixed).astype(ya_ref.dtype)
            yield

    pd = yield from project(wr_ref, REST_D, 4)

    for b, rows in enumerate(batch_rows):
        conv_b = _causal_conv(pb[rows, GROUP:2 * GROUP] * pb[rows, 2 * GROUP:3 * GROUP],
                              extb_ref.at[b], convb_ref, scw_ref, SC_WIDTH)
        yb_ref[b] = (pb[rows, 0:GROUP] * conv_b).astype(yb_ref.dtype)

    for b, rows in enumerate(batch_rows):
        cg_ref[b] = g_wide[rows]
        cpk_ref[b, :, 0:GROUP] = qn[rows].astype(BF16)
        cpk_ref[b, :, GROUP:2 * GROUP] = kn[rows].astype(BF16)
        cpk_ref[b, :, 2 * GROUP:3 * GROUP] = qkv[rows, 2 * GROUP:3 * GROUP].astype(BF16)
        cpk_ref[b, :, 3 * GROUP:4 * GROUP] = beta[rows].astype(BF16)
        cpk_ref[b, :, 4 * GROUP:5 * GROUP] = zg_c[rows].astype(BF16)
        dla_ref[b] = log_a[rows]
        dpk_ref[b, :, 0:3 * GROUP] = pd[rows, 0:3 * GROUP].astype(BF16)
        dpk_ref[b, :, 3 * GROUP:4 * GROUP] = _silu(pd[rows, 3 * GROUP:4 * GROUP]).astype(BF16)


def _repack_w_in_rest(w_ref, wr_ref):
    c_a = MAIN_COLS
    c_z = c_a + 2 * HEADS
    d0 = c_z + GROUP
    d_g = d0 + 3 * GROUP
    d_z = d_g + GATE_RANK
    wr_ref[:, REST_CZ:REST_CZ + GROUP] = w_ref[:, c_z:d0]
    wr_ref[:, REST_D:REST_D + 3 * GROUP] = w_ref[:, d0:d_g]
    wr_ref[:, REST_D + 3 * GROUP:REST_D + 4 * GROUP] = w_ref[:, d_z:d_z + GROUP]
    wr_ref[:, REST_G:REST_G + GATE_COLS] = jnp.zeros((wr_ref.shape[0], GATE_COLS), wr_ref.dtype)
    wr_ref[:, REST_G:REST_G + 2 * HEADS] = w_ref[:, c_a:c_z]
    wr_ref[:, REST_G + 2 * HEADS:REST_G + 2 * HEADS + GATE_RANK] = w_ref[:, d_g:d_z]


def _gate_weight(wg_ref):
    top = jnp.zeros((2 * HEADS, GROUP), F32)
    bottom = jnp.zeros((GATE_COLS - 2 * HEADS - GATE_RANK, GROUP), F32)
    return jnp.concatenate([top, wg_ref[...].astype(F32), bottom], axis=0).astype(BF16)


def _spread_spatial_bias(bs_ref, bsw_ref):
    eye = (lax.broadcasted_iota(jnp.int32, (SGU_CHUNK, SGU_CHUNK), 0)
           == lax.broadcasted_iota(jnp.int32, (SGU_CHUNK, SGU_CHUNK), 1))
    head_of_lane = lax.broadcasted_iota(jnp.int32, (SGU_CHUNK, GROUP), 1) // HEAD_DIM
    out = jnp.zeros((SGU_CHUNK, GROUP), F32)
    for hd in range(HEADS):
        col = jnp.sum(jnp.where(eye, bs_ref[hd:hd + 1, :].astype(F32), 0.0), axis=1, keepdims=True)
        out = jnp.where(head_of_lane == hd, col, out)
    bsw_ref[...] = out


def _inproj_kernel(layer, n_cast, x_ref, n1_ref, lnw_ref, lnb_ref, gb_ref, alog_ref, dtb_ref, w_ref, ws_ref, bs_ref,
                   scw_ref, dcw_ref, ea_ref, eb_ref, bd_ref, wg_ref, *rest):
    cast_in, outs, cast_out, scratch = _split_refs(rest, n_cast, 6, n_cast, 6)
    vec = {V_NORM1: _param_row(n1_ref, layer), V_LNW: _param_row(lnw_ref, layer), V_LNB: _param_row(lnb_ref, layer),
           V_GB: _param_row(gb_ref, layer), V_ALOG: _head_vec(alog_ref, layer), V_DTB: _head_vec(dtb_ref, layer)}
    ya_ref, yb_ref, cpk_ref, cg_ref, dpk_ref, dla_ref = outs
    extb_ref, extc_ref, convb_ref, convc_ref, wr_ref, bsw_ref = scratch
    _run_casts(cast_in, cast_out)

    @pl.when(pl.program_id(0) == 0)
    def _():
        extb_ref[...] = jnp.zeros(extb_ref.shape, F32)
        extc_ref[...] = jnp.zeros(extc_ref.shape, F32)
        _repack_w_in_rest(w_ref, wr_ref)
        _spread_spatial_bias(bs_ref, bsw_ref)

    _run_stages([_inproj_stages(x_ref, vec, w_ref, wr_ref, ws_ref, bsw_ref, scw_ref, dcw_ref,
                                ea_ref, eb_ref, bd_ref, wg_ref,
                                ya_ref, yb_ref, cpk_ref, cg_ref, dpk_ref, dla_ref, extb_ref, extc_ref,
                                convb_ref, convc_ref)])


def _inproj(x, n1, lnw, lnb, gb, alog, dtb, w_in_b, layer, ws, bs, scw, dcw, ea, eb, bd, wg, cast_jobs=()):
    nb, seq, _ = x.shape
    tm = SEQ_TILE
    n_tiles = seq // tm
    assert n_tiles >= CAST_BLOCKS or not cast_jobs
    cast_in_specs, cast_out_specs, cast_out_shapes = _cast_specs(cast_jobs)
    tile = lambda i: (0, i, 0)
    whole = lambda a: pl.BlockSpec(a.shape, (lambda i: (0,) * a.ndim))
    per_layer = lambda a: _layer_param(a, layer)
    outs = ((GROUP, BF16), (GROUP, BF16), (C_PACK, BF16), (GROUP, F32), (D_PACK, BF16), (GROUP, F32))
    return pl.pallas_call(
        functools.partial(_inproj_kernel, layer, len(cast_jobs)),
        grid=(n_tiles,),
        in_specs=[pl.BlockSpec((nb, tm, D_MODEL), tile),
                  whole(n1), whole(lnw), whole(lnb), whole(gb),
                  pl.BlockSpec(memory_space=pltpu.SMEM), pl.BlockSpec(memory_space=pltpu.SMEM),
                  _layer_block(w_in_b, layer),
                  per_layer(ws), per_layer(bs), per_layer(scw),
                  per_layer(dcw), whole(ea), whole(eb), whole(bd),
                  per_layer(wg)]
                 + cast_in_specs,
        out_specs=[pl.BlockSpec((nb, tm, n), tile) for n, _ in outs] + cast_out_specs,
        out_shape=[jax.ShapeDtypeStruct((nb, seq, n), dt) for n, dt in outs] + cast_out_shapes,
        scratch_shapes=[pltpu.VMEM((nb, GROUP // LANES, TAIL + tm, LANES), F32),
                        pltpu.VMEM((nb, 3 * GROUP // LANES, TAIL + tm, LANES), F32),
                        pltpu.VMEM((GROUP // LANES, tm, LANES), F32),
                        pltpu.VMEM((3 * GROUP // LANES, tm, LANES), F32),
                        pltpu.VMEM((D_MODEL, REST_COLS), BF16),
                        pltpu.VMEM((SGU_CHUNK, GROUP), F32)],
        compiler_params=pltpu.CompilerParams(dimension_semantics=("arbitrary",),
                                             vmem_limit_bytes=VMEM_LIMIT),
        name="in_proj",
    )(x, n1, lnw, lnb, gb, alog, dtb, w_in_b, ws, bs, scw, dcw, ea, eb, bd, wg,
      *[w for w, _, _ in cast_jobs])


def _chunk_masks():
    i = lax.broadcasted_iota(jnp.int32, (CHUNK, GROUP), 0)
    j = lax.broadcasted_iota(jnp.int32, (CHUNK, GROUP), 1) % CHUNK
    return i == j, i >= j, i > j


def _const_inputs():
    lane_head = np.arange(GROUP) // HEAD_DIM
    bd = (lane_head[:, None] == lane_head[None, :]).astype(np.float32)
    ltri = np.tril(np.ones((CHUNK, CHUNK), np.float32))
    return jnp.asarray(bd, BF16), jnp.asarray(ltri, BF16)


def _chunk_rows(c):
    return slice(c * CHUNK, (c + 1) * CHUNK)


def _gdn_stages(cpk_ref, cg_ref, nw_ref, bd_ref, ltri_ref, y_ref, ac_ref, bb_ref, d_ref, dec_ref, o_ref, s_ref):
    nb, tm = cpk_ref.shape[0], cpk_ref.shape[1]
    cps = tm // CHUNK
    eye_w, causal_w, strict_w = _chunk_masks()
    eye_f = eye_w.astype(F32)
    ltri = ltri_ref[...]
    bd = bd_ref[...]

    yield from _gdn_prepare([(b, c) for b in range(nb) for c in range(cps)], cps, cpk_ref, cg_ref,
                            ac_ref, bb_ref, d_ref, dec_ref, eye_w, causal_w, strict_w, eye_f, ltri, bd)

    for c in range(cps):
        for b in range(nb):
            i = b * cps + c
            s = s_ref[b]
            r = _dot(ac_ref[i], _blockdiag(s, bd))
            s_ref[b] = s * dec_ref[i, 0:1, :] + r[0:CHUNK] + bb_ref[i]
            o_ref[b, _chunk_rows(c), :] = r[CHUNK:2 * CHUNK] + d_ref[i]
        yield

    for b in range(nb):
        o = o_ref[b]
        zg = cpk_ref[b, :, 4 * GROUP:5 * GROUP].astype(F32)
        y_ref[b] = (o * lax.rsqrt(_head_mean_sq(o, bd) + EPS) * nw_ref[...] * zg).astype(y_ref.dtype)


def _gdn_prepare(items, cps, cpk_ref, cg_ref, ac_ref, bb_ref, d_ref, dec_ref,
                 eye_w, causal_w, strict_w, eye_f, ltri, bd):
    n = len(items)

    def load(col, i):
        b, c = items[i]
        return cpk_ref[b, _chunk_rows(c), col * GROUP:(col + 1) * GROUP]

    qn = [load(0, i) for i in range(n)]
    kn = [load(1, i) for i in range(n)]
    beta = [load(3, i) for i in range(n)]
    gc = [_dot_sel_lhs(ltri, cg_ref[b, _chunk_rows(c), :]) for b, c in items]
    yield
    eg = [jnp.exp(x) for x in gc]
    kb = [kn[i] * beta[i] for i in range(n)]
    eye_b = eye_f.astype(BF16)
    r1 = [_dot_nt(jnp.concatenate([kb[i], qn[i], eye_b], axis=0), _blockdiag(kn[i], bd))
          for i in range(n)]
    yield
    low, attn, g_row = [], [], []
    for i in range(n):
        g_row.append(jnp.sum(jnp.where(eye_w, gc[i], 0.0), axis=0, keepdims=True))
        decay = jnp.where(causal_w, jnp.exp(jnp.minimum(gc[i] - g_row[i], 0.0)), 0.0)
        low.append(jnp.where(strict_w, r1[i][0:CHUNK] * decay, 0.0))
        attn.append((r1[i][CHUNK:2 * CHUNK] * decay).astype(BF16))
    t = [eye_f - x for x in low]
    p = [_dot(x.astype(BF16), _blockdiag(x, bd)) for x in low]
    yield
    n_levels = CHUNK.bit_length() - 1
    for level in range(1, n_levels):
        if level < n_levels - 1:
            r2 = [_dot(jnp.concatenate([p[i], t[i]], axis=0).astype(BF16), _blockdiag(p[i], bd))
                  for i in range(n)]
            p = [x[0:CHUNK] for x in r2]
            t = [t[i] + r2[i][CHUNK:2 * CHUNK] for i in range(n)]
        else:
            t = [t[i] + _dot(t[i].astype(BF16), _blockdiag(p[i], bd)) for i in range(n)]
        yield
    gc_last = [x[CHUNK - 1:CHUNK, :] for x in gc]
    kd_t = [(r1[i][2 * CHUNK:3 * CHUNK] * jnp.exp(gc_last[i] - g_row[i])).astype(BF16) for i in range(n)]
    lhs_t = [_dot(jnp.concatenate([attn[i], kd_t[i]], axis=0), _blockdiag(t[i], bd)) for i in range(n)]
    yield
    prod = [_dot(lhs_t[i].astype(BF16),
                 jnp.concatenate([_blockdiag(kb[i] * eg[i], bd), _blockdiag(load(2, i) * beta[i], bd)], axis=1))
            for i in range(n)]
    for i, (b, c) in enumerate(items):
        j = b * cps + c
        ac_ref[j, 0:CHUNK, :] = (-prod[i][CHUNK:2 * CHUNK, 0:GROUP]).astype(BF16)
        bb_ref[j] = prod[i][CHUNK:2 * CHUNK, GROUP:2 * GROUP]
        ac_ref[j, CHUNK:2 * CHUNK, :] = (qn[i] * eg[i] - prod[i][0:CHUNK, 0:GROUP]).astype(BF16)
        d_ref[j] = prod[i][0:CHUNK, GROUP:2 * GROUP]
        dec_ref[j] = jnp.broadcast_to(jnp.exp(gc_last[i]), (TAIL, GROUP))
    yield


def _gla_stages(dpk_ref, dla_ref, nw_ref, bd_ref, ltri_ref, y_ref, o_ref, st_ref):
    nb, tm = dpk_ref.shape[0], dpk_ref.shape[1]
    cps = tm // CHUNK
    _, causal_w, _ = _chunk_masks()
    ltri = ltri_ref[...]
    bd = bd_ref[...]
    bdf = bd.astype(F32)
    mid = CHUNK // 2

    items = [(b, c) for c in range(cps) for b in range(nb)]
    n = len(items)
    q = [dpk_ref[b, _chunk_rows(c), 0:GROUP].astype(F32) * (HEAD_DIM ** -0.5) for b, c in items]
    k = [dpk_ref[b, _chunk_rows(c), GROUP:2 * GROUP].astype(F32) for b, c in items]
    v = [dpk_ref[b, _chunk_rows(c), 2 * GROUP:3 * GROUP] for b, c in items]
    gcum = [_dot_sel_lhs(ltri, dla_ref[b, _chunk_rows(c), :]) for b, c in items]
    yield
    g_mid = [x[mid:mid + 1, :] for x in gcum]
    g_last = [x[CHUNK - 1:CHUNK, :] for x in gcum]
    attn = [jnp.where(causal_w,
                      _dot_nt((q[i] * jnp.exp(gcum[i] - g_mid[i])).astype(BF16),
                              _blockdiag(k[i] * jnp.exp(g_mid[i] - gcum[i]), bd)), 0.0).astype(BF16)
            for i in range(n)]
    yield
    upd = [bdf * _dot_tn((k[i] * jnp.exp(g_last[i] - gcum[i])).astype(BF16), v[i]) for i in range(n)]
    yield
    o_intra = [_dot(attn[i], _blockdiag(v[i], bd)) for i in range(n)]
    qg = [(q[i] * jnp.exp(gcum[i])).astype(BF16) for i in range(n)]
    yield
    for i, (b, c) in enumerate(items):
        st = st_ref[b]
        o_ref[b, _chunk_rows(c), :] = o_intra[i] + _dot(qg[i], st.astype(BF16))
        dec_col = jnp.transpose(jnp.broadcast_to(jnp.exp(g_last[i]), (GROUP, GROUP)))
        st_ref[b] = st * dec_col + upd[i]
        if b == nb - 1:
            yield

    for b in range(nb):
        o = o_ref[b]
        zg = dpk_ref[b, :, 3 * GROUP:4 * GROUP].astype(F32)
        y_ref[b] = (o * lax.rsqrt(_head_mean_sq(o, bd) + EPS) * nw_ref[...] * zg).astype(y_ref.dtype)


def _tile_heads(row):
    eye = (lax.broadcasted_iota(jnp.int32, (HEAD_DIM, HEAD_DIM), 0)
           == lax.broadcasted_iota(jnp.int32, (HEAD_DIM, HEAD_DIM), 1))
    col = jnp.sum(jnp.where(eye, row, 0.0), axis=1, keepdims=True)
    hit = (lax.broadcasted_iota(jnp.int32, (HEAD_DIM, GROUP), 0)
           == lax.broadcasted_iota(jnp.int32, (HEAD_DIM, GROUP), 1) % HEAD_DIM)
    return jnp.sum(jnp.where(hit, col, 0.0), axis=0, keepdims=True)


def _recur_kernel(layer, cpk_ref, cg_ref, dpk_ref, dla_ref, nwc_raw_ref, nwd_raw_ref, bd_ref, ltri_ref,
                  yc_ref, yd_ref, ac_ref, bb_ref, d_ref, dec_ref, oc_ref, s_ref, od_ref, st_ref, nw_ref):
    nwc_ref = nw_ref.at[pl.ds(0, 1), pl.ds(0, GROUP)]
    nwd_ref = nw_ref.at[pl.ds(1, 1), pl.ds(0, GROUP)]

    @pl.when(pl.program_id(0) == 0)
    def _():
        s_ref[...] = jnp.zeros(s_ref.shape, F32)
        st_ref[...] = jnp.zeros(st_ref.shape, F32)
        nw_ref[...] = jnp.zeros(nw_ref.shape, F32)
        nw_ref[0:1, :] = _tile_heads(nwc_raw_ref[layer:layer + 1, :].astype(F32))
        nw_ref[1:2, :] = _tile_heads(nwd_raw_ref[layer:layer + 1, :].astype(F32))

    _run_stages([_gla_stages(dpk_ref, dla_ref, nwd_ref, bd_ref, ltri_ref, yd_ref, od_ref, st_ref),
                 _gdn_stages(cpk_ref, cg_ref, nwc_ref, bd_ref, ltri_ref, yc_ref,
                             ac_ref, bb_ref, d_ref, dec_ref, oc_ref, s_ref)])


def _recur(cpk, cg, dpk, dla, nwc, nwd, bd, ltri, layer):
    nb, seq, _ = cpk.shape
    tm = SEQ_TILE
    n_chunks = nb * tm // CHUNK
    tile = lambda i: (0, i, 0)
    whole = lambda a: pl.BlockSpec(a.shape, (lambda i: (0,) * a.ndim))
    out = jax.ShapeDtypeStruct((nb, seq, GROUP), BF16)
    return pl.pallas_call(
        functools.partial(_recur_kernel, layer),
        grid=(seq // tm,),
        in_specs=[pl.BlockSpec((nb, tm, C_PACK), tile),
                  pl.BlockSpec((nb, tm, GROUP), tile),
                  pl.BlockSpec((nb, tm, D_PACK), tile),
                  pl.BlockSpec((nb, tm, GROUP), tile),
                  whole(nwc), whole(nwd), whole(bd), whole(ltri)],
        out_specs=[pl.BlockSpec((nb, tm, GROUP), tile)] * 2,
        out_shape=[out, out],
        scratch_shapes=[pltpu.VMEM((n_chunks, 2 * CHUNK, GROUP), BF16),
                        pltpu.VMEM((n_chunks, CHUNK, GROUP), F32),
                        pltpu.VMEM((n_chunks, CHUNK, GROUP), F32),
                        pltpu.VMEM((n_chunks, TAIL, GROUP), F32),
                        pltpu.VMEM((nb, tm, GROUP), F32),
                        pltpu.VMEM((nb, CHUNK, GROUP), F32),
                        pltpu.VMEM((nb, tm, GROUP), F32),
                        pltpu.VMEM((nb, GROUP, GROUP), F32),
                        pltpu.VMEM((TAIL, GROUP), F32)],
        compiler_params=pltpu.CompilerParams(dimension_semantics=("arbitrary",),
                                             vmem_limit_bytes=VMEM_LIMIT),
        name="recur",
    )(cpk, cg, dpk, dla, nwc, nwd, bd, ltri)


def _outproj_ffn_kernel(final, layer, x_ref, ya_ref, yb_ref, yc_ref, yd_ref, wo_ref, n2_ref, post_ref, wgu_ref,
                        wd_ref, *rest):
    post_w = post_ref[...].reshape(1, D_MODEL) if final else post_ref[layer + 1:layer + 2, :]
    if final:
        o_ref, act_ref = rest
    else:
        o_ref, h_ref, act_ref = rest
    x1 = x_ref[...]
    for idx, y_ref in enumerate((ya_ref, yb_ref, yc_ref, yd_ref)):
        x1 = x1 + _dot(y_ref[...], wo_ref[idx * GROUP:(idx + 1) * GROUP, :])
    ms = jnp.mean(x1 * x1, axis=-1, keepdims=True)
    h = (x1 * lax.rsqrt(ms + EPS) * n2_ref[layer:layer + 1, :]).astype(BF16)
    for j in range(D_FF // FF_CHUNK):
        cols = slice(j * FF_CHUNK, (j + 1) * FF_CHUNK)
        gate = _dot(h, wgu_ref[:, cols])
        up = _dot(h, wgu_ref[:, D_FF + j * FF_CHUNK:D_FF + (j + 1) * FF_CHUNK])
        act_ref[:, cols] = (_silu(gate) * up).astype(BF16)
    x2 = x1 + _dot(act_ref[...], wd_ref[...])
    if final:
        ms2 = jnp.mean(x2 * x2, axis=-1, keepdims=True)
        x2 = x2 * lax.rsqrt(ms2 + EPS) * post_w
    else:
        h_ref[...] = _rmsnorm_bf16(x2, post_w)
    o_ref[...] = x2


def _outproj_ffn(x, ys, wo, n2, post, wgu, wd, layer, final):
    m = x.shape[0]
    n_steps = m // ROW_TILE
    row = lambda i: (i, 0)
    whole = lambda a: pl.BlockSpec(a.shape, (lambda i: (0,) * a.ndim))
    out_specs = [pl.BlockSpec((ROW_TILE, D_MODEL), row)]
    out_shape = [jax.ShapeDtypeStruct((m, D_MODEL), F32)]
    if not final:
        out_specs.append(pl.BlockSpec((ROW_TILE, D_MODEL), row))
        out_shape.append(jax.ShapeDtypeStruct((m, D_MODEL), BF16))
    return pl.pallas_call(
        functools.partial(_outproj_ffn_kernel, final, layer),
        grid=(n_steps,),
        in_specs=[pl.BlockSpec((ROW_TILE, D_MODEL), row)]
                 + [pl.BlockSpec((ROW_TILE, GROUP), row)] * 4
                 + [_resident(wo), whole(n2), whole(post), _resident(wgu), _resident(wd)],
        out_specs=out_specs,
        out_shape=out_shape,
        scratch_shapes=[pltpu.VMEM((ROW_TILE, D_FF), BF16)],
        compiler_params=pltpu.CompilerParams(dimension_semantics=("arbitrary",),
                                             vmem_limit_bytes=VMEM_LIMIT),
        name="outproj_ffn",
    )(x, *ys, wo, n2, post, wgu, wd)


def _gate_selectors():
    ea = np.zeros((GATE_COLS, GROUP), np.float32)
    eb = np.zeros((GATE_COLS, GROUP), np.float32)
    for h in range(HEADS):
        ea[h, h * HEAD_DIM:(h + 1) * HEAD_DIM] = 1.0
        eb[HEADS + h, h * HEAD_DIM:(h + 1) * HEAD_DIM] = 1.0
    return jnp.asarray(ea, BF16), jnp.asarray(eb, BF16)


def kernel(x, norm1_w, w_in, sgu_ln_w, sgu_ln_b, sgu_w_spatial, sgu_b_spatial, sc_conv_w, dn_conv_w, dn_a_log, dn_dt_bias, dn_norm_w, gla_w_gate2, gla_gate_bias, gla_norm_w, w_out, norm2_w, w_gate_up, w_down, final_norm_w):
    bsz, seq, d = x.shape
    depth = w_in.shape[0]
    assert seq % ROW_TILE == 0 and seq % SEQ_TILE == 0 and d == D_MODEL
    m = bsz * seq
    xf = x.reshape(m, d)
    bd, ltri = _const_inputs()
    ea, eb = _gate_selectors()
    w_in_b = w_in.astype(BF16)
    ffn_jobs = lambda layer: [(w_out, layer, D_MODEL), (w_gate_up, layer, 2 * D_FF), (w_down, layer, D_MODEL)]
    inproj_params = (sgu_w_spatial, sgu_b_spatial, sc_conv_w, dn_conv_w, ea, eb, bd, gla_w_gate2)
    inproj_vecs = (norm1_w, sgu_ln_w, sgu_ln_b, gla_gate_bias, dn_a_log, dn_dt_bias)
    x_in = xf.reshape(bsz, seq, d)
    for l in range(depth):
        ya, yb, cpk, cg, dpk, dla, wo_b, wgu_b, wd_b = _inproj(x_in, *inproj_vecs, w_in_b, l, *inproj_params,
                                                               cast_jobs=ffn_jobs(l))
        yc, yd = _recur(cpk, cg, dpk, dla, dn_norm_w, gla_norm_w, bd, ltri, l)
        last = l == depth - 1
        outs = _outproj_ffn(xf, tuple(y.reshape(m, GROUP) for y in (ya, yb, yc, yd)),
                            wo_b, norm2_w, final_norm_w if last else norm1_w, wgu_b, wd_b,
                            layer=l, final=last)
        xf = outs[0]
        if not last:
            x_in = outs[1].reshape(bsz, seq, d)
    return xf.reshape(bsz, seq, d)
```
